```python
import math
import jax, jax.numpy as jnp
from jax import lax
import numpy as np

D_MODEL = 1024
BATCH = 32
SEQ = 2048
DEPTH = 2

GRID_W = 64
CTX_LEN = 256

NA_HEADS = 4
NA_DIM = 64
NA_WIN_H = 8
NA_WIN_W = 16
NA_QBLK_W = 16
NA_KBAND_W = NA_QBLK_W + NA_WIN_W
RET_HEADS = 4
RET_DK = 64
RET_DV = 64
RET_CHUNK = 128
GQA_HEADS = 4
GQA_KV_HEADS = 2
GQA_DIM = 64
DIFF_HEADS = 4
DIFF_QK_DIM = 32
DIFF_V_DIM = 64
Q_BLOCK = 128
N_BRANCH = 4
BRANCH_W = 256
ROPE_THETA = 10000.0
EPS = 1e-6
NEG_INF = -1e30
N_EXPERTS = 16
N_GROUPS = 4
EXPERTS_PER_GROUP = N_EXPERTS // N_GROUPS
TOP_K = 2
D_EXPERT = 512
MOE_BLOCK = 512

NA_COLS = 3 * NA_HEADS * NA_DIM
RET_COLS = RET_HEADS * (2 * RET_DK + 2 * RET_DV)
GQA_COLS = (GQA_HEADS + 2 * GQA_KV_HEADS) * GQA_DIM
DIFF_COLS = DIFF_HEADS * (4 * DIFF_QK_DIM + DIFF_V_DIM)
MIX_COLS = NA_COLS + RET_COLS + GQA_COLS + DIFF_COLS
MIX_SPLITS = (NA_COLS, NA_COLS + RET_COLS, NA_COLS + RET_COLS + GQA_COLS)
IN_COLS = MIX_COLS + N_BRANCH * D_MODEL

kernel_name = 'hybrid_grid_diffusion_block'


def rms_norm(x, g):
    xf = x.astype(jnp.float32)
    y = xf * lax.rsqrt(jnp.mean(xf * xf, axis=-1, keepdims=True) + EPS)
    return (y * g.astype(jnp.float32)).astype(x.dtype)


def modulate(h, shift, scale):
    return h * (1.0 + scale) + shift


def split_heads(t, n):
    b, l, _ = t.shape
    return t.reshape(b, l, n, -1).transpose(0, 2, 1, 3)


def merge_heads(t):
    b, n, l, d = t.shape
    return t.transpose(0, 2, 1, 3).reshape(b, l, n * d)


def grid_positions(length):
    t = jnp.arange(length, dtype=jnp.int32)
    return t // GRID_W, t % GRID_W


def axial_rope(x, rows, cols):
    d = x.shape[-1]
    half = d // 2
    nf = half // 2
    inv = 1.0 / (ROPE_THETA ** (jnp.arange(nf, dtype=jnp.float32) / nf))

    def rot(xp, pos):
        ang = pos.astype(jnp.float32)[:, None] * inv[None, :]
        cos, sin = jnp.cos(ang), jnp.sin(ang)
        x1 = xp[..., :nf].astype(jnp.float32)
        x2 = xp[..., nf:].astype(jnp.float32)
        return jnp.concatenate([x1 * cos - x2 * sin, x2 * cos + x1 * sin], axis=-1)

    return jnp.concatenate([rot(x[..., :half], rows), rot(x[..., half:], cols)], axis=-1).astype(x.dtype)


def blocked_attention(q, k, v, scale):
    b, hk, g, lq, d = q.shape
    nb = lq // Q_BLOCK
    qb = q.reshape(b, hk, g, nb, Q_BLOCK, d).transpose(3, 0, 1, 2, 4, 5)

    def one_block(qi):
        s = jnp.einsum('bhgqd,bhkd->bhgqk', qi, k).astype(jnp.float32) * scale
        p = jax.nn.softmax(s, axis=-1).astype(v.dtype)
        return jnp.einsum('bhgqk,bhkd->bhgqd', p, v)

    o = lax.map(one_block, qb)
    return o.transpose(1, 2, 3, 0, 4, 5).reshape(b, hk, g, lq, v.shape[-1])


def blocked_diff_attention(q, k, v, lam, scale):
    b, h, _, lq, dq = q.shape
    nb = lq // Q_BLOCK
    qb = q.reshape(b, h, 2, nb, Q_BLOCK, dq).transpose(3, 0, 1, 2, 4, 5)

    def one_block(qi):
        s = jnp.einsum('bhmqd,bhmkd->bhmqk', qi, k).astype(jnp.float32) * scale
        p = jax.nn.softmax(s, axis=-1)
        a = (p[:, :, 0] - lam * p[:, :, 1]).astype(v.dtype)
        return jnp.einsum('bhqk,bhkd->bhqd', a, v)

    o = lax.map(one_block, qb)
    return o.transpose(1, 2, 0, 3, 4).reshape(b, h, lq, v.shape[-1])


def neighbourhood_attention(p_lat, p_ctx, rpb, need_ctx):
    b, l, _ = p_lat.shape
    rows = l // GRID_W
    win_h = min(NA_WIN_H, rows)
    n_cb = GRID_W // NA_QBLK_W
    n_win = win_h * NA_KBAND_W
    scale = NA_DIM ** -0.5
    q, k, v = (split_heads(t, NA_HEADS) for t in jnp.split(p_lat, 3, axis=-1))
    qc, kc, vc = (split_heads(t, NA_HEADS) for t in jnp.split(p_ctx, 3, axis=-1))
    qg = (q * scale).reshape(b, NA_HEADS, rows, GRID_W, NA_DIM)
    kg = k.reshape(b, NA_HEADS, rows, GRID_W, NA_DIM)
    vg = v.reshape(b, NA_HEADS, rows, GRID_W, NA_DIM)
    band_start = np.clip(np.arange(n_cb) * NA_QBLK_W - NA_WIN_W // 2, 0, GRID_W - NA_KBAND_W)
    band_cols = band_start[:, None] + np.arange(NA_KBAND_W)[None, :]
    q_cols = np.arange(GRID_W).reshape(n_cb, NA_QBLK_W)
    win_start = np.clip(q_cols - NA_WIN_W // 2, 0, GRID_W - NA_WIN_W)
    col_ok = ((band_cols[:, None, :] >= win_start[:, :, None])
              & (band_cols[:, None, :] < win_start[:, :, None] + NA_WIN_W))
    col_idx = np.clip(band_cols[:, None, :] - q_cols[:, :, None], -(NA_WIN_W - 1), NA_WIN_W - 1) + NA_WIN_W - 1
    key_ok = np.broadcast_to(col_ok[:, :, None, :], (n_cb, NA_QBLK_W, win_h, NA_KBAND_W)).reshape(n_cb, NA_QBLK_W, n_win)

    def attend_row(r):
        r0 = jnp.clip(r - win_h // 2, 0, rows - win_h)
        k_band = lax.dynamic_slice_in_dim(kg, r0, win_h, axis=2)[:, :, :, band_cols]
        v_band = lax.dynamic_slice_in_dim(vg, r0, win_h, axis=2)[:, :, :, band_cols]
        k_blk = k_band.transpose(0, 1, 3, 2, 4, 5).reshape(b, NA_HEADS, n_cb, n_win, NA_DIM)
        v_blk = v_band.transpose(0, 1, 3, 2, 4, 5).reshape(b, NA_HEADS, n_cb, n_win, NA_DIM)
        q_row = lax.dynamic_index_in_dim(qg, r, axis=2, keepdims=False).reshape(b, NA_HEADS, n_cb, NA_QBLK_W, NA_DIM)
        row_idx = r0 + jnp.arange(win_h) - r + NA_WIN_H - 1
        bias = rpb[:, row_idx][:, :, col_idx]
        bias = bias.transpose(0, 2, 3, 1, 4).reshape(NA_HEADS, n_cb, NA_QBLK_W, n_win)
        s_win = jnp.einsum('bhcqd,bhckd->bhcqk', q_row, k_blk).astype(jnp.float32) + bias.astype(jnp.float32)
        s_win = jnp.where(key_ok, s_win, NEG_INF)
        s_ctx = jnp.einsum('bhcqd,bhkd->bhcqk', q_row, kc).astype(jnp.float32)
        p = jax.nn.softmax(jnp.concatenate([s_win, s_ctx], axis=-1), axis=-1).astype(v.dtype)
        o = (jnp.einsum('bhcqk,bhckd->bhcqd', p[..., :n_win], v_blk)
             + jnp.einsum('bhcqk,bhkd->bhcqd', p[..., n_win:], vc))
        return o.reshape(b, NA_HEADS, GRID_W, NA_DIM)

    o = lax.map(attend_row, jnp.arange(rows, dtype=jnp.int32))
    y_lat = o.transpose(1, 0, 3, 2, 4).reshape(b, l, NA_HEADS * NA_DIM)
    y_ctx = None
    if need_ctx:
        y_ctx = merge_heads(blocked_attention(qc[:, :, None], kc, vc, scale)[:, :, 0])
    return y_lat, y_ctx


def retention_scan(q, k, v, log_gamma, state0):
    b, h, l, _ = q.shape
    dv = v.shape[-1]
    n = l // RET_CHUNK

    def to_chunks(t):
        return t.reshape(b, h, n, RET_CHUNK, t.shape[-1]).transpose(2, 0, 1, 3, 4)

    pos = jnp.arange(RET_CHUNK, dtype=jnp.float32)
    lag = pos[:, None] - pos[None, :]
    lower = lag >= 0
    intra = jnp.where(lower, jnp.exp(jnp.where(lower, lag, 0.0)[None] * log_gamma[:, None, None]), 0.0)
    q_dec = jnp.exp((pos + 1.0)[None, :] * log_gamma[:, None])
    k_dec = jnp.exp((RET_CHUNK - 1.0 - pos)[None, :] * log_gamma[:, None])
    chunk_dec = jnp.exp(RET_CHUNK * log_gamma)

    def step(state, inp):
        qi, ki, vi = inp
        inner = jnp.einsum('bhqd,bhkd->bhqk', qi, ki) * intra
        o = (jnp.einsum('bhqk,bhke->bhqe', inner, vi)
             + jnp.einsum('bhqd,bhde->bhqe', qi * q_dec[..., None], state))
        state = state * chunk_dec[:, None, None] + jnp.einsum('bhkd,bhke->bhde', ki * k_dec[..., None], vi)
        return state, o

    state, o = lax.scan(step, state0, (to_chunks(q), to_chunks(k), to_chunks(v)))
    return o.transpose(1, 2, 0, 3, 4).reshape(b, h, l, dv), state


def retention_mixer(p_lat, p_ctx, log_decay_param, rows_pos, cols_pos, need_ctx):
    splits = [RET_HEADS * RET_DK, 2 * RET_HEADS * RET_DK, 2 * RET_HEADS * RET_DK + RET_HEADS * RET_DV]

    def prep(p, rope):
        q, k, v, g = jnp.split(p, splits, axis=-1)
        q, k, v = (split_heads(t, RET_HEADS).astype(jnp.float32) for t in (q, k, v))
        if rope:
            q = axial_rope(q, rows_pos, cols_pos)
            k = axial_rope(k, rows_pos, cols_pos)
        return q, k * RET_DK ** -0.5, v, g

    ql, kl, vl, gl = prep(p_lat, True)
    qc, kc, vc, gc = prep(p_ctx, False)
    log_gamma = jnp.log1p(-jnp.exp(log_decay_param.astype(jnp.float32)))
    zero = jnp.zeros((p_lat.shape[0], RET_HEADS, RET_DK, RET_DV), jnp.float32)

    def flip(t):
        return jnp.flip(t, axis=2)

    oc_f, s_f = retention_scan(qc, kc, vc, log_gamma[0], zero)
    oc_b, s_b = retention_scan(flip(qc), flip(kc), flip(vc), log_gamma[1], zero)
    ol_f, _ = retention_scan(ql, kl, vl, log_gamma[0], s_f)
    ol_b, _ = retention_scan(flip(ql), flip(kl), flip(vl), log_gamma[1], s_b)

    def finish(o, g):
        mu = jnp.mean(o, axis=-1, keepdims=True)
        var = jnp.mean(jnp.square(o - mu), axis=-1, keepdims=True)
        o = (o - mu) * lax.rsqrt(var + EPS)
        return jax.nn.silu(g) * merge_heads(o).astype(g.dtype)

    y_lat = finish(ol_f + flip(ol_b), gl)
    y_ctx = finish(oc_f + flip(oc_b), gc) if need_ctx else None
    return y_lat, y_ctx


def gqa_mixer(p_lat, p_ctx, q_gain, k_gain, rows_pos, cols_pos, need_ctx):
    nq = GQA_HEADS * GQA_DIM
    nkv = GQA_KV_HEADS * GQA_DIM
    group = GQA_HEADS // GQA_KV_HEADS
    scale = GQA_DIM ** -0.5

    def prep(p):
        q, k, v = jnp.split(p, [nq, nq + nkv], axis=-1)
        return (rms_norm(split_heads(q, GQA_HEADS), q_gain),
                rms_norm(split_heads(k, GQA_KV_HEADS), k_gain),
                split_heads(v, GQA_KV_HEADS))

    def grouped(q):
        b, _, l, d = q.shape
        return q.reshape(b, GQA_KV_HEADS, group, l, d)

    def ungroup(o):
        b, hk, g, l, d = o.shape
        return merge_heads(o.reshape(b, hk * g, l, d))

    ql, kl, vl = prep(p_lat)
    qc, kc, vc = prep(p_ctx)
    ql = axial_rope(ql, rows_pos, cols_pos)
    kl = axial_rope(kl, rows_pos, cols_pos)
    k_all = jnp.concatenate([kl, kc], axis=2)
    v_all = jnp.concatenate([vl, vc], axis=2)
    y_lat = ungroup(blocked_attention(grouped(ql), k_all, v_all, scale))
    y_ctx = ungroup(blocked_attention(grouped(qc), kc, vc, scale)) if need_ctx else None
    return y_lat, y_ctx


def diff_mixer(p_lat, p_ctx, lam_params, subln_gain, layer_idx, rows_pos, cols_pos, need_ctx):
    nqk = DIFF_HEADS * 2 * DIFF_QK_DIM
    scale = DIFF_QK_DIM ** -0.5

    def prep(p):
        b, l, _ = p.shape
        q, k, v = jnp.split(p, [nqk, 2 * nqk], axis=-1)
        q = q.reshape(b, l, DIFF_HEADS, 2, DIFF_QK_DIM).transpose(0, 2, 3, 1, 4)
        k = k.reshape(b, l, DIFF_HEADS, 2, DIFF_QK_DIM).transpose(0, 2, 3, 1, 4)
        return q, k, split_heads(v, DIFF_HEADS)

    ql, kl, vl = prep(p_lat)
    qc, kc, vc = prep(p_ctx)
    ql = axial_rope(ql, rows_pos, cols_pos)
    kl = axial_rope(kl, rows_pos, cols_pos)
    lam_init = 0.8 - 0.6 * math.exp(-0.3 * layer_idx)
    lp = lam_params.astype(jnp.float32)
    lam = jnp.exp(jnp.sum(lp[0] * lp[1])) - jnp.exp(jnp.sum(lp[2] * lp[3])) + lam_init

    def finish(o):
        return merge_heads(rms_norm(o, subln_gain) * (1.0 - lam_init))

    k_all = jnp.concatenate([kl, kc], axis=3)
    v_all = jnp.concatenate([vl, vc], axis=2)
    y_lat = finish(blocked_diff_attention(ql, k_all, v_all, lam, scale))
    y_ctx = finish(blocked_diff_attention(qc, kc, vc, lam, scale)) if need_ctx else None
    return y_lat, y_ctx


def token_mixers(h_lat, h_ctx, w_in, na_rpb, ret_log_decay, gqa_q_gain, gqa_k_gain,
                 diff_lambda, diff_subln, w_branch, w_out, layer_idx, need_ctx):
    d = h_lat.shape[-1]
    rows_pos, cols_pos = grid_positions(h_lat.shape[1])
    w_mix = w_in[:, :MIX_COLS]
    a_l, b_l, c_l, d_l = jnp.split(h_lat @ w_mix, MIX_SPLITS, axis=-1)
    a_c, b_c, c_c, d_c = jnp.split(h_ctx @ w_mix, MIX_SPLITS, axis=-1)
    ya = neighbourhood_attention(a_l, a_c, na_rpb, need_ctx)
    yb = retention_mixer(b_l, b_c, ret_log_decay, rows_pos, cols_pos, need_ctx)
    yc = gqa_mixer(c_l, c_c, gqa_q_gain, gqa_k_gain, rows_pos, cols_pos, need_ctx)
    yd = diff_mixer(d_l, d_c, diff_lambda, diff_subln, layer_idx, rows_pos, cols_pos, need_ctx)

    def merge(h, ys):
        acc = None
        for n, y in enumerate(ys):
            gate = jax.nn.sigmoid(h @ w_in[:, MIX_COLS + n * d: MIX_COLS + (n + 1) * d])
            term = gate * (y @ w_branch[n])
            acc = term if acc is None else acc + term
        return acc @ w_out

    y_lat = merge(h_lat, (ya[0], yb[0], yc[0], yd[0]))
    y_ctx = merge(h_ctx, (ya[1], yb[1], yc[1], yd[1])) if need_ctx else None
    return y_lat, y_ctx


def expert_dispatch(h, e_idx, e_w, w_gate, w_up, w_down):
    n, d = h.shape
    a = n * TOP_K
    flat_e = e_idx.reshape(a)
    flat_tok = jnp.repeat(jnp.arange(n, dtype=jnp.int32), TOP_K)
    flat_w = e_w.reshape(a)
    order = jnp.argsort(flat_e)
    se = flat_e[order]
    counts = jnp.bincount(flat_e, length=N_EXPERTS)
    padded = (counts + MOE_BLOCK - 1) // MOE_BLOCK * MOE_BLOCK
    pad_end = jnp.cumsum(padded)
    pad_start = pad_end - padded
    grp_start = jnp.cumsum(counts) - counts
    dest = pad_start[se] + jnp.arange(a, dtype=jnp.int32) - grp_start[se]
    n_blocks = -(-a // MOE_BLOCK) + N_EXPERTS
    total = n_blocks * MOE_BLOCK
    slot_tok = jnp.full((total,), n, jnp.int32).at[dest].set(flat_tok[order])
    slot_w = jnp.zeros((total,), flat_w.dtype).at[dest].set(flat_w[order])
    blk_e = jnp.minimum(jnp.searchsorted(pad_end, jnp.arange(n_blocks, dtype=jnp.int32) * MOE_BLOCK, side='right'),
                        N_EXPERTS - 1)
    h_pad = jnp.concatenate([h, jnp.zeros((1, d), h.dtype)], axis=0)
    xb = h_pad[slot_tok].reshape(n_blocks, MOE_BLOCK, d)

    def run(args):
        xi, e = args
        return (jax.nn.silu(xi @ w_gate[e]) * (xi @ w_up[e])) @ w_down[e]

    yb = lax.map(run, (xb, blk_e)).reshape(total, d)
    out = jnp.zeros((n + 1, d), h.dtype).at[slot_tok].add(yb * slot_w[:, None].astype(yb.dtype))
    return out[:n]


def routed_moe(h, w_router, router_bias, w_gate, w_up, w_down):
    n = h.shape[0]
    s = jax.nn.sigmoid((h @ w_router).astype(jnp.float32))
    sel = (s + router_bias.astype(jnp.float32)).reshape(n, N_GROUPS, EXPERTS_PER_GROUP)
    group_score = jnp.sum(lax.top_k(sel, TOP_K)[0], axis=-1)
    g_idx = jnp.argmax(group_score, axis=-1).astype(jnp.int32)
    in_group = jnp.take_along_axis(sel, g_idx[:, None, None], axis=1)[:, 0]
    _, local = lax.top_k(in_group, TOP_K)
    e_idx = g_idx[:, None] * EXPERTS_PER_GROUP + local.astype(jnp.int32)
    w = jnp.take_along_axis(s, e_idx, axis=1)
    w = w / jnp.sum(w, axis=-1, keepdims=True)
    return expert_dispatch(h, e_idx, w, w_gate, w_up, w_down)


def setup_inputs(seed: int = 0) -> dict:
    key = jax.random.key(seed)
    ks = jax.random.split(key, 23)
    d = D_MODEL

    def normal(k, shape, std):
        return jax.random.normal(k, shape, jnp.float32) * std

    ret_base = -(5.0 + jnp.arange(RET_HEADS, dtype=jnp.float32)) * math.log(2.0)
    return {
        'x': normal(ks[0], (BATCH, SEQ, d), 1.0),
        'c': normal(ks[1], (BATCH, d), 1.0),
        'ctx': normal(ks[2], (BATCH, CTX_LEN, d), 1.0),
        'c_ctx': normal(ks[3], (d,), 1.0),
        'w_mod': normal(ks[4], (DEPTH, d, 6 * d), 0.5 * d ** -0.5),
        'b_mod': normal(ks[5], (DEPTH, 6 * d), 0.02),
        'g_norm1': 1.0 + normal(ks[6], (DEPTH, d), 0.02),
        'g_norm2': 1.0 + normal(ks[7], (DEPTH, d), 0.02),
        'w_in': normal(ks[8], (DEPTH, d, IN_COLS), d ** -0.5),
        'na_rpb': normal(ks[9], (DEPTH, NA_HEADS, 2 * NA_WIN_H - 1, 2 * NA_WIN_W - 1), 0.02),
        'ret_log_decay': ret_base + normal(ks[10], (DEPTH, 2, RET_HEADS), 0.05),
        'gqa_q_gain': 1.0 + normal(ks[11], (DEPTH, GQA_DIM), 0.02),
        'gqa_k_gain': 1.0 + normal(ks[12], (DEPTH, GQA_DIM), 0.02),
        'diff_lambda': normal(ks[13], (DEPTH, 4, DIFF_QK_DIM), 0.1),
        'diff_subln': 1.0 + normal(ks[14], (DEPTH, DIFF_V_DIM), 0.02),
        'w_branch': normal(ks[15], (DEPTH, N_BRANCH, BRANCH_W, d), BRANCH_W ** -0.5),
        'w_out': normal(ks[16], (DEPTH, d, d), d ** -0.5),
        'w_router': normal(ks[17], (d, N_EXPERTS), d ** -0.5),
        'router_bias': normal(ks[18], (N_EXPERTS,), 0.01),
        'w_gate_e': normal(ks[19], (DEPTH, N_EXPERTS, d, D_EXPERT), d ** -0.5),
        'w_up_e': normal(ks[20], (DEPTH, N_EXPERTS, d, D_EXPERT), d ** -0.5),
        'w_down_e': normal(ks[21], (DEPTH, N_EXPERTS, D_EXPERT, d), D_EXPERT ** -0.5),
        'g_final': 1.0 + normal(ks[22], (d,), 0.02),
    }


def reference(x, c, ctx, c_ctx, w_mod, b_mod, g_norm1, g_norm2, w_in, na_rpb, ret_log_decay,
              gqa_q_gain, gqa_k_gain, diff_lambda, diff_subln, w_branch, w_out, w_router,
              router_bias, w_gate_e, w_up_e, w_down_e, g_final):
    b, l, d = x.shape
    n_ctx = ctx.shape[1]
    hc = ctx
    for layer in range(DEPTH):
        need_ctx = layer < DEPTH - 1
        mod_lat = (jax.nn.silu(c) @ w_mod[layer] + b_mod[layer])[:, None, :]
        mod_ctx = (jax.nn.silu(c_ctx) @ w_mod[layer] + b_mod[layer])[None, None, :]
        sh1, sc1, gt1, sh2, sc2, gt2 = jnp.split(mod_lat, 6, axis=-1)
        sh1c, sc1c, gt1c, sh2c, sc2c, gt2c = jnp.split(mod_ctx, 6, axis=-1)
        h_lat = modulate(rms_norm(x, g_norm1[layer]), sh1, sc1)
        h_ctx = modulate(rms_norm(hc, g_norm1[layer]), sh1c, sc1c)
        y_lat, y_ctx = token_mixers(h_lat, h_ctx, w_in[layer], na_rpb[layer], ret_log_decay[layer],
                                    gqa_q_gain[layer], gqa_k_gain[layer], diff_lambda[layer],
                                    diff_subln[layer], w_branch[layer], w_out[layer], layer, need_ctx)
        x = x + gt1 * y_lat
        m_lat = modulate(rms_norm(x, g_norm2[layer]), sh2, sc2).reshape(b * l, d)
        if need_ctx:
            hc = hc + gt1c * y_ctx
            m_ctx = modulate(rms_norm(hc, g_norm2[layer]), sh2c, sc2c).reshape(b * n_ctx, d)
            f = routed_moe(jnp.concatenate([m_lat, m_ctx], axis=0), w_router, router_bias,
                           w_gate_e[layer], w_up_e[layer], w_down_e[layer])
            x = x + gt2 * f[:b * l].reshape(b, l, d)
            hc = hc + gt2c * f[b * l:].reshape(b, n_ctx, d)
        else:
            f = routed_moe(m_lat, w_router, router_bias, w_gate_e[layer], w_up_e[layer], w_down_e[layer])
            x = x + gt2 * f.reshape(b, l, d)
    return rms_norm(x, g_final)
```

```python
import functools
import math

import numpy as np
import jax
import jax.numpy as jnp
from jax import lax
from jax.experimental import pallas as pl
from jax.experimental.pallas import tpu as pltpu

F32 = jnp.float32
BF16 = jnp.bfloat16

D_MODEL = 1024
DEPTH = 2
GRID_W = 64
NA_HEADS = 4
NA_DIM = 64
NA_WIN_H = 8
NA_WIN_W = 16
RET_HEADS = 4
RET_DK = 64
RET_CHUNK = 128
GQA_HEADS = 4
GQA_KV_HEADS = 2
GQA_DIM = 64
DIFF_HEADS = 4
DIFF_QK_DIM = 32
DIFF_V_DIM = 64
N_BRANCH = 4
BRANCH_W = 256
ROPE_THETA = 10000.0
EPS = 1e-6
NEG_INF = -1e30
N_EXPERTS = 16
N_GROUPS = 4
EXPERTS_PER_GROUP = 4
TOP_K = 2
D_EXPERT = 512

MIX_COLS = 3072
IN_COLS = MIX_COLS + N_BRANCH * D_MODEL
NA_Q, NA_K, NA_V = 0, 1, 2
RET_Q, RET_K, RET_V, RET_G = 3, 4, 5, 6
GQA_Q = 7
DIFF_Q, DIFF_K, DIFF_V = 9, 10, 11

VMEM_LIMIT = 56 * 1024 * 1024
MOE_BLK = 512
CMB_TM = 256


def _dot(a, b):
    return jnp.dot(a, b, preferred_element_type=F32)


def _dot_nt(a, b):
    return lax.dot_general(a, b, (((1,), (1,)), ((), ())), preferred_element_type=F32)


def _dot_tn(a, b):
    return lax.dot_general(a, b, (((0,), (0,)), ((), ())), preferred_element_type=F32)


def _sigmoid(x):
    return 1.0 / (1.0 + jnp.exp(-x))


def _params(*sem):
    return pltpu.CompilerParams(dimension_semantics=sem, vmem_limit_bytes=VMEM_LIMIT)


def _swap_halves(x, dist):
    pieces = []
    for j in range(x.shape[-1] // 128):
        xs = x[:, j * 128:(j + 1) * 128]
        lane = lax.broadcasted_iota(jnp.int32, xs.shape, 1)
        up = pltpu.roll(xs, 128 - dist, 1)
        dn = pltpu.roll(xs, dist, 1)
        pieces.append(jnp.where((lane % (2 * dist)) < dist, up, dn))
    return pieces[0] if len(pieces) == 1 else jnp.concatenate(pieces, axis=-1)


def _rope(x, cos, sin, dist):
    return x * cos + _swap_halves(x, dist) * sin


def _mod_kernel(c_ref, w_ref, b_ref, o_ref):
    c = c_ref[...]
    a = (c * _sigmoid(c)).astype(BF16)
    o_ref[...] = _dot(a, w_ref[...].astype(BF16)) + b_ref[...]


def _modulation(cc, w_mod, b_mod):
    nrow = cc.shape[0]
    depth, d, n6 = w_mod.shape
    tn = 1536
    return pl.pallas_call(
        _mod_kernel,
        grid=(depth, n6 // tn),
        in_specs=[pl.BlockSpec((nrow, d), lambda l, j: (0, 0)),
                  pl.BlockSpec((None, d, tn), lambda l, j: (l, 0, j)),
                  pl.BlockSpec((None, 1, tn), lambda l, j: (l, 0, j))],
        out_specs=pl.BlockSpec((None, nrow, tn), lambda l, j: (l, 0, j)),
        out_shape=jax.ShapeDtypeStruct((depth, nrow, n6), F32),
        compiler_params=_params("parallel", "arbitrary"),
        name="modulation",
    )(cc, w_mod, b_mod.reshape(depth, 1, n6))


def _proj_kernel(x_ref, g_ref, mod_ref, w_ref, o_ref, h_ref):
    j = pl.program_id(1)

    @pl.when(j == 0)
    def _():
        x = x_ref[...]
        y = x * lax.rsqrt(jnp.mean(x * x, axis=-1, keepdims=True) + EPS) * g_ref[...]
        h_ref[...] = (y * (1.0 + mod_ref[1:2, :]) + mod_ref[0:1, :]).astype(BF16)

    acc = _dot(h_ref[...], w_ref[...])

    @pl.when(j < MIX_COLS // D_MODEL)
    def _():
        o_ref[...] = acc.astype(BF16)

    @pl.when(j >= MIX_COLS // D_MODEL)
    def _():
        o_ref[...] = _sigmoid(acc).astype(BF16)


def _mod_index(tm, n_lat, seq, batch):
    nlat_blk = n_lat // tm
    bpb = seq // tm

    def index(i):
        return jnp.where(i < nlat_blk, i // bpb, batch)
    return index


def _project(x_all, g, modtab, w_bf, tm, n_lat, seq, batch):
    n_tot, d = x_all.shape
    ncol = w_bf.shape[1]
    tn = D_MODEL
    midx = _mod_index(tm, n_lat, seq, batch)
    return pl.pallas_call(
        _proj_kernel,
        grid=(n_tot // tm, ncol // tn),
        in_specs=[pl.BlockSpec((tm, d), lambda i, j: (i, 0)),
                  pl.BlockSpec((1, d), lambda i, j: (0, 0)),
                  pl.BlockSpec((None, 8, d), lambda i, j: (midx(i), 0, 0)),
                  pl.BlockSpec((d, tn), lambda i, j: (0, j))],
        out_specs=pl.BlockSpec((tm, tn), lambda i, j: (i, j)),
        out_shape=jax.ShapeDtypeStruct((n_tot, ncol), BF16),
        scratch_shapes=[pltpu.VMEM((tm, d), BF16)],
        compiler_params=_params("parallel", "arbitrary"),
        name="norm_project",
    )(x_all, g.reshape(1, d), modtab, w_bf)


def _na_kernel(need_ctx, rows, q_ref, k_ref, v_ref, qc_ref, kc_ref, vc_ref, tb_ref, y_ref, *rest):
    scale = NA_DIM ** -0.5
    win_h = NA_WIN_H
    band = win_h * GRID_W
    for h in range(NA_HEADS):
        hs = slice(h * NA_DIM, (h + 1) * NA_DIM)
        kc = kc_ref[:, hs]
        vc = vc_ref[:, hs]

        def row_body(r, carry):
            r0 = jnp.clip(r - win_h // 2, 0, rows - win_h)
            dd = r - r0
            qs = pl.multiple_of(r * GRID_W, GRID_W)
            ks = pl.multiple_of(r0 * GRID_W, GRID_W)
            q_r = q_ref[pl.ds(qs, GRID_W), hs]
            kb = k_ref[pl.ds(ks, band), hs]
            vb = v_ref[pl.ds(ks, band), hs]
            s_w = _dot_nt(q_r, kb) * scale + tb_ref[h, dd]
            s_c = _dot_nt(q_r, kc) * scale
            m = jnp.maximum(jnp.max(s_w, axis=-1, keepdims=True), jnp.max(s_c, axis=-1, keepdims=True))
            p_w = jnp.exp(s_w - m)
            p_c = jnp.exp(s_c - m)
            l = jnp.sum(p_w, axis=-1, keepdims=True) + jnp.sum(p_c, axis=-1, keepdims=True)
            o = (_dot(p_w.astype(BF16), vb) + _dot(p_c.astype(BF16), vc)) / l
            y_ref[pl.ds(qs, GRID_W), hs] = o.astype(BF16)
            return carry

        lax.fori_loop(0, rows, row_body, 0)

        if need_ctx:
            yc_ref = rest[0]
            s = _dot_nt(qc_ref[:, hs], kc) * scale
            m = jnp.max(s, axis=-1, keepdims=True)
            p = jnp.exp(s - m)
            l = jnp.sum(p, axis=-1, keepdims=True)
            yc_ref[:, hs] = (_dot(p.astype(BF16), vc) / l).astype(BF16)


def _na_bias_table(rpb, rows):
    win_h = min(NA_WIN_H, rows)
    qc = np.arange(GRID_W)[:, None]
    kc = np.arange(GRID_W)[None, :]
    win_start = np.clip(qc - NA_WIN_W // 2, 0, GRID_W - NA_WIN_W)
    col_ok = (kc >= win_start) & (kc < win_start + NA_WIN_W)
    col_idx = np.clip(kc - qc, -(NA_WIN_W - 1), NA_WIN_W - 1) + NA_WIN_W - 1
    dd = np.arange(win_h)[:, None]
    ii = np.arange(win_h)[None, :]
    row_idx = ii - dd + NA_WIN_H - 1
    t = rpb.astype(F32)[:, row_idx][:, :, :, col_idx]
    t = jnp.where(col_ok[None, None, None], t, NEG_INF)
    return t.transpose(0, 1, 3, 2, 4).reshape(NA_HEADS, win_h, GRID_W, win_h * GRID_W)


def _na_mixer(p_all, rpb, batch, seq, n_ctx, need_ctx):
    n_lat = batch * seq
    rows = seq // GRID_W
    tb = _na_bias_table(rpb, rows)
    cb = n_lat // n_ctx
    w = 256
    lat = lambda c: pl.BlockSpec((seq, w), lambda b, c=c: (b, c))
    ctx = lambda c: pl.BlockSpec((n_ctx, w), lambda b, c=c: (cb + b, c))
    out_shape = [jax.ShapeDtypeStruct((n_lat, w), BF16)]
    out_specs = [pl.BlockSpec((seq, w), lambda b: (b, 0))]
    if need_ctx:
        out_shape.append(jax.ShapeDtypeStruct((batch * n_ctx, w), BF16))
        out_specs.append(pl.BlockSpec((n_ctx, w), lambda b: (b, 0)))
    return pl.pallas_call(
        functools.partial(_na_kernel, need_ctx, rows),
        grid=(batch,),
        in_specs=[lat(NA_Q), lat(NA_K), lat(NA_V), ctx(NA_Q), ctx(NA_K), ctx(NA_V),
                  pl.BlockSpec(tb.shape, lambda b: (0, 0, 0, 0))],
        out_specs=out_specs,
        out_shape=out_shape,
        compiler_params=_params("parallel"),
        name="na_mixer",
    )(p_all, p_all, p_all, p_all, p_all, p_all, tb)


def _ret_kernel(need_ctx, seq, n_ctx, lg_ref, q_ref, k_ref, v_ref, g_ref, qc_ref, kc_ref, vc_ref, gc_ref,
                cos_ref, sin_ref, y_ref, *rest):
    if need_ctx:
        yc_ref, qr, kr, oacc, ocacc = rest
    else:
        qr, kr, oacc, ocacc = rest
    ch = RET_CHUNK
    n_lat_ch = seq // ch
    n_ctx_ch = n_ctx // ch
    kscale = RET_DK ** -0.5

    def prep(i, carry):
        rs = pl.multiple_of(i * ch, ch)
        c = cos_ref[pl.ds(rs, ch), :]
        s = sin_ref[pl.ds(rs, ch), :]
        qr[pl.ds(rs, ch), :] = _rope(q_ref[pl.ds(rs, ch), :].astype(F32), c, s, 16)
        kr[pl.ds(rs, ch), :] = _rope(k_ref[pl.ds(rs, ch), :].astype(F32), c, s, 16) * kscale
        return carry

    lax.fori_loop(0, n_lat_ch, prep, 0)

    pos = lax.broadcasted_iota(jnp.int32, (ch, RET_DK), 0).astype(F32)
    ri = lax.broadcasted_iota(jnp.int32, (ch, ch), 0)
    ci = lax.broadcasted_iota(jnp.int32, (ch, ch), 1)
    lag = (ri - ci).astype(F32)

    for h in range(RET_HEADS):
        hs = slice(h * RET_DK, (h + 1) * RET_DK)
        for dirn in range(2):
            lg = lg_ref[dirn, h]
            if dirn == 0:
                keep = ri >= ci
                dmat = jnp.where(keep, jnp.exp(jnp.where(keep, lag, 0.0) * lg), 0.0)
                qdec = jnp.exp((pos + 1.0) * lg)
                kdec = jnp.exp((ch - 1.0 - pos) * lg)
            else:
                keep = ci >= ri
                dmat = jnp.where(keep, jnp.exp(jnp.where(keep, -lag, 0.0) * lg), 0.0)
                qdec = jnp.exp((ch - pos) * lg)
                kdec = jnp.exp(pos * lg)
            cdec = jnp.exp(ch * lg)

            def step(state, qi, ki, vi):
                inner = _dot_nt(qi.astype(BF16), ki.astype(BF16)) * dmat
                o = (_dot(inner.astype(BF16), vi)
                     + _dot((qi * qdec).astype(BF16), state.astype(BF16)))
                state = state * cdec + _dot_tn((ki * kdec).astype(BF16), vi)
                return state, o

            state = jnp.zeros((RET_DK, RET_DK), F32)
            order = range(n_ctx_ch) if dirn == 0 else range(n_ctx_ch - 1, -1, -1)
            for n in order:
                rs = slice(n * ch, (n + 1) * ch)
                state, o = step(state, qc_ref[rs, hs].astype(F32), kc_ref[rs, hs].astype(F32) * kscale,
                                vc_ref[rs, hs])
                if need_ctx:
                    if dirn == 0:
                        ocacc[rs, hs] = o
                    else:
                        ocacc[rs, hs] += o

            def lat_body(i, state):
                n = i if dirn == 0 else n_lat_ch - 1 - i
                rs = pl.multiple_of(n * ch, ch)
                state, o = step(state, qr[pl.ds(rs, ch), hs], kr[pl.ds(rs, ch), hs], v_ref[pl.ds(rs, ch), hs])
                if dirn == 0:
                    oacc[pl.ds(rs, ch), hs] = o
                else:
                    oacc[pl.ds(rs, ch), hs] += o
                return state

            lax.fori_loop(0, n_lat_ch, lat_body, state)

    def finish(o, g):
        outs = []
        for h in range(RET_HEADS):
            oh = o[:, h * RET_DK:(h + 1) * RET_DK]
            mu = jnp.mean(oh, axis=-1, keepdims=True)
            var = jnp.mean(jnp.square(oh - mu), axis=-1, keepdims=True)
            outs.append((oh - mu) * lax.rsqrt(var + EPS))
        on = jnp.concatenate(outs, axis=-1)
        g = g.astype(F32)
        return (g * _sigmoid(g) * on).astype(BF16)

    def fin_body(i, carry):
        rs = pl.multiple_of(i * ch, ch)
        y_ref[pl.ds(rs, ch), :] = finish(oacc[pl.ds(rs, ch), :], g_ref[pl.ds(rs, ch), :])
        return carry

    lax.fori_loop(0, n_lat_ch, fin_body, 0)
    if need_ctx:
        for n in range(n_ctx_ch):
            rs = slice(n * ch, (n + 1) * ch)
            yc_ref[rs, :] = finish(ocacc[rs, :], gc_ref[rs, :])


def _rope_tables(seq, head_dim, width, dist):
    half = head_dim // 2
    nf = half // 2
    assert nf == dist
    inv = 1.0 / (ROPE_THETA ** (np.arange(nf, dtype=np.float32) / nf))
    t = np.arange(seq)
    rows = (t // GRID_W).astype(np.float32)[:, None] * inv[None, :]
    cols = (t % GRID_W).astype(np.float32)[:, None] * inv[None, :]
    cos = np.concatenate([np.cos(rows), np.cos(rows), np.cos(cols), np.cos(cols)], axis=-1)
    sin = np.concatenate([-np.sin(rows), np.sin(rows), -np.sin(cols), np.sin(cols)], axis=-1)
    reps = width // head_dim
    return (jnp.asarray(np.tile(cos, (1, reps)), F32), jnp.asarray(np.tile(sin, (1, reps)), F32))


def _ret_mixer(p_all, log_decay, batch, seq, n_ctx, need_ctx):
    n_lat = batch * seq
    cb = n_lat // n_ctx
    w = 256
    log_gamma = jnp.log1p(-jnp.exp(log_decay.astype(F32)))
    cos, sin = _rope_tables(seq, RET_DK, w, 16)
    lat = lambda c: pl.BlockSpec((seq, w), lambda b, c=c: (b, c))
    ctx = lambda c: pl.BlockSpec((n_ctx, w), lambda b, c=c: (cb + b, c))
    whole = pl.BlockSpec((seq, w), lambda b: (0, 0))
    out_shape = [jax.ShapeDtypeStruct((n_lat, w), BF16)]
    out_specs = [pl.BlockSpec((seq, w), lambda b: (b, 0))]
    if need_ctx:
        out_shape.append(jax.ShapeDtypeStruct((batch * n_ctx, w), BF16))
        out_specs.append(pl.BlockSpec((n_ctx, w), lambda b: (b, 0)))
    return pl.pallas_call(
        functools.partial(_ret_kernel, need_ctx, seq, n_ctx),
        grid=(batch,),
        in_specs=[pl.BlockSpec(memory_space=pltpu.SMEM),
                  lat(RET_Q), lat(RET_K), lat(RET_V), lat(RET_G),
                  ctx(RET_Q), ctx(RET_K), ctx(RET_V), ctx(RET_G), whole, whole],
        out_specs=out_specs,
        out_shape=out_shape,
        scratch_shapes=[pltpu.VMEM((seq, w), F32), pltpu.VMEM((seq, w), F32),
                        pltpu.VMEM((seq, w), F32), pltpu.VMEM((n_ctx, w), F32)],
        compiler_params=_params("parallel"),
        name="retention_mixer",
    )(log_gamma, p_all, p_all, p_all, p_all, p_all, p_all, p_all, p_all, cos, sin)


def _head_inv_rms(x, h, dim):
    xs = x[:, h * dim:(h + 1) * dim]
    return lax.rsqrt(jnp.mean(xs * xs, axis=-1, keepdims=True) + EPS)


def _gqa_kernel(n_lat_k, n_ctx, *refs):
    if n_lat_k:
        (q_ref, k_ref, v_ref, kc_ref, vc_ref, qg_ref, kg_ref, cq_ref, sq_ref, ck_ref, sk_ref,
         y_ref, kp, vp) = refs
    else:
        q_ref, kc_ref, vc_ref, qg_ref, kg_ref, y_ref, kp, vp = refs
    dim = GQA_DIM
    scale = dim ** -0.5
    group = GQA_HEADS // GQA_KV_HEADS
    nk = n_lat_k + n_ctx

    @pl.when(pl.program_id(1) == 0)
    def _():
        kg = kg_ref[...]
        kcf = kc_ref[...].astype(F32)
        for g in range(GQA_KV_HEADS):
            gs = slice(g * dim, (g + 1) * dim)
            kp[g, n_lat_k:nk, :] = ((kcf * kg)[:, gs] * _head_inv_rms(kcf, g, dim)).astype(BF16)
            vp[g, n_lat_k:nk, :] = vc_ref[:, gs]
        if n_lat_k:
            ck = 512
            for c in range(n_lat_k // ck):
                rs = slice(c * ck, (c + 1) * ck)
                kf = k_ref[rs, :].astype(F32)
                kro = _rope(kf * kg, ck_ref[rs, :], sk_ref[rs, :], 16)
                for g in range(GQA_KV_HEADS):
                    gs = slice(g * dim, (g + 1) * dim)
                    kp[g, rs, :] = (kro[:, gs] * _head_inv_rms(kf, g, dim)).astype(BF16)
                    vp[g, rs, :] = v_ref[rs, gs]

    qf = q_ref[...].astype(F32)
    qro = qf * qg_ref[...]
    if n_lat_k:
        qro = _rope(qro, cq_ref[...], sq_ref[...], 16)
    for hq in range(GQA_HEADS):
        g = hq // group
        hs = slice(hq * dim, (hq + 1) * dim)
        qh = (qro[:, hs] * _head_inv_rms(qf, hq, dim)).astype(BF16)
        s = _dot_nt(qh, kp[g]) * scale
        m = jnp.max(s, axis=-1, keepdims=True)
        p = jnp.exp(s - m)
        l = jnp.sum(p, axis=-1, keepdims=True)
        y_ref[:, hs] = (_dot(p.astype(BF16), vp[g]) / l).astype(BF16)


def _gqa_mixer(p_all, q_gain, k_gain, batch, seq, n_ctx, need_ctx):
    n_lat = batch * seq
    cb = n_lat // n_ctx
    dim = GQA_DIM
    qg = jnp.tile(q_gain.astype(F32), GQA_HEADS).reshape(1, GQA_HEADS * dim)
    kg = jnp.tile(k_gain.astype(F32), GQA_KV_HEADS).reshape(1, GQA_KV_HEADS * dim)
    cq, sq = _rope_tables(seq, dim, GQA_HEADS * dim, 16)
    ck, sk = _rope_tables(seq, dim, GQA_KV_HEADS * dim, 16)
    tq = 512
    nqb = seq // tq
    kvw = GQA_KV_HEADS * dim
    const = lambda shape: pl.BlockSpec(shape, lambda b, i: (0,) * len(shape))
    scratch = lambda nk: [pltpu.VMEM((GQA_KV_HEADS, nk, dim), BF16), pltpu.VMEM((GQA_KV_HEADS, nk, dim), BF16)]
    y_lat = pl.pallas_call(
        functools.partial(_gqa_kernel, seq, n_ctx),
        grid=(batch, nqb),
        in_specs=[pl.BlockSpec((tq, 256), lambda b, i: (b * nqb + i, GQA_Q)),
                  pl.BlockSpec((seq, kvw), lambda b, i: (b, 16)),
                  pl.BlockSpec((seq, kvw), lambda b, i: (b, 17)),
                  pl.BlockSpec((n_ctx, kvw), lambda b, i: (cb + b, 16)),
                  pl.BlockSpec((n_ctx, kvw), lambda b, i: (cb + b, 17)),
                  const((1, 256)), const((1, kvw)),
                  pl.BlockSpec((tq, 256), lambda b, i: (i, 0)),
                  pl.BlockSpec((tq, 256), lambda b, i: (i, 0)),
                  const((seq, kvw)), const((seq, kvw))],
        out_specs=pl.BlockSpec((tq, 256), lambda b, i: (b * nqb + i, 0)),
        out_shape=jax.ShapeDtypeStruct((n_lat, 256), BF16),
        scratch_shapes=scratch(seq + n_ctx),
        compiler_params=_params("parallel", "arbitrary"),
        name="gqa_mixer",
    )(p_all, p_all, p_all, p_all, p_all, qg, kg, cq, sq, ck, sk)
    if not need_ctx:
        return y_lat, None
    y_ctx = pl.pallas_call(
        functools.partial(_gqa_kernel, 0, n_ctx),
        grid=(batch, 1),
        in_specs=[pl.BlockSpec((n_ctx, 256), lambda b, i: (cb + b, GQA_Q)),
                  pl.BlockSpec((n_ctx, kvw), lambda b, i: (cb + b, 16)),
                  pl.BlockSpec((n_ctx, kvw), lambda b, i: (cb + b, 17)),
                  const((1, 256)), const((1, kvw))],
        out_specs=pl.BlockSpec((n_ctx, 256), lambda b, i: (b, 0)),
        out_shape=jax.ShapeDtypeStruct((batch * n_ctx, 256), BF16),
        scratch_shapes=scratch(n_ctx),
        compiler_params=_params("parallel", "arbitrary"),
        name="gqa_mixer_ctx",
    )(p_all, p_all, p_all, qg, kg)
    return y_lat, y_ctx


def _diff_kernel(n_lat_k, n_ctx, lam_init, *refs):
    if n_lat_k:
        (q_ref, k_ref, v_ref, kc_ref, vc_ref, lp_ref, sg_ref, cq_ref, sq_ref, ck_ref, sk_ref,
         y_ref, kp) = refs
    else:
        q_ref, kc_ref, vc_ref, lp_ref, sg_ref, y_ref, kp = refs
    dq = DIFF_QK_DIM
    dv = DIFF_V_DIM
    scale = dq ** -0.5
    nk = n_lat_k + n_ctx

    @pl.when(pl.program_id(1) == 0)
    def _():
        kp[n_lat_k:nk, :] = kc_ref[...]
        if n_lat_k:
            ck = 512
            for c in range(n_lat_k // ck):
                rs = slice(c * ck, (c + 1) * ck)
                kp[rs, :] = _rope(k_ref[rs, :].astype(F32), ck_ref[rs, :], sk_ref[rs, :], 8).astype(BF16)

    lp = lp_ref[...]
    lam = (jnp.exp(jnp.sum(lp[0:1, :] * lp[1:2, :], axis=-1, keepdims=True))
           - jnp.exp(jnp.sum(lp[2:3, :] * lp[3:4, :], axis=-1, keepdims=True)) + lam_init)

    qf = q_ref[...].astype(F32)
    if n_lat_k:
        qf = _rope(qf, cq_ref[...], sq_ref[...], 8)
    lane = lax.broadcasted_iota(jnp.int32, (1, 2 * dq), 1)
    first = lane < dq

    def softmax(s):
        m = jnp.max(s, axis=-1, keepdims=True)
        p = jnp.exp(s - m)
        return p / jnp.sum(p, axis=-1, keepdims=True)

    for h in range(DIFF_HEADS):
        hs = slice(h * 2 * dq, (h + 1) * 2 * dq)
        qh = qf[:, hs]
        kh = kp[:, hs]
        p1 = softmax(_dot_nt(jnp.where(first, qh, 0.0).astype(BF16), kh) * scale)
        p2 = softmax(_dot_nt(jnp.where(first, 0.0, qh).astype(BF16), kh) * scale)
        a = (p1 - lam * p2).astype(BF16)
        vs = slice(h * dv, (h + 1) * dv)
        if n_lat_k:
            o = _dot(a[:, :n_lat_k], v_ref[:, vs]) + _dot(a[:, n_lat_k:], vc_ref[:, vs])
        else:
            o = _dot(a, vc_ref[:, vs])
        on = o * lax.rsqrt(jnp.mean(o * o, axis=-1, keepdims=True) + EPS) * sg_ref[...]
        y_ref[:, vs] = (on * (1.0 - lam_init)).astype(BF16)


def _diff_mixer(p_all, lam_params, subln, layer_idx, batch, seq, n_ctx, need_ctx):
    n_lat = batch * seq
    cb = n_lat // n_ctx
    lam_init = 0.8 - 0.6 * math.exp(-0.3 * layer_idx)
    lp = jnp.zeros((8, 128), F32).at[:4, :DIFF_QK_DIM].set(lam_params.astype(F32))
    sg = subln.astype(F32).reshape(1, DIFF_V_DIM)
    cq, sq = _rope_tables(seq, DIFF_QK_DIM, 256, 8)
    tq = 512
    nqb = seq // tq
    const = lambda shape: pl.BlockSpec(shape, lambda b, i: (0,) * len(shape))
    y_lat = pl.pallas_call(
        functools.partial(_diff_kernel, seq, n_ctx, lam_init),
        grid=(batch, nqb),
        in_specs=[pl.BlockSpec((tq, 256), lambda b, i: (b * nqb + i, DIFF_Q)),
                  pl.BlockSpec((seq, 256), lambda b, i: (b, DIFF_K)),
                  pl.BlockSpec((seq, 256), lambda b, i: (b, DIFF_V)),
                  pl.BlockSpec((n_ctx, 256), lambda b, i: (cb + b, DIFF_K)),
                  pl.BlockSpec((n_ctx, 256), lambda b, i: (cb + b, DIFF_V)),
                  const((8, 128)), const((1, DIFF_V_DIM)),
                  pl.BlockSpec((tq, 256), lambda b, i: (i, 0)),
                  pl.BlockSpec((tq, 256), lambda b, i: (i, 0)),
                  const((seq, 256)), const((seq, 256))],
        out_specs=pl.BlockSpec((tq, 256), lambda b, i: (b * nqb + i, 0)),
        out_shape=jax.ShapeDtypeStruct((n_lat, 256), BF16),
        scratch_shapes=[pltpu.VMEM((seq + n_ctx, 256), BF16)],
        compiler_params=_params("parallel", "arbitrary"),
        name="diff_mixer",
    )(p_all, p_all, p_all, p_all, p_all, lp, sg, cq, sq, cq, sq)
    if not need_ctx:
        return y_lat, None
    y_ctx = pl.pallas_call(
        functools.partial(_diff_kernel, 0, n_ctx, lam_init),
        grid=(batch, 1),
        in_specs=[pl.BlockSpec((n_ctx, 256), lambda b, i: (cb + b, DIFF_Q)),
                  pl.BlockSpec((n_ctx, 256), lambda b, i: (cb + b, DIFF_K)),
                  pl.BlockSpec((n_ctx, 256), lambda b, i: (cb + b, DIFF_V)),
                  const((8, 128)), const((1, DIFF_V_DIM))],
        out_specs=pl.BlockSpec((n_ctx, 256), lambda b, i: (b, 0)),
        out_shape=jax.ShapeDtypeStruct((batch * n_ctx, 256), BF16),
        scratch_shapes=[pltpu.VMEM((n_ctx, 256), BF16)],
        compiler_params=_params("parallel", "arbitrary"),
        name="diff_mixer_ctx",
    )(p_all, p_all, p_all, lp, sg)
    return y_lat, y_ctx


def _merge_kernel(x_ref, g0, g1, g2, g3, ya, yb, yc, yd, wb_ref, wo_ref, mod_ref, gn_ref, wr_ref,
                  xo_ref, m_ref, lg_ref):
    acc = None
    for n, (g_ref, y_ref) in enumerate(((g0, ya), (g1, yb), (g2, yc), (g3, yd))):
        term = g_ref[...].astype(F32) * _dot(y_ref[...], wb_ref[n])
        acc = term if acc is None else acc + term
    y = _dot(acc.astype(BF16), wo_ref[...])
    x = x_ref[...] + mod_ref[2:3, :] * y
    xo_ref[...] = x
    xn = x * lax.rsqrt(jnp.mean(x * x, axis=-1, keepdims=True) + EPS) * gn_ref[...]
    m = xn * (1.0 + mod_ref[4:5, :]) + mod_ref[3:4, :]
    m_ref[...] = m
    m_hi = m.astype(BF16)
    m_lo = (m - m_hi.astype(F32)).astype(BF16)
    w = wr_ref[...]
    w_hi = w.astype(BF16)
    w_lo = (w - w_hi.astype(F32)).astype(BF16)
    lg_ref[...] = _dot_nt(w_hi, m_hi) + (_dot_nt(w_hi, m_lo) + _dot_nt(w_lo, m_hi))


def _merge(x_all, p_all, ys, wb_bf, wo_bf, modtab, gn, wr_t, tm, n_rows, n_lat, seq, batch):
    d = D_MODEL
    midx = _mod_index(tm, n_lat, seq, batch)
    gate = lambda n: pl.BlockSpec((tm, d), lambda i, n=n: (i, MIX_COLS // d + n))
    yspec = pl.BlockSpec((tm, BRANCH_W), lambda i: (i, 0))
    return pl.pallas_call(
        _merge_kernel,
        grid=(n_rows // tm,),
        in_specs=[pl.BlockSpec((tm, d), lambda i: (i, 0)),
                  gate(0), gate(1), gate(2), gate(3), yspec, yspec, yspec, yspec,
                  pl.BlockSpec((N_BRANCH, BRANCH_W, d), lambda i: (0, 0, 0)),
                  pl.BlockSpec((d, d), lambda i: (0, 0)),
                  pl.BlockSpec((None, 8, d), lambda i: (midx(i), 0, 0)),
                  pl.BlockSpec((1, d), lambda i: (0, 0)),
                  pl.BlockSpec((N_EXPERTS, d), lambda i: (0, 0))],
        out_specs=[pl.BlockSpec((tm, d), lambda i: (i, 0)),
                   pl.BlockSpec((tm, d), lambda i: (i, 0)),
                   pl.BlockSpec((N_EXPERTS, tm), lambda i: (0, i))],
        out_shape=[jax.ShapeDtypeStruct((n_rows, d), F32),
                   jax.ShapeDtypeStruct((n_rows, d), F32),
                   jax.ShapeDtypeStruct((N_EXPERTS, n_rows), F32)],
        compiler_params=_params("parallel"),
        name="merge_norm_route",
    )(x_all, p_all, p_all, p_all, p_all, *ys, wb_bf, wo_bf, modtab, gn.reshape(1, d), wr_t)


def _route_kernel(lg_ref, b_ref, idx_ref, w_ref):
    s = _sigmoid(lg_ref[...])
    sel = s + b_ref[...]
    row = lambda a, e: a[e:e + 1, :]
    gsz = EXPERTS_PER_GROUP
    g_idx = None
    best = None
    for g in range(N_GROUPS):
        v = [row(sel, g * gsz + i) for i in range(gsz)]
        score = None
        for i in range(gsz):
            for j in range(i + 1, gsz):
                pair = v[i] + v[j]
                score = pair if score is None else jnp.maximum(score, pair)
        if g == 0:
            best, g_idx = score, jnp.zeros(score.shape, jnp.int32)
        else:
            better = score > best
            best = jnp.where(better, score, best)
            g_idx = jnp.where(better, g, g_idx)

    def in_group(a, i):
        out = row(a, i)
        for g in range(1, N_GROUPS):
            out = jnp.where(g_idx == g, row(a, g * gsz + i), out)
        return out

    v = [in_group(sel, i) for i in range(gsz)]
    sv = [in_group(s, i) for i in range(gsz)]

    def arg_first_max(vals):
        bv, bi = vals[0], jnp.zeros(vals[0].shape, jnp.int32)
        for i in range(1, gsz):
            better = vals[i] > bv
            bv = jnp.where(better, vals[i], bv)
            bi = jnp.where(better, i, bi)
        return bi

    i1 = arg_first_max(v)
    i2 = arg_first_max([jnp.where(i1 == i, -jnp.inf, v[i]) for i in range(gsz)])

    def pick(vals, idx):
        out = vals[0]
        for i in range(1, gsz):
            out = jnp.where(idx == i, vals[i], out)
        return out

    w1 = pick(sv, i1)
    w2 = pick(sv, i2)
    tot = w1 + w2
    idx_ref[0:1, :] = g_idx * gsz + i1
    idx_ref[1:2, :] = g_idx * gsz + i2
    w_ref[0:1, :] = w1 / tot
    w_ref[1:2, :] = w2 / tot


def _route(logits_t, router_bias):
    e, n = logits_t.shape
    tn = math.gcd(n, 2048)
    return pl.pallas_call(
        _route_kernel,
        grid=(n // tn,),
        in_specs=[pl.BlockSpec((e, tn), lambda i: (0, i)),
                  pl.BlockSpec((e, 1), lambda i: (0, 0))],
        out_specs=[pl.BlockSpec((TOP_K, tn), lambda i: (0, i)),
                   pl.BlockSpec((TOP_K, tn), lambda i: (0, i))],
        out_shape=[jax.ShapeDtypeStruct((TOP_K, n), jnp.int32),
                   jax.ShapeDtypeStruct((TOP_K, n), F32)],
        compiler_params=_params("parallel"),
        name="route_top2",
    )(logits_t, router_bias.astype(F32).reshape(e, 1))


def _gather_rows(idx_ref, slot, src_hbm, dst_ref, sem, n):
    def body(r, carry):
        t = idx_ref[slot, r]
        pltpu.make_async_copy(src_hbm.at[pl.ds(t, 1)], dst_ref.at[slot, pl.ds(r, 1)], sem.at[slot]).start()
        return carry
    lax.fori_loop(0, n, body, 0, unroll=8)


def _wait_rows(src_hbm, dst_ref, sem, slot, n):
    pltpu.make_async_copy(src_hbm.at[pl.ds(0, n)], dst_ref.at[slot], sem.at[slot]).wait()


def _prefetch_block(idx_hbm, src_hbm, idx_s, buf, sem_i, sem_x, blk_i, slot, n):
    cp = pltpu.make_async_copy(idx_hbm.at[blk_i], idx_s.at[slot], sem_i.at[slot])
    cp.start()
    cp.wait()
    _gather_rows(idx_s, slot, src_hbm, buf, sem_x, n)


def _expert_kernel(be_ref, tok_hbm, m_hbm, wg_ref, wu_ref, wd_ref, y_ref, idx_s, xbuf, sem_i, sem_x):
    i = pl.program_id(0)
    n = pl.num_programs(0)
    slot = i % 2
    blk = xbuf.shape[1]
    fetch = functools.partial(_prefetch_block, tok_hbm, m_hbm, idx_s, xbuf, sem_i, sem_x)

    @pl.when(i == 0)
    def _():
        fetch(0, 0, blk)

    @pl.when(i + 1 < n)
    def _():
        fetch(i + 1, 1 - slot, blk)

    _wait_rows(m_hbm, xbuf, sem_x, slot, blk)
    x = xbuf[slot].astype(BF16)
    a = _dot(x, wg_ref[...])
    h = a * _sigmoid(a) * _dot(x, wu_ref[...])
    y_ref[...] = _dot(h.astype(BF16), wd_ref[...])


def _experts(m_all, slot_tok, blk_e, wg_bf, wu_bf, wd_bf):
    n_blocks, blk = slot_tok.shape
    d = D_MODEL
    grid_spec = pltpu.PrefetchScalarGridSpec(
        num_scalar_prefetch=1,
        grid=(n_blocks,),
        in_specs=[pl.BlockSpec(memory_space=pl.ANY),
                  pl.BlockSpec(memory_space=pl.ANY),
                  pl.BlockSpec((None, d, D_EXPERT), lambda i, be: (be[i], 0, 0)),
                  pl.BlockSpec((None, d, D_EXPERT), lambda i, be: (be[i], 0, 0)),
                  pl.BlockSpec((None, D_EXPERT, d), lambda i, be: (be[i], 0, 0))],
        out_specs=pl.BlockSpec((blk, d), lambda i, be: (i, 0)),
        scratch_shapes=[pltpu.SMEM((2, blk), jnp.int32), pltpu.VMEM((2, blk, d), F32),
                        pltpu.SemaphoreType.DMA((2,)), pltpu.SemaphoreType.DMA((2,))])
    return pl.pallas_call(
        _expert_kernel,
        grid_spec=grid_spec,
        out_shape=jax.ShapeDtypeStruct((n_blocks * blk, d), F32),
        compiler_params=_params("arbitrary"),
        name="expert_ffn",
    )(blk_e, slot_tok, m_all, wg_bf, wu_bf, wd_bf)


def _combine_kernel(final, pos_hbm, y_hbm, x_ref, w_ref, mod_ref, gf_ref, o_ref, idx_s, ybuf, sem_i, sem_y):
    i = pl.program_id(0)
    n = pl.num_programs(0)
    slot = i % 2
    tm = x_ref.shape[0]
    fetch = functools.partial(_prefetch_block, pos_hbm, y_hbm, idx_s, ybuf, sem_i, sem_y)

    @pl.when(i == 0)
    def _():
        fetch(0, 0, 2 * tm)

    @pl.when(i + 1 < n)
    def _():
        fetch(i + 1, 1 - slot, 2 * tm)

    _wait_rows(y_hbm, ybuf, sem_y, slot, 2 * tm)
    w = w_ref[...]
    f = w[:, 0:1] * ybuf[slot, 0:tm, :] + w[:, 1:2] * ybuf[slot, tm:2 * tm, :]
    x = x_ref[...] + mod_ref[5:6, :] * f
    if final:
        x = x * lax.rsqrt(jnp.mean(x * x, axis=-1, keepdims=True) + EPS) * gf_ref[...]
    o_ref[...] = x


def _combine(x_rows, y_slots, pos, w_tok, modtab, g_final, final, tm, n_lat, seq, batch):
    n_rows, d = x_rows.shape
    midx = _mod_index(tm, n_lat, seq, batch)
    return pl.pallas_call(
        functools.partial(_combine_kernel, final),
        grid=(n_rows // tm,),
        in_specs=[pl.BlockSpec(memory_space=pl.ANY),
                  pl.BlockSpec(memory_space=pl.ANY),
                  pl.BlockSpec((tm, d), lambda i: (i, 0)),
                  pl.BlockSpec((tm, TOP_K), lambda i: (i, 0)),
                  pl.BlockSpec((None, 8, d), lambda i: (midx(i), 0, 0)),
                  pl.BlockSpec((1, d), lambda i: (0, 0))],
        out_specs=pl.BlockSpec((tm, d), lambda i: (i, 0)),
        out_shape=jax.ShapeDtypeStruct((n_rows, d), F32),
        scratch_shapes=[pltpu.SMEM((2, 2 * tm), jnp.int32), pltpu.VMEM((2, 2 * tm, d), F32),
                        pltpu.SemaphoreType.DMA((2,)), pltpu.SemaphoreType.DMA((2,))],
        compiler_params=_params("arbitrary"),
        name="moe_combine",
    )(pos, y_slots, x_rows, w_tok, modtab, g_final.astype(F32).reshape(1, d))


def _dispatch_plan(idx2, blk, tm):
    n = idx2.shape[1]
    a = TOP_K * n
    flat_e = idx2.T.reshape(a)
    onehot = (flat_e[:, None] == jnp.arange(N_EXPERTS, dtype=jnp.int32)[None, :]).astype(jnp.int32)
    csum = jnp.cumsum(onehot, axis=0)
    rank = jnp.sum(onehot * csum, axis=1) - 1
    counts = csum[-1]
    padded = (counts + blk - 1) // blk * blk
    pad_end = jnp.cumsum(padded)
    pad_start = pad_end - padded
    dest = (pad_start[flat_e] + rank).astype(jnp.int32)
    n_blocks = a // blk + N_EXPERTS
    slot_tok = jnp.zeros((n_blocks * blk,), jnp.int32).at[dest].set(jnp.arange(a, dtype=jnp.int32) // TOP_K)
    blk_e = jnp.minimum(jnp.searchsorted(pad_end, jnp.arange(n_blocks, dtype=jnp.int32) * blk, side='right'),
                        N_EXPERTS - 1).astype(jnp.int32)
    pos = dest.reshape(n // tm, tm, TOP_K).transpose(0, 2, 1).reshape(n // tm, TOP_K * tm)
    return slot_tok.reshape(n_blocks, blk), blk_e, pos


def kernel(x, c, ctx, c_ctx, w_mod, b_mod, g_norm1, g_norm2, w_in, na_rpb, ret_log_decay, gqa_q_gain, gqa_k_gain,
           diff_lambda, diff_subln, w_branch, w_out, w_router, router_bias, w_gate_e, w_up_e, w_down_e, g_final):
    batch, seq, d = x.shape
    n_ctx = ctx.shape[1]
    n_lat = batch * seq
    n_ctx_tot = batch * n_ctx
    n_tot = n_lat + n_ctx_tot
    tm = 1024 if n_ctx_tot % 1024 == 0 else 512

    x_all = jnp.concatenate([x.reshape(n_lat, d), ctx.reshape(n_ctx_tot, d)], axis=0)

    nrow = -(-(batch + 1) // 8) * 8
    cc = jnp.zeros((nrow, d), F32).at[:batch].set(c).at[batch].set(c_ctx)
    mod = _modulation(cc, w_mod, b_mod)
    modtab = jnp.pad(mod.reshape(DEPTH, nrow, 6, d), ((0, 0), (0, 0), (0, 2), (0, 0)))

    wr_t = w_router.astype(F32).T
    out = None
    for layer in range(DEPTH):
        need_ctx = layer < DEPTH - 1
        n_rows = n_tot if need_ctx else n_lat
        p_all = _project(x_all, g_norm1[layer], modtab[layer], w_in[layer].astype(BF16), tm, n_lat, seq, batch)
        ya = _na_mixer(p_all, na_rpb[layer], batch, seq, n_ctx, need_ctx)
        yb = _ret_mixer(p_all, ret_log_decay[layer], batch, seq, n_ctx, need_ctx)
        yc = _gqa_mixer(p_all, gqa_q_gain[layer], gqa_k_gain[layer], batch, seq, n_ctx, need_ctx)
        yd = _diff_mixer(p_all, diff_lambda[layer], diff_subln[layer], layer, batch, seq, n_ctx, need_ctx)
        if need_ctx:
            ys = [jnp.concatenate([yl, yc_], axis=0) for yl, yc_ in (ya, yb, yc, yd)]
        else:
            ys = [ya[0], yb[0], yc[0], yd[0]]
        x_mid, m_all, logits_t = _merge(x_all, p_all, ys, w_branch[layer].astype(BF16), w_out[layer].astype(BF16),
                                        modtab[layer], g_norm2[layer], wr_t, tm, n_rows, n_lat, seq, batch)
        idx2, w2 = _route(logits_t, router_bias)
        slot_tok, blk_e, pos = _dispatch_plan(idx2, MOE_BLK, CMB_TM)
        y_slots = _experts(m_all, slot_tok, blk_e, w_gate_e[layer].astype(BF16), w_up_e[layer].astype(BF16),
                           w_down_e[layer].astype(BF16))
        out = _combine(x_mid, y_slots, pos, w2.T, modtab[layer], g_final, not need_ctx, CMB_TM, n_lat, seq, batch)
        x_all = out
    return out.reshape(batch, seq, d)
```

```python
import functools
import math

import numpy as np
import jax
import jax.numpy as jnp
from jax import lax
from jax.experimental import pallas as pl
from jax.experimental.pallas import tpu as pltpu

F32 = jnp.float32
BF16 = jnp.bfloat16

D_MODEL = 1024
DEPTH = 2
GRID_W = 64
NA_HEADS = 4
NA_DIM = 64
NA_WIN_H = 8
NA_WIN_W = 16
RET_HEADS = 4
RET_DK = 64
RET_CHUNK = 128
GQA_HEADS = 4
GQA_KV_HEADS = 2
GQA_DIM = 64
DIFF_HEADS = 4
DIFF_QK_DIM = 32
DIFF_V_DIM = 64
N_BRANCH = 4
BRANCH_W = 256
ROPE_THETA = 10000.0
EPS = 1e-6
NEG_INF = -1e30
N_EXPERTS = 16
N_GROUPS = 4
EXPERTS_PER_GROUP = 4
TOP_K = 2
D_EXPERT = 512

MIX_COLS = 3072
IN_COLS = MIX_COLS + N_BRANCH * D_MODEL
NA_Q, NA_K, NA_V = 0, 1, 2
RET_Q, RET_K, RET_V, RET_G = 3, 4, 5, 6
GQA_Q = 7
DIFF_Q, DIFF_K, DIFF_V = 9, 10, 11

LOG2E = math.log2(math.e)
NA_GROUP_ROWS = 4
NA_BAND_ROWS = 12

VMEM_LIMIT = 56 * 1024 * 1024
MOE_BLK = 512
CMB_TM = 256


def _dot(a, b):
    return jnp.dot(a, b, preferred_element_type=F32)


def _dot_nt(a, b):
    return lax.dot_general(a, b, (((1,), (1,)), ((), ())), preferred_element_type=F32)


def _dot_tn(a, b):
    return lax.dot_general(a, b, (((0,), (0,)), ((), ())), preferred_element_type=F32)


def _sigmoid(x):
    return 1.0 / (1.0 + jnp.exp(-x))


def _params(*sem):
    return pltpu.CompilerParams(dimension_semantics=sem, vmem_limit_bytes=VMEM_LIMIT)


def _swap_halves(x, dist):
    pieces = []
    for j in range(x.shape[-1] // 128):
        xs = x[:, j * 128:(j + 1) * 128]
        lane = lax.broadcasted_iota(jnp.int32, xs.shape, 1)
        up = pltpu.roll(xs, 128 - dist, 1)
        dn = pltpu.roll(xs, dist, 1)
        pieces.append(jnp.where((lane % (2 * dist)) < dist, up, dn))
    return pieces[0] if len(pieces) == 1 else jnp.concatenate(pieces, axis=-1)


def _rope(x, cos, sin, dist):
    return x * cos + _swap_halves(x, dist) * sin


def _mod_kernel(c_ref, w_ref, b_ref, o_ref):
    c = c_ref[...]
    a = (c * _sigmoid(c)).astype(BF16)
    o_ref[...] = _dot(a, w_ref[...].astype(BF16)) + b_ref[...]


def _modulation(cc, w_mod, b_mod):
    nrow = cc.shape[0]
    depth, d, n6 = w_mod.shape
    tn = 1536
    return pl.pallas_call(
        _mod_kernel,
        grid=(depth, n6 // tn),
        in_specs=[pl.BlockSpec((nrow, d), lambda l, j: (0, 0)),
                  pl.BlockSpec((None, d, tn), lambda l, j: (l, 0, j)),
                  pl.BlockSpec((None, 1, tn), lambda l, j: (l, 0, j))],
        out_specs=pl.BlockSpec((None, nrow, tn), lambda l, j: (l, 0, j)),
        out_shape=jax.ShapeDtypeStruct((depth, nrow, n6), F32),
        compiler_params=_params("parallel", "arbitrary"),
        name="modulation",
    )(cc, w_mod, b_mod.reshape(depth, 1, n6))


def _mix_col_scale():
    cs = np.ones((1, MIX_COLS), np.float32)
    cs[0, NA_Q * 256:(NA_Q + 1) * 256] = NA_DIM ** -0.5 * LOG2E
    cs[0, DIFF_Q * 256:(DIFF_Q + 1) * 256] = DIFF_QK_DIM ** -0.5 * LOG2E
    cs[0, RET_K * 256:(RET_K + 1) * 256] = RET_DK ** -0.5
    return jnp.asarray(cs)


def _proj_kernel(x_ref, g_ref, mod_ref, w_ref, cs_ref, o_ref, h_ref):
    j = pl.program_id(1)

    @pl.when(j == 0)
    def _():
        x = x_ref[...]
        y = x * lax.rsqrt(jnp.mean(x * x, axis=-1, keepdims=True) + EPS) * g_ref[...]
        h_ref[...] = (y * (1.0 + mod_ref[1:2, :]) + mod_ref[0:1, :]).astype(BF16)

    acc = _dot(h_ref[...], w_ref[...])

    @pl.when(j < MIX_COLS // D_MODEL)
    def _():
        o_ref[...] = (acc * cs_ref[...]).astype(BF16)

    @pl.when(j >= MIX_COLS // D_MODEL)
    def _():
        o_ref[...] = _sigmoid(acc).astype(BF16)


def _mod_index(tm, n_lat, seq, batch):
    nlat_blk = n_lat // tm
    bpb = seq // tm

    def index(i):
        return jnp.where(i < nlat_blk, i // bpb, batch)
    return index


def _project(x_all, g, modtab, w_bf, tm, n_lat, seq, batch):
    n_tot, d = x_all.shape
    ncol = w_bf.shape[1]
    tn = D_MODEL
    n_mix = MIX_COLS // tn
    midx = _mod_index(tm, n_lat, seq, batch)
    return pl.pallas_call(
        _proj_kernel,
        grid=(n_tot // tm, ncol // tn),
        in_specs=[pl.BlockSpec((tm, d), lambda i, j: (i, 0)),
                  pl.BlockSpec((1, d), lambda i, j: (0, 0)),
                  pl.BlockSpec((None, 8, d), lambda i, j: (midx(i), 0, 0)),
                  pl.BlockSpec((d, tn), lambda i, j: (0, j)),
                  pl.BlockSpec((1, tn), lambda i, j: (0, jnp.minimum(j, n_mix - 1)))],
        out_specs=pl.BlockSpec((tm, tn), lambda i, j: (i, j)),
        out_shape=jax.ShapeDtypeStruct((n_tot, ncol), BF16),
        scratch_shapes=[pltpu.VMEM((tm, d), BF16)],
        compiler_params=_params("parallel", "arbitrary"),
        name="norm_project",
    )(x_all, g.reshape(1, d), modtab, w_bf, _mix_col_scale())


def _na_kernel(need_ctx, rows, q_ref, k_ref, v_ref, qc_ref, kc_ref, vc_ref, tb_ref, y_ref, *rest):
    gq = NA_GROUP_ROWS * GRID_W
    band = NA_BAND_ROWS * GRID_W
    n_groups = rows // NA_GROUP_ROWS

    def group_body(g, carry):
        u = jnp.clip(g * NA_GROUP_ROWS - NA_WIN_H // 2, 0, rows - NA_BAND_ROWS)
        typ = jnp.where(g == 0, 0, jnp.where(g == n_groups - 1, 2, 1))
        qs = pl.multiple_of(g * gq, gq)
        ks = pl.multiple_of(u * GRID_W, GRID_W)
        for h in range(NA_HEADS):
            hs = slice(h * NA_DIM, (h + 1) * NA_DIM)
            q = q_ref[pl.ds(qs, gq), hs]
            vc = vc_ref[:, hs]
            s_w = _dot_nt(q, k_ref[pl.ds(ks, band), hs]) + tb_ref[h, typ]
            s_c = _dot_nt(q, kc_ref[:, hs])
            m = jnp.maximum(jnp.max(s_w, axis=-1, keepdims=True), jnp.max(s_c, axis=-1, keepdims=True))
            p_w = jnp.exp2(s_w - m)
            p_c = jnp.exp2(s_c - m)
            l = jnp.sum(p_w, axis=-1, keepdims=True) + jnp.sum(p_c, axis=-1, keepdims=True)
            o = (_dot(p_w.astype(BF16), v_ref[pl.ds(ks, band), hs]) + _dot(p_c.astype(BF16), vc)) / l
            y_ref[pl.ds(qs, gq), hs] = o.astype(BF16)
        return carry

    lax.fori_loop(0, n_groups, group_body, 0)

    if need_ctx:
        yc_ref = rest[0]
        for h in range(NA_HEADS):
            hs = slice(h * NA_DIM, (h + 1) * NA_DIM)
            s = _dot_nt(qc_ref[:, hs], kc_ref[:, hs])
            p = jnp.exp2(s - jnp.max(s, axis=-1, keepdims=True))
            l = jnp.sum(p, axis=-1, keepdims=True)
            yc_ref[:, hs] = (_dot(p.astype(BF16), vc_ref[:, hs]) / l).astype(BF16)


def _na_bias_table(rpb, rows):
    assert rows % NA_GROUP_ROWS == 0 and rows >= NA_BAND_ROWS + NA_GROUP_ROWS
    n_groups = rows // NA_GROUP_ROWS
    qc = np.arange(GRID_W)[:, None]
    kc = np.arange(GRID_W)[None, :]
    win_start = np.clip(qc - NA_WIN_W // 2, 0, GRID_W - NA_WIN_W)
    col_ok = (kc >= win_start) & (kc < win_start + NA_WIN_W)
    col_idx = np.clip(kc - qc, -(NA_WIN_W - 1), NA_WIN_W - 1) + NA_WIN_W - 1
    tables = []
    for g in (0, 1, n_groups - 1):
        u = int(np.clip(g * NA_GROUP_ROWS - NA_WIN_H // 2, 0, rows - NA_BAND_ROWS))
        r = g * NA_GROUP_ROWS + np.arange(NA_GROUP_ROWS)[:, None]
        kr = u + np.arange(NA_BAND_ROWS)[None, :]
        r0 = np.clip(r - NA_WIN_H // 2, 0, rows - NA_WIN_H)
        row_ok = (kr >= r0) & (kr < r0 + NA_WIN_H)
        row_idx = np.clip(kr - r + NA_WIN_H - 1, 0, 2 * NA_WIN_H - 2)
        t = rpb.astype(F32)[:, row_idx][:, :, :, col_idx]
        ok = row_ok[:, :, None, None] & col_ok[None, None]
        t = jnp.where(ok[None], t * LOG2E, NEG_INF)
        tables.append(t.transpose(0, 1, 3, 2, 4).reshape(NA_HEADS, NA_GROUP_ROWS * GRID_W, NA_BAND_ROWS * GRID_W))
    return jnp.stack(tables, axis=1)


def _na_mixer(p_all, rpb, batch, seq, n_ctx, need_ctx):
    n_lat = batch * seq
    rows = seq // GRID_W
    tb = _na_bias_table(rpb, rows)
    cb = n_lat // n_ctx
    w = 256
    lat = lambda c: pl.BlockSpec((seq, w), lambda b, c=c: (b, c))
    ctx = lambda c: pl.BlockSpec((n_ctx, w), lambda b, c=c: (cb + b, c))
    out_shape = [jax.ShapeDtypeStruct((n_lat, w), BF16)]
    out_specs = [pl.BlockSpec((seq, w), lambda b: (b, 0))]
    if need_ctx:
        out_shape.append(jax.ShapeDtypeStruct((batch * n_ctx, w), BF16))
        out_specs.append(pl.BlockSpec((n_ctx, w), lambda b: (b, 0)))
    return pl.pallas_call(
        functools.partial(_na_kernel, need_ctx, rows),
        grid=(batch,),
        in_specs=[lat(NA_Q), lat(NA_K), lat(NA_V), ctx(NA_Q), ctx(NA_K), ctx(NA_V),
                  pl.BlockSpec(tb.shape, lambda b: (0, 0, 0, 0))],
        out_specs=out_specs,
        out_shape=out_shape,
        compiler_params=_params("parallel"),
        name="na_mixer",
    )(p_all, p_all, p_all, p_all, p_all, p_all, tb)


def _ret_kernel(need_ctx, seq, n_ctx, lg_ref, q_ref, k_ref, v_ref, g_ref, qc_ref, kc_ref, vc_ref, gc_ref,
                cos_ref, sin_ref, y_ref, *rest):
    if need_ctx:
        yc_ref, qr, kr, o_f, o_b, oc_f, oc_b, dm, qd, kd = rest
    else:
        qr, kr, o_f, o_b, oc_f, oc_b, dm, qd, kd = rest
    ch = RET_CHUNK
    n_lat_ch = seq // ch
    n_ctx_ch = n_ctx // ch

    def prep(i, carry):
        rs = pl.multiple_of(i * ch, ch)
        c = cos_ref[pl.ds(rs, ch), :]
        s = sin_ref[pl.ds(rs, ch), :]
        qr[pl.ds(rs, ch), :] = _rope(q_ref[pl.ds(rs, ch), :].astype(F32), c, s, 16)
        kr[pl.ds(rs, ch), :] = _rope(k_ref[pl.ds(rs, ch), :].astype(F32), c, s, 16)
        return carry

    lax.fori_loop(0, n_lat_ch, prep, 0)

    pos = lax.broadcasted_iota(jnp.int32, (ch, RET_DK), 0).astype(F32)
    ri = lax.broadcasted_iota(jnp.int32, (ch, ch), 0)
    ci = lax.broadcasted_iota(jnp.int32, (ch, ch), 1)
    lag = (ri - ci).astype(F32)

    chains = [(h, dirn) for h in range(RET_HEADS) for dirn in range(2)]
    cdec = []
    for c, (h, dirn) in enumerate(chains):
        lg = lg_ref[dirn, h]
        if dirn == 0:
            keep = ri >= ci
            dm[c] = jnp.where(keep, jnp.exp(jnp.where(keep, lag, 0.0) * lg), 0.0)
            qd[c] = jnp.exp((pos + 1.0) * lg)
            kd[c] = jnp.exp((ch - 1.0 - pos) * lg)
        else:
            keep = ci >= ri
            dm[c] = jnp.where(keep, jnp.exp(jnp.where(keep, -lag, 0.0) * lg), 0.0)
            qd[c] = jnp.exp((ch - pos) * lg)
            kd[c] = jnp.exp(pos * lg)
        cdec.append(jnp.exp(ch * lg))

    def step(c, state, qi, ki, vi):
        inner = _dot_nt(qi.astype(BF16), ki.astype(BF16)) * dm[c]
        o = _dot(inner.astype(BF16), vi) + _dot((qi * qd[c]).astype(BF16), state.astype(BF16))
        state = state * cdec[c] + _dot_tn((ki * kd[c]).astype(BF16), vi)
        return state, o

    states = []
    for c, (h, dirn) in enumerate(chains):
        hs = slice(h * RET_DK, (h + 1) * RET_DK)
        state = jnp.zeros((RET_DK, RET_DK), F32)
        for n in (range(n_ctx_ch) if dirn == 0 else range(n_ctx_ch - 1, -1, -1)):
            rs = slice(n * ch, (n + 1) * ch)
            state, o = step(c, state, qc_ref[rs, hs].astype(F32), kc_ref[rs, hs].astype(F32), vc_ref[rs, hs])
            if need_ctx:
                (oc_f if dirn == 0 else oc_b)[rs, hs] = o
        states.append(state)

    def lat_body(i, states):
        new = []
        for c, (h, dirn) in enumerate(chains):
            hs = slice(h * RET_DK, (h + 1) * RET_DK)
            rs = pl.multiple_of((i if dirn == 0 else n_lat_ch - 1 - i) * ch, ch)
            state, o = step(c, states[c], qr[pl.ds(rs, ch), hs], kr[pl.ds(rs, ch), hs], v_ref[pl.ds(rs, ch), hs])
            (o_f if dirn == 0 else o_b)[pl.ds(rs, ch), hs] = o
            new.append(state)
        return tuple(new)

    lax.fori_loop(0, n_lat_ch, lat_body, tuple(states))

    def finish(o, g):
        outs = []
        for h in range(RET_HEADS):
            oh = o[:, h * RET_DK:(h + 1) * RET_DK]
            mu = jnp.mean(oh, axis=-1, keepdims=True)
            var = jnp.mean(jnp.square(oh - mu), axis=-1, keepdims=True)
            outs.append((oh - mu) * lax.rsqrt(var + EPS))
        on = jnp.concatenate(outs, axis=-1)
        g = g.astype(F32)
        return (g * _sigmoid(g) * on).astype(BF16)

    def fin_body(i, carry):
        rs = pl.multiple_of(i * ch, ch)
        y_ref[pl.ds(rs, ch), :] = finish(o_f[pl.ds(rs, ch), :] + o_b[pl.ds(rs, ch), :], g_ref[pl.ds(rs, ch), :])
        return carry

    lax.fori_loop(0, n_lat_ch, fin_body, 0)
    if need_ctx:
        for n in range(n_ctx_ch):
            rs = slice(n * ch, (n + 1) * ch)
            yc_ref[rs, :] = finish(oc_f[rs, :] + oc_b[rs, :], gc_ref[rs, :])


def _rope_tables(seq, head_dim, width, dist):
    half = head_dim // 2
    nf = half // 2
    assert nf == dist
    inv = 1.0 / (ROPE_THETA ** (np.arange(nf, dtype=np.float32) / nf))
    t = np.arange(seq)
    rows = (t // GRID_W).astype(np.float32)[:, None] * inv[None, :]
    cols = (t % GRID_W).astype(np.float32)[:, None] * inv[None, :]
    cos = np.concatenate([np.cos(rows), np.cos(rows), np.cos(cols), np.cos(cols)], axis=-1)
    sin = np.concatenate([-np.sin(rows), np.sin(rows), -np.sin(cols), np.sin(cols)], axis=-1)
    reps = width // head_dim
    return (jnp.asarray(np.tile(cos, (1, reps)), F32), jnp.asarray(np.tile(sin, (1, reps)), F32))


def _ret_mixer(p_all, log_decay, batch, seq, n_ctx, need_ctx):
    n_lat = batch * seq
    cb = n_lat // n_ctx
    w = 256
    log_gamma = jnp.log1p(-jnp.exp(log_decay.astype(F32)))
    cos, sin = _rope_tables(seq, RET_DK, w, 16)
    lat = lambda c: pl.BlockSpec((seq, w), lambda b, c=c: (b, c))
    ctx = lambda c: pl.BlockSpec((n_ctx, w), lambda b, c=c: (cb + b, c))
    whole = pl.BlockSpec((seq, w), lambda b: (0, 0))
    out_shape = [jax.ShapeDtypeStruct((n_lat, w), BF16)]
    out_specs = [pl.BlockSpec((seq, w), lambda b: (b, 0))]
    if need_ctx:
        out_shape.append(jax.ShapeDtypeStruct((batch * n_ctx, w), BF16))
        out_specs.append(pl.BlockSpec((n_ctx, w), lambda b: (b, 0)))
    return pl.pallas_call(
        functools.partial(_ret_kernel, need_ctx, seq, n_ctx),
        grid=(batch,),
        in_specs=[pl.BlockSpec(memory_space=pltpu.SMEM),
                  lat(RET_Q), lat(RET_K), lat(RET_V), lat(RET_G),
                  ctx(RET_Q), ctx(RET_K), ctx(RET_V), ctx(RET_G), whole, whole],
        out_specs=out_specs,
        out_shape=out_shape,
        scratch_shapes=[pltpu.VMEM((seq, w), F32), pltpu.VMEM((seq, w), F32),
                        pltpu.VMEM((seq, w), F32), pltpu.VMEM((seq, w), F32),
                        pltpu.VMEM((n_ctx, w), F32), pltpu.VMEM((n_ctx, w), F32),
                        pltpu.VMEM((2 * RET_HEADS, RET_CHUNK, RET_CHUNK), F32),
                        pltpu.VMEM((2 * RET_HEADS, RET_CHUNK, RET_DK), F32),
                        pltpu.VMEM((2 * RET_HEADS, RET_CHUNK, RET_DK), F32)],
        compiler_params=_params("parallel"),
        name="retention_mixer",
    )(log_gamma, p_all, p_all, p_all, p_all, p_all, p_all, p_all, p_all, cos, sin)


def _head_inv_rms(x, h, dim):
    xs = x[:, h * dim:(h + 1) * dim]
    return lax.rsqrt(jnp.mean(xs * xs, axis=-1, keepdims=True) + EPS)


def _gqa_kernel(n_lat_k, n_ctx, *refs):
    if n_lat_k:
        (q_ref, k_ref, v_ref, kc_ref, vc_ref, qg_ref, kg_ref, cq_ref, sq_ref, ck_ref, sk_ref,
         y_ref, kp, vp) = refs
    else:
        q_ref, kc_ref, vc_ref, qg_ref, kg_ref, y_ref, kp, vp = refs
    dim = GQA_DIM
    group = GQA_HEADS // GQA_KV_HEADS
    nk = n_lat_k + n_ctx

    @pl.when(pl.program_id(1) == 0)
    def _():
        kg = kg_ref[...]
        kcf = kc_ref[...].astype(F32)
        for g in range(GQA_KV_HEADS):
            gs = slice(g * dim, (g + 1) * dim)
            kp[g, n_lat_k:nk, :] = ((kcf * kg)[:, gs] * _head_inv_rms(kcf, g, dim)).astype(BF16)
            vp[g, n_lat_k:nk, :] = vc_ref[:, gs]
        if n_lat_k:
            ck = 512
            for c in range(n_lat_k // ck):
                rs = slice(c * ck, (c + 1) * ck)
                kf = k_ref[rs, :].astype(F32)
                kro = _rope(kf * kg, ck_ref[rs, :], sk_ref[rs, :], 16)
                for g in range(GQA_KV_HEADS):
                    gs = slice(g * dim, (g + 1) * dim)
                    kp[g, rs, :] = (kro[:, gs] * _head_inv_rms(kf, g, dim)).astype(BF16)
                    vp[g, rs, :] = v_ref[rs, gs]

    qf = q_ref[...].astype(F32)
    qro = qf * qg_ref[...]
    if n_lat_k:
        qro = _rope(qro, cq_ref[...], sq_ref[...], 16)
    for hq in range(GQA_HEADS):
        g = hq // group
        hs = slice(hq * dim, (hq + 1) * dim)
        qh = (qro[:, hs] * _head_inv_rms(qf, hq, dim)).astype(BF16)
        s = _dot_nt(qh, kp[g])
        m = jnp.max(s, axis=-1, keepdims=True)
        p = jnp.exp2(s - m)
        l = jnp.sum(p, axis=-1, keepdims=True)
        y_ref[:, hs] = (_dot(p.astype(BF16), vp[g]) / l).astype(BF16)


def _gqa_mixer(p_all, q_gain, k_gain, batch, seq, n_ctx, need_ctx):
    n_lat = batch * seq
    cb = n_lat // n_ctx
    dim = GQA_DIM
    qg = jnp.tile(q_gain.astype(F32) * (dim ** -0.5 * LOG2E), GQA_HEADS).reshape(1, GQA_HEADS * dim)
    kg = jnp.tile(k_gain.astype(F32), GQA_KV_HEADS).reshape(1, GQA_KV_HEADS * dim)
    cq, sq = _rope_tables(seq, dim, GQA_HEADS * dim, 16)
    ck, sk = _rope_tables(seq, dim, GQA_KV_HEADS * dim, 16)
    tq = 512
    nqb = seq // tq
    kvw = GQA_KV_HEADS * dim
    const = lambda shape: pl.BlockSpec(shape, lambda b, i: (0,) * len(shape))
    scratch = lambda nk: [pltpu.VMEM((GQA_KV_HEADS, nk, dim), BF16), pltpu.VMEM((GQA_KV_HEADS, nk, dim), BF16)]
    y_lat = pl.pallas_call(
        functools.partial(_gqa_kernel, seq, n_ctx),
        grid=(batch, nqb),
        in_specs=[pl.BlockSpec((tq, 256), lambda b, i: (b * nqb + i, GQA_Q)),
                  pl.BlockSpec((seq, kvw), lambda b, i: (b, 16)),
                  pl.BlockSpec((seq, kvw), lambda b, i: (b, 17)),
                  pl.BlockSpec((n_ctx, kvw), lambda b, i: (cb + b, 16)),
                  pl.BlockSpec((n_ctx, kvw), lambda b, i: (cb + b, 17)),
                  const((1, 256)), const((1, kvw)),
                  pl.BlockSpec((tq, 256), lambda b, i: (i, 0)),
                  pl.BlockSpec((tq, 256), lambda b, i: (i, 0)),
                  const((seq, kvw)), const((seq, kvw))],
        out_specs=pl.BlockSpec((tq, 256), lambda b, i: (b * nqb + i, 0)),
        out_shape=jax.ShapeDtypeStruct((n_lat, 256), BF16),
        scratch_shapes=scratch(seq + n_ctx),
        compiler_params=_params("parallel", "arbitrary"),
        name="gqa_mixer",
    )(p_all, p_all, p_all, p_all, p_all, qg, kg, cq, sq, ck, sk)
    if not need_ctx:
        return y_lat, None
    y_ctx = pl.pallas_call(
        functools.partial(_gqa_kernel, 0, n_ctx),
        grid=(batch, 1),
        in_specs=[pl.BlockSpec((n_ctx, 256), lambda b, i: (cb + b, GQA_Q)),
                  pl.BlockSpec((n_ctx, kvw), lambda b, i: (cb + b, 16)),
                  pl.BlockSpec((n_ctx, kvw), lambda b, i: (cb + b, 17)),
                  const((1, 256)), const((1, kvw))],
        out_specs=pl.BlockSpec((n_ctx, 256), lambda b, i: (b, 0)),
        out_shape=jax.ShapeDtypeStruct((batch * n_ctx, 256), BF16),
        scratch_shapes=scratch(n_ctx),
        compiler_params=_params("parallel", "arbitrary"),
        name="gqa_mixer_ctx",
    )(p_all, p_all, p_all, qg, kg)
    return y_lat, y_ctx


def _diff_kernel(n_lat_k, n_ctx, lam_init, *refs):
    if n_lat_k:
        (q_ref, k_ref, v_ref, kc_ref, vc_ref, lp_ref, sg_ref, cq_ref, sq_ref, ck_ref, sk_ref,
         y_ref, kp) = refs
    else:
        q_ref, kc_ref, vc_ref, lp_ref, sg_ref, y_ref, kp = refs
    dq = DIFF_QK_DIM
    dv = DIFF_V_DIM
    nk = n_lat_k + n_ctx

    @pl.when(pl.program_id(1) == 0)
    def _():
        kp[n_lat_k:nk, :] = kc_ref[...]
        if n_lat_k:
            ck = 512
            for c in range(n_lat_k // ck):
                rs = slice(c * ck, (c + 1) * ck)
                kp[rs, :] = _rope(k_ref[rs, :].astype(F32), ck_ref[rs, :], sk_ref[rs, :], 8).astype(BF16)

    lp = lp_ref[...]
    lam = (jnp.exp(jnp.sum(lp[0:1, :] * lp[1:2, :], axis=-1, keepdims=True))
           - jnp.exp(jnp.sum(lp[2:3, :] * lp[3:4, :], axis=-1, keepdims=True)) + lam_init)

    qf = q_ref[...].astype(F32)
    if n_lat_k:
        qf = _rope(qf, cq_ref[...], sq_ref[...], 8)
    lane = lax.broadcasted_iota(jnp.int32, (1, 2 * dq), 1)
    first = lane < dq

    def attend(qm, kh, vs):
        s = _dot_nt(qm.astype(BF16), kh)
        p = jnp.exp2(s - jnp.max(s, axis=-1, keepdims=True))
        l = jnp.sum(p, axis=-1, keepdims=True)
        p = p.astype(BF16)
        if n_lat_k:
            o = _dot(p[:, :n_lat_k], v_ref[:, vs]) + _dot(p[:, n_lat_k:], vc_ref[:, vs])
        else:
            o = _dot(p, vc_ref[:, vs])
        return o / l

    for h in range(DIFF_HEADS):
        hs = slice(h * 2 * dq, (h + 1) * 2 * dq)
        qh = qf[:, hs]
        kh = kp[:, hs]
        vs = slice(h * dv, (h + 1) * dv)
        o = attend(jnp.where(first, qh, 0.0), kh, vs) - lam * attend(jnp.where(first, 0.0, qh), kh, vs)
        on = o * lax.rsqrt(jnp.mean(o * o, axis=-1, keepdims=True) + EPS) * sg_ref[...]
        y_ref[:, vs] = (on * (1.0 - lam_init)).astype(BF16)


def _diff_mixer(p_all, lam_params, subln, layer_idx, batch, seq, n_ctx, need_ctx):
    n_lat = batch * seq
    cb = n_lat // n_ctx
    lam_init = 0.8 - 0.6 * math.exp(-0.3 * layer_idx)
    lp = jnp.zeros((8, 128), F32).at[:4, :DIFF_QK_DIM].set(lam_params.astype(F32))
    sg = subln.astype(F32).reshape(1, DIFF_V_DIM)
    cq, sq = _rope_tables(seq, DIFF_QK_DIM, 256, 8)
    tq = 512
    nqb = seq // tq
    const = lambda shape: pl.BlockSpec(shape, lambda b, i: (0,) * len(shape))
    y_lat = pl.pallas_call(
        functools.partial(_diff_kernel, seq, n_ctx, lam_init),
        grid=(batch, nqb),
        in_specs=[pl.BlockSpec((tq, 256), lambda b, i: (b * nqb + i, DIFF_Q)),
                  pl.BlockSpec((seq, 256), lambda b, i: (b, DIFF_K)),
                  pl.BlockSpec((seq, 256), lambda b, i: (b, DIFF_V)),
                  pl.BlockSpec((n_ctx, 256), lambda b, i: (cb + b, DIFF_K)),
                  pl.BlockSpec((n_ctx, 256), lambda b, i: (cb + b, DIFF_V)),
                  const((8, 128)), const((1, DIFF_V_DIM)),
                  pl.BlockSpec((tq, 256), lambda b, i: (i, 0)),
                  pl.BlockSpec((tq, 256), lambda b, i: (i, 0)),
                  const((seq, 256)), const((seq, 256))],
        out_specs=pl.BlockSpec((tq, 256), lambda b, i: (b * nqb + i, 0)),
        out_shape=jax.ShapeDtypeStruct((n_lat, 256), BF16),
        scratch_shapes=[pltpu.VMEM((seq + n_ctx, 256), BF16)],
        compiler_params=_params("parallel", "arbitrary"),
        name="diff_mixer",
    )(p_all, p_all, p_all, p_all, p_all, lp, sg, cq, sq, cq, sq)
    if not need_ctx:
        return y_lat, None
    y_ctx = pl.pallas_call(
        functools.partial(_diff_kernel, 0, n_ctx, lam_init),
        grid=(batch, 1),
        in_specs=[pl.BlockSpec((n_ctx, 256), lambda b, i: (cb + b, DIFF_Q)),
                  pl.BlockSpec((n_ctx, 256), lambda b, i: (cb + b, DIFF_K)),
                  pl.BlockSpec((n_ctx, 256), lambda b, i: (cb + b, DIFF_V)),
                  const((8, 128)), const((1, DIFF_V_DIM))],
        out_specs=pl.BlockSpec((n_ctx, 256), lambda b, i: (b, 0)),
        out_shape=jax.ShapeDtypeStruct((batch * n_ctx, 256), BF16),
        scratch_shapes=[pltpu.VMEM((n_ctx, 256), BF16)],
        compiler_params=_params("parallel", "arbitrary"),
        name="diff_mixer_ctx",
    )(p_all, p_all, p_all, lp, sg)
    return y_lat, y_ctx


def _merge_kernel(x_ref, g0, g1, g2, g3, ya, yb, yc, yd, wb_ref, wo_ref, mod_ref, gn_ref, wr_ref,
                  xo_ref, m_ref, lg_ref):
    acc = None
    for n, (g_ref, y_ref) in enumerate(((g0, ya), (g1, yb), (g2, yc), (g3, yd))):
        term = g_ref[...].astype(F32) * _dot(y_ref[...], wb_ref[n])
        acc = term if acc is None else acc + term
    y = _dot(acc.astype(BF16), wo_ref[...])
    x = x_ref[...] + mod_ref[2:3, :] * y
    xo_ref[...] = x
    xn = x * lax.rsqrt(jnp.mean(x * x, axis=-1, keepdims=True) + EPS) * gn_ref[...]
    m = xn * (1.0 + mod_ref[4:5, :]) + mod_ref[3:4, :]
    m_ref[...] = m
    m_hi = m.astype(BF16)
    m_lo = (m - m_hi.astype(F32)).astype(BF16)
    w = wr_ref[...]
    w_hi = w.astype(BF16)
    w_lo = (w - w_hi.astype(F32)).astype(BF16)
    lg_ref[...] = _dot_nt(w_hi, m_hi) + (_dot_nt(w_hi, m_lo) + _dot_nt(w_lo, m_hi))


def _merge(x_all, p_all, ys, wb_bf, wo_bf, modtab, gn, wr_t, tm, n_rows, n_lat, seq, batch):
    d = D_MODEL
    midx = _mod_index(tm, n_lat, seq, batch)
    gate = lambda n: pl.BlockSpec((tm, d), lambda i, n=n: (i, MIX_COLS // d + n))
    yspec = pl.BlockSpec((tm, BRANCH_W), lambda i: (i, 0))
    return pl.pallas_call(
        _merge_kernel,
        grid=(n_rows // tm,),
        in_specs=[pl.BlockSpec((tm, d), lambda i: (i, 0)),
                  gate(0), gate(1), gate(2), gate(3), yspec, yspec, yspec, yspec,
                  pl.BlockSpec((N_BRANCH, BRANCH_W, d), lambda i: (0, 0, 0)),
                  pl.BlockSpec((d, d), lambda i: (0, 0)),
                  pl.BlockSpec((None, 8, d), lambda i: (midx(i), 0, 0)),
                  pl.BlockSpec((1, d), lambda i: (0, 0)),
                  pl.BlockSpec((N_EXPERTS, d), lambda i: (0, 0))],
        out_specs=[pl.BlockSpec((tm, d), lambda i: (i, 0)),
                   pl.BlockSpec((tm, d), lambda i: (i, 0)),
                   pl.BlockSpec((N_EXPERTS, tm), lambda i: (0, i))],
        out_shape=[jax.ShapeDtypeStruct((n_rows, d), F32),
                   jax.ShapeDtypeStruct((n_rows, d), F32),
                   jax.ShapeDtypeStruct((N_EXPERTS, n_rows), F32)],
        compiler_params=_params("parallel"),
        name="merge_norm_route",
    )(x_all, p_all, p_all, p_all, p_all, *ys, wb_bf, wo_bf, modtab, gn.reshape(1, d), wr_t)


def _route_kernel(lg_ref, b_ref, idx_ref, w_ref):
    s = _sigmoid(lg_ref[...])
    sel = s + b_ref[...]
    row = lambda a, e: a[e:e + 1, :]
    gsz = EXPERTS_PER_GROUP
    g_idx = None
    best = None
    for g in range(N_GROUPS):
        v = [row(sel, g * gsz + i) for i in range(gsz)]
        score = None
        for i in range(gsz):
            for j in range(i + 1, gsz):
                pair = v[i] + v[j]
                score = pair if score is None else jnp.maximum(score, pair)
        if g == 0:
            best, g_idx = score, jnp.zeros(score.shape, jnp.int32)
        else:
            better = score > best
            best = jnp.where(better, score, best)
            g_idx = jnp.where(better, g, g_idx)

    def in_group(a, i):
        out = row(a, i)
        for g in range(1, N_GROUPS):
            out = jnp.where(g_idx == g, row(a, g * gsz + i), out)
        return out

    v = [in_group(sel, i) for i in range(gsz)]
    sv = [in_group(s, i) for i in range(gsz)]

    def arg_first_max(vals):
        bv, bi = vals[0], jnp.zeros(vals[0].shape, jnp.int32)
        for i in range(1, gsz):
            better = vals[i] > bv
            bv = jnp.where(better, vals[i], bv)
            bi = jnp.where(better, i, bi)
        return bi

    i1 = arg_first_max(v)
    i2 = arg_first_max([jnp.where(i1 == i, -jnp.inf, v[i]) for i in range(gsz)])

    def pick(vals, idx):
        out = vals[0]
        for i in range(1, gsz):
            out = jnp.where(idx == i, vals[i], out)
        return out

    w1 = pick(sv, i1)
    w2 = pick(sv, i2)
    tot = w1 + w2
    idx_ref[0:1, :] = g_idx * gsz + i1
    idx_ref[1:2, :] = g_idx * gsz + i2
    w_ref[0:1, :] = w1 / tot
    w_ref[1:2, :] = w2 / tot


def _route(logits_t, router_bias):
    e, n = logits_t.shape
    tn = math.gcd(n, 2048)
    return pl.pallas_call(
        _route_kernel,
        grid=(n // tn,),
        in_specs=[pl.BlockSpec((e, tn), lambda i: (0, i)),
                  pl.BlockSpec((e, 1), lambda i: (0, 0))],
        out_specs=[pl.BlockSpec((TOP_K, tn), lambda i: (0, i)),
                   pl.BlockSpec((TOP_K, tn), lambda i: (0, i))],
        out_shape=[jax.ShapeDtypeStruct((TOP_K, n), jnp.int32),
                   jax.ShapeDtypeStruct((TOP_K, n), F32)],
        compiler_params=_params("parallel"),
        name="route_top2",
    )(logits_t, router_bias.astype(F32).reshape(e, 1))


def _dispatch_kernel(pos_ref, m_ref, xs_in, xs_out, sem):
    del xs_in
    i = pl.program_id(0)
    tm = m_ref.shape[0]
    n = TOP_K * tm
    for j in range(n):
        dst = pos_ref[i, j]
        pltpu.make_async_copy(m_ref.at[pl.ds(j % tm, 1)], xs_out.at[pl.ds(dst, 1)], sem).start(priority=j % 2)
    pltpu.make_async_copy(xs_out.at[pl.ds(0, n)], xs_out.at[pl.ds(0, n)], sem).wait()


def _dispatch(m_all, pos, n_slots):
    n_rows, d = m_all.shape
    n_steps, n = pos.shape
    tm = n // TOP_K
    grid_spec = pltpu.PrefetchScalarGridSpec(
        num_scalar_prefetch=1,
        grid=(n_steps,),
        in_specs=[pl.BlockSpec((tm, d), lambda i, pos: (i, 0)),
                  pl.BlockSpec(memory_space=pl.ANY)],
        out_specs=pl.BlockSpec(memory_space=pl.ANY),
        scratch_shapes=[pltpu.SemaphoreType.DMA(())])
    return pl.pallas_call(
        _dispatch_kernel,
        grid_spec=grid_spec,
        out_shape=jax.ShapeDtypeStruct((n_slots, d), F32),
        input_output_aliases={2: 0},
        compiler_params=_params("arbitrary"),
        name="moe_dispatch",
    )(pos, m_all, jnp.zeros((n_slots, d), F32))


def _expert_kernel(be_ref, x_ref, wg_ref, wu_ref, wd_ref, y_ref):
    x = x_ref[...].astype(BF16)
    a = _dot(x, wg_ref[...])
    h = a * _sigmoid(a) * _dot(x, wu_ref[...])
    y_ref[...] = _dot(h.astype(BF16), wd_ref[...])


def _experts(x_slots, blk_e, blk, wg_bf, wu_bf, wd_bf):
    n_slots, d = x_slots.shape
    grid_spec = pltpu.PrefetchScalarGridSpec(
        num_scalar_prefetch=1,
        grid=(n_slots // blk,),
        in_specs=[pl.BlockSpec((blk, d), lambda i, be: (i, 0)),
                  pl.BlockSpec((None, d, D_EXPERT), lambda i, be: (be[i], 0, 0)),
                  pl.BlockSpec((None, d, D_EXPERT), lambda i, be: (be[i], 0, 0)),
                  pl.BlockSpec((None, D_EXPERT, d), lambda i, be: (be[i], 0, 0))],
        out_specs=pl.BlockSpec((blk, d), lambda i, be: (i, 0)))
    return pl.pallas_call(
        _expert_kernel,
        grid_spec=grid_spec,
        out_shape=jax.ShapeDtypeStruct((n_slots, d), F32),
        compiler_params=_params("arbitrary"),
        name="expert_ffn",
    )(blk_e, x_slots, wg_bf, wu_bf, wd_bf)


def _combine_kernel(final, pos_ref, y_hbm, x_ref, w_ref, mod_ref, gf_ref, o_ref, ybuf, sem):
    i = pl.program_id(0)
    last = pl.num_programs(0) - 1
    slot = i % 2
    tm = x_ref.shape[0]
    n = TOP_K * tm

    def issue(step, dst_slot):
        for j in range(n):
            src = pos_ref[step, j]
            pltpu.make_async_copy(y_hbm.at[pl.ds(src, 1)], ybuf.at[dst_slot, pl.ds(j, 1)],
                                  sem.at[dst_slot]).start(priority=j % 2)

    def wait(dst_slot):
        pltpu.make_async_copy(y_hbm.at[pl.ds(0, n)], ybuf.at[dst_slot], sem.at[dst_slot]).wait()

    @pl.when(i == 0)
    def _():
        issue(0, 0)

    wait(slot)
    issue(jnp.minimum(i + 1, last), 1 - slot)
    w = w_ref[...]
    f = w[:, 0:1] * ybuf[slot, 0:tm, :] + w[:, 1:2] * ybuf[slot, tm:n, :]
    x = x_ref[...] + mod_ref[5:6, :] * f
    if final:
        x = x * lax.rsqrt(jnp.mean(x * x, axis=-1, keepdims=True) + EPS) * gf_ref[...]
    o_ref[...] = x

    @pl.when(i == last)
    def _():
        wait(1 - slot)


def _combine(x_rows, y_slots, pos, w_tok, modtab, g_final, final, n_lat, seq, batch):
    n_rows, d = x_rows.shape
    n_steps, n = pos.shape
    tm = n // TOP_K
    midx = _mod_index(tm, n_lat, seq, batch)
    grid_spec = pltpu.PrefetchScalarGridSpec(
        num_scalar_prefetch=1,
        grid=(n_steps,),
        in_specs=[pl.BlockSpec(memory_space=pl.ANY),
                  pl.BlockSpec((tm, d), lambda i, pos: (i, 0)),
                  pl.BlockSpec((tm, TOP_K), lambda i, pos: (i, 0)),
                  pl.BlockSpec((None, 8, d), lambda i, pos: (midx(i), 0, 0)),
                  pl.BlockSpec((1, d), lambda i, pos: (0, 0))],
        out_specs=pl.BlockSpec((tm, d), lambda i, pos: (i, 0)),
        scratch_shapes=[pltpu.VMEM((2, n, d), F32), pltpu.SemaphoreType.DMA((2,))])
    return pl.pallas_call(
        functools.partial(_combine_kernel, final),
        grid_spec=grid_spec,
        out_shape=jax.ShapeDtypeStruct((n_rows, d), F32),
        compiler_params=_params("arbitrary"),
        name="moe_combine",
    )(pos, y_slots, x_rows, w_tok, modtab, g_final.astype(F32).reshape(1, d))


def _dispatch_plan(idx2, blk, tm):
    n = idx2.shape[1]
    a = TOP_K * n
    flat_e = idx2.T.reshape(a)
    onehot = (flat_e[:, None] == jnp.arange(N_EXPERTS, dtype=jnp.int32)[None, :]).astype(jnp.int32)
    csum = jnp.cumsum(onehot, axis=0)
    rank = jnp.sum(onehot * csum, axis=1) - 1
    counts = csum[-1]
    padded = (counts + blk - 1) // blk * blk
    pad_end = jnp.cumsum(padded)
    pad_start = pad_end - padded
    dest = (pad_start[flat_e] + rank).astype(jnp.int32)
    n_blocks = a // blk + N_EXPERTS
    blk_e = jnp.minimum(jnp.searchsorted(pad_end, jnp.arange(n_blocks, dtype=jnp.int32) * blk, side='right'),
                        N_EXPERTS - 1).astype(jnp.int32)
    pos = dest.reshape(n // tm, tm, TOP_K).transpose(0, 2, 1).reshape(n // tm, TOP_K * tm)
    return pos, blk_e, n_blocks * blk


def kernel(x, c, ctx, c_ctx, w_mod, b_mod, g_norm1, g_norm2, w_in, na_rpb, ret_log_decay, gqa_q_gain, gqa_k_gain,
           diff_lambda, diff_subln, w_branch, w_out, w_router, router_bias, w_gate_e, w_up_e, w_down_e, g_final):
    batch, seq, d = x.shape
    n_ctx = ctx.shape[1]
    n_lat = batch * seq
    n_ctx_tot = batch * n_ctx
    n_tot = n_lat + n_ctx_tot
    tm = 1024 if n_ctx_tot % 1024 == 0 else 512

    x_all = jnp.concatenate([x.reshape(n_lat, d), ctx.reshape(n_ctx_tot, d)], axis=0)

    nrow = -(-(batch + 1) // 8) * 8
    cc = jnp.zeros((nrow, d), F32).at[:batch].set(c).at[batch].set(c_ctx)
    mod = _modulation(cc, w_mod, b_mod)
    modtab = jnp.pad(mod.reshape(DEPTH, nrow, 6, d), ((0, 0), (0, 0), (0, 2), (0, 0)))

    wr_t = w_router.astype(F32).T
    out = None
    for layer in range(DEPTH):
        need_ctx = layer < DEPTH - 1
        n_rows = n_tot if need_ctx else n_lat
        p_all = _project(x_all, g_norm1[layer], modtab[layer], w_in[layer].astype(BF16), tm, n_lat, seq, batch)
        ya = _na_mixer(p_all, na_rpb[layer], batch, seq, n_ctx, need_ctx)
        yb = _ret_mixer(p_all, ret_log_decay[layer], batch, seq, n_ctx, need_ctx)
        yc = _gqa_mixer(p_all, gqa_q_gain[layer], gqa_k_gain[layer], batch, seq, n_ctx, need_ctx)
        yd = _diff_mixer(p_all, diff_lambda[layer], diff_subln[layer], layer, batch, seq, n_ctx, need_ctx)
        if need_ctx:
            ys = [jnp.concatenate([yl, yc_], axis=0) for yl, yc_ in (ya, yb, yc, yd)]
        else:
            ys = [ya[0], yb[0], yc[0], yd[0]]
        x_mid, m_all, logits_t = _merge(x_all, p_all, ys, w_branch[layer].astype(BF16), w_out[layer].astype(BF16),
                                        modtab[layer], g_norm2[layer], wr_t, tm, n_rows, n_lat, seq, batch)
        idx2, w2 = _route(logits_t, router_bias)
        pos, blk_e, n_slots = _dispatch_plan(idx2, MOE_BLK, CMB_TM)
        x_slots = _dispatch(m_all, pos, n_slots)
        y_slots = _experts(x_slots, blk_e, MOE_BLK, w_gate_e[layer].astype(BF16), w_up_e[layer].astype(BF16),
                           w_down_e[layer].astype(BF16))
        out = _combine(x_mid, y_slots, pos, w2.T, modtab[layer], g_final, not need_ctx, n_lat, seq, batch)
        x_all = out
    return out.reshape(batch, seq, d)
```

```python
import functools
import math

import numpy as np
import jax
import jax.numpy as jnp
from jax import lax
from jax.experimental import pallas as pl
from jax.experimental.pallas import tpu as pltpu

F32 = jnp.float32
BF16 = jnp.bfloat16

D_MODEL = 1024
DEPTH = 2
GRID_W = 64
NA_HEADS = 4
NA_DIM = 64
NA_WIN_H = 8
NA_WIN_W = 16
RET_HEADS = 4
RET_DK = 64
RET_CHUNK = 128
GQA_HEADS = 4
GQA_KV_HEADS = 2
GQA_DIM = 64
DIFF_HEADS = 4
DIFF_QK_DIM = 32
DIFF_V_DIM = 64
N_BRANCH = 4
BRANCH_W = 256
ROPE_THETA = 10000.0
EPS = 1e-6
NEG_INF = -1e30
N_EXPERTS = 16
N_GROUPS = 4
EXPERTS_PER_GROUP = 4
TOP_K = 2
D_EXPERT = 512

MIX_COLS = 3072
IN_COLS = MIX_COLS + N_BRANCH * D_MODEL
NA_Q, NA_K, NA_V = 0, 1, 2
RET_Q, RET_K, RET_V, RET_G = 3, 4, 5, 6
GQA_Q = 7
DIFF_Q, DIFF_K, DIFF_V = 9, 10, 11

LOG2E = math.log2(math.e)
NA_GROUP_ROWS = 4
NA_BAND_ROWS = 12

VMEM_LIMIT = 56 * 1024 * 1024
MOE_BLK = 512
CMB_TM = 256
MERGE_TM = 512


def _dot(a, b):
    return jnp.dot(a, b, preferred_element_type=F32)


def _dot_nt(a, b):
    return lax.dot_general(a, b, (((1,), (1,)), ((), ())), preferred_element_type=F32)


def _dot_tn(a, b):
    return lax.dot_general(a, b, (((0,), (0,)), ((), ())), preferred_element_type=F32)


def _sigmoid(x):
    return 1.0 / (1.0 + jnp.exp(-x))


def _params(*sem):
    return pltpu.CompilerParams(dimension_semantics=sem, vmem_limit_bytes=VMEM_LIMIT)


def _swap_halves(x, dist):
    pieces = []
    for j in range(x.shape[-1] // 128):
        xs = x[:, j * 128:(j + 1) * 128]
        lane = lax.broadcasted_iota(jnp.int32, xs.shape, 1)
        up = pltpu.roll(xs, 128 - dist, 1)
        dn = pltpu.roll(xs, dist, 1)
        pieces.append(jnp.where((lane % (2 * dist)) < dist, up, dn))
    return pieces[0] if len(pieces) == 1 else jnp.concatenate(pieces, axis=-1)


def _rope(x, cos, sin, dist):
    return x * cos + _swap_halves(x, dist) * sin


def _mod_kernel(c_ref, w_ref, b_ref, o_ref):
    c = c_ref[...]
    a = (c * _sigmoid(c)).astype(BF16)
    o_ref[...] = _dot(a, w_ref[...].astype(BF16)) + b_ref[...]


def _modulation(cc, w_mod, b_mod):
    nrow = cc.shape[0]
    depth, d, n6 = w_mod.shape
    tn = 1536
    return pl.pallas_call(
        _mod_kernel,
        grid=(depth, n6 // tn),
        in_specs=[pl.BlockSpec((nrow, d), lambda l, j: (0, 0)),
                  pl.BlockSpec((None, d, tn), lambda l, j: (l, 0, j)),
                  pl.BlockSpec((None, 1, tn), lambda l, j: (l, 0, j))],
        out_specs=pl.BlockSpec((None, nrow, tn), lambda l, j: (l, 0, j)),
        out_shape=jax.ShapeDtypeStruct((depth, nrow, n6), F32),
        compiler_params=_params("parallel", "arbitrary"),
        name="modulation",
    )(cc, w_mod, b_mod.reshape(depth, 1, n6))


def _mix_col_scale():
    cs = np.ones((1, MIX_COLS), np.float32)
    cs[0, NA_Q * 256:(NA_Q + 1) * 256] = NA_DIM ** -0.5 * LOG2E
    cs[0, DIFF_Q * 256:(DIFF_Q + 1) * 256] = DIFF_QK_DIM ** -0.5 * LOG2E
    cs[0, RET_K * 256:(RET_K + 1) * 256] = RET_DK ** -0.5
    return jnp.asarray(cs)


def _proj_kernel(x_ref, g_ref, mod_ref, w_ref, cs_ref, o_ref, h_ref):
    j = pl.program_id(1)

    @pl.when(j == 0)
    def _():
        x = x_ref[...]
        y = x * lax.rsqrt(jnp.mean(x * x, axis=-1, keepdims=True) + EPS) * g_ref[...]
        h_ref[...] = (y * (1.0 + mod_ref[1:2, :]) + mod_ref[0:1, :]).astype(BF16)

    acc = _dot(h_ref[...], w_ref[...])
    gate = 0.5 * jnp.tanh(0.5 * acc) + 0.5
    o_ref[...] = jnp.where(j < MIX_COLS // D_MODEL, acc * cs_ref[...], gate).astype(BF16)


def _mod_index(tm, n_lat, seq, batch):
    nlat_blk = n_lat // tm
    bpb = seq // tm

    def index(i):
        return jnp.where(i < nlat_blk, i // bpb, batch)
    return index


def _project(x_all, g, modtab, w_bf, tm, n_lat, seq, batch):
    n_tot, d = x_all.shape
    ncol = w_bf.shape[1]
    tn = D_MODEL
    n_mix = MIX_COLS // tn
    midx = _mod_index(tm, n_lat, seq, batch)
    return pl.pallas_call(
        _proj_kernel,
        grid=(n_tot // tm, ncol // tn),
        in_specs=[pl.BlockSpec((tm, d), lambda i, j: (i, 0)),
                  pl.BlockSpec((1, d), lambda i, j: (0, 0)),
                  pl.BlockSpec((None, 8, d), lambda i, j: (midx(i), 0, 0)),
                  pl.BlockSpec((d, tn), lambda i, j: (0, j)),
                  pl.BlockSpec((1, tn), lambda i, j: (0, jnp.minimum(j, n_mix - 1)))],
        out_specs=pl.BlockSpec((tm, tn), lambda i, j: (i, j)),
        out_shape=jax.ShapeDtypeStruct((n_tot, ncol), BF16),
        scratch_shapes=[pltpu.VMEM((tm, d), BF16)],
        compiler_params=_params("parallel", "arbitrary"),
        name="norm_project",
    )(x_all, g.reshape(1, d), modtab, w_bf, _mix_col_scale())


def _na_kernel(need_ctx, rows, q_ref, k_ref, v_ref, qc_ref, kc_ref, vc_ref, tb_ref, y_ref, *rest):
    gq = NA_GROUP_ROWS * GRID_W
    band = NA_BAND_ROWS * GRID_W
    n_groups = rows // NA_GROUP_ROWS

    def group_body(g, carry):
        u = jnp.clip(g * NA_GROUP_ROWS - NA_WIN_H // 2, 0, rows - NA_BAND_ROWS)
        typ = jnp.where(g == 0, 0, jnp.where(g == n_groups - 1, 2, 1))
        qs = pl.multiple_of(g * gq, gq)
        ks = pl.multiple_of(u * GRID_W, GRID_W)
        for h in range(NA_HEADS):
            hs = slice(h * NA_DIM, (h + 1) * NA_DIM)
            q = q_ref[pl.ds(qs, gq), hs]
            vc = vc_ref[:, hs]
            s_w = _dot_nt(q, k_ref[pl.ds(ks, band), hs]) + tb_ref[h, typ]
            s_c = _dot_nt(q, kc_ref[:, hs])
            m = jnp.maximum(jnp.max(s_w, axis=-1, keepdims=True), jnp.max(s_c, axis=-1, keepdims=True))
            p_w = jnp.exp2(s_w - m)
            p_c = jnp.exp2(s_c - m)
            l = jnp.sum(p_w, axis=-1, keepdims=True) + jnp.sum(p_c, axis=-1, keepdims=True)
            o = (_dot(p_w.astype(BF16), v_ref[pl.ds(ks, band), hs]) + _dot(p_c.astype(BF16), vc)) / l
            y_ref[pl.ds(qs, gq), hs] = o.astype(BF16)
        return carry

    lax.fori_loop(0, n_groups, group_body, 0)

    if need_ctx:
        yc_ref = rest[0]
        for h in range(NA_HEADS):
            hs = slice(h * NA_DIM, (h + 1) * NA_DIM)
            s = _dot_nt(qc_ref[:, hs], kc_ref[:, hs])
            p = jnp.exp2(s - jnp.max(s, axis=-1, keepdims=True))
            l = jnp.sum(p, axis=-1, keepdims=True)
            yc_ref[:, hs] = (_dot(p.astype(BF16), vc_ref[:, hs]) / l).astype(BF16)


def _na_bias_table(rpb, rows):
    assert rows % NA_GROUP_ROWS == 0 and rows >= NA_BAND_ROWS + NA_GROUP_ROWS
    n_groups = rows // NA_GROUP_ROWS
    qc = np.arange(GRID_W)[:, None]
    kc = np.arange(GRID_W)[None, :]
    win_start = np.clip(qc - NA_WIN_W // 2, 0, GRID_W - NA_WIN_W)
    col_ok = (kc >= win_start) & (kc < win_start + NA_WIN_W)
    col_idx = np.clip(kc - qc, -(NA_WIN_W - 1), NA_WIN_W - 1) + NA_WIN_W - 1
    tiles = jnp.where(col_ok[None, None], rpb.astype(F32)[:, :, col_idx] * LOG2E, NEG_INF)
    masked = jnp.full((NA_HEADS, GRID_W, GRID_W), NEG_INF, F32)
    tables = []
    for g in (0, 1, n_groups - 1):
        u = int(np.clip(g * NA_GROUP_ROWS - NA_WIN_H // 2, 0, rows - NA_BAND_ROWS))
        per_row = []
        for a in range(NA_GROUP_ROWS):
            r = g * NA_GROUP_ROWS + a
            r0 = int(np.clip(r - NA_WIN_H // 2, 0, rows - NA_WIN_H))
            pieces = [tiles[:, kr - r + NA_WIN_H - 1] if r0 <= kr < r0 + NA_WIN_H else masked
                      for kr in range(u, u + NA_BAND_ROWS)]
            per_row.append(jnp.concatenate(pieces, axis=-1))
        tables.append(jnp.concatenate(per_row, axis=1))
    return jnp.stack(tables, axis=1)


def _na_mixer(p_all, rpb, batch, seq, n_ctx, need_ctx):
    n_lat = batch * seq
    rows = seq // GRID_W
    tb = _na_bias_table(rpb, rows)
    cb = n_lat // n_ctx
    w = 256
    lat = lambda c: pl.BlockSpec((seq, w), lambda b, c=c: (b, c))
    ctx = lambda c: pl.BlockSpec((n_ctx, w), lambda b, c=c: (cb + b, c))
    out_shape = [jax.ShapeDtypeStruct((n_lat, w), BF16)]
    out_specs = [pl.BlockSpec((seq, w), lambda b: (b, 0))]
    if need_ctx:
        out_shape.append(jax.ShapeDtypeStruct((batch * n_ctx, w), BF16))
        out_specs.append(pl.BlockSpec((n_ctx, w), lambda b: (b, 0)))
    return pl.pallas_call(
        functools.partial(_na_kernel, need_ctx, rows),
        grid=(batch,),
        in_specs=[lat(NA_Q), lat(NA_K), lat(NA_V), ctx(NA_Q), ctx(NA_K), ctx(NA_V),
                  pl.BlockSpec(tb.shape, lambda b: (0, 0, 0, 0))],
        out_specs=out_specs,
        out_shape=out_shape,
        compiler_params=_params("parallel"),
        name="na_mixer",
    )(p_all, p_all, p_all, p_all, p_all, p_all, tb)


def _ret_kernel(need_ctx, seq, n_ctx, lg_ref, q_ref, k_ref, v_ref, g_ref, qc_ref, kc_ref, vc_ref, gc_ref,
                cos_ref, sin_ref, y_ref, *rest):
    if need_ctx:
        yc_ref, qr, kr, o_f, o_b, oc_f, oc_b, dm, qd, kd = rest
    else:
        qr, kr, o_f, o_b, oc_f, oc_b, dm, qd, kd = rest
    ch = RET_CHUNK
    n_lat_ch = seq // ch
    n_ctx_ch = n_ctx // ch

    def prep(i, carry):
        rs = pl.multiple_of(i * ch, ch)
        c = cos_ref[pl.ds(rs, ch), :]
        s = sin_ref[pl.ds(rs, ch), :]
        qr[pl.ds(rs, ch), :] = _rope(q_ref[pl.ds(rs, ch), :].astype(F32), c, s, 16)
        kr[pl.ds(rs, ch), :] = _rope(k_ref[pl.ds(rs, ch), :].astype(F32), c, s, 16)
        return carry

    lax.fori_loop(0, n_lat_ch, prep, 0)

    pos = lax.broadcasted_iota(jnp.int32, (ch, RET_DK), 0).astype(F32)
    ri = lax.broadcasted_iota(jnp.int32, (ch, ch), 0)
    ci = lax.broadcasted_iota(jnp.int32, (ch, ch), 1)
    lag = (ri - ci).astype(F32)

    chains = [(h, dirn) for h in range(RET_HEADS) for dirn in range(2)]
    cdec = []
    for c, (h, dirn) in enumerate(chains):
        lg = lg_ref[dirn, h]
        if dirn == 0:
            keep = ri >= ci
            dm[c] = jnp.where(keep, jnp.exp(jnp.where(keep, lag, 0.0) * lg), 0.0)
            qd[c] = jnp.exp((pos + 1.0) * lg)
            kd[c] = jnp.exp((ch - 1.0 - pos) * lg)
        else:
            keep = ci >= ri
            dm[c] = jnp.where(keep, jnp.exp(jnp.where(keep, -lag, 0.0) * lg), 0.0)
            qd[c] = jnp.exp((ch - pos) * lg)
            kd[c] = jnp.exp(pos * lg)
        cdec.append(jnp.exp(ch * lg))

    def step(c, state, qi, ki, vi):
        inner = _dot_nt(qi.astype(BF16), ki.astype(BF16)) * dm[c]
        o = _dot(inner.astype(BF16), vi) + _dot((qi * qd[c]).astype(BF16), state.astype(BF16))
        state = state * cdec[c] + _dot_tn((ki * kd[c]).astype(BF16), vi)
        return state, o

    states = []
    for c, (h, dirn) in enumerate(chains):
        hs = slice(h * RET_DK, (h + 1) * RET_DK)
        state = jnp.zeros((RET_DK, RET_DK), F32)
        for n in (range(n_ctx_ch) if dirn == 0 else range(n_ctx_ch - 1, -1, -1)):
            rs = slice(n * ch, (n + 1) * ch)
            state, o = step(c, state, qc_ref[rs, hs].astype(F32), kc_ref[rs, hs].astype(F32), vc_ref[rs, hs])
            if need_ctx:
                (oc_f if dirn == 0 else oc_b)[rs, hs] = o
        states.append(state)

    def lat_body(i, states):
        new = []
        for c, (h, dirn) in enumerate(chains):
            hs = slice(h * RET_DK, (h + 1) * RET_DK)
            rs = pl.multiple_of((i if dirn == 0 else n_lat_ch - 1 - i) * ch, ch)
            state, o = step(c, states[c], qr[pl.ds(rs, ch), hs], kr[pl.ds(rs, ch), hs], v_ref[pl.ds(rs, ch), hs])
            (o_f if dirn == 0 else o_b)[pl.ds(rs, ch), hs] = o
            new.append(state)
        return tuple(new)

    lax.fori_loop(0, n_lat_ch, lat_body, tuple(states))

    def finish(o, g):
        outs = []
        for h in range(RET_HEADS):
            oh = o[:, h * RET_DK:(h + 1) * RET_DK]
            mu = jnp.mean(oh, axis=-1, keepdims=True)
            var = jnp.mean(jnp.square(oh - mu), axis=-1, keepdims=True)
            outs.append((oh - mu) * lax.rsqrt(var + EPS))
        on = jnp.concatenate(outs, axis=-1)
        g = g.astype(F32)
        return (g * _sigmoid(g) * on).astype(BF16)

    def fin_body(i, carry):
        rs = pl.multiple_of(i * ch, ch)
        y_ref[pl.ds(rs, ch), :] = finish(o_f[pl.ds(rs, ch), :] + o_b[pl.ds(rs, ch), :], g_ref[pl.ds(rs, ch), :])
        return carry

    lax.fori_loop(0, n_lat_ch, fin_body, 0)
    if need_ctx:
        for n in range(n_ctx_ch):
            rs = slice(n * ch, (n + 1) * ch)
            yc_ref[rs, :] = finish(oc_f[rs, :] + oc_b[rs, :], gc_ref[rs, :])


def _rope_tables(seq, head_dim, width, dist):
    half = head_dim // 2
    nf = half // 2
    assert nf == dist
    inv = 1.0 / (ROPE_THETA ** (np.arange(nf, dtype=np.float32) / nf))
    t = np.arange(seq)
    rows = (t // GRID_W).astype(np.float32)[:, None] * inv[None, :]
    cols = (t % GRID_W).astype(np.float32)[:, None] * inv[None, :]
    cos = np.concatenate([np.cos(rows), np.cos(rows), np.cos(cols), np.cos(cols)], axis=-1)
    sin = np.concatenate([-np.sin(rows), np.sin(rows), -np.sin(cols), np.sin(cols)], axis=-1)
    reps = width // head_dim
    return (jnp.asarray(np.tile(cos, (1, reps)), F32), jnp.asarray(np.tile(sin, (1, reps)), F32))


def _ret_mixer(p_all, log_decay, batch, seq, n_ctx, need_ctx):
    n_lat = batch * seq
    cb = n_lat // n_ctx
    w = 256
    log_gamma = jnp.log1p(-jnp.exp(log_decay.astype(F32)))
    cos, sin = _rope_tables(seq, RET_DK, w, 16)
    lat = lambda c: pl.BlockSpec((seq, w), lambda b, c=c: (b, c))
    ctx = lambda c: pl.BlockSpec((n_ctx, w), lambda b, c=c: (cb + b, c))
    whole = pl.BlockSpec((seq, w), lambda b: (0, 0))
    out_shape = [jax.ShapeDtypeStruct((n_lat, w), BF16)]
    out_specs = [pl.BlockSpec((seq, w), lambda b: (b, 0))]
    if need_ctx:
        out_shape.append(jax.ShapeDtypeStruct((batch * n_ctx, w), BF16))
        out_specs.append(pl.BlockSpec((n_ctx, w), lambda b: (b, 0)))
    return pl.pallas_call(
        functools.partial(_ret_kernel, need_ctx, seq, n_ctx),
        grid=(batch,),
        in_specs=[pl.BlockSpec(memory_space=pltpu.SMEM),
                  lat(RET_Q), lat(RET_K), lat(RET_V), lat(RET_G),
                  ctx(RET_Q), ctx(RET_K), ctx(RET_V), ctx(RET_G), whole, whole],
        out_specs=out_specs,
        out_shape=out_shape,
        scratch_shapes=[pltpu.VMEM((seq, w), F32), pltpu.VMEM((seq, w), F32),
                        pltpu.VMEM((seq, w), F32), pltpu.VMEM((seq, w), F32),
                        pltpu.VMEM((n_ctx, w), F32), pltpu.VMEM((n_ctx, w), F32),
                        pltpu.VMEM((2 * RET_HEADS, RET_CHUNK, RET_CHUNK), F32),
                        pltpu.VMEM((2 * RET_HEADS, RET_CHUNK, RET_DK), F32),
                        pltpu.VMEM((2 * RET_HEADS, RET_CHUNK, RET_DK), F32)],
        compiler_params=_params("parallel"),
        name="retention_mixer",
    )(log_gamma, p_all, p_all, p_all, p_all, p_all, p_all, p_all, p_all, cos, sin)


def _swap_matrix(width, dist):
    i = np.arange(width)
    partner = np.where(i % (2 * dist) < dist, i + dist, i - dist)
    p = np.zeros((width, width), np.float32)
    p[partner, i] = 1.0
    return jnp.asarray(p, BF16)


def _block_ones(width, block):
    i = np.arange(width)
    return jnp.asarray((i[:, None] // block == i[None, :] // block).astype(np.float32), BF16)


def _sumsq_blocks(xf, bd_ref):
    sq = xf * xf
    hi = sq.astype(BF16)
    lo = (sq - hi.astype(F32)).astype(BF16)
    bd = bd_ref[...]
    return _dot(hi, bd) + _dot(lo, bd)


def _rope_mxu(x, cos, sin, perm_ref):
    return x.astype(F32) * cos + _dot(x, perm_ref[...]) * sin


def _softmax_pv(qm, k, v_ones):
    s = _dot_nt(qm, k)
    p = jnp.exp2((s - jnp.max(s, axis=-1, keepdims=True)).astype(BF16))
    oe = _dot(p, v_ones)
    return oe[:, :128] / oe[:, 128:]


def _lane_masks(n_parts):
    lane = lax.broadcasted_iota(jnp.int32, (1, 128), 1)
    return [jnp.where(lane // (128 // n_parts) == i, 1.0, 0.0).astype(BF16) for i in range(n_parts)]


def _gqa_kernel(n_lat_k, n_ctx, *refs):
    if n_lat_k:
        (q_ref, k_ref, v_ref, kc_ref, vc_ref, qa_ref, qb_ref, ka_ref, kb_ref, kg_ref, pq_ref, pk_ref,
         bdq_ref, bdk_ref, y_ref, kp, vx) = refs
    else:
        q_ref, kc_ref, vc_ref, qa_ref, kg_ref, bdq_ref, bdk_ref, y_ref, kp, vx = refs
    dim = GQA_DIM
    nk = n_lat_k + n_ctx

    def inv_rms(xf, bd_ref):
        return lax.rsqrt(_sumsq_blocks(xf, bd_ref) * (1.0 / dim) + EPS)

    @pl.when(pl.program_id(1) == 0)
    def _():
        def put(rs, kn, v):
            ones = jnp.ones(v.shape, BF16)
            kp[0, rs, :] = kn.astype(BF16)
            kp[1, rs, :] = pltpu.roll(kn, dim, 1).astype(BF16)
            vx[0, rs, :] = jnp.concatenate([v, ones], axis=-1)
            vx[1, rs, :] = jnp.concatenate([pltpu.roll(v.astype(F32), dim, 1).astype(BF16), ones], axis=-1)

        kc = kc_ref[...]
        kcf = kc.astype(F32)
        put(slice(n_lat_k, nk), kcf * kg_ref[...] * inv_rms(kcf, bdk_ref), vc_ref[...])
        ck = 512
        for c in range(n_lat_k // ck):
            rs = slice(c * ck, (c + 1) * ck)
            k = k_ref[rs, :]
            put(rs, _rope_mxu(k, ka_ref[rs, :], kb_ref[rs, :], pk_ref) * inv_rms(k.astype(F32), bdk_ref), v_ref[rs, :])

    q = q_ref[...]
    qf = q.astype(F32)
    qn = _rope_mxu(q, qa_ref[...], qb_ref[...], pq_ref) if n_lat_k else qf * qa_ref[...]
    qn = (qn * inv_rms(qf, bdq_ref)).astype(BF16)
    low, high = _lane_masks(2)
    outs = []
    for g in range(GQA_KV_HEADS):
        qv = qn[:, g * 128:(g + 1) * 128]
        o_low = _softmax_pv(qv * low, kp[g], vx[g])
        o_high = _softmax_pv(qv * high, kp[1 - g], vx[1 - g])
        outs.append(jnp.where(low > 0, o_low, o_high))
    y_ref[...] = jnp.concatenate(outs, axis=-1).astype(BF16)


def _gqa_mixer(p_all, q_gain, k_gain, batch, seq, n_ctx, need_ctx):
    n_lat = batch * seq
    cb = n_lat // n_ctx
    dim = GQA_DIM
    qw = GQA_HEADS * dim
    kvw = GQA_KV_HEADS * dim
    assert qw == 256 and kvw == 128

    def swapped(g):
        return g.reshape(-1, 2, 16)[:, ::-1].reshape(1, -1)

    qg = jnp.tile(q_gain.astype(F32) * (dim ** -0.5 * LOG2E), GQA_HEADS).reshape(1, qw)
    kg = jnp.tile(k_gain.astype(F32), GQA_KV_HEADS).reshape(1, kvw)
    cq, sq = _rope_tables(seq, dim, qw, 16)
    ck, sk = _rope_tables(seq, dim, kvw, 16)
    qa, qb = cq * qg, sq * swapped(qg)
    ka, kb = ck * kg, sk * swapped(kg)
    pq, pk = _swap_matrix(qw, 16), _swap_matrix(kvw, 16)
    bdq, bdk = _block_ones(qw, dim), _block_ones(kvw, dim)
    tq = 512
    nqb = seq // tq
    const = lambda shape: pl.BlockSpec(shape, lambda b, i: (0,) * len(shape))
    scratch = lambda nk: [pltpu.VMEM((2, nk, kvw), BF16), pltpu.VMEM((2, nk, 2 * kvw), BF16)]
    y_lat = pl.pallas_call(
        functools.partial(_gqa_kernel, seq, n_ctx),
        grid=(batch, nqb),
        in_specs=[pl.BlockSpec((tq, qw), lambda b, i: (b * nqb + i, GQA_Q)),
                  pl.BlockSpec((seq, kvw), lambda b, i: (b, 16)),
                  pl.BlockSpec((seq, kvw), lambda b, i: (b, 17)),
                  pl.BlockSpec((n_ctx, kvw), lambda b, i: (cb + b, 16)),
                  pl.BlockSpec((n_ctx, kvw), lambda b, i: (cb + b, 17)),
                  pl.BlockSpec((tq, qw), lambda b, i: (i, 0)),
                  pl.BlockSpec((tq, qw), lambda b, i: (i, 0)),
                  const((seq, kvw)), const((seq, kvw)), const((1, kvw)),
                  const((qw, qw)), const((kvw, kvw)), const((qw, qw)), const((kvw, kvw))],
        out_specs=pl.BlockSpec((tq, qw), lambda b, i: (b * nqb + i, 0)),
        out_shape=jax.ShapeDtypeStruct((n_lat, qw), BF16),
        scratch_shapes=scratch(seq + n_ctx),
        compiler_params=_params("parallel", "arbitrary"),
        name="gqa_mixer",
    )(p_all, p_all, p_all, p_all, p_all, qa, qb, ka, kb, kg, pq, pk, bdq, bdk)
    if not need_ctx:
        return y_lat, None
    y_ctx = pl.pallas_call(
        functools.partial(_gqa_kernel, 0, n_ctx),
        grid=(batch, 1),
        in_specs=[pl.BlockSpec((n_ctx, qw), lambda b, i: (cb + b, GQA_Q)),
                  pl.BlockSpec((n_ctx, kvw), lambda b, i: (cb + b, 16)),
                  pl.BlockSpec((n_ctx, kvw), lambda b, i: (cb + b, 17)),
                  const((1, qw)), const((1, kvw)), const((qw, qw)), const((kvw, kvw))],
        out_specs=pl.BlockSpec((n_ctx, qw), lambda b, i: (b, 0)),
        out_shape=jax.ShapeDtypeStruct((batch * n_ctx, qw), BF16),
        scratch_shapes=scratch(n_ctx),
        compiler_params=_params("parallel", "arbitrary"),
        name="gqa_mixer_ctx",
    )(p_all, p_all, p_all, qg, kg, bdq, bdk)
    return y_lat, y_ctx


def _diff_kernel(n_lat_k, n_ctx, lam_init, *refs):
    if n_lat_k:
        (q_ref, k_ref, v_ref, kc_ref, vc_ref, lp_ref, sg_ref, cq_ref, sq_ref, ck_ref, sk_ref, perm_ref, bd_ref,
         y_ref, kp, vx) = refs
    else:
        q_ref, kc_ref, vc_ref, lp_ref, sg_ref, bd_ref, y_ref, kp, vx = refs
    nk = n_lat_k + n_ctx
    n_pairs = DIFF_HEADS // 2

    @pl.when(pl.program_id(1) == 0)
    def _():
        def put(rs, k, v):
            kp[rs, :] = k
            ones = jnp.ones((v.shape[0], 128), BF16)
            for pr in range(n_pairs):
                vx[pr, rs, :] = jnp.concatenate([v[:, pr * 128:(pr + 1) * 128], ones], axis=-1)

        put(slice(n_lat_k, nk), kc_ref[...], vc_ref[...])
        ck = 512
        for c in range(n_lat_k // ck):
            rs = slice(c * ck, (c + 1) * ck)
            put(rs, _rope_mxu(k_ref[rs, :], ck_ref[rs, :], sk_ref[rs, :], perm_ref).astype(BF16), v_ref[rs, :])

    lp = lp_ref[...]
    lam = (jnp.exp(jnp.sum(lp[0:1, :] * lp[1:2, :], axis=-1, keepdims=True))
           - jnp.exp(jnp.sum(lp[2:3, :] * lp[3:4, :], axis=-1, keepdims=True)) + lam_init)

    q = q_ref[...]
    if n_lat_k:
        q = _rope_mxu(q, cq_ref[...], sq_ref[...], perm_ref).astype(BF16)
    quarter = _lane_masks(4)
    low = _lane_masks(2)[0]
    outs = []
    for pr in range(n_pairs):
        ps = slice(pr * 128, (pr + 1) * 128)
        qv = q[:, ps]
        kv = kp[:, ps]
        o_head = [_softmax_pv(qv * quarter[2 * hh], kv, vx[pr]) - lam * _softmax_pv(qv * quarter[2 * hh + 1], kv, vx[pr])
                  for hh in range(2)]
        outs.append(jnp.where(low > 0, o_head[0], o_head[1]))
    o = jnp.concatenate(outs, axis=-1)
    inv = lax.rsqrt(_sumsq_blocks(o, bd_ref) * (1.0 / DIFF_V_DIM) + EPS)
    y_ref[...] = (o * inv * sg_ref[...] * (1.0 - lam_init)).astype(BF16)


def _diff_mixer(p_all, lam_params, subln, layer_idx, batch, seq, n_ctx, need_ctx):
    n_lat = batch * seq
    cb = n_lat // n_ctx
    lam_init = 0.8 - 0.6 * math.exp(-0.3 * layer_idx)
    lp = jnp.zeros((8, 128), F32).at[:4, :DIFF_QK_DIM].set(lam_params.astype(F32))
    w = DIFF_HEADS * DIFF_V_DIM
    assert w == 256 and DIFF_HEADS * 2 * DIFF_QK_DIM == w
    sg = jnp.tile(subln.astype(F32), DIFF_HEADS).reshape(1, w)
    cq, sq = _rope_tables(seq, DIFF_QK_DIM, w, 8)
    perm = _swap_matrix(w, 8)
    bd = _block_ones(w, DIFF_V_DIM)
    tq = 512
    nqb = seq // tq
    const = lambda shape: pl.BlockSpec(shape, lambda b, i: (0,) * len(shape))
    scratch = lambda nk: [pltpu.VMEM((nk, w), BF16), pltpu.VMEM((DIFF_HEADS // 2, nk, 256), BF16)]
    y_lat = pl.pallas_call(
        functools.partial(_diff_kernel, seq, n_ctx, lam_init),
        grid=(batch, nqb),
        in_specs=[pl.BlockSpec((tq, w), lambda b, i: (b * nqb + i, DIFF_Q)),
                  pl.BlockSpec((seq, w), lambda b, i: (b, DIFF_K)),
                  pl.BlockSpec((seq, w), lambda b, i: (b, DIFF_V)),
                  pl.BlockSpec((n_ctx, w), lambda b, i: (cb + b, DIFF_K)),
                  pl.BlockSpec((n_ctx, w), lambda b, i: (cb + b, DIFF_V)),
                  const((8, 128)), const((1, w)),
                  pl.BlockSpec((tq, w), lambda b, i: (i, 0)),
                  pl.BlockSpec((tq, w), lambda b, i: (i, 0)),
                  const((seq, w)), const((seq, w)), const((w, w)), const((w, w))],
        out_specs=pl.BlockSpec((tq, w), lambda b, i: (b * nqb + i, 0)),
        out_shape=jax.ShapeDtypeStruct((n_lat, w), BF16),
        scratch_shapes=scratch(seq + n_ctx),
        compiler_params=_params("parallel", "arbitrary"),
        name="diff_mixer",
    )(p_all, p_all, p_all, p_all, p_all, lp, sg, cq, sq, cq, sq, perm, bd)
    if not need_ctx:
        return y_lat, None
    y_ctx = pl.pallas_call(
        functools.partial(_diff_kernel, 0, n_ctx, lam_init),
        grid=(batch, 1),
        in_specs=[pl.BlockSpec((n_ctx, w), lambda b, i: (cb + b, DIFF_Q)),
                  pl.BlockSpec((n_ctx, w), lambda b, i: (cb + b, DIFF_K)),
                  pl.BlockSpec((n_ctx, w), lambda b, i: (cb + b, DIFF_V)),
                  const((8, 128)), const((1, w)), const((w, w))],
        out_specs=pl.BlockSpec((n_ctx, w), lambda b, i: (b, 0)),
        out_shape=jax.ShapeDtypeStruct((batch * n_ctx, w), BF16),
        scratch_shapes=scratch(n_ctx),
        compiler_params=_params("parallel", "arbitrary"),
        name="diff_mixer_ctx",
    )(p_all, p_all, p_all, lp, sg, bd)
    return y_lat, y_ctx


def _merge_kernel(nlat_blk, has_ctx, x_ref, *refs):
    g_refs = refs[:N_BRANCH]
    y_refs = refs[N_BRANCH:2 * N_BRANCH]
    refs = refs[2 * N_BRANCH:]
    if has_ctx:
        yc_refs, refs = refs[:N_BRANCH], refs[N_BRANCH:]
        is_ctx = pl.program_id(0) >= nlat_blk
    wb_ref, wo_ref, mod_ref, gn_ref, wr_ref, xo_ref, m_ref, lg_ref = refs
    acc = None
    for n in range(N_BRANCH):
        y = y_refs[n][...]
        if has_ctx:
            y = jnp.where(is_ctx, yc_refs[n][...], y)
        term = g_refs[n][...].astype(F32) * _dot(y, wb_ref[n])
        acc = term if acc is None else acc + term
    y = _dot(acc.astype(BF16), wo_ref[...])
    x = x_ref[...] + mod_ref[2:3, :] * y
    xo_ref[...] = x
    xn = x * lax.rsqrt(jnp.mean(x * x, axis=-1, keepdims=True) + EPS) * gn_ref[...]
    m = xn * (1.0 + mod_ref[4:5, :]) + mod_ref[3:4, :]
    m_ref[...] = m
    m_hi = m.astype(BF16)
    m_lo = (m - m_hi.astype(F32)).astype(BF16)
    w = wr_ref[...]
    w_hi = w.astype(BF16)
    w_lo = (w - w_hi.astype(F32)).astype(BF16)
    lg_ref[...] = _dot_nt(w_hi, m_hi) + (_dot_nt(w_hi, m_lo) + _dot_nt(w_lo, m_hi))


def _merge(x_all, p_all, ys_lat, ys_ctx, wb_bf, wo_bf, modtab, gn, wr_t, tm, n_rows, n_lat, seq, batch):
    d = D_MODEL
    midx = _mod_index(tm, n_lat, seq, batch)
    nlat_blk = n_lat // tm
    has_ctx = ys_ctx is not None
    gate = lambda n: pl.BlockSpec((tm, d), lambda i, n=n: (i, MIX_COLS // d + n))
    y_specs = [pl.BlockSpec((tm, BRANCH_W), lambda i: (jnp.minimum(i, nlat_blk - 1), 0))] * N_BRANCH
    ys = list(ys_lat)
    if has_ctx:
        y_specs += [pl.BlockSpec((tm, BRANCH_W), lambda i: (jnp.maximum(i - nlat_blk, 0), 0))] * N_BRANCH
        ys += list(ys_ctx)
    return pl.pallas_call(
        functools.partial(_merge_kernel, nlat_blk, has_ctx),
        grid=(n_rows // tm,),
        in_specs=[pl.BlockSpec((tm, d), lambda i: (i, 0)),
                  gate(0), gate(1), gate(2), gate(3), *y_specs,
                  pl.BlockSpec((N_BRANCH, BRANCH_W, d), lambda i: (0, 0, 0)),
                  pl.BlockSpec((d, d), lambda i: (0, 0)),
                  pl.BlockSpec((None, 8, d), lambda i: (midx(i), 0, 0)),
                  pl.BlockSpec((1, d), lambda i: (0, 0)),
                  pl.BlockSpec((N_EXPERTS, d), lambda i: (0, 0))],
        out_specs=[pl.BlockSpec((tm, d), lambda i: (i, 0)),
                   pl.BlockSpec((tm, d), lambda i: (i, 0)),
                   pl.BlockSpec((N_EXPERTS, tm), lambda i: (0, i))],
        out_shape=[jax.ShapeDtypeStruct((n_rows, d), F32),
                   jax.ShapeDtypeStruct((n_rows, d), F32),
                   jax.ShapeDtypeStruct((N_EXPERTS, n_rows), F32)],
        compiler_params=_params("parallel"),
        name="merge_norm_route",
    )(x_all, p_all, p_all, p_all, p_all, *ys, wb_bf, wo_bf, modtab, gn.reshape(1, d), wr_t)


def _route_kernel(lg_ref, b_ref, idx_ref, w_ref):
    s = _sigmoid(lg_ref[...])
    sel = s + b_ref[...]
    row = lambda a, e: a[e:e + 1, :]
    gsz = EXPERTS_PER_GROUP
    g_idx = None
    best = None
    for g in range(N_GROUPS):
        v = [row(sel, g * gsz + i) for i in range(gsz)]
        score = None
        for i in range(gsz):
            for j in range(i + 1, gsz):
                pair = v[i] + v[j]
                score = pair if score is None else jnp.maximum(score, pair)
        if g == 0:
            best, g_idx = score, jnp.zeros(score.shape, jnp.int32)
        else:
            better = score > best
            best = jnp.where(better, score, best)
            g_idx = jnp.where(better, g, g_idx)

    def in_group(a, i):
        out = row(a, i)
        for g in range(1, N_GROUPS):
            out = jnp.where(g_idx == g, row(a, g * gsz + i), out)
        return out

    v = [in_group(sel, i) for i in range(gsz)]
    sv = [in_group(s, i) for i in range(gsz)]

    def arg_first_max(vals):
        bv, bi = vals[0], jnp.zeros(vals[0].shape, jnp.int32)
        for i in range(1, gsz):
            better = vals[i] > bv
            bv = jnp.where(better, vals[i], bv)
            bi = jnp.where(better, i, bi)
        return bi

    i1 = arg_first_max(v)
    i2 = arg_first_max([jnp.where(i1 == i, -jnp.inf, v[i]) for i in range(gsz)])

    def pick(vals, idx):
        out = vals[0]
        for i in range(1, gsz):
            out = jnp.where(idx == i, vals[i], out)
        return out

    w1 = pick(sv, i1)
    w2 = pick(sv, i2)
    tot = w1 + w2
    idx_ref[0:1, :] = g_idx * gsz + i1
    idx_ref[1:2, :] = g_idx * gsz + i2
    w_ref[0:1, :] = w1 / tot
    w_ref[1:2, :] = w2 / tot


def _route(logits_t, router_bias):
    e, n = logits_t.shape
    tn = math.gcd(n, 2048)
    return pl.pallas_call(
        _route_kernel,
        grid=(n // tn,),
        in_specs=[pl.BlockSpec((e, tn), lambda i: (0, i)),
                  pl.BlockSpec((e, 1), lambda i: (0, 0))],
        out_specs=[pl.BlockSpec((TOP_K, tn), lambda i: (0, i)),
                   pl.BlockSpec((TOP_K, tn), lambda i: (0, i))],
        out_shape=[jax.ShapeDtypeStruct((TOP_K, n), jnp.int32),
                   jax.ShapeDtypeStruct((TOP_K, n), F32)],
        compiler_params=_params("parallel"),
        name="route_top2",
    )(logits_t, router_bias.astype(F32).reshape(e, 1))


def _dispatch_kernel(n_tok_steps, pos_ref, m_ref, xs_out, zrow, sem):
    i = pl.program_id(0)
    tm = m_ref.shape[0]
    n = TOP_K * tm

    def wait():
        pltpu.make_async_copy(xs_out.at[pl.ds(0, n)], xs_out.at[pl.ds(0, n)], sem).wait()

    @pl.when(i < n_tok_steps)
    def _():
        for j in range(n):
            dst = pos_ref[i, j]
            pltpu.make_async_copy(m_ref.at[pl.ds(j % tm, 1)], xs_out.at[pl.ds(dst, 1)], sem).start(priority=j % 2)
        wait()

    @pl.when(i >= n_tok_steps)
    def _():
        zrow[...] = jnp.zeros(zrow.shape, zrow.dtype)
        for j in range(n):
            dst = pos_ref[i, j]
            pltpu.make_async_copy(zrow.at[pl.ds(0, 1)], xs_out.at[pl.ds(dst, 1)], sem).start(priority=j % 2)
        wait()


def _dispatch(m_all, pos, pad_slots, n_slots):
    n_rows, d = m_all.shape
    n_tok_steps, n = pos.shape
    tm = n // TOP_K
    steps = jnp.concatenate([pos, pad_slots], axis=0)
    grid_spec = pltpu.PrefetchScalarGridSpec(
        num_scalar_prefetch=1,
        grid=(steps.shape[0],),
        in_specs=[pl.BlockSpec((tm, d), lambda i, pos: (jnp.minimum(i, n_tok_steps - 1), 0))],
        out_specs=pl.BlockSpec(memory_space=pl.ANY),
        scratch_shapes=[pltpu.VMEM((8, d), F32), pltpu.SemaphoreType.DMA(())])
    return pl.pallas_call(
        functools.partial(_dispatch_kernel, n_tok_steps),
        grid_spec=grid_spec,
        out_shape=jax.ShapeDtypeStruct((n_slots, d), F32),
        compiler_params=_params("arbitrary"),
        name="moe_dispatch",
    )(steps, m_all)


def _expert_kernel(be_ref, x_ref, wg_ref, wu_ref, wd_ref, y_ref):
    x = x_ref[...].astype(BF16)
    a = _dot(x, wg_ref[...])
    h = a * _sigmoid(a) * _dot(x, wu_ref[...])
    y_ref[...] = _dot(h.astype(BF16), wd_ref[...])


def _experts(x_slots, blk_e, blk, wg_bf, wu_bf, wd_bf):
    n_slots, d = x_slots.shape
    grid_spec = pltpu.PrefetchScalarGridSpec(
        num_scalar_prefetch=1,
        grid=(n_slots // blk,),
        in_specs=[pl.BlockSpec((blk, d), lambda i, be: (i, 0)),
                  pl.BlockSpec((None, d, D_EXPERT), lambda i, be: (be[i], 0, 0)),
                  pl.BlockSpec((None, d, D_EXPERT), lambda i, be: (be[i], 0, 0)),
                  pl.BlockSpec((None, D_EXPERT, d), lambda i, be: (be[i], 0, 0))],
        out_specs=pl.BlockSpec((blk, d), lambda i, be: (i, 0)))
    return pl.pallas_call(
        _expert_kernel,
        grid_spec=grid_spec,
        out_shape=jax.ShapeDtypeStruct((n_slots, d), F32),
        compiler_params=_params("arbitrary"),
        name="expert_ffn",
    )(blk_e, x_slots, wg_bf, wu_bf, wd_bf)


def _combine_kernel(final, pos_ref, y_hbm, x_ref, w_ref, mod_ref, gf_ref, o_ref, ybuf, sem):
    i = pl.program_id(0)
    last = pl.num_programs(0) - 1
    slot = i % 2
    tm = x_ref.shape[0]
    n = TOP_K * tm

    def issue(step, dst_slot):
        for j in range(n):
            src = pos_ref[step, j]
            pltpu.make_async_copy(y_hbm.at[pl.ds(src, 1)], ybuf.at[dst_slot, pl.ds(j, 1)],
                                  sem.at[dst_slot]).start(priority=j % 2)

    def wait(dst_slot):
        pltpu.make_async_copy(y_hbm.at[pl.ds(0, n)], ybuf.at[dst_slot], sem.at[dst_slot]).wait()

    @pl.when(i == 0)
    def _():
        issue(0, 0)

    wait(slot)
    issue(jnp.minimum(i + 1, last), 1 - slot)
    w = w_ref[...]
    f = w[:, 0:1] * ybuf[slot, 0:tm, :] + w[:, 1:2] * ybuf[slot, tm:n, :]
    x = x_ref[...] + mod_ref[5:6, :] * f
    if final:
        x = x * lax.rsqrt(jnp.mean(x * x, axis=-1, keepdims=True) + EPS) * gf_ref[...]
    o_ref[...] = x

    @pl.when(i == last)
    def _():
        wait(1 - slot)


def _combine(x_rows, y_slots, pos, w_tok, modtab, g_final, final, n_lat, seq, batch):
    n_rows, d = x_rows.shape
    n_steps, n = pos.shape
    tm = n // TOP_K
    midx = _mod_index(tm, n_lat, seq, batch)
    grid_spec = pltpu.PrefetchScalarGridSpec(
        num_scalar_prefetch=1,
        grid=(n_steps,),
        in_specs=[pl.BlockSpec(memory_space=pl.ANY),
                  pl.BlockSpec((tm, d), lambda i, pos: (i, 0)),
                  pl.BlockSpec((tm, TOP_K), lambda i, pos: (i, 0)),
                  pl.BlockSpec((None, 8, d), lambda i, pos: (midx(i), 0, 0)),
                  pl.BlockSpec((1, d), lambda i, pos: (0, 0))],
        out_specs=pl.BlockSpec((tm, d), lambda i, pos: (i, 0)),
        scratch_shapes=[pltpu.VMEM((2, n, d), F32), pltpu.SemaphoreType.DMA((2,))])
    return pl.pallas_call(
        functools.partial(_combine_kernel, final),
        grid_spec=grid_spec,
        out_shape=jax.ShapeDtypeStruct((n_rows, d), F32),
        compiler_params=_params("arbitrary"),
        name="moe_combine",
    )(pos, y_slots, x_rows, w_tok, modtab, g_final.astype(F32).reshape(1, d))


def _dispatch_plan(idx2, blk, tm):
    n = idx2.shape[1]
    a = TOP_K * n
    flat_e = idx2.T.reshape(a)
    onehot = (flat_e[:, None] == jnp.arange(N_EXPERTS, dtype=jnp.int32)[None, :]).astype(jnp.int32)
    csum = jnp.cumsum(onehot, axis=0)
    rank = jnp.sum(onehot * csum, axis=1) - 1
    counts = csum[-1]
    padded = (counts + blk - 1) // blk * blk
    pad_end = jnp.cumsum(padded)
    pad_start = pad_end - padded
    dest = (pad_start[flat_e] + rank).astype(jnp.int32)
    n_blocks = a // blk + N_EXPERTS
    blk_e = jnp.minimum(jnp.searchsorted(pad_end, jnp.arange(n_blocks, dtype=jnp.int32) * blk, side='right'),
                        N_EXPERTS - 1).astype(jnp.int32)
    pos = dest.reshape(n // tm, tm, TOP_K).transpose(0, 2, 1).reshape(n // tm, TOP_K * tm)
    n_slots = n_blocks * blk
    starts = jnp.concatenate([pad_start + counts, pad_end[-1:]])
    lens = jnp.concatenate([padded - counts, n_slots - pad_end[-1:]])
    ends = jnp.cumsum(lens)
    j = jnp.arange(N_EXPERTS * blk, dtype=jnp.int32)
    seg = jnp.searchsorted(ends, j, side='right')
    pad_slots = (starts[seg] + j - (ends[seg] - lens[seg])).astype(jnp.int32)
    return pos, pad_slots.reshape(-1, TOP_K * tm), blk_e, n_slots


def kernel(x, c, ctx, c_ctx, w_mod, b_mod, g_norm1, g_norm2, w_in, na_rpb, ret_log_decay, gqa_q_gain, gqa_k_gain,
           diff_lambda, diff_subln, w_branch, w_out, w_router, router_bias, w_gate_e, w_up_e, w_down_e, g_final):
    batch, seq, d = x.shape
    n_ctx = ctx.shape[1]
    n_lat = batch * seq
    n_ctx_tot = batch * n_ctx
    n_tot = n_lat + n_ctx_tot
    tm = 1024 if n_ctx_tot % 1024 == 0 else 512

    x_all = jnp.concatenate([x.reshape(n_lat, d), ctx.reshape(n_ctx_tot, d)], axis=0)

    nrow = -(-(batch + 1) // 8) * 8
    cc = jnp.zeros((nrow, d), F32).at[:batch].set(c).at[batch].set(c_ctx)
    mod = _modulation(cc, w_mod, b_mod)
    modtab = jnp.pad(mod.reshape(DEPTH, nrow, 6, d), ((0, 0), (0, 0), (0, 2), (0, 0)))

    wr_t = w_router.astype(F32).T
    out = None
    for layer in range(DEPTH):
        need_ctx = layer < DEPTH - 1
        n_rows = n_tot if need_ctx else n_lat
        p_all = _project(x_all, g_norm1[layer], modtab[layer], w_in[layer].astype(BF16), tm, n_lat, seq, batch)
        ya = _na_mixer(p_all, na_rpb[layer], batch, seq, n_ctx, need_ctx)
        yb = _ret_mixer(p_all, ret_log_decay[layer], batch, seq, n_ctx, need_ctx)
        yc = _gqa_mixer(p_all, gqa_q_gain[layer], gqa_k_gain[layer], batch, seq, n_ctx, need_ctx)
        yd = _diff_mixer(p_all, diff_lambda[layer], diff_subln[layer], layer, batch, seq, n_ctx, need_ctx)
        ys_lat = [ya[0], yb[0], yc[0], yd[0]]
        ys_ctx = [ya[1], yb[1], yc[1], yd[1]] if need_ctx else None
        x_mid, m_all, logits_t = _merge(x_all, p_all, ys_lat, ys_ctx, w_branch[layer].astype(BF16),
                                        w_out[layer].astype(BF16), modtab[layer], g_norm2[layer], wr_t, MERGE_TM,
                                        n_rows, n_lat, seq, batch)
        idx2, w2 = _route(logits_t, router_bias)
        pos, pad_slots, blk_e, n_slots = _dispatch_plan(idx2, MOE_BLK, CMB_TM)
        x_slots = _dispatch(m_all, pos, pad_slots, n_slots)
        y_slots = _experts(x_slots, blk_e, MOE_BLK, w_gate_e[layer].astype(BF16), w_up_e[layer].astype(BF16),
                           w_down_e[layer].astype(BF16))
        out = _combine(x_mid, y_slots, pos, w2.T, modtab[layer], g_final, not need_ctx, n_lat, seq, batch)
        x_all = out
    return out.reshape(batch, seq, d)
```

```python
import functools
import math

import numpy as np
import jax
import jax.numpy as jnp
from jax import lax
from jax.experimental import pallas as pl
from jax.experimental.pallas import tpu as pltpu

F32 = jnp.float32
BF16 = jnp.bfloat16

D_MODEL = 1024
DEPTH = 2
GRID_W = 64
NA_HEADS = 4
NA_DIM = 64
NA_WIN_H = 8
NA_WIN_W = 16
RET_HEADS = 4
RET_DK = 64
RET_CHUNK = 128
GQA_HEADS = 4
GQA_KV_HEADS = 2
GQA_DIM = 64
DIFF_HEADS = 4
DIFF_QK_DIM = 32
DIFF_V_DIM = 64
N_BRANCH = 4
BRANCH_W = 256
ROPE_THETA = 10000.0
EPS = 1e-6
NEG_INF = -1e30
N_EXPERTS = 16
N_GROUPS = 4
EXPERTS_PER_GROUP = 4
TOP_K = 2
D_EXPERT = 512

MIX_COLS = 3072
IN_COLS = MIX_COLS + N_BRANCH * D_MODEL
NA_Q, NA_K, NA_V = 0, 1, 2
RET_Q, RET_K, RET_V, RET_G = 3, 4, 5, 6
GQA_Q = 7
DIFF_Q, DIFF_K, DIFF_V = 9, 10, 11

LOG2E = math.log2(math.e)
NA_GROUP_ROWS = 4
NA_BAND_ROWS = 12

VMEM_LIMIT = 56 * 1024 * 1024
MOE_BLK = 512
CMB_TM = 512
MERGE_TM = 512


ROW_TILE = D_MODEL // 128


def _store_tile_rows(ref, x):
    n = x.shape[0]
    for j in range(ROW_TILE):
        ref[pl.ds(j, n, stride=ROW_TILE), :] = x[:, j * 128:(j + 1) * 128]


def _load_tile_rows(ref, start, n):
    return jnp.concatenate([ref[pl.ds(start + j, n, stride=ROW_TILE), :] for j in range(ROW_TILE)], axis=-1)


def _dot(a, b):
    return jnp.dot(a, b, preferred_element_type=F32)


def _dot_nt(a, b):
    return lax.dot_general(a, b, (((1,), (1,)), ((), ())), preferred_element_type=F32)


def _dot_tn(a, b):
    return lax.dot_general(a, b, (((0,), (0,)), ((), ())), preferred_element_type=F32)


def _sigmoid(x):
    return 1.0 / (1.0 + jnp.exp(-x))


def _params(*sem):
    return pltpu.CompilerParams(dimension_semantics=sem, vmem_limit_bytes=VMEM_LIMIT)


def _swap_halves(x, dist):
    pieces = []
    for j in range(x.shape[-1] // 128):
        xs = x[:, j * 128:(j + 1) * 128]
        lane = lax.broadcasted_iota(jnp.int32, xs.shape, 1)
        up = pltpu.roll(xs, 128 - dist, 1)
        dn = pltpu.roll(xs, dist, 1)
        pieces.append(jnp.where((lane % (2 * dist)) < dist, up, dn))
    return pieces[0] if len(pieces) == 1 else jnp.concatenate(pieces, axis=-1)


def _rope(x, cos, sin, dist):
    return x * cos + _swap_halves(x, dist) * sin


def _mod_kernel(c_ref, w_ref, b_ref, o_ref):
    c = c_ref[...]
    a = (c * _sigmoid(c)).astype(BF16)
    o_ref[...] = _dot(a, w_ref[...].astype(BF16)) + b_ref[...]


def _modulation(cc, w_mod, b_mod):
    nrow = cc.shape[0]
    depth, d, n6 = w_mod.shape
    tn = 1536
    return pl.pallas_call(
        _mod_kernel,
        grid=(depth, n6 // tn),
        in_specs=[pl.BlockSpec((nrow, d), lambda l, j: (0, 0)),
                  pl.BlockSpec((None, d, tn), lambda l, j: (l, 0, j)),
                  pl.BlockSpec((None, 1, tn), lambda l, j: (l, 0, j))],
        out_specs=pl.BlockSpec((None, nrow, tn), lambda l, j: (l, 0, j)),
        out_shape=jax.ShapeDtypeStruct((depth, nrow, n6), F32),
        compiler_params=_params("parallel", "arbitrary"),
        name="modulation",
    )(cc, w_mod, b_mod.reshape(depth, 1, n6))


def _mix_col_scale():
    cs = np.ones((1, MIX_COLS), np.float32)
    cs[0, NA_Q * 256:(NA_Q + 1) * 256] = NA_DIM ** -0.5 * LOG2E
    cs[0, DIFF_Q * 256:(DIFF_Q + 1) * 256] = DIFF_QK_DIM ** -0.5 * LOG2E
    cs[0, RET_K * 256:(RET_K + 1) * 256] = RET_DK ** -0.5
    return jnp.asarray(cs)


def _proj_kernel(x_ref, g_ref, mod_ref, w_ref, cs_ref, o_ref, h_ref):
    j = pl.program_id(1)

    @pl.when(j == 0)
    def _():
        x = x_ref[...]
        y = x * lax.rsqrt(jnp.mean(x * x, axis=-1, keepdims=True) + EPS) * g_ref[...]
        h_ref[...] = (y * (1.0 + mod_ref[1:2, :]) + mod_ref[0:1, :]).astype(BF16)

    acc = _dot(h_ref[...], w_ref[...])
    gate = 0.5 * jnp.tanh(0.5 * acc) + 0.5
    o_ref[...] = jnp.where(j < MIX_COLS // D_MODEL, acc * cs_ref[...], gate).astype(BF16)


def _mod_index(tm, n_lat, seq, batch):
    nlat_blk = n_lat // tm
    bpb = seq // tm

    def index(i):
        return jnp.where(i < nlat_blk, i // bpb, batch)
    return index


def _project(x_all, g, modtab, w_bf, tm, n_lat, seq, batch):
    n_tot, d = x_all.shape
    ncol = w_bf.shape[1]
    tn = D_MODEL
    n_mix = MIX_COLS // tn
    midx = _mod_index(tm, n_lat, seq, batch)
    return pl.pallas_call(
        _proj_kernel,
        grid=(n_tot // tm, ncol // tn),
        in_specs=[pl.BlockSpec((tm, d), lambda i, j: (i, 0)),
                  pl.BlockSpec((1, d), lambda i, j: (0, 0)),
                  pl.BlockSpec((None, 8, d), lambda i, j: (midx(i), 0, 0)),
                  pl.BlockSpec((d, tn), lambda i, j: (0, j)),
                  pl.BlockSpec((1, tn), lambda i, j: (0, jnp.minimum(j, n_mix - 1)))],
        out_specs=pl.BlockSpec((tm, tn), lambda i, j: (i, j)),
        out_shape=jax.ShapeDtypeStruct((n_tot, ncol), BF16),
        scratch_shapes=[pltpu.VMEM((tm, d), BF16)],
        compiler_params=_params("parallel", "arbitrary"),
        name="norm_project",
    )(x_all, g.reshape(1, d), modtab, w_bf, _mix_col_scale())


def _na_kernel(need_ctx, rows, q_ref, k_ref, v_ref, qc_ref, kc_ref, vc_ref, tb_ref, y_ref, *rest):
    gq = NA_GROUP_ROWS * GRID_W
    band = NA_BAND_ROWS * GRID_W
    n_groups = rows // NA_GROUP_ROWS

    def group_body(g, carry):
        u = jnp.clip(g * NA_GROUP_ROWS - NA_WIN_H // 2, 0, rows - NA_BAND_ROWS)
        typ = jnp.where(g == 0, 0, jnp.where(g == n_groups - 1, 2, 1))
        qs = pl.multiple_of(g * gq, gq)
        ks = pl.multiple_of(u * GRID_W, GRID_W)
        for h in range(NA_HEADS):
            hs = slice(h * NA_DIM, (h + 1) * NA_DIM)
            q = q_ref[pl.ds(qs, gq), hs]
            vc = vc_ref[:, hs]
            s_w = _dot_nt(q, k_ref[pl.ds(ks, band), hs]) + tb_ref[h, typ]
            s_c = _dot_nt(q, kc_ref[:, hs])
            m = jnp.maximum(jnp.max(s_w, axis=-1, keepdims=True), jnp.max(s_c, axis=-1, keepdims=True))
            p_w = jnp.exp2(s_w - m)
            p_c = jnp.exp2(s_c - m)
            l = jnp.sum(p_w, axis=-1, keepdims=True) + jnp.sum(p_c, axis=-1, keepdims=True)
            o = (_dot(p_w.astype(BF16), v_ref[pl.ds(ks, band), hs]) + _dot(p_c.astype(BF16), vc)) / l
            y_ref[pl.ds(qs, gq), hs] = o.astype(BF16)
        return carry

    lax.fori_loop(0, n_groups, group_body, 0)

    if need_ctx:
        yc_ref = rest[0]
        for h in range(NA_HEADS):
            hs = slice(h * NA_DIM, (h + 1) * NA_DIM)
            s = _dot_nt(qc_ref[:, hs], kc_ref[:, hs])
            p = jnp.exp2(s - jnp.max(s, axis=-1, keepdims=True))
            l = jnp.sum(p, axis=-1, keepdims=True)
            yc_ref[:, hs] = (_dot(p.astype(BF16), vc_ref[:, hs]) / l).astype(BF16)


def _na_bias_table(rpb, rows):
    assert rows % NA_GROUP_ROWS == 0 and rows >= NA_BAND_ROWS + NA_GROUP_ROWS
    n_groups = rows // NA_GROUP_ROWS
    qc = np.arange(GRID_W)[:, None]
    kc = np.arange(GRID_W)[None, :]
    win_start = np.clip(qc - NA_WIN_W // 2, 0, GRID_W - NA_WIN_W)
    col_ok = (kc >= win_start) & (kc < win_start + NA_WIN_W)
    col_idx = np.clip(kc - qc, -(NA_WIN_W - 1), NA_WIN_W - 1) + NA_WIN_W - 1
    tiles = jnp.where(col_ok[None, None], rpb.astype(F32)[:, :, col_idx] * LOG2E, NEG_INF)
    masked = jnp.full((NA_HEADS, GRID_W, GRID_W), NEG_INF, F32)
    tables = []
    for g in (0, 1, n_groups - 1):
        u = int(np.clip(g * NA_GROUP_ROWS - NA_WIN_H // 2, 0, rows - NA_BAND_ROWS))
        per_row = []
        for a in range(NA_GROUP_ROWS):
            r = g * NA_GROUP_ROWS + a
            r0 = int(np.clip(r - NA_WIN_H // 2, 0, rows - NA_WIN_H))
            pieces = [tiles[:, kr - r + NA_WIN_H - 1] if r0 <= kr < r0 + NA_WIN_H else masked
                      for kr in range(u, u + NA_BAND_ROWS)]
            per_row.append(jnp.concatenate(pieces, axis=-1))
        tables.append(jnp.concatenate(per_row, axis=1))
    return jnp.stack(tables, axis=1)


def _na_mixer(p_all, rpb, batch, seq, n_ctx, need_ctx):
    n_lat = batch * seq
    rows = seq // GRID_W
    tb = _na_bias_table(rpb, rows)
    cb = n_lat // n_ctx
    w = 256
    lat = lambda c: pl.BlockSpec((seq, w), lambda b, c=c: (b, c))
    ctx = lambda c: pl.BlockSpec((n_ctx, w), lambda b, c=c: (cb + b, c))
    out_shape = [jax.ShapeDtypeStruct((n_lat, w), BF16)]
    out_specs = [pl.BlockSpec((seq, w), lambda b: (b, 0))]
    if need_ctx:
        out_shape.append(jax.ShapeDtypeStruct((batch * n_ctx, w), BF16))
        out_specs.append(pl.BlockSpec((n_ctx, w), lambda b: (b, 0)))
    return pl.pallas_call(
        functools.partial(_na_kernel, need_ctx, rows),
        grid=(batch,),
        in_specs=[lat(NA_Q), lat(NA_K), lat(NA_V), ctx(NA_Q), ctx(NA_K), ctx(NA_V),
                  pl.BlockSpec(tb.shape, lambda b: (0, 0, 0, 0))],
        out_specs=out_specs,
        out_shape=out_shape,
        compiler_params=_params("parallel"),
        name="na_mixer",
    )(p_all, p_all, p_all, p_all, p_all, p_all, tb)


def _ret_kernel(need_ctx, seq, n_ctx, lg_ref, q_ref, k_ref, v_ref, g_ref, qc_ref, kc_ref, vc_ref, gc_ref,
                cos_ref, sin_ref, y_ref, *rest):
    if need_ctx:
        yc_ref, qr, kr, o_f, o_b, oc_f, oc_b, dm, qd, kd = rest
    else:
        qr, kr, o_f, o_b, oc_f, oc_b, dm, qd, kd = rest
    ch = RET_CHUNK
    n_lat_ch = seq // ch
    n_ctx_ch = n_ctx // ch

    def prep(i, carry):
        rs = pl.multiple_of(i * ch, ch)
        c = cos_ref[pl.ds(rs, ch), :]
        s = sin_ref[pl.ds(rs, ch), :]
        qr[pl.ds(rs, ch), :] = _rope(q_ref[pl.ds(rs, ch), :].astype(F32), c, s, 16)
        kr[pl.ds(rs, ch), :] = _rope(k_ref[pl.ds(rs, ch), :].astype(F32), c, s, 16)
        return carry

    lax.fori_loop(0, n_lat_ch, prep, 0)

    pos = lax.broadcasted_iota(jnp.int32, (ch, RET_DK), 0).astype(F32)
    ri = lax.broadcasted_iota(jnp.int32, (ch, ch), 0)
    ci = lax.broadcasted_iota(jnp.int32, (ch, ch), 1)
    lag = (ri - ci).astype(F32)

    chains = [(h, dirn) for h in range(RET_HEADS) for dirn in range(2)]
    cdec = []
    for c, (h, dirn) in enumerate(chains):
        lg = lg_ref[dirn, h]
        if dirn == 0:
            keep = ri >= ci
            dm[c] = jnp.where(keep, jnp.exp(jnp.where(keep, lag, 0.0) * lg), 0.0)
            qd[c] = jnp.exp((pos + 1.0) * lg)
            kd[c] = jnp.exp((ch - 1.0 - pos) * lg)
        else:
            keep = ci >= ri
            dm[c] = jnp.where(keep, jnp.exp(jnp.where(keep, -lag, 0.0) * lg), 0.0)
            qd[c] = jnp.exp((ch - pos) * lg)
            kd[c] = jnp.exp(pos * lg)
        cdec.append(jnp.exp(ch * lg))

    def step(c, state, qi, ki, vi):
        inner = _dot_nt(qi.astype(BF16), ki.astype(BF16)) * dm[c]
        o = _dot(inner.astype(BF16), vi) + _dot((qi * qd[c]).astype(BF16), state.astype(BF16))
        state = state * cdec[c] + _dot_tn((ki * kd[c]).astype(BF16), vi)
        return state, o

    states = []
    for c, (h, dirn) in enumerate(chains):
        hs = slice(h * RET_DK, (h + 1) * RET_DK)
        state = jnp.zeros((RET_DK, RET_DK), F32)
        for n in (range(n_ctx_ch) if dirn == 0 else range(n_ctx_ch - 1, -1, -1)):
            rs = slice(n * ch, (n + 1) * ch)
            state, o = step(c, state, qc_ref[rs, hs].astype(F32), kc_ref[rs, hs].astype(F32), vc_ref[rs, hs])
            if need_ctx:
                (oc_f if dirn == 0 else oc_b)[rs, hs] = o
        states.append(state)

    def lat_body(i, states):
        new = []
        for c, (h, dirn) in enumerate(chains):
            hs = slice(h * RET_DK, (h + 1) * RET_DK)
            rs = pl.multiple_of((i if dirn == 0 else n_lat_ch - 1 - i) * ch, ch)
            state, o = step(c, states[c], qr[pl.ds(rs, ch), hs], kr[pl.ds(rs, ch), hs], v_ref[pl.ds(rs, ch), hs])
            (o_f if dirn == 0 else o_b)[pl.ds(rs, ch), hs] = o
            new.append(state)
        return tuple(new)

    lax.fori_loop(0, n_lat_ch, lat_body, tuple(states))

    def finish(o, g):
        outs = []
        for h in range(RET_HEADS):
            oh = o[:, h * RET_DK:(h + 1) * RET_DK]
            mu = jnp.mean(oh, axis=-1, keepdims=True)
            var = jnp.mean(jnp.square(oh - mu), axis=-1, keepdims=True)
            outs.append((oh - mu) * lax.rsqrt(var + EPS))
        on = jnp.concatenate(outs, axis=-1)
        g = g.astype(F32)
        return (g * _sigmoid(g) * on).astype(BF16)

    def fin_body(i, carry):
        rs = pl.multiple_of(i * ch, ch)
        y_ref[pl.ds(rs, ch), :] = finish(o_f[pl.ds(rs, ch), :] + o_b[pl.ds(rs, ch), :], g_ref[pl.ds(rs, ch), :])
        return carry

    lax.fori_loop(0, n_lat_ch, fin_body, 0)
    if need_ctx:
        for n in range(n_ctx_ch):
            rs = slice(n * ch, (n + 1) * ch)
            yc_ref[rs, :] = finish(oc_f[rs, :] + oc_b[rs, :], gc_ref[rs, :])


def _rope_tables(seq, head_dim, width, dist):
    half = head_dim // 2
    nf = half // 2
    assert nf == dist
    inv = 1.0 / (ROPE_THETA ** (np.arange(nf, dtype=np.float32) / nf))
    t = np.arange(seq)
    rows = (t // GRID_W).astype(np.float32)[:, None] * inv[None, :]
    cols = (t % GRID_W).astype(np.float32)[:, None] * inv[None, :]
    cos = np.concatenate([np.cos(rows), np.cos(rows), np.cos(cols), np.cos(cols)], axis=-1)
    sin = np.concatenate([-np.sin(rows), np.sin(rows), -np.sin(cols), np.sin(cols)], axis=-1)
    reps = width // head_dim
    return (jnp.asarray(np.tile(cos, (1, reps)), F32), jnp.asarray(np.tile(sin, (1, reps)), F32))


def _ret_mixer(p_all, log_decay, batch, seq, n_ctx, need_ctx):
    n_lat = batch * seq
    cb = n_lat // n_ctx
    w = 256
    log_gamma = jnp.log1p(-jnp.exp(log_decay.astype(F32)))
    cos, sin = _rope_tables(seq, RET_DK, w, 16)
    lat = lambda c: pl.BlockSpec((seq, w), lambda b, c=c: (b, c))
    ctx = lambda c: pl.BlockSpec((n_ctx, w), lambda b, c=c: (cb + b, c))
    whole = pl.BlockSpec((seq, w), lambda b: (0, 0))
    out_shape = [jax.ShapeDtypeStruct((n_lat, w), BF16)]
    out_specs = [pl.BlockSpec((seq, w), lambda b: (b, 0))]
    if need_ctx:
        out_shape.append(jax.ShapeDtypeStruct((batch * n_ctx, w), BF16))
        out_specs.append(pl.BlockSpec((n_ctx, w), lambda b: (b, 0)))
    return pl.pallas_call(
        functools.partial(_ret_kernel, need_ctx, seq, n_ctx),
        grid=(batch,),
        in_specs=[pl.BlockSpec(memory_space=pltpu.SMEM),
                  lat(RET_Q), lat(RET_K), lat(RET_V), lat(RET_G),
                  ctx(RET_Q), ctx(RET_K), ctx(RET_V), ctx(RET_G), whole, whole],
        out_specs=out_specs,
        out_shape=out_shape,
        scratch_shapes=[pltpu.VMEM((seq, w), F32), pltpu.VMEM((seq, w), F32),
                        pltpu.VMEM((seq, w), F32), pltpu.VMEM((seq, w), F32),
                        pltpu.VMEM((n_ctx, w), F32), pltpu.VMEM((n_ctx, w), F32),
                        pltpu.VMEM((2 * RET_HEADS, RET_CHUNK, RET_CHUNK), F32),
                        pltpu.VMEM((2 * RET_HEADS, RET_CHUNK, RET_DK), F32),
                        pltpu.VMEM((2 * RET_HEADS, RET_CHUNK, RET_DK), F32)],
        compiler_params=_params("parallel"),
        name="retention_mixer",
    )(log_gamma, p_all, p_all, p_all, p_all, p_all, p_all, p_all, p_all, cos, sin)


def _swap_matrix(width, dist):
    i = np.arange(width)
    partner = np.where(i % (2 * dist) < dist, i + dist, i - dist)
    p = np.zeros((width, width), np.float32)
    p[partner, i] = 1.0
    return jnp.asarray(p, BF16)


def _block_ones(width, block):
    i = np.arange(width)
    return jnp.asarray((i[:, None] // block == i[None, :] // block).astype(np.float32), BF16)


def _sumsq_blocks(xf, bd_ref):
    sq = xf * xf
    hi = sq.astype(BF16)
    lo = (sq - hi.astype(F32)).astype(BF16)
    bd = bd_ref[...]
    return _dot(hi, bd) + _dot(lo, bd)


def _rope_mxu(x, cos, sin, perm_ref):
    return x.astype(F32) * cos + _dot(x, perm_ref[...]) * sin


def _softmax_pv(qm, k, v_ones):
    s = _dot_nt(qm, k)
    p = jnp.exp2((s - jnp.max(s, axis=-1, keepdims=True)).astype(BF16))
    oe = _dot(p, v_ones)
    return oe[:, :128] / oe[:, 128:]


def _lane_masks(n_parts):
    lane = lax.broadcasted_iota(jnp.int32, (1, 128), 1)
    return [jnp.where(lane // (128 // n_parts) == i, 1.0, 0.0).astype(BF16) for i in range(n_parts)]


def _gqa_kernel(n_lat_k, n_ctx, *refs):
    if n_lat_k:
        (q_ref, k_ref, v_ref, kc_ref, vc_ref, qa_ref, qb_ref, ka_ref, kb_ref, kg_ref, pq_ref, pk_ref,
         bdq_ref, bdk_ref, y_ref, kp, vx) = refs
    else:
        q_ref, kc_ref, vc_ref, qa_ref, kg_ref, bdq_ref, bdk_ref, y_ref, kp, vx = refs
    dim = GQA_DIM
    nk = n_lat_k + n_ctx

    def inv_rms(xf, bd_ref):
        return lax.rsqrt(_sumsq_blocks(xf, bd_ref) * (1.0 / dim) + EPS)

    @pl.when(pl.program_id(1) == 0)
    def _():
        def put(rs, kn, v):
            ones = jnp.ones(v.shape, BF16)
            kp[0, rs, :] = kn.astype(BF16)
            kp[1, rs, :] = pltpu.roll(kn, dim, 1).astype(BF16)
            vx[0, rs, :] = jnp.concatenate([v, ones], axis=-1)
            vx[1, rs, :] = jnp.concatenate([pltpu.roll(v.astype(F32), dim, 1).astype(BF16), ones], axis=-1)

        kc = kc_ref[...]
        kcf = kc.astype(F32)
        put(slice(n_lat_k, nk), kcf * kg_ref[...] * inv_rms(kcf, bdk_ref), vc_ref[...])
        ck = 512
        for c in range(n_lat_k // ck):
            rs = slice(c * ck, (c + 1) * ck)
            k = k_ref[rs, :]
            put(rs, _rope_mxu(k, ka_ref[rs, :], kb_ref[rs, :], pk_ref) * inv_rms(k.astype(F32), bdk_ref), v_ref[rs, :])

    q = q_ref[...]
    qf = q.astype(F32)
    qn = _rope_mxu(q, qa_ref[...], qb_ref[...], pq_ref) if n_lat_k else qf * qa_ref[...]
    qn = (qn * inv_rms(qf, bdq_ref)).astype(BF16)
    low, high = _lane_masks(2)
    outs = []
    for g in range(GQA_KV_HEADS):
        qv = qn[:, g * 128:(g + 1) * 128]
        o_low = _softmax_pv(qv * low, kp[g], vx[g])
        o_high = _softmax_pv(qv * high, kp[1 - g], vx[1 - g])
        outs.append(jnp.where(low > 0, o_low, o_high))
    y_ref[...] = jnp.concatenate(outs, axis=-1).astype(BF16)


def _gqa_mixer(p_all, q_gain, k_gain, batch, seq, n_ctx, need_ctx):
    n_lat = batch * seq
    cb = n_lat // n_ctx
    dim = GQA_DIM
    qw = GQA_HEADS * dim
    kvw = GQA_KV_HEADS * dim
    assert qw == 256 and kvw == 128

    def swapped(g):
        return g.reshape(-1, 2, 16)[:, ::-1].reshape(1, -1)

    qg = jnp.tile(q_gain.astype(F32) * (dim ** -0.5 * LOG2E), GQA_HEADS).reshape(1, qw)
    kg = jnp.tile(k_gain.astype(F32), GQA_KV_HEADS).reshape(1, kvw)
    cq, sq = _rope_tables(seq, dim, qw, 16)
    ck, sk = _rope_tables(seq, dim, kvw, 16)
    qa, qb = cq * qg, sq * swapped(qg)
    ka, kb = ck * kg, sk * swapped(kg)
    pq, pk = _swap_matrix(qw, 16), _swap_matrix(kvw, 16)
    bdq, bdk = _block_ones(qw, dim), _block_ones(kvw, dim)
    tq = 512
    nqb = seq // tq
    const = lambda shape: pl.BlockSpec(shape, lambda b, i: (0,) * len(shape))
    scratch = lambda nk: [pltpu.VMEM((2, nk, kvw), BF16), pltpu.VMEM((2, nk, 2 * kvw), BF16)]
    y_lat = pl.pallas_call(
        functools.partial(_gqa_kernel, seq, n_ctx),
        grid=(batch, nqb),
        in_specs=[pl.BlockSpec((tq, qw), lambda b, i: (b * nqb + i, GQA_Q)),
                  pl.BlockSpec((seq, kvw), lambda b, i: (b, 16)),
                  pl.BlockSpec((seq, kvw), lambda b, i: (b, 17)),
                  pl.BlockSpec((n_ctx, kvw), lambda b, i: (cb + b, 16)),
                  pl.BlockSpec((n_ctx, kvw), lambda b, i: (cb + b, 17)),
                  pl.BlockSpec((tq, qw), lambda b, i: (i, 0)),
                  pl.BlockSpec((tq, qw), lambda b, i: (i, 0)),
                  const((seq, kvw)), const((seq, kvw)), const((1, kvw)),
                  const((qw, qw)), const((kvw, kvw)), const((qw, qw)), const((kvw, kvw))],
        out_specs=pl.BlockSpec((tq, qw), lambda b, i: (b * nqb + i, 0)),
        out_shape=jax.ShapeDtypeStruct((n_lat, qw), BF16),
        scratch_shapes=scratch(seq + n_ctx),
        compiler_params=_params("parallel", "arbitrary"),
        name="gqa_mixer",
    )(p_all, p_all, p_all, p_all, p_all, qa, qb, ka, kb, kg, pq, pk, bdq, bdk)
    if not need_ctx:
        return y_lat, None
    y_ctx = pl.pallas_call(
        functools.partial(_gqa_kernel, 0, n_ctx),
        grid=(batch, 1),
        in_specs=[pl.BlockSpec((n_ctx, qw), lambda b, i: (cb + b, GQA_Q)),
                  pl.BlockSpec((n_ctx, kvw), lambda b, i: (cb + b, 16)),
                  pl.BlockSpec((n_ctx, kvw), lambda b, i: (cb + b, 17)),
                  const((1, qw)), const((1, kvw)), const((qw, qw)), const((kvw, kvw))],
        out_specs=pl.BlockSpec((n_ctx, qw), lambda b, i: (b, 0)),
        out_shape=jax.ShapeDtypeStruct((batch * n_ctx, qw), BF16),
        scratch_shapes=scratch(n_ctx),
        compiler_params=_params("parallel", "arbitrary"),
        name="gqa_mixer_ctx",
    )(p_all, p_all, p_all, qg, kg, bdq, bdk)
    return y_lat, y_ctx


def _diff_kernel(n_lat_k, n_ctx, lam_init, *refs):
    if n_lat_k:
        (q_ref, k_ref, v_ref, kc_ref, vc_ref, lp_ref, sg_ref, cq_ref, sq_ref, ck_ref, sk_ref, perm_ref, bd_ref,
         y_ref, kp, vx) = refs
    else:
        q_ref, kc_ref, vc_ref, lp_ref, sg_ref, bd_ref, y_ref, kp, vx = refs
    nk = n_lat_k + n_ctx
    n_pairs = DIFF_HEADS // 2

    @pl.when(pl.program_id(1) == 0)
    def _():
        def put(rs, k, v):
            kp[rs, :] = k
            ones = jnp.ones((v.shape[0], 128), BF16)
            for pr in range(n_pairs):
                vx[pr, rs, :] = jnp.concatenate([v[:, pr * 128:(pr + 1) * 128], ones], axis=-1)

        put(slice(n_lat_k, nk), kc_ref[...], vc_ref[...])
        ck = 512
        for c in range(n_lat_k // ck):
            rs = slice(c * ck, (c + 1) * ck)
            put(rs, _rope_mxu(k_ref[rs, :], ck_ref[rs, :], sk_ref[rs, :], perm_ref).astype(BF16), v_ref[rs, :])

    lp = lp_ref[...]
    lam = (jnp.exp(jnp.sum(lp[0:1, :] * lp[1:2, :], axis=-1, keepdims=True))
           - jnp.exp(jnp.sum(lp[2:3, :] * lp[3:4, :], axis=-1, keepdims=True)) + lam_init)

    q = q_ref[...]
    if n_lat_k:
        q = _rope_mxu(q, cq_ref[...], sq_ref[...], perm_ref).astype(BF16)
    quarter = _lane_masks(4)
    low = _lane_masks(2)[0]
    outs = []
    for pr in range(n_pairs):
        ps = slice(pr * 128, (pr + 1) * 128)
        qv = q[:, ps]
        kv = kp[:, ps]
        o_head = [_softmax_pv(qv * quarter[2 * hh], kv, vx[pr]) - lam * _softmax_pv(qv * quarter[2 * hh + 1], kv, vx[pr])
                  for hh in range(2)]
        outs.append(jnp.where(low > 0, o_head[0], o_head[1]))
    o = jnp.concatenate(outs, axis=-1)
    inv = lax.rsqrt(_sumsq_blocks(o, bd_ref) * (1.0 / DIFF_V_DIM) + EPS)
    y_ref[...] = (o * inv * sg_ref[...] * (1.0 - lam_init)).astype(BF16)


def _diff_mixer(p_all, lam_params, subln, layer_idx, batch, seq, n_ctx, need_ctx):
    n_lat = batch * seq
    cb = n_lat // n_ctx
    lam_init = 0.8 - 0.6 * math.exp(-0.3 * layer_idx)
    lp = jnp.zeros((8, 128), F32).at[:4, :DIFF_QK_DIM].set(lam_params.astype(F32))
    w = DIFF_HEADS * DIFF_V_DIM
    assert w == 256 and DIFF_HEADS * 2 * DIFF_QK_DIM == w
    sg = jnp.tile(subln.astype(F32), DIFF_HEADS).reshape(1, w)
    cq, sq = _rope_tables(seq, DIFF_QK_DIM, w, 8)
    perm = _swap_matrix(w, 8)
    bd = _block_ones(w, DIFF_V_DIM)
    tq = 512
    nqb = seq // tq
    const = lambda shape: pl.BlockSpec(shape, lambda b, i: (0,) * len(shape))
    scratch = lambda nk: [pltpu.VMEM((nk, w), BF16), pltpu.VMEM((DIFF_HEADS // 2, nk, 256), BF16)]
    y_lat = pl.pallas_call(
        functools.partial(_diff_kernel, seq, n_ctx, lam_init),
        grid=(batch, nqb),
        in_specs=[pl.BlockSpec((tq, w), lambda b, i: (b * nqb + i, DIFF_Q)),
                  pl.BlockSpec((seq, w), lambda b, i: (b, DIFF_K)),
                  pl.BlockSpec((seq, w), lambda b, i: (b, DIFF_V)),
                  pl.BlockSpec((n_ctx, w), lambda b, i: (cb + b, DIFF_K)),
                  pl.BlockSpec((n_ctx, w), lambda b, i: (cb + b, DIFF_V)),
                  const((8, 128)), const((1, w)),
                  pl.BlockSpec((tq, w), lambda b, i: (i, 0)),
                  pl.BlockSpec((tq, w), lambda b, i: (i, 0)),
                  const((seq, w)), const((seq, w)), const((w, w)), const((w, w))],
        out_specs=pl.BlockSpec((tq, w), lambda b, i: (b * nqb + i, 0)),
        out_shape=jax.ShapeDtypeStruct((n_lat, w), BF16),
        scratch_shapes=scratch(seq + n_ctx),
        compiler_params=_params("parallel", "arbitrary"),
        name="diff_mixer",
    )(p_all, p_all, p_all, p_all, p_all, lp, sg, cq, sq, cq, sq, perm, bd)
    if not need_ctx:
        return y_lat, None
    y_ctx = pl.pallas_call(
        functools.partial(_diff_kernel, 0, n_ctx, lam_init),
        grid=(batch, 1),
        in_specs=[pl.BlockSpec((n_ctx, w), lambda b, i: (cb + b, DIFF_Q)),
                  pl.BlockSpec((n_ctx, w), lambda b, i: (cb + b, DIFF_K)),
                  pl.BlockSpec((n_ctx, w), lambda b, i: (cb + b, DIFF_V)),
                  const((8, 128)), const((1, w)), const((w, w))],
        out_specs=pl.BlockSpec((n_ctx, w), lambda b, i: (b, 0)),
        out_shape=jax.ShapeDtypeStruct((batch * n_ctx, w), BF16),
        scratch_shapes=scratch(n_ctx),
        compiler_params=_params("parallel", "arbitrary"),
        name="diff_mixer_ctx",
    )(p_all, p_all, p_all, lp, sg, bd)
    return y_lat, y_ctx


def _merge_kernel(nlat_blk, has_ctx, x_ref, *refs):
    g_refs = refs[:N_BRANCH]
    y_refs = refs[N_BRANCH:2 * N_BRANCH]
    refs = refs[2 * N_BRANCH:]
    if has_ctx:
        yc_refs, refs = refs[:N_BRANCH], refs[N_BRANCH:]
        is_ctx = pl.program_id(0) >= nlat_blk
    wb_ref, wo_ref, mod_ref, gn_ref, wr_ref, xo_ref, m_ref, lg_ref = refs
    acc = None
    for n in range(N_BRANCH):
        y = y_refs[n][...]
        if has_ctx:
            y = jnp.where(is_ctx, yc_refs[n][...], y)
        term = g_refs[n][...].astype(F32) * _dot(y, wb_ref[n])
        acc = term if acc is None else acc + term
    y = _dot(acc.astype(BF16), wo_ref[...])
    x = x_ref[...] + mod_ref[2:3, :] * y
    xo_ref[...] = x
    xn = x * lax.rsqrt(jnp.mean(x * x, axis=-1, keepdims=True) + EPS) * gn_ref[...]
    m = xn * (1.0 + mod_ref[4:5, :]) + mod_ref[3:4, :]
    _store_tile_rows(m_ref, m)
    m_hi = m.astype(BF16)
    m_lo = (m - m_hi.astype(F32)).astype(BF16)
    w = wr_ref[...]
    w_hi = w.astype(BF16)
    w_lo = (w - w_hi.astype(F32)).astype(BF16)
    lg_ref[...] = _dot_nt(w_hi, m_hi) + (_dot_nt(w_hi, m_lo) + _dot_nt(w_lo, m_hi))


def _merge(x_all, p_all, ys_lat, ys_ctx, wb_bf, wo_bf, modtab, gn, wr_t, tm, n_rows, n_lat, seq, batch):
    d = D_MODEL
    midx = _mod_index(tm, n_lat, seq, batch)
    nlat_blk = n_lat // tm
    has_ctx = ys_ctx is not None
    gate = lambda n: pl.BlockSpec((tm, d), lambda i, n=n: (i, MIX_COLS // d + n))
    y_specs = [pl.BlockSpec((tm, BRANCH_W), lambda i: (jnp.minimum(i, nlat_blk - 1), 0))] * N_BRANCH
    ys = list(ys_lat)
    if has_ctx:
        y_specs += [pl.BlockSpec((tm, BRANCH_W), lambda i: (jnp.maximum(i - nlat_blk, 0), 0))] * N_BRANCH
        ys += list(ys_ctx)
    return pl.pallas_call(
        functools.partial(_merge_kernel, nlat_blk, has_ctx),
        grid=(n_rows // tm,),
        in_specs=[pl.BlockSpec((tm, d), lambda i: (i, 0)),
                  gate(0), gate(1), gate(2), gate(3), *y_specs,
                  pl.BlockSpec((N_BRANCH, BRANCH_W, d), lambda i: (0, 0, 0)),
                  pl.BlockSpec((d, d), lambda i: (0, 0)),
                  pl.BlockSpec((None, 8, d), lambda i: (midx(i), 0, 0)),
                  pl.BlockSpec((1, d), lambda i: (0, 0)),
                  pl.BlockSpec((N_EXPERTS, d), lambda i: (0, 0))],
        out_specs=[pl.BlockSpec((tm, d), lambda i: (i, 0)),
                   pl.BlockSpec((tm * ROW_TILE, 128), lambda i: (i, 0)),
                   pl.BlockSpec((N_EXPERTS, tm), lambda i: (0, i))],
        out_shape=[jax.ShapeDtypeStruct((n_rows, d), F32),
                   jax.ShapeDtypeStruct((n_rows * ROW_TILE, 128), F32),
                   jax.ShapeDtypeStruct((N_EXPERTS, n_rows), F32)],
        compiler_params=_params("parallel"),
        name="merge_norm_route",
    )(x_all, p_all, p_all, p_all, p_all, *ys, wb_bf, wo_bf, modtab, gn.reshape(1, d), wr_t)


def _route_kernel(lg_ref, b_ref, idx_ref, w_ref):
    s = _sigmoid(lg_ref[...])
    sel = s + b_ref[...]
    row = lambda a, e: a[e:e + 1, :]
    gsz = EXPERTS_PER_GROUP
    g_idx = None
    best = None
    for g in range(N_GROUPS):
        v = [row(sel, g * gsz + i) for i in range(gsz)]
        score = None
        for i in range(gsz):
            for j in range(i + 1, gsz):
                pair = v[i] + v[j]
                score = pair if score is None else jnp.maximum(score, pair)
        if g == 0:
            best, g_idx = score, jnp.zeros(score.shape, jnp.int32)
        else:
            better = score > best
            best = jnp.where(better, score, best)
            g_idx = jnp.where(better, g, g_idx)

    def in_group(a, i):
        out = row(a, i)
        for g in range(1, N_GROUPS):
            out = jnp.where(g_idx == g, row(a, g * gsz + i), out)
        return out

    v = [in_group(sel, i) for i in range(gsz)]
    sv = [in_group(s, i) for i in range(gsz)]

    def arg_first_max(vals):
        bv, bi = vals[0], jnp.zeros(vals[0].shape, jnp.int32)
        for i in range(1, gsz):
            better = vals[i] > bv
            bv = jnp.where(better, vals[i], bv)
            bi = jnp.where(better, i, bi)
        return bi

    i1 = arg_first_max(v)
    i2 = arg_first_max([jnp.where(i1 == i, -jnp.inf, v[i]) for i in range(gsz)])

    def pick(vals, idx):
        out = vals[0]
        for i in range(1, gsz):
            out = jnp.where(idx == i, vals[i], out)
        return out

    w1 = pick(sv, i1)
    w2 = pick(sv, i2)
    tot = w1 + w2
    idx_ref[0:1, :] = g_idx * gsz + i1
    idx_ref[1:2, :] = g_idx * gsz + i2
    w_ref[0:1, :] = w1 / tot
    w_ref[1:2, :] = w2 / tot


def _route(logits_t, router_bias):
    e, n = logits_t.shape
    tn = math.gcd(n, 2048)
    return pl.pallas_call(
        _route_kernel,
        grid=(n // tn,),
        in_specs=[pl.BlockSpec((e, tn), lambda i: (0, i)),
                  pl.BlockSpec((e, 1), lambda i: (0, 0))],
        out_specs=[pl.BlockSpec((TOP_K, tn), lambda i: (0, i)),
                   pl.BlockSpec((TOP_K, tn), lambda i: (0, i))],
        out_shape=[jax.ShapeDtypeStruct((TOP_K, n), jnp.int32),
                   jax.ShapeDtypeStruct((TOP_K, n), F32)],
        compiler_params=_params("parallel"),
        name="route_top2",
    )(logits_t, router_bias.astype(F32).reshape(e, 1))


def _dispatch_kernel(n_tok_steps, pos_ref, m_ref, xs_out, zrow, sem):
    i = pl.program_id(0)
    rt = ROW_TILE
    tm = m_ref.shape[0] // rt
    n = TOP_K * tm

    def wait():
        pltpu.make_async_copy(xs_out.at[pl.ds(0, n * rt)], xs_out.at[pl.ds(0, n * rt)], sem).wait()

    @pl.when(i < n_tok_steps)
    def _():
        for j in range(n):
            dst = pl.multiple_of(pos_ref[i, j], rt)
            pltpu.make_async_copy(m_ref.at[pl.ds((j % tm) * rt, rt)], xs_out.at[pl.ds(dst, rt)],
                                  sem).start(priority=j % 2)
        wait()

    @pl.when(i >= n_tok_steps)
    def _():
        zrow[...] = jnp.zeros(zrow.shape, zrow.dtype)
        for j in range(n):
            dst = pl.multiple_of(pos_ref[i, j], rt)
            pltpu.make_async_copy(zrow, xs_out.at[pl.ds(dst, rt)], sem).start(priority=j % 2)
        wait()


def _dispatch(m_tiles, pos, pad_slots, n_slots):
    rt = ROW_TILE
    n_tok_steps, n = pos.shape
    tm = n // TOP_K
    steps = jnp.concatenate([pos, pad_slots], axis=0)
    grid_spec = pltpu.PrefetchScalarGridSpec(
        num_scalar_prefetch=1,
        grid=(steps.shape[0],),
        in_specs=[pl.BlockSpec((tm * rt, 128), lambda i, pos: (jnp.minimum(i, n_tok_steps - 1), 0))],
        out_specs=pl.BlockSpec(memory_space=pl.ANY),
        scratch_shapes=[pltpu.VMEM((rt, 128), F32), pltpu.SemaphoreType.DMA(())])
    return pl.pallas_call(
        functools.partial(_dispatch_kernel, n_tok_steps),
        grid_spec=grid_spec,
        out_shape=jax.ShapeDtypeStruct((n_slots * rt, 128), F32),
        compiler_params=_params("arbitrary"),
        name="moe_dispatch",
    )(steps, m_tiles)


def _expert_kernel(be_ref, x_ref, wg_ref, wu_ref, wd_ref, y_ref):
    blk = x_ref.shape[0] // ROW_TILE
    x = _load_tile_rows(x_ref, 0, blk).astype(BF16)
    a = _dot(x, wg_ref[...])
    h = a * _sigmoid(a) * _dot(x, wu_ref[...])
    _store_tile_rows(y_ref, _dot(h.astype(BF16), wd_ref[...]))


def _experts(x_slots, blk_e, blk, wg_bf, wu_bf, wd_bf):
    rt = ROW_TILE
    d = D_MODEL
    n_slots = x_slots.shape[0] // rt
    grid_spec = pltpu.PrefetchScalarGridSpec(
        num_scalar_prefetch=1,
        grid=(n_slots // blk,),
        in_specs=[pl.BlockSpec((blk * rt, 128), lambda i, be: (i, 0)),
                  pl.BlockSpec((None, d, D_EXPERT), lambda i, be: (be[i], 0, 0)),
                  pl.BlockSpec((None, d, D_EXPERT), lambda i, be: (be[i], 0, 0)),
                  pl.BlockSpec((None, D_EXPERT, d), lambda i, be: (be[i], 0, 0))],
        out_specs=pl.BlockSpec((blk * rt, 128), lambda i, be: (i, 0)))
    return pl.pallas_call(
        _expert_kernel,
        grid_spec=grid_spec,
        out_shape=jax.ShapeDtypeStruct((n_slots * rt, 128), F32),
        compiler_params=_params("arbitrary"),
        name="expert_ffn",
    )(blk_e, x_slots, wg_bf, wu_bf, wd_bf)


def _combine_kernel(final, pos_ref, y_hbm, x_ref, w_ref, mod_ref, gf_ref, o_ref, ybuf, sem):
    i = pl.program_id(0)
    last = pl.num_programs(0) - 1
    slot = i % 2
    rt = ROW_TILE
    tm = x_ref.shape[0]
    n = TOP_K * tm

    def issue(step, dst_slot):
        for j in range(n):
            src = pl.multiple_of(pos_ref[step, j], rt)
            pltpu.make_async_copy(y_hbm.at[pl.ds(src, rt)], ybuf.at[dst_slot, pl.ds(j * rt, rt)],
                                  sem.at[dst_slot]).start(priority=j % 2)

    def wait(dst_slot):
        pltpu.make_async_copy(y_hbm.at[pl.ds(0, n * rt)], ybuf.at[dst_slot], sem.at[dst_slot]).wait()

    @pl.when(i == 0)
    def _():
        issue(0, 0)

    wait(slot)
    issue(jnp.minimum(i + 1, last), 1 - slot)
    w = w_ref[...]
    yb = ybuf.at[slot]
    f = w[:, 0:1] * _load_tile_rows(yb, 0, tm) + w[:, 1:2] * _load_tile_rows(yb, tm * rt, tm)
    x = x_ref[...] + mod_ref[5:6, :] * f
    if final:
        x = x * lax.rsqrt(jnp.mean(x * x, axis=-1, keepdims=True) + EPS) * gf_ref[...]
    o_ref[...] = x

    @pl.when(i == last)
    def _():
        wait(1 - slot)


def _combine(x_rows, y_slots, pos, w_tok, modtab, g_final, final, n_lat, seq, batch):
    n_rows, d = x_rows.shape
    n_steps, n = pos.shape
    tm = n // TOP_K
    midx = _mod_index(tm, n_lat, seq, batch)
    grid_spec = pltpu.PrefetchScalarGridSpec(
        num_scalar_prefetch=1,
        grid=(n_steps,),
        in_specs=[pl.BlockSpec(memory_space=pl.ANY),
                  pl.BlockSpec((tm, d), lambda i, pos: (i, 0)),
                  pl.BlockSpec((tm, TOP_K), lambda i, pos: (i, 0)),
                  pl.BlockSpec((None, 8, d), lambda i, pos: (midx(i), 0, 0)),
                  pl.BlockSpec((1, d), lambda i, pos: (0, 0))],
        out_specs=pl.BlockSpec((tm, d), lambda i, pos: (i, 0)),
        scratch_shapes=[pltpu.VMEM((2, n * ROW_TILE, 128), F32), pltpu.SemaphoreType.DMA((2,))])
    return pl.pallas_call(
        functools.partial(_combine_kernel, final),
        grid_spec=grid_spec,
        out_shape=jax.ShapeDtypeStruct((n_rows, d), F32),
        compiler_params=_params("arbitrary"),
        name="moe_combine",
    )(pos, y_slots, x_rows, w_tok, modtab, g_final.astype(F32).reshape(1, d))


def _dispatch_plan(idx2, blk, tm):
    n = idx2.shape[1]
    a = TOP_K * n
    flat_e = idx2.T.reshape(a)
    onehot = (flat_e[:, None] == jnp.arange(N_EXPERTS, dtype=jnp.int32)[None, :]).astype(jnp.int32)
    csum = jnp.cumsum(onehot, axis=0)
    rank = jnp.sum(onehot * csum, axis=1) - 1
    counts = csum[-1]
    padded = (counts + blk - 1) // blk * blk
    pad_end = jnp.cumsum(padded)
    pad_start = pad_end - padded
    dest = (pad_start[flat_e] + rank).astype(jnp.int32)
    n_blocks = a // blk + N_EXPERTS
    blk_e = jnp.minimum(jnp.searchsorted(pad_end, jnp.arange(n_blocks, dtype=jnp.int32) * blk, side='right'),
                        N_EXPERTS - 1).astype(jnp.int32)
    pos = dest.reshape(n // tm, tm, TOP_K).transpose(0, 2, 1).reshape(n // tm, TOP_K * tm) * ROW_TILE
    n_slots = n_blocks * blk
    starts = jnp.concatenate([pad_start + counts, pad_end[-1:]])
    lens = jnp.concatenate([padded - counts, n_slots - pad_end[-1:]])
    ends = jnp.cumsum(lens)
    j = jnp.arange(N_EXPERTS * blk, dtype=jnp.int32)
    seg = jnp.searchsorted(ends, j, side='right')
    pad_slots = ((starts[seg] + j - (ends[seg] - lens[seg])) * ROW_TILE).astype(jnp.int32)
    return pos, pad_slots.reshape(-1, TOP_K * tm), blk_e, n_slots


def kernel(x, c, ctx, c_ctx, w_mod, b_mod, g_norm1, g_norm2, w_in, na_rpb, ret_log_decay, gqa_q_gain, gqa_k_gain,
           diff_lambda, diff_subln, w_branch, w_out, w_router, router_bias, w_gate_e, w_up_e, w_down_e, g_final):
    batch, seq, d = x.shape
    n_ctx = ctx.shape[1]
    n_lat = batch * seq
    n_ctx_tot = batch * n_ctx
    n_tot = n_lat + n_ctx_tot
    tm = 1024 if n_ctx_tot % 1024 == 0 else 512

    x_all = jnp.concatenate([x.reshape(n_lat, d), ctx.reshape(n_ctx_tot, d)], axis=0)

    nrow = -(-(batch + 1) // 8) * 8
    cc = jnp.zeros((nrow, d), F32).at[:batch].set(c).at[batch].set(c_ctx)
    mod = _modulation(cc, w_mod, b_mod)
    modtab = jnp.pad(mod.reshape(DEPTH, nrow, 6, d), ((0, 0), (0, 0), (0, 2), (0, 0)))

    wr_t = w_router.astype(F32).T
    out = None
    for layer in range(DEPTH):
        need_ctx = layer < DEPTH - 1
        n_rows = n_tot if need_ctx else n_lat
        p_all = _project(x_all, g_norm1[layer], modtab[layer], w_in[layer].astype(BF16), tm, n_lat, seq, batch)
        ya = _na_mixer(p_all, na_rpb[layer], batch, seq, n_ctx, need_ctx)
        yb = _ret_mixer(p_all, ret_log_decay[layer], batch, seq, n_ctx, need_ctx)
        yc = _gqa_mixer(p_all, gqa_q_gain[layer], gqa_k_gain[layer], batch, seq, n_ctx, need_ctx)
        yd = _diff_mixer(p_all, diff_lambda[layer], diff_subln[layer], layer, batch, seq, n_ctx, need_ctx)
        ys_lat = [ya[0], yb[0], yc[0], yd[0]]
        ys_ctx = [ya[1], yb[1], yc[1], yd[1]] if need_ctx else None
        x_mid, m_all, logits_t = _merge(x_all, p_all, ys_lat, ys_ctx, w_branch[layer].astype(BF16),
                                        w_out[layer].astype(BF16), modtab[layer], g_norm2[layer], wr_t, MERGE_TM,
                                        n_rows, n_lat, seq, batch)
        idx2, w2 = _route(logits_t, router_bias)
        pos, pad_slots, blk_e, n_slots = _dispatch_plan(idx2, MOE_BLK, CMB_TM)
        x_slots = _dispatch(m_all, pos, pad_slots, n_slots)
        y_slots = _experts(x_slots, blk_e, MOE_BLK, w_gate_e[layer].astype(BF16), w_up_e[layer].astype(BF16),
                           w_down_e[layer].astype(BF16))
        out = _combine(x_mid, y_slots, pos, w2.T, modtab[layer], g_final, not need_ctx, n_lat, seq, batch)
        x_all = out
    return out.reshape(batch, seq, d)
```

```python
import functools
import math

import numpy as np
import jax
import jax.numpy as jnp
from jax import lax
from jax.experimental import pallas as pl
from jax.experimental.pallas import tpu as pltpu

F32 = jnp.float32
BF16 = jnp.bfloat16

D_MODEL = 1024
DEPTH = 2
GRID_W = 64
NA_HEADS = 4
NA_DIM = 64
NA_WIN_H = 8
NA_WIN_W = 16
RET_HEADS = 4
RET_DK = 64
RET_CHUNK = 128
GQA_HEADS = 4
GQA_KV_HEADS = 2
GQA_DIM = 64
DIFF_HEADS = 4
DIFF_QK_DIM = 32
DIFF_V_DIM = 64
N_BRANCH = 4
BRANCH_W = 256
ROPE_THETA = 10000.0
EPS = 1e-6
NEG_INF = -1e30
N_EXPERTS = 16
N_GROUPS = 4
EXPERTS_PER_GROUP = 4
TOP_K = 2
D_EXPERT = 512

MIX_COLS = 3072
IN_COLS = MIX_COLS + N_BRANCH * D_MODEL
NA_Q, NA_K, NA_V = 0, 1, 2
RET_Q, RET_K, RET_V, RET_G = 3, 4, 5, 6
GQA_Q = 7
DIFF_Q, DIFF_K, DIFF_V = 9, 10, 11

LOG2E = math.log2(math.e)
NA_GROUP_ROWS = 4
NA_BAND_ROWS = 12

VMEM_LIMIT = 56 * 1024 * 1024
MOE_BLK = 512
CMB_TM = 512
MERGE_TM = 512


ROW_TILE = D_MODEL // 128


def _store_tile_rows(ref, x):
    n = x.shape[0]
    for j in range(ROW_TILE):
        ref[pl.ds(j, n, stride=ROW_TILE), :] = x[:, j * 128:(j + 1) * 128]


def _load_tile_rows(ref, start, n):
    return jnp.concatenate([ref[pl.ds(start + j, n, stride=ROW_TILE), :] for j in range(ROW_TILE)], axis=-1)


def _dot(a, b):
    return jnp.dot(a, b, preferred_element_type=F32)


def _dot_nt(a, b):
    return lax.dot_general(a, b, (((1,), (1,)), ((), ())), preferred_element_type=F32)


def _dot_tn(a, b):
    return lax.dot_general(a, b, (((0,), (0,)), ((), ())), preferred_element_type=F32)


def _sigmoid(x):
    return 1.0 / (1.0 + jnp.exp(-x))


def _params(*sem):
    return pltpu.CompilerParams(dimension_semantics=sem, vmem_limit_bytes=VMEM_LIMIT)


def _swap_halves(x, dist):
    pieces = []
    for j in range(x.shape[-1] // 128):
        xs = x[:, j * 128:(j + 1) * 128]
        lane = lax.broadcasted_iota(jnp.int32, xs.shape, 1)
        up = pltpu.roll(xs, 128 - dist, 1)
        dn = pltpu.roll(xs, dist, 1)
        pieces.append(jnp.where((lane % (2 * dist)) < dist, up, dn))
    return pieces[0] if len(pieces) == 1 else jnp.concatenate(pieces, axis=-1)


def _rope(x, cos, sin, dist):
    return x * cos + _swap_halves(x, dist) * sin


def _mod_kernel(c_ref, w_ref, b_ref, o_ref):
    c = c_ref[...]
    a = (c * _sigmoid(c)).astype(BF16)
    o_ref[...] = _dot(a, w_ref[...].astype(BF16)) + b_ref[...]


def _modulation(cc, w_mod, b_mod):
    nrow = cc.shape[0]
    depth, d, n6 = w_mod.shape
    tn = 1536
    return pl.pallas_call(
        _mod_kernel,
        grid=(depth, n6 // tn),
        in_specs=[pl.BlockSpec((nrow, d), lambda l, j: (0, 0)),
                  pl.BlockSpec((None, d, tn), lambda l, j: (l, 0, j)),
                  pl.BlockSpec((None, 1, tn), lambda l, j: (l, 0, j))],
        out_specs=pl.BlockSpec((None, nrow, tn), lambda l, j: (l, 0, j)),
        out_shape=jax.ShapeDtypeStruct((depth, nrow, n6), F32),
        compiler_params=_params("parallel", "arbitrary"),
        name="modulation",
    )(cc, w_mod, b_mod.reshape(depth, 1, n6))


def _mix_col_scale():
    cs = np.ones((1, MIX_COLS), np.float32)
    cs[0, NA_Q * 256:(NA_Q + 1) * 256] = NA_DIM ** -0.5 * LOG2E
    cs[0, DIFF_Q * 256:(DIFF_Q + 1) * 256] = DIFF_QK_DIM ** -0.5 * LOG2E
    cs[0, RET_K * 256:(RET_K + 1) * 256] = RET_DK ** -0.5
    return jnp.asarray(cs)


def _row_blocks(parts, tm, n_lat):
    d = parts[0].shape[1]
    nlat_blk = n_lat // tm
    if len(parts) == 1:
        return [pl.BlockSpec((tm, d), lambda i, *_: (i, 0))]
    return [pl.BlockSpec((tm, d), lambda i, *_: (jnp.minimum(i, nlat_blk - 1), 0)),
            pl.BlockSpec((tm, d), lambda i, *_: (jnp.maximum(i - nlat_blk, 0), 0))]


def _pick_rows(x_refs, nlat_blk):
    if len(x_refs) == 1:
        return x_refs[0][...]
    return jnp.where(pl.program_id(0) >= nlat_blk, x_refs[1][...], x_refs[0][...])


def _proj_kernel(nlat_blk, n_parts, *refs):
    x_refs, (g_ref, mod_ref, w_ref, cs_ref, o_ref, h_ref) = refs[:n_parts], refs[n_parts:]
    j = pl.program_id(1)

    @pl.when(j == 0)
    def _():
        x = _pick_rows(x_refs, nlat_blk)
        y = x * lax.rsqrt(jnp.mean(x * x, axis=-1, keepdims=True) + EPS) * g_ref[...]
        h_ref[...] = (y * (1.0 + mod_ref[1:2, :]) + mod_ref[0:1, :]).astype(BF16)

    acc = _dot(h_ref[...], w_ref[...])
    gate = 0.5 * jnp.tanh(0.5 * acc) + 0.5
    o_ref[...] = jnp.where(j < MIX_COLS // D_MODEL, acc * cs_ref[...], gate).astype(BF16)


def _mod_index(tm, n_lat, seq, batch):
    nlat_blk = n_lat // tm
    bpb = seq // tm

    def index(i):
        return jnp.where(i < nlat_blk, i // bpb, batch)
    return index


def _project(x_parts, g, modtab, w_bf, tm, n_lat, seq, batch):
    n_tot = sum(p.shape[0] for p in x_parts)
    d = x_parts[0].shape[1]
    ncol = w_bf.shape[1]
    tn = D_MODEL
    n_mix = MIX_COLS // tn
    midx = _mod_index(tm, n_lat, seq, batch)
    return pl.pallas_call(
        functools.partial(_proj_kernel, n_lat // tm, len(x_parts)),
        grid=(n_tot // tm, ncol // tn),
        in_specs=[*_row_blocks(x_parts, tm, n_lat),
                  pl.BlockSpec((1, d), lambda i, j: (0, 0)),
                  pl.BlockSpec((None, 8, d), lambda i, j: (midx(i), 0, 0)),
                  pl.BlockSpec((d, tn), lambda i, j: (0, j)),
                  pl.BlockSpec((1, tn), lambda i, j: (0, jnp.minimum(j, n_mix - 1)))],
        out_specs=pl.BlockSpec((tm, tn), lambda i, j: (i, j)),
        out_shape=jax.ShapeDtypeStruct((n_tot, ncol), BF16),
        scratch_shapes=[pltpu.VMEM((tm, d), BF16)],
        compiler_params=_params("parallel", "arbitrary"),
        name="norm_project",
    )(*x_parts, g.reshape(1, d), modtab, w_bf, _mix_col_scale())


def _na_kernel(need_ctx, rows, q_ref, k_ref, v_ref, qc_ref, kc_ref, vc_ref, tb_ref, y_ref, *rest):
    gq = NA_GROUP_ROWS * GRID_W
    band = NA_BAND_ROWS * GRID_W
    n_groups = rows // NA_GROUP_ROWS

    def group_body(g, carry):
        u = jnp.clip(g * NA_GROUP_ROWS - NA_WIN_H // 2, 0, rows - NA_BAND_ROWS)
        typ = jnp.where(g == 0, 0, jnp.where(g == n_groups - 1, 2, 1))
        qs = pl.multiple_of(g * gq, gq)
        ks = pl.multiple_of(u * GRID_W, GRID_W)
        for h in range(NA_HEADS):
            hs = slice(h * NA_DIM, (h + 1) * NA_DIM)
            q = q_ref[pl.ds(qs, gq), hs]
            vc = vc_ref[:, hs]
            s_w = _dot_nt(q, k_ref[pl.ds(ks, band), hs]) + tb_ref[h, typ]
            s_c = _dot_nt(q, kc_ref[:, hs])
            m = jnp.maximum(jnp.max(s_w, axis=-1, keepdims=True), jnp.max(s_c, axis=-1, keepdims=True))
            p_w = jnp.exp2(s_w - m)
            p_c = jnp.exp2(s_c - m)
            l = jnp.sum(p_w, axis=-1, keepdims=True) + jnp.sum(p_c, axis=-1, keepdims=True)
            o = (_dot(p_w.astype(BF16), v_ref[pl.ds(ks, band), hs]) + _dot(p_c.astype(BF16), vc)) / l
            y_ref[pl.ds(qs, gq), hs] = o.astype(BF16)
        return carry

    lax.fori_loop(0, n_groups, group_body, 0)

    if need_ctx:
        yc_ref = rest[0]
        for h in range(NA_HEADS):
            hs = slice(h * NA_DIM, (h + 1) * NA_DIM)
            s = _dot_nt(qc_ref[:, hs], kc_ref[:, hs])
            p = jnp.exp2(s - jnp.max(s, axis=-1, keepdims=True))
            l = jnp.sum(p, axis=-1, keepdims=True)
            yc_ref[:, hs] = (_dot(p.astype(BF16), vc_ref[:, hs]) / l).astype(BF16)


def _na_bias_table(rpb, rows):
    assert rows % NA_GROUP_ROWS == 0 and rows >= NA_BAND_ROWS + NA_GROUP_ROWS
    n_groups = rows // NA_GROUP_ROWS
    qc = np.arange(GRID_W)[:, None]
    kc = np.arange(GRID_W)[None, :]
    win_start = np.clip(qc - NA_WIN_W // 2, 0, GRID_W - NA_WIN_W)
    col_ok = (kc >= win_start) & (kc < win_start + NA_WIN_W)
    col_idx = np.clip(kc - qc, -(NA_WIN_W - 1), NA_WIN_W - 1) + NA_WIN_W - 1
    tiles = jnp.where(col_ok[None, None], rpb.astype(F32)[:, :, col_idx] * LOG2E, NEG_INF)
    masked = jnp.full((NA_HEADS, GRID_W, GRID_W), NEG_INF, F32)
    tables = []
    for g in (0, 1, n_groups - 1):
        u = int(np.clip(g * NA_GROUP_ROWS - NA_WIN_H // 2, 0, rows - NA_BAND_ROWS))
        per_row = []
        for a in range(NA_GROUP_ROWS):
            r = g * NA_GROUP_ROWS + a
            r0 = int(np.clip(r - NA_WIN_H // 2, 0, rows - NA_WIN_H))
            pieces = [tiles[:, kr - r + NA_WIN_H - 1] if r0 <= kr < r0 + NA_WIN_H else masked
                      for kr in range(u, u + NA_BAND_ROWS)]
            per_row.append(jnp.concatenate(pieces, axis=-1))
        tables.append(jnp.concatenate(per_row, axis=1))
    return jnp.stack(tables, axis=1)


def _na_mixer(p_all, rpb, batch, seq, n_ctx, need_ctx):
    n_lat = batch * seq
    rows = seq // GRID_W
    tb = _na_bias_table(rpb, rows)
    cb = n_lat // n_ctx
    w = 256
    lat = lambda c: pl.BlockSpec((seq, w), lambda b, c=c: (b, c))
    ctx = lambda c: pl.BlockSpec((n_ctx, w), lambda b, c=c: (cb + b, c))
    out_shape = [jax.ShapeDtypeStruct((n_lat, w), BF16)]
    out_specs = [pl.BlockSpec((seq, w), lambda b: (b, 0))]
    if need_ctx:
        out_shape.append(jax.ShapeDtypeStruct((batch * n_ctx, w), BF16))
        out_specs.append(pl.BlockSpec((n_ctx, w), lambda b: (b, 0)))
    return pl.pallas_call(
        functools.partial(_na_kernel, need_ctx, rows),
        grid=(batch,),
        in_specs=[lat(NA_Q), lat(NA_K), lat(NA_V), ctx(NA_Q), ctx(NA_K), ctx(NA_V),
                  pl.BlockSpec(tb.shape, lambda b: (0, 0, 0, 0))],
        out_specs=out_specs,
        out_shape=out_shape,
        compiler_params=_params("parallel"),
        name="na_mixer",
    )(p_all, p_all, p_all, p_all, p_all, p_all, tb)


def _ret_kernel(need_ctx, seq, n_ctx, lg_ref, q_ref, k_ref, v_ref, g_ref, qc_ref, kc_ref, vc_ref, gc_ref,
                cos_ref, sin_ref, perm_ref, bd_ref, y_ref, *rest):
    if need_ctx:
        yc_ref, qr, kr, krt, kct, o_f, o_b, oc_f, oc_b, dm, qd, kdt = rest
    else:
        qr, kr, krt, kct, o_f, o_b, oc_f, oc_b, dm, qd, kdt = rest
    ch = RET_CHUNK
    n_lat_ch = seq // ch
    n_ctx_ch = n_ctx // ch
    n_pairs = RET_HEADS // 2
    assert ch == 128 and 2 * RET_DK == 128

    def prep(i, carry):
        rs = pl.multiple_of(i * ch, ch)
        c = cos_ref[pl.ds(rs, ch), :]
        s = sin_ref[pl.ds(rs, ch), :]
        qr[pl.ds(rs, ch), :] = _rope_mxu(q_ref[pl.ds(rs, ch), :], c, s, perm_ref)
        k = _rope_mxu(k_ref[pl.ds(rs, ch), :], c, s, perm_ref)
        kr[pl.ds(rs, ch), :] = k
        for p in range(n_pairs):
            krt[i, p * 128:(p + 1) * 128, :] = k[:, p * 128:(p + 1) * 128].T
        return carry

    lax.fori_loop(0, n_lat_ch, prep, 0)
    for n in range(n_ctx_ch):
        kc = kc_ref[n * ch:(n + 1) * ch, :].astype(F32)
        for p in range(n_pairs):
            kct[n, p * 128:(p + 1) * 128, :] = kc[:, p * 128:(p + 1) * 128].T

    ri = lax.broadcasted_iota(jnp.int32, (ch, ch), 0)
    ci = lax.broadcasted_iota(jnp.int32, (ch, ch), 1)
    lag = (ri - ci).astype(F32)
    rowf = ri.astype(F32)
    colf = ci.astype(F32)
    low_lane = ci < RET_DK
    low_row = ri < RET_DK
    block_diag = jnp.where(low_lane == low_row, 1.0, 0.0)
    low, high = _lane_masks(2)

    chains = [(p, dirn) for p in range(n_pairs) for dirn in range(2)]
    cdec = []
    for c, (p, dirn) in enumerate(chains):
        lg_a = lg_ref[dirn, 2 * p]
        lg_b = lg_ref[dirn, 2 * p + 1]
        lg_lane = jnp.where(low_lane, lg_a, lg_b)
        lg_row = jnp.where(low_row, lg_a, lg_b)
        for hh, lg in enumerate((lg_a, lg_b)):
            if dirn == 0:
                keep = ri >= ci
                dm[2 * c + hh] = jnp.where(keep, jnp.exp(jnp.where(keep, lag, 0.0) * lg), 0.0)
            else:
                keep = ci >= ri
                dm[2 * c + hh] = jnp.where(keep, jnp.exp(jnp.where(keep, -lag, 0.0) * lg), 0.0)
        if dirn == 0:
            qd[c] = jnp.exp((rowf + 1.0) * lg_lane)
            kdt[c] = jnp.exp((ch - 1.0 - colf) * lg_row)
        else:
            qd[c] = jnp.exp((ch - rowf) * lg_lane)
            kdt[c] = jnp.exp(colf * lg_row)
        cdec.append(jnp.exp(ch * lg_lane[0:1, :]))

    def step(c, state, qi, ki, kti, vi):
        kb = ki.astype(BF16)
        inner_a = (_dot_nt((qi * low).astype(BF16), kb) * dm[2 * c]).astype(BF16)
        inner_b = (_dot_nt((qi * high).astype(BF16), kb) * dm[2 * c + 1]).astype(BF16)
        o = (jnp.where(low_lane, _dot(inner_a, vi), _dot(inner_b, vi))
             + _dot((qi * qd[c]).astype(BF16), state.astype(BF16)))
        state = state * cdec[c] + block_diag * _dot((kti * kdt[c]).astype(BF16), vi)
        return state, o

    states = []
    for c, (p, dirn) in enumerate(chains):
        ps = slice(p * 128, (p + 1) * 128)
        state = jnp.zeros((128, 128), F32)
        for n in (range(n_ctx_ch) if dirn == 0 else range(n_ctx_ch - 1, -1, -1)):
            rs = slice(n * ch, (n + 1) * ch)
            state, o = step(c, state, qc_ref[rs, ps].astype(F32), kc_ref[rs, ps].astype(F32), kct[n, ps, :],
                            vc_ref[rs, ps])
            if need_ctx:
                (oc_f if dirn == 0 else oc_b)[rs, ps] = o
        states.append(state)

    def lat_body(i, states):
        new = []
        for c, (p, dirn) in enumerate(chains):
            ps = slice(p * 128, (p + 1) * 128)
            n = i if dirn == 0 else n_lat_ch - 1 - i
            rs = pl.ds(pl.multiple_of(n * ch, ch), ch)
            state, o = step(c, states[c], qr[rs, ps], kr[rs, ps], krt[n, ps, :], v_ref[rs, ps])
            (o_f if dirn == 0 else o_b)[rs, ps] = o
            new.append(state)
        return tuple(new)

    lax.fori_loop(0, n_lat_ch, lat_body, tuple(states))

    def head_mean(x):
        hi = x.astype(BF16)
        lo = (x - hi.astype(F32)).astype(BF16)
        bd = bd_ref[...]
        return (_dot(hi, bd) + _dot(lo, bd)) * (1.0 / RET_DK)

    def finish(o, g):
        cen = o - head_mean(o)
        on = cen * lax.rsqrt(head_mean(cen * cen) + EPS)
        g = g.astype(F32)
        return (g * _sigmoid(g) * on).astype(BF16)

    def fin_body(i, carry):
        rs = pl.multiple_of(i * ch, ch)
        y_ref[pl.ds(rs, ch), :] = finish(o_f[pl.ds(rs, ch), :] + o_b[pl.ds(rs, ch), :], g_ref[pl.ds(rs, ch), :])
        return carry

    lax.fori_loop(0, n_lat_ch, fin_body, 0)
    if need_ctx:
        for n in range(n_ctx_ch):
            rs = slice(n * ch, (n + 1) * ch)
            yc_ref[rs, :] = finish(oc_f[rs, :] + oc_b[rs, :], gc_ref[rs, :])


def _rope_tables(seq, head_dim, width, dist):
    half = head_dim // 2
    nf = half // 2
    assert nf == dist
    inv = 1.0 / (ROPE_THETA ** (np.arange(nf, dtype=np.float32) / nf))
    t = np.arange(seq)
    rows = (t // GRID_W).astype(np.float32)[:, None] * inv[None, :]
    cols = (t % GRID_W).astype(np.float32)[:, None] * inv[None, :]
    cos = np.concatenate([np.cos(rows), np.cos(rows), np.cos(cols), np.cos(cols)], axis=-1)
    sin = np.concatenate([-np.sin(rows), np.sin(rows), -np.sin(cols), np.sin(cols)], axis=-1)
    reps = width // head_dim
    return (jnp.asarray(np.tile(cos, (1, reps)), F32), jnp.asarray(np.tile(sin, (1, reps)), F32))


def _ret_mixer(p_all, log_decay, batch, seq, n_ctx, need_ctx):
    n_lat = batch * seq
    cb = n_lat // n_ctx
    w = 256
    ch = RET_CHUNK
    log_gamma = jnp.log1p(-jnp.exp(log_decay.astype(F32)))
    cos, sin = _rope_tables(seq, RET_DK, w, 16)
    lat = lambda c: pl.BlockSpec((seq, w), lambda b, c=c: (b, c))
    ctx = lambda c: pl.BlockSpec((n_ctx, w), lambda b, c=c: (cb + b, c))
    whole = pl.BlockSpec((seq, w), lambda b: (0, 0))
    out_shape = [jax.ShapeDtypeStruct((n_lat, w), BF16)]
    out_specs = [pl.BlockSpec((seq, w), lambda b: (b, 0))]
    if need_ctx:
        out_shape.append(jax.ShapeDtypeStruct((batch * n_ctx, w), BF16))
        out_specs.append(pl.BlockSpec((n_ctx, w), lambda b: (b, 0)))
    return pl.pallas_call(
        functools.partial(_ret_kernel, need_ctx, seq, n_ctx),
        grid=(batch,),
        in_specs=[pl.BlockSpec(memory_space=pltpu.SMEM),
                  lat(RET_Q), lat(RET_K), lat(RET_V), lat(RET_G),
                  ctx(RET_Q), ctx(RET_K), ctx(RET_V), ctx(RET_G), whole, whole,
                  pl.BlockSpec((w, w), lambda b: (0, 0)), pl.BlockSpec((w, w), lambda b: (0, 0))],
        out_specs=out_specs,
        out_shape=out_shape,
        scratch_shapes=[pltpu.VMEM((seq, w), F32), pltpu.VMEM((seq, w), F32),
                        pltpu.VMEM((seq // ch, w, ch), F32), pltpu.VMEM((n_ctx // ch, w, ch), F32),
                        pltpu.VMEM((seq, w), F32), pltpu.VMEM((seq, w), F32),
                        pltpu.VMEM((n_ctx, w), F32), pltpu.VMEM((n_ctx, w), F32),
                        pltpu.VMEM((2 * RET_HEADS, ch, ch), F32),
                        pltpu.VMEM((RET_HEADS, ch, 128), F32), pltpu.VMEM((RET_HEADS, 128, ch), F32)],
        compiler_params=_params("parallel"),
        name="retention_mixer",
    )(log_gamma, p_all, p_all, p_all, p_all, p_all, p_all, p_all, p_all, cos, sin,
      _swap_matrix(w, 16), _block_ones(w, RET_DK))


def _swap_matrix(width, dist):
    i = np.arange(width)
    partner = np.where(i % (2 * dist) < dist, i + dist, i - dist)
    p = np.zeros((width, width), np.float32)
    p[partner, i] = 1.0
    return jnp.asarray(p, BF16)


def _block_ones(width, block):
    i = np.arange(width)
    return jnp.asarray((i[:, None] // block == i[None, :] // block).astype(np.float32), BF16)


def _sumsq_blocks(xf, bd_ref):
    sq = xf * xf
    hi = sq.astype(BF16)
    lo = (sq - hi.astype(F32)).astype(BF16)
    bd = bd_ref[...]
    return _dot(hi, bd) + _dot(lo, bd)


def _rope_mxu(x, cos, sin, perm_ref):
    return x.astype(F32) * cos + _dot(x, perm_ref[...]) * sin


def _softmax_pv(qm, k, v_ones):
    s = _dot_nt(qm, k)
    p = jnp.exp2((s - jnp.max(s, axis=-1, keepdims=True)).astype(BF16))
    oe = _dot(p, v_ones)
    return oe[:, :128] / oe[:, 128:]


def _lane_masks(n_parts):
    lane = lax.broadcasted_iota(jnp.int32, (1, 128), 1)
    return [jnp.where(lane // (128 // n_parts) == i, 1.0, 0.0).astype(BF16) for i in range(n_parts)]


def _gqa_kernel(n_lat_k, n_ctx, *refs):
    if n_lat_k:
        (q_ref, k_ref, v_ref, kc_ref, vc_ref, qa_ref, qb_ref, ka_ref, kb_ref, kg_ref, pq_ref, pk_ref,
         bdq_ref, bdk_ref, y_ref, kp, vx) = refs
    else:
        q_ref, kc_ref, vc_ref, qa_ref, kg_ref, bdq_ref, bdk_ref, y_ref, kp, vx = refs
    dim = GQA_DIM
    nk = n_lat_k + n_ctx

    def inv_rms(xf, bd_ref):
        return lax.rsqrt(_sumsq_blocks(xf, bd_ref) * (1.0 / dim) + EPS)

    @pl.when(pl.program_id(1) == 0)
    def _():
        def put(rs, kn, v):
            ones = jnp.ones(v.shape, BF16)
            kp[0, rs, :] = kn.astype(BF16)
            kp[1, rs, :] = pltpu.roll(kn, dim, 1).astype(BF16)
            vx[0, rs, :] = jnp.concatenate([v, ones], axis=-1)
            vx[1, rs, :] = jnp.concatenate([pltpu.roll(v.astype(F32), dim, 1).astype(BF16), ones], axis=-1)

        kc = kc_ref[...]
        kcf = kc.astype(F32)
        put(slice(n_lat_k, nk), kcf * kg_ref[...] * inv_rms(kcf, bdk_ref), vc_ref[...])
        ck = 512
        for c in range(n_lat_k // ck):
            rs = slice(c * ck, (c + 1) * ck)
            k = k_ref[rs, :]
            put(rs, _rope_mxu(k, ka_ref[rs, :], kb_ref[rs, :], pk_ref) * inv_rms(k.astype(F32), bdk_ref), v_ref[rs, :])

    q = q_ref[...]
    qf = q.astype(F32)
    qn = _rope_mxu(q, qa_ref[...], qb_ref[...], pq_ref) if n_lat_k else qf * qa_ref[...]
    qn = (qn * inv_rms(qf, bdq_ref)).astype(BF16)
    low, high = _lane_masks(2)
    outs = []
    for g in range(GQA_KV_HEADS):
        qv = qn[:, g * 128:(g + 1) * 128]
        o_low = _softmax_pv(qv * low, kp[g], vx[g])
        o_high = _softmax_pv(qv * high, kp[1 - g], vx[1 - g])
        outs.append(jnp.where(low > 0, o_low, o_high))
    y_ref[...] = jnp.concatenate(outs, axis=-1).astype(BF16)


def _gqa_mixer(p_all, q_gain, k_gain, batch, seq, n_ctx, need_ctx):
    n_lat = batch * seq
    cb = n_lat // n_ctx
    dim = GQA_DIM
    qw = GQA_HEADS * dim
    kvw = GQA_KV_HEADS * dim
    assert qw == 256 and kvw == 128

    def swapped(g):
        return g.reshape(-1, 2, 16)[:, ::-1].reshape(1, -1)

    qg = jnp.tile(q_gain.astype(F32) * (dim ** -0.5 * LOG2E), GQA_HEADS).reshape(1, qw)
    kg = jnp.tile(k_gain.astype(F32), GQA_KV_HEADS).reshape(1, kvw)
    cq, sq = _rope_tables(seq, dim, qw, 16)
    ck, sk = _rope_tables(seq, dim, kvw, 16)
    qa, qb = cq * qg, sq * swapped(qg)
    ka, kb = ck * kg, sk * swapped(kg)
    pq, pk = _swap_matrix(qw, 16), _swap_matrix(kvw, 16)
    bdq, bdk = _block_ones(qw, dim), _block_ones(kvw, dim)
    tq = 512
    nqb = seq // tq
    const = lambda shape: pl.BlockSpec(shape, lambda b, i: (0,) * len(shape))
    scratch = lambda nk: [pltpu.VMEM((2, nk, kvw), BF16), pltpu.VMEM((2, nk, 2 * kvw), BF16)]
    y_lat = pl.pallas_call(
        functools.partial(_gqa_kernel, seq, n_ctx),
        grid=(batch, nqb),
        in_specs=[pl.BlockSpec((tq, qw), lambda b, i: (b * nqb + i, GQA_Q)),
                  pl.BlockSpec((seq, kvw), lambda b, i: (b, 16)),
                  pl.BlockSpec((seq, kvw), lambda b, i: (b, 17)),
                  pl.BlockSpec((n_ctx, kvw), lambda b, i: (cb + b, 16)),
                  pl.BlockSpec((n_ctx, kvw), lambda b, i: (cb + b, 17)),
                  pl.BlockSpec((tq, qw), lambda b, i: (i, 0)),
                  pl.BlockSpec((tq, qw), lambda b, i: (i, 0)),
                  const((seq, kvw)), const((seq, kvw)), const((1, kvw)),
                  const((qw, qw)), const((kvw, kvw)), const((qw, qw)), const((kvw, kvw))],
        out_specs=pl.BlockSpec((tq, qw), lambda b, i: (b * nqb + i, 0)),
        out_shape=jax.ShapeDtypeStruct((n_lat, qw), BF16),
        scratch_shapes=scratch(seq + n_ctx),
        compiler_params=_params("parallel", "arbitrary"),
        name="gqa_mixer",
    )(p_all, p_all, p_all, p_all, p_all, qa, qb, ka, kb, kg, pq, pk, bdq, bdk)
    if not need_ctx:
        return y_lat, None
    y_ctx = pl.pallas_call(
        functools.partial(_gqa_kernel, 0, n_ctx),
        grid=(batch, 1),
        in_specs=[pl.BlockSpec((n_ctx, qw), lambda b, i: (cb + b, GQA_Q)),
                  pl.BlockSpec((n_ctx, kvw), lambda b, i: (cb + b, 16)),
                  pl.BlockSpec((n_ctx, kvw), lambda b, i: (cb + b, 17)),
                  const((1, qw)), const((1, kvw)), const((qw, qw)), const((kvw, kvw))],
        out_specs=pl.BlockSpec((n_ctx, qw), lambda b, i: (b, 0)),
        out_shape=jax.ShapeDtypeStruct((batch * n_ctx, qw), BF16),
        scratch_shapes=scratch(n_ctx),
        compiler_params=_params("parallel", "arbitrary"),
        name="gqa_mixer_ctx",
    )(p_all, p_all, p_all, qg, kg, bdq, bdk)
    return y_lat, y_ctx


def _diff_kernel(n_lat_k, n_ctx, lam_init, *refs):
    if n_lat_k:
        (q_ref, k_ref, v_ref, kc_ref, vc_ref, lp_ref, sg_ref, cq_ref, sq_ref, ck_ref, sk_ref, perm_ref, bd_ref,
         y_ref, kp, vx) = refs
    else:
        q_ref, kc_ref, vc_ref, lp_ref, sg_ref, bd_ref, y_ref, kp, vx = refs
    nk = n_lat_k + n_ctx
    n_pairs = DIFF_HEADS // 2

    @pl.when(pl.program_id(1) == 0)
    def _():
        def put(rs, k, v):
            kp[rs, :] = k
            ones = jnp.ones((v.shape[0], 128), BF16)
            for pr in range(n_pairs):
                vx[pr, rs, :] = jnp.concatenate([v[:, pr * 128:(pr + 1) * 128], ones], axis=-1)

        put(slice(n_lat_k, nk), kc_ref[...], vc_ref[...])
        ck = 512
        for c in range(n_lat_k // ck):
            rs = slice(c * ck, (c + 1) * ck)
            put(rs, _rope_mxu(k_ref[rs, :], ck_ref[rs, :], sk_ref[rs, :], perm_ref).astype(BF16), v_ref[rs, :])

    lp = lp_ref[...]
    lam = (jnp.exp(jnp.sum(lp[0:1, :] * lp[1:2, :], axis=-1, keepdims=True))
           - jnp.exp(jnp.sum(lp[2:3, :] * lp[3:4, :], axis=-1, keepdims=True)) + lam_init)

    q = q_ref[...]
    if n_lat_k:
        q = _rope_mxu(q, cq_ref[...], sq_ref[...], perm_ref).astype(BF16)
    quarter = _lane_masks(4)
    low = _lane_masks(2)[0]
    outs = []
    for pr in range(n_pairs):
        ps = slice(pr * 128, (pr + 1) * 128)
        qv = q[:, ps]
        kv = kp[:, ps]
        o_head = [_softmax_pv(qv * quarter[2 * hh], kv, vx[pr]) - lam * _softmax_pv(qv * quarter[2 * hh + 1], kv, vx[pr])
                  for hh in range(2)]
        outs.append(jnp.where(low > 0, o_head[0], o_head[1]))
    o = jnp.concatenate(outs, axis=-1)
    inv = lax.rsqrt(_sumsq_blocks(o, bd_ref) * (1.0 / DIFF_V_DIM) + EPS)
    y_ref[...] = (o * inv * sg_ref[...] * (1.0 - lam_init)).astype(BF16)


def _diff_mixer(p_all, lam_params, subln, layer_idx, batch, seq, n_ctx, need_ctx):
    n_lat = batch * seq
    cb = n_lat // n_ctx
    lam_init = 0.8 - 0.6 * math.exp(-0.3 * layer_idx)
    lp = jnp.zeros((8, 128), F32).at[:4, :DIFF_QK_DIM].set(lam_params.astype(F32))
    w = DIFF_HEADS * DIFF_V_DIM
    assert w == 256 and DIFF_HEADS * 2 * DIFF_QK_DIM == w
    sg = jnp.tile(subln.astype(F32), DIFF_HEADS).reshape(1, w)
    cq, sq = _rope_tables(seq, DIFF_QK_DIM, w, 8)
    perm = _swap_matrix(w, 8)
    bd = _block_ones(w, DIFF_V_DIM)
    tq = 512
    nqb = seq // tq
    const = lambda shape: pl.BlockSpec(shape, lambda b, i: (0,) * len(shape))
    scratch = lambda nk: [pltpu.VMEM((nk, w), BF16), pltpu.VMEM((DIFF_HEADS // 2, nk, 256), BF16)]
    y_lat = pl.pallas_call(
        functools.partial(_diff_kernel, seq, n_ctx, lam_init),
        grid=(batch, nqb),
        in_specs=[pl.BlockSpec((tq, w), lambda b, i: (b * nqb + i, DIFF_Q)),
                  pl.BlockSpec((seq, w), lambda b, i: (b, DIFF_K)),
                  pl.BlockSpec((seq, w), lambda b, i: (b, DIFF_V)),
                  pl.BlockSpec((n_ctx, w), lambda b, i: (cb + b, DIFF_K)),
                  pl.BlockSpec((n_ctx, w), lambda b, i: (cb + b, DIFF_V)),
                  const((8, 128)), const((1, w)),
                  pl.BlockSpec((tq, w), lambda b, i: (i, 0)),
                  pl.BlockSpec((tq, w), lambda b, i: (i, 0)),
                  const((seq, w)), const((seq, w)), const((w, w)), const((w, w))],
        out_specs=pl.BlockSpec((tq, w), lambda b, i: (b * nqb + i, 0)),
        out_shape=jax.ShapeDtypeStruct((n_lat, w), BF16),
        scratch_shapes=scratch(seq + n_ctx),
        compiler_params=_params("parallel", "arbitrary"),
        name="diff_mixer",
    )(p_all, p_all, p_all, p_all, p_all, lp, sg, cq, sq, cq, sq, perm, bd)
    if not need_ctx:
        return y_lat, None
    y_ctx = pl.pallas_call(
        functools.partial(_diff_kernel, 0, n_ctx, lam_init),
        grid=(batch, 1),
        in_specs=[pl.BlockSpec((n_ctx, w), lambda b, i: (cb + b, DIFF_Q)),
                  pl.BlockSpec((n_ctx, w), lambda b, i: (cb + b, DIFF_K)),
                  pl.BlockSpec((n_ctx, w), lambda b, i: (cb + b, DIFF_V)),
                  const((8, 128)), const((1, w)), const((w, w))],
        out_specs=pl.BlockSpec((n_ctx, w), lambda b, i: (b, 0)),
        out_shape=jax.ShapeDtypeStruct((batch * n_ctx, w), BF16),
        scratch_shapes=scratch(n_ctx),
        compiler_params=_params("parallel", "arbitrary"),
        name="diff_mixer_ctx",
    )(p_all, p_all, p_all, lp, sg, bd)
    return y_lat, y_ctx


def _merge_kernel(nlat_blk, has_ctx, n_parts, *refs):
    x_refs, refs = refs[:n_parts], refs[n_parts:]
    g_refs = refs[:N_BRANCH]
    y_refs = refs[N_BRANCH:2 * N_BRANCH]
    refs = refs[2 * N_BRANCH:]
    if has_ctx:
        yc_refs, refs = refs[:N_BRANCH], refs[N_BRANCH:]
        is_ctx = pl.program_id(0) >= nlat_blk
    wb_ref, wo_ref, mod_ref, gn_ref, wr_ref, xo_ref, m_ref, lg_ref = refs
    acc = None
    for n in range(N_BRANCH):
        y = y_refs[n][...]
        if has_ctx:
            y = jnp.where(is_ctx, yc_refs[n][...], y)
        term = g_refs[n][...].astype(F32) * _dot(y, wb_ref[n])
        acc = term if acc is None else acc + term
    y = _dot(acc.astype(BF16), wo_ref[...])
    x = _pick_rows(x_refs, nlat_blk) + mod_ref[2:3, :] * y
    xo_ref[...] = x
    xn = x * lax.rsqrt(jnp.mean(x * x, axis=-1, keepdims=True) + EPS) * gn_ref[...]
    m = xn * (1.0 + mod_ref[4:5, :]) + mod_ref[3:4, :]
    _store_tile_rows(m_ref, m)
    m_hi = m.astype(BF16)
    m_lo = (m - m_hi.astype(F32)).astype(BF16)
    w = wr_ref[...]
    w_hi = w.astype(BF16)
    w_lo = (w - w_hi.astype(F32)).astype(BF16)
    lg_ref[...] = _dot_nt(w_hi, m_hi) + (_dot_nt(w_hi, m_lo) + _dot_nt(w_lo, m_hi))


def _merge(x_parts, p_all, ys_lat, ys_ctx, wb_bf, wo_bf, modtab, gn, wr_t, tm, n_rows, n_lat, seq, batch):
    d = D_MODEL
    midx = _mod_index(tm, n_lat, seq, batch)
    nlat_blk = n_lat // tm
    has_ctx = ys_ctx is not None
    gate = lambda n: pl.BlockSpec((tm, d), lambda i, n=n: (i, MIX_COLS // d + n))
    y_specs = [pl.BlockSpec((tm, BRANCH_W), lambda i: (jnp.minimum(i, nlat_blk - 1), 0))] * N_BRANCH
    ys = list(ys_lat)
    if has_ctx:
        y_specs += [pl.BlockSpec((tm, BRANCH_W), lambda i: (jnp.maximum(i - nlat_blk, 0), 0))] * N_BRANCH
        ys += list(ys_ctx)
    return pl.pallas_call(
        functools.partial(_merge_kernel, nlat_blk, has_ctx, len(x_parts)),
        grid=(n_rows // tm,),
        in_specs=[*_row_blocks(x_parts, tm, n_lat),
                  gate(0), gate(1), gate(2), gate(3), *y_specs,
                  pl.BlockSpec((N_BRANCH, BRANCH_W, d), lambda i: (0, 0, 0)),
                  pl.BlockSpec((d, d), lambda i: (0, 0)),
                  pl.BlockSpec((None, 8, d), lambda i: (midx(i), 0, 0)),
                  pl.BlockSpec((1, d), lambda i: (0, 0)),
                  pl.BlockSpec((N_EXPERTS, d), lambda i: (0, 0))],
        out_specs=[pl.BlockSpec((tm, d), lambda i: (i, 0)),
                   pl.BlockSpec((tm * ROW_TILE, 128), lambda i: (i, 0)),
                   pl.BlockSpec((N_EXPERTS, tm), lambda i: (0, i))],
        out_shape=[jax.ShapeDtypeStruct((n_rows, d), F32),
                   jax.ShapeDtypeStruct((n_rows * ROW_TILE, 128), F32),
                   jax.ShapeDtypeStruct((N_EXPERTS, n_rows), F32)],
        compiler_params=_params("parallel"),
        name="merge_norm_route",
    )(*x_parts, p_all, p_all, p_all, p_all, *ys, wb_bf, wo_bf, modtab, gn.reshape(1, d), wr_t)


def _route_kernel(lg_ref, b_ref, tri_ref, idx_ref, w_ref, rank_ref, cnt_ref):
    s = _sigmoid(lg_ref[...])
    sel = s + b_ref[...]
    row = lambda a, e: a[e:e + 1, :]
    gsz = EXPERTS_PER_GROUP
    g_idx = None
    best = None
    for g in range(N_GROUPS):
        v = [row(sel, g * gsz + i) for i in range(gsz)]
        score = None
        for i in range(gsz):
            for j in range(i + 1, gsz):
                pair = v[i] + v[j]
                score = pair if score is None else jnp.maximum(score, pair)
        if g == 0:
            best, g_idx = score, jnp.zeros(score.shape, jnp.int32)
        else:
            better = score > best
            best = jnp.where(better, score, best)
            g_idx = jnp.where(better, g, g_idx)

    def in_group(a, i):
        out = row(a, i)
        for g in range(1, N_GROUPS):
            out = jnp.where(g_idx == g, row(a, g * gsz + i), out)
        return out

    v = [in_group(sel, i) for i in range(gsz)]
    sv = [in_group(s, i) for i in range(gsz)]

    def arg_first_max(vals):
        bv, bi = vals[0], jnp.zeros(vals[0].shape, jnp.int32)
        for i in range(1, gsz):
            better = vals[i] > bv
            bv = jnp.where(better, vals[i], bv)
            bi = jnp.where(better, i, bi)
        return bi

    i1 = arg_first_max(v)
    i2 = arg_first_max([jnp.where(i1 == i, -jnp.inf, v[i]) for i in range(gsz)])

    def pick(vals, idx):
        out = vals[0]
        for i in range(1, gsz):
            out = jnp.where(idx == i, vals[i], out)
        return out

    w1 = pick(sv, i1)
    w2 = pick(sv, i2)
    tot = w1 + w2
    e1 = g_idx * gsz + i1
    e2 = g_idx * gsz + i2
    idx_ref[0:1, :] = e1
    idx_ref[1:2, :] = e2
    w_ref[0:1, :] = w1 / tot
    w_ref[1:2, :] = w2 / tot

    @pl.when(pl.program_id(0) == 0)
    def _():
        cnt_ref[...] = jnp.zeros(cnt_ref.shape, cnt_ref.dtype)

    n_e, tn = s.shape
    erow = lax.broadcasted_iota(jnp.int32, (n_e, tn), 0)
    oh1 = jnp.where(erow == e1, 1.0, 0.0)
    oh2 = jnp.where(erow == e2, 1.0, 0.0)
    oh = (oh1 + oh2).astype(BF16)
    base = cnt_ref[...]
    r1, r2 = [], []
    for c in range(tn // 128):
        cs = slice(c * 128, (c + 1) * 128)
        before = base + _dot(oh[:, cs], tri_ref[...])
        r1.append(jnp.sum(oh1[:, cs] * before, axis=0, keepdims=True))
        r2.append(jnp.sum(oh2[:, cs] * before, axis=0, keepdims=True))
        base = base + jnp.sum(oh[:, cs].astype(F32), axis=1, keepdims=True)
    cnt_ref[...] = base
    rank_ref[0:1, :] = jnp.concatenate(r1, axis=-1).astype(jnp.int32)
    rank_ref[1:2, :] = jnp.concatenate(r2, axis=-1).astype(jnp.int32)


def _route(logits_t, router_bias):
    e, n = logits_t.shape
    tn = math.gcd(n, 2048)
    i = np.arange(128)
    tri = jnp.asarray((i[:, None] < i[None, :]).astype(np.float32), BF16)
    idx2, w2, rank, cnt = pl.pallas_call(
        _route_kernel,
        grid=(n // tn,),
        in_specs=[pl.BlockSpec((e, tn), lambda i: (0, i)),
                  pl.BlockSpec((e, 1), lambda i: (0, 0)),
                  pl.BlockSpec((128, 128), lambda i: (0, 0))],
        out_specs=[pl.BlockSpec((TOP_K, tn), lambda i: (0, i)),
                   pl.BlockSpec((TOP_K, tn), lambda i: (0, i)),
                   pl.BlockSpec((TOP_K, tn), lambda i: (0, i)),
                   pl.BlockSpec((e, 128), lambda i: (0, 0))],
        out_shape=[jax.ShapeDtypeStruct((TOP_K, n), jnp.int32),
                   jax.ShapeDtypeStruct((TOP_K, n), F32),
                   jax.ShapeDtypeStruct((TOP_K, n), jnp.int32),
                   jax.ShapeDtypeStruct((e, 128), F32)],
        compiler_params=_params("arbitrary"),
        name="route_top2",
    )(logits_t, router_bias.astype(F32).reshape(e, 1), tri)
    return idx2, w2, rank, cnt[:, 0].astype(jnp.int32)


def _dispatch_kernel(n_tok_steps, pos_ref, m_ref, xs_out, zrow, sem):
    i = pl.program_id(0)
    rt = ROW_TILE
    tm = m_ref.shape[0] // rt
    n = TOP_K * tm

    def wait():
        pltpu.make_async_copy(xs_out.at[pl.ds(0, n * rt)], xs_out.at[pl.ds(0, n * rt)], sem).wait()

    @pl.when(i < n_tok_steps)
    def _():
        for j in range(n):
            dst = pl.multiple_of(pos_ref[i, j], rt)
            pltpu.make_async_copy(m_ref.at[pl.ds((j % tm) * rt, rt)], xs_out.at[pl.ds(dst, rt)],
                                  sem).start(priority=j % 2)
        wait()

    @pl.when(i >= n_tok_steps)
    def _():
        zrow[...] = jnp.zeros(zrow.shape, zrow.dtype)
        for j in range(n):
            dst = pl.multiple_of(pos_ref[i, j], rt)
            pltpu.make_async_copy(zrow, xs_out.at[pl.ds(dst, rt)], sem).start(priority=j % 2)
        wait()


def _dispatch(m_tiles, pos, pad_slots, n_slots):
    rt = ROW_TILE
    n_tok_steps, n = pos.shape
    tm = n // TOP_K
    steps = jnp.concatenate([pos, pad_slots], axis=0)
    grid_spec = pltpu.PrefetchScalarGridSpec(
        num_scalar_prefetch=1,
        grid=(steps.shape[0],),
        in_specs=[pl.BlockSpec((tm * rt, 128), lambda i, pos: (jnp.minimum(i, n_tok_steps - 1), 0))],
        out_specs=pl.BlockSpec(memory_space=pl.ANY),
        scratch_shapes=[pltpu.VMEM((rt, 128), F32), pltpu.SemaphoreType.DMA(())])
    return pl.pallas_call(
        functools.partial(_dispatch_kernel, n_tok_steps),
        grid_spec=grid_spec,
        out_shape=jax.ShapeDtypeStruct((n_slots * rt, 128), F32),
        compiler_params=_params("arbitrary"),
        name="moe_dispatch",
    )(steps, m_tiles)


def _expert_kernel(be_ref, x_ref, wg_ref, wu_ref, wd_ref, y_ref):
    blk = x_ref.shape[0] // ROW_TILE
    x = _load_tile_rows(x_ref, 0, blk).astype(BF16)
    a = _dot(x, wg_ref[...])
    h = a * _sigmoid(a) * _dot(x, wu_ref[...])
    _store_tile_rows(y_ref, _dot(h.astype(BF16), wd_ref[...]))


def _experts(x_slots, blk_e, blk, wg_bf, wu_bf, wd_bf):
    rt = ROW_TILE
    d = D_MODEL
    n_slots = x_slots.shape[0] // rt
    grid_spec = pltpu.PrefetchScalarGridSpec(
        num_scalar_prefetch=1,
        grid=(n_slots // blk,),
        in_specs=[pl.BlockSpec((blk * rt, 128), lambda i, be: (i, 0)),
                  pl.BlockSpec((None, d, D_EXPERT), lambda i, be: (be[i], 0, 0)),
                  pl.BlockSpec((None, d, D_EXPERT), lambda i, be: (be[i], 0, 0)),
                  pl.BlockSpec((None, D_EXPERT, d), lambda i, be: (be[i], 0, 0))],
        out_specs=pl.BlockSpec((blk * rt, 128), lambda i, be: (i, 0)))
    return pl.pallas_call(
        _expert_kernel,
        grid_spec=grid_spec,
        out_shape=jax.ShapeDtypeStruct((n_slots * rt, 128), F32),
        compiler_params=_params("arbitrary"),
        name="expert_ffn",
    )(blk_e, x_slots, wg_bf, wu_bf, wd_bf)


def _combine_kernel(final, pos_ref, y_hbm, x_ref, w_ref, mod_ref, gf_ref, o_ref, ybuf, sem):
    i = pl.program_id(0)
    last = pl.num_programs(0) - 1
    slot = i % 2
    rt = ROW_TILE
    tm = x_ref.shape[0]
    n = TOP_K * tm

    def issue(step, dst_slot):
        for j in range(n):
            src = pl.multiple_of(pos_ref[step, j], rt)
            pltpu.make_async_copy(y_hbm.at[pl.ds(src, rt)], ybuf.at[dst_slot, pl.ds(j * rt, rt)],
                                  sem.at[dst_slot]).start(priority=j % 2)

    def wait(dst_slot):
        pltpu.make_async_copy(y_hbm.at[pl.ds(0, n * rt)], ybuf.at[dst_slot], sem.at[dst_slot]).wait()

    @pl.when(i == 0)
    def _():
        issue(0, 0)

    wait(slot)
    issue(jnp.minimum(i + 1, last), 1 - slot)
    w = w_ref[...]
    yb = ybuf.at[slot]
    f = w[:, 0:1] * _load_tile_rows(yb, 0, tm) + w[:, 1:2] * _load_tile_rows(yb, tm * rt, tm)
    x = x_ref[...] + mod_ref[5:6, :] * f
    if final:
        x = x * lax.rsqrt(jnp.mean(x * x, axis=-1, keepdims=True) + EPS) * gf_ref[...]
    o_ref[...] = x

    @pl.when(i == last)
    def _():
        wait(1 - slot)


def _combine(x_rows, y_slots, pos, w_tok, modtab, g_final, final, n_lat, seq, batch):
    n_rows, d = x_rows.shape
    n_steps, n = pos.shape
    tm = n // TOP_K
    midx = _mod_index(tm, n_lat, seq, batch)
    grid_spec = pltpu.PrefetchScalarGridSpec(
        num_scalar_prefetch=1,
        grid=(n_steps,),
        in_specs=[pl.BlockSpec(memory_space=pl.ANY),
                  pl.BlockSpec((tm, d), lambda i, pos: (i, 0)),
                  pl.BlockSpec((tm, TOP_K), lambda i, pos: (i, 0)),
                  pl.BlockSpec((None, 8, d), lambda i, pos: (midx(i), 0, 0)),
                  pl.BlockSpec((1, d), lambda i, pos: (0, 0))],
        out_specs=pl.BlockSpec((tm, d), lambda i, pos: (i, 0)),
        scratch_shapes=[pltpu.VMEM((2, n * ROW_TILE, 128), F32), pltpu.SemaphoreType.DMA((2,))])
    return pl.pallas_call(
        functools.partial(_combine_kernel, final),
        grid_spec=grid_spec,
        out_shape=jax.ShapeDtypeStruct((n_rows, d), F32),
        compiler_params=_params("arbitrary"),
        name="moe_combine",
    )(pos, y_slots, x_rows, w_tok, modtab, g_final.astype(F32).reshape(1, d))


def _dispatch_plan(idx2, rank, counts, blk, tm):
    n = idx2.shape[1]
    a = TOP_K * n
    padded = (counts + blk - 1) // blk * blk
    pad_end = jnp.cumsum(padded)
    pad_start = pad_end - padded
    dest = (pad_start[idx2] + rank).astype(jnp.int32)
    n_blocks = a // blk + N_EXPERTS
    blk_e = jnp.minimum(jnp.searchsorted(pad_end, jnp.arange(n_blocks, dtype=jnp.int32) * blk, side='right'),
                        N_EXPERTS - 1).astype(jnp.int32)
    pos = dest.reshape(TOP_K, n // tm, tm).transpose(1, 0, 2).reshape(n // tm, TOP_K * tm) * ROW_TILE
    n_slots = n_blocks * blk
    starts = jnp.concatenate([pad_start + counts, pad_end[-1:]])
    lens = jnp.concatenate([padded - counts, n_slots - pad_end[-1:]])
    ends = jnp.cumsum(lens)
    j = jnp.arange(N_EXPERTS * blk, dtype=jnp.int32)
    seg = jnp.searchsorted(ends, j, side='right')
    pad_slots = ((starts[seg] + j - (ends[seg] - lens[seg])) * ROW_TILE).astype(jnp.int32)
    return pos, pad_slots.reshape(-1, TOP_K * tm), blk_e, n_slots


def kernel(x, c, ctx, c_ctx, w_mod, b_mod, g_norm1, g_norm2, w_in, na_rpb, ret_log_decay, gqa_q_gain, gqa_k_gain,
           diff_lambda, diff_subln, w_branch, w_out, w_router, router_bias, w_gate_e, w_up_e, w_down_e, g_final):
    batch, seq, d = x.shape
    n_ctx = ctx.shape[1]
    n_lat = batch * seq
    n_ctx_tot = batch * n_ctx
    n_tot = n_lat + n_ctx_tot
    tm = 1024 if n_ctx_tot % 1024 == 0 else 512

    x_parts = (x.reshape(n_lat, d), ctx.reshape(n_ctx_tot, d))

    nrow = -(-(batch + 1) // 8) * 8
    cc = jnp.zeros((nrow, d), F32).at[:batch].set(c).at[batch].set(c_ctx)
    mod = _modulation(cc, w_mod, b_mod)
    modtab = jnp.pad(mod.reshape(DEPTH, nrow, 6, d), ((0, 0), (0, 0), (0, 2), (0, 0)))

    wr_t = w_router.astype(F32).T
    out = None
    for layer in range(DEPTH):
        need_ctx = layer < DEPTH - 1
        n_rows = n_tot if need_ctx else n_lat
        p_all = _project(x_parts, g_norm1[layer], modtab[layer], w_in[layer].astype(BF16), tm, n_lat, seq, batch)
        ya = _na_mixer(p_all, na_rpb[layer], batch, seq, n_ctx, need_ctx)
        yb = _ret_mixer(p_all, ret_log_decay[layer], batch, seq, n_ctx, need_ctx)
        yc = _gqa_mixer(p_all, gqa_q_gain[layer], gqa_k_gain[layer], batch, seq, n_ctx, need_ctx)
        yd = _diff_mixer(p_all, diff_lambda[layer], diff_subln[layer], layer, batch, seq, n_ctx, need_ctx)
        ys_lat = [ya[0], yb[0], yc[0], yd[0]]
        ys_ctx = [ya[1], yb[1], yc[1], yd[1]] if need_ctx else None
        x_mid, m_all, logits_t = _merge(x_parts, p_all, ys_lat, ys_ctx, w_branch[layer].astype(BF16),
                                        w_out[layer].astype(BF16), modtab[layer], g_norm2[layer], wr_t, MERGE_TM,
                                        n_rows, n_lat, seq, batch)
        idx2, w2, rank, counts = _route(logits_t, router_bias)
        pos, pad_slots, blk_e, n_slots = _dispatch_plan(idx2, rank, counts, MOE_BLK, CMB_TM)
        x_slots = _dispatch(m_all, pos, pad_slots, n_slots)
        y_slots = _experts(x_slots, blk_e, MOE_BLK, w_gate_e[layer].astype(BF16), w_up_e[layer].astype(BF16),
                           w_down_e[layer].astype(BF16))
        out = _combine(x_mid, y_slots, pos, w2.T, modtab[layer], g_final, not need_ctx, n_lat, seq, batch)
        x_parts = (out,)
    return out.reshape(batch, seq, d)
```

```python
import functools
import math

import numpy as np
import jax
import jax.numpy as jnp
from jax import lax
from jax.experimental import pallas as pl
from jax.experimental.pallas import tpu as pltpu

F32 = jnp.float32
BF16 = jnp.bfloat16

D_MODEL = 1024
DEPTH = 2
GRID_W = 64
NA_HEADS = 4
NA_DIM = 64
NA_WIN_H = 8
NA_WIN_W = 16
RET_HEADS = 4
RET_DK = 64
RET_CHUNK = 128
GQA_HEADS = 4
GQA_KV_HEADS = 2
GQA_DIM = 64
DIFF_HEADS = 4
DIFF_QK_DIM = 32
DIFF_V_DIM = 64
N_BRANCH = 4
BRANCH_W = 256
ROPE_THETA = 10000.0
EPS = 1e-6
NEG_INF = -1e30
N_EXPERTS = 16
N_GROUPS = 4
EXPERTS_PER_GROUP = 4
TOP_K = 2
D_EXPERT = 512

MIX_COLS = 3072
IN_COLS = MIX_COLS + N_BRANCH * D_MODEL
NA_Q, NA_K, NA_V = 0, 1, 2
RET_Q, RET_K, RET_V, RET_G = 3, 4, 5, 6
GQA_Q = 7
DIFF_Q, DIFF_K, DIFF_V = 9, 10, 11

LOG2E = math.log2(math.e)
NA_GROUP_ROWS = 4
NA_BAND_ROWS = 12

VMEM_LIMIT = 56 * 1024 * 1024
MOE_BLK = 512
CMB_TM = 512
MERGE_TM = 512


ROW_TILE = D_MODEL // 128


def _store_tile_rows(ref, x):
    n = x.shape[0]
    for j in range(ROW_TILE):
        ref[pl.ds(j, n, stride=ROW_TILE), :] = x[:, j * 128:(j + 1) * 128]


def _load_tile_rows(ref, start, n):
    return jnp.concatenate([ref[pl.ds(start + j, n, stride=ROW_TILE), :] for j in range(ROW_TILE)], axis=-1)


def _dot(a, b):
    return jnp.dot(a, b, preferred_element_type=F32)


def _dot_nt(a, b):
    return lax.dot_general(a, b, (((1,), (1,)), ((), ())), preferred_element_type=F32)


def _dot_tn(a, b):
    return lax.dot_general(a, b, (((0,), (0,)), ((), ())), preferred_element_type=F32)


def _sigmoid(x):
    return 1.0 / (1.0 + jnp.exp(-x))


def _params(*sem):
    return pltpu.CompilerParams(dimension_semantics=sem, vmem_limit_bytes=VMEM_LIMIT)


def _swap_halves(x, dist):
    pieces = []
    for j in range(x.shape[-1] // 128):
        xs = x[:, j * 128:(j + 1) * 128]
        lane = lax.broadcasted_iota(jnp.int32, xs.shape, 1)
        up = pltpu.roll(xs, 128 - dist, 1)
        dn = pltpu.roll(xs, dist, 1)
        pieces.append(jnp.where((lane % (2 * dist)) < dist, up, dn))
    return pieces[0] if len(pieces) == 1 else jnp.concatenate(pieces, axis=-1)


def _rope(x, cos, sin, dist):
    return x * cos + _swap_halves(x, dist) * sin


def _mod_kernel(c_ref, w_ref, b_ref, o_ref):
    c = c_ref[...]
    a = (c * _sigmoid(c)).astype(BF16)
    o_ref[...] = _dot(a, w_ref[...].astype(BF16)) + b_ref[...]


def _modulation(cc, w_mod, b_mod):
    nrow = cc.shape[0]
    depth, d, n6 = w_mod.shape
    tn = 1536
    return pl.pallas_call(
        _mod_kernel,
        grid=(depth, n6 // tn),
        in_specs=[pl.BlockSpec((nrow, d), lambda l, j: (0, 0)),
                  pl.BlockSpec((None, d, tn), lambda l, j: (l, 0, j)),
                  pl.BlockSpec((None, 1, tn), lambda l, j: (l, 0, j))],
        out_specs=pl.BlockSpec((None, nrow, tn), lambda l, j: (l, 0, j)),
        out_shape=jax.ShapeDtypeStruct((depth, nrow, n6), F32),
        compiler_params=_params("parallel", "arbitrary"),
        name="modulation",
    )(cc, w_mod, b_mod.reshape(depth, 1, n6))


def _mix_col_scale():
    cs = np.ones((1, MIX_COLS), np.float32)
    cs[0, NA_Q * 256:(NA_Q + 1) * 256] = NA_DIM ** -0.5 * LOG2E
    cs[0, DIFF_Q * 256:(DIFF_Q + 1) * 256] = DIFF_QK_DIM ** -0.5 * LOG2E
    cs[0, RET_K * 256:(RET_K + 1) * 256] = RET_DK ** -0.5
    return jnp.asarray(cs)


def _row_blocks(parts, tm, n_lat):
    d = parts[0].shape[1]
    nlat_blk = n_lat // tm
    if len(parts) == 1:
        return [pl.BlockSpec((tm, d), lambda i, *_: (i, 0))]
    return [pl.BlockSpec((tm, d), lambda i, *_: (jnp.minimum(i, nlat_blk - 1), 0)),
            pl.BlockSpec((tm, d), lambda i, *_: (jnp.maximum(i - nlat_blk, 0), 0))]


def _pick_rows(x_refs, nlat_blk):
    if len(x_refs) == 1:
        return x_refs[0][...]
    return jnp.where(pl.program_id(0) >= nlat_blk, x_refs[1][...], x_refs[0][...])


def _proj_kernel(nlat_blk, n_parts, *refs):
    x_refs, (g_ref, mod_ref, w_ref, cs_ref, o_ref, h_ref) = refs[:n_parts], refs[n_parts:]
    j = pl.program_id(1)

    @pl.when(j == 0)
    def _():
        x = _pick_rows(x_refs, nlat_blk)
        y = x * lax.rsqrt(jnp.mean(x * x, axis=-1, keepdims=True) + EPS) * g_ref[...]
        h_ref[...] = (y * (1.0 + mod_ref[1:2, :]) + mod_ref[0:1, :]).astype(BF16)

    acc = _dot(h_ref[...], w_ref[...])
    gate = 0.5 * jnp.tanh(0.5 * acc) + 0.5
    o_ref[...] = jnp.where(j < MIX_COLS // D_MODEL, acc * cs_ref[...], gate).astype(BF16)


def _mod_index(tm, n_lat, seq, batch):
    nlat_blk = n_lat // tm
    bpb = seq // tm

    def index(i):
        return jnp.where(i < nlat_blk, i // bpb, batch)
    return index


def _project(x_parts, g, modtab, w_bf, tm, n_lat, seq, batch):
    n_tot = sum(p.shape[0] for p in x_parts)
    d = x_parts[0].shape[1]
    ncol = w_bf.shape[1]
    tn = D_MODEL
    n_mix = MIX_COLS // tn
    midx = _mod_index(tm, n_lat, seq, batch)
    return pl.pallas_call(
        functools.partial(_proj_kernel, n_lat // tm, len(x_parts)),
        grid=(n_tot // tm, ncol // tn),
        in_specs=[*_row_blocks(x_parts, tm, n_lat),
                  pl.BlockSpec((1, d), lambda i, j: (0, 0)),
                  pl.BlockSpec((None, 8, d), lambda i, j: (midx(i), 0, 0)),
                  pl.BlockSpec((d, tn), lambda i, j: (0, j)),
                  pl.BlockSpec((1, tn), lambda i, j: (0, jnp.minimum(j, n_mix - 1)))],
        out_specs=pl.BlockSpec((tm, tn), lambda i, j: (i, j)),
        out_shape=jax.ShapeDtypeStruct((n_tot, ncol), BF16),
        scratch_shapes=[pltpu.VMEM((tm, d), BF16)],
        compiler_params=_params("parallel", "arbitrary"),
        name="norm_project",
    )(*x_parts, g.reshape(1, d), modtab, w_bf, _mix_col_scale())


def _na_kernel(need_ctx, rows, q_ref, k_ref, v_ref, qc_ref, kc_ref, vc_ref, tb_ref, y_ref, *rest):
    gq = NA_GROUP_ROWS * GRID_W
    band = NA_BAND_ROWS * GRID_W
    n_groups = rows // NA_GROUP_ROWS

    def group_body(g, carry):
        u = jnp.clip(g * NA_GROUP_ROWS - NA_WIN_H // 2, 0, rows - NA_BAND_ROWS)
        typ = jnp.where(g == 0, 0, jnp.where(g == n_groups - 1, 2, 1))
        qs = pl.multiple_of(g * gq, gq)
        ks = pl.multiple_of(u * GRID_W, GRID_W)
        for h in range(NA_HEADS):
            hs = slice(h * NA_DIM, (h + 1) * NA_DIM)
            q = q_ref[pl.ds(qs, gq), hs]
            vc = vc_ref[:, hs]
            s_w = _dot_nt(q, k_ref[pl.ds(ks, band), hs]) + tb_ref[h, typ]
            s_c = _dot_nt(q, kc_ref[:, hs])
            m = jnp.maximum(jnp.max(s_w, axis=-1, keepdims=True), jnp.max(s_c, axis=-1, keepdims=True))
            p_w = jnp.exp2(s_w - m)
            p_c = jnp.exp2(s_c - m)
            l = jnp.sum(p_w, axis=-1, keepdims=True) + jnp.sum(p_c, axis=-1, keepdims=True)
            o = (_dot(p_w.astype(BF16), v_ref[pl.ds(ks, band), hs]) + _dot(p_c.astype(BF16), vc)) / l
            y_ref[pl.ds(qs, gq), hs] = o.astype(BF16)
        return carry

    lax.fori_loop(0, n_groups, group_body, 0)

    if need_ctx:
        yc_ref = rest[0]
        for h in range(NA_HEADS):
            hs = slice(h * NA_DIM, (h + 1) * NA_DIM)
            s = _dot_nt(qc_ref[:, hs], kc_ref[:, hs])
            p = jnp.exp2(s - jnp.max(s, axis=-1, keepdims=True))
            l = jnp.sum(p, axis=-1, keepdims=True)
            yc_ref[:, hs] = (_dot(p.astype(BF16), vc_ref[:, hs]) / l).astype(BF16)


def _na_bias_table(rpb, rows):
    assert rows % NA_GROUP_ROWS == 0 and rows >= NA_BAND_ROWS + NA_GROUP_ROWS
    n_groups = rows // NA_GROUP_ROWS
    qc = np.arange(GRID_W)[:, None]
    kc = np.arange(GRID_W)[None, :]
    win_start = np.clip(qc - NA_WIN_W // 2, 0, GRID_W - NA_WIN_W)
    col_ok = (kc >= win_start) & (kc < win_start + NA_WIN_W)
    col_idx = np.clip(kc - qc, -(NA_WIN_W - 1), NA_WIN_W - 1) + NA_WIN_W - 1
    n_col = 2 * NA_WIN_W - 1
    onehot = jnp.asarray((np.arange(n_col)[:, None] == col_idx.reshape(1, -1)).astype(np.float32))
    picked = jnp.dot(rpb.astype(F32).reshape(-1, n_col), onehot, precision=lax.Precision.HIGHEST)
    picked = picked.reshape(NA_HEADS, 2 * NA_WIN_H - 1, GRID_W, GRID_W)
    tiles = jnp.where(col_ok[None, None], picked * LOG2E, NEG_INF)
    masked = jnp.full((NA_HEADS, GRID_W, GRID_W), NEG_INF, F32)
    tables = []
    for g in (0, 1, n_groups - 1):
        u = int(np.clip(g * NA_GROUP_ROWS - NA_WIN_H // 2, 0, rows - NA_BAND_ROWS))
        per_row = []
        for a in range(NA_GROUP_ROWS):
            r = g * NA_GROUP_ROWS + a
            r0 = int(np.clip(r - NA_WIN_H // 2, 0, rows - NA_WIN_H))
            pieces = [tiles[:, kr - r + NA_WIN_H - 1] if r0 <= kr < r0 + NA_WIN_H else masked
                      for kr in range(u, u + NA_BAND_ROWS)]
            per_row.append(jnp.concatenate(pieces, axis=-1))
        tables.append(jnp.concatenate(per_row, axis=1))
    return jnp.stack(tables, axis=1)


def _na_mixer(p_all, rpb, batch, seq, n_ctx, need_ctx):
    n_lat = batch * seq
    rows = seq // GRID_W
    tb = _na_bias_table(rpb, rows)
    cb = n_lat // n_ctx
    w = 256
    lat = lambda c: pl.BlockSpec((seq, w), lambda b, c=c: (b, c))
    ctx = lambda c: pl.BlockSpec((n_ctx, w), lambda b, c=c: (cb + b, c))
    out_shape = [jax.ShapeDtypeStruct((n_lat, w), BF16)]
    out_specs = [pl.BlockSpec((seq, w), lambda b: (b, 0))]
    if need_ctx:
        out_shape.append(jax.ShapeDtypeStruct((batch * n_ctx, w), BF16))
        out_specs.append(pl.BlockSpec((n_ctx, w), lambda b: (b, 0)))
    return pl.pallas_call(
        functools.partial(_na_kernel, need_ctx, rows),
        grid=(batch,),
        in_specs=[lat(NA_Q), lat(NA_K), lat(NA_V), ctx(NA_Q), ctx(NA_K), ctx(NA_V),
                  pl.BlockSpec(tb.shape, lambda b: (0, 0, 0, 0))],
        out_specs=out_specs,
        out_shape=out_shape,
        compiler_params=_params("parallel"),
        name="na_mixer",
    )(p_all, p_all, p_all, p_all, p_all, p_all, tb)


def _ret_kernel(need_ctx, seq, n_ctx, lg_ref, q_ref, k_ref, v_ref, g_ref, qc_ref, kc_ref, vc_ref, gc_ref,
                cos_ref, sin_ref, perm_ref, bd_ref, y_ref, *rest):
    if need_ctx:
        yc_ref, qr, kr, krt, kct, o_f, o_b, oc_f, oc_b, dm, qd, kdt = rest
    else:
        qr, kr, krt, kct, o_f, o_b, oc_f, oc_b, dm, qd, kdt = rest
    ch = RET_CHUNK
    n_lat_ch = seq // ch
    n_ctx_ch = n_ctx // ch
    n_pairs = RET_HEADS // 2
    assert ch == 128 and 2 * RET_DK == 128

    def prep(i, carry):
        rs = pl.multiple_of(i * ch, ch)
        c = cos_ref[pl.ds(rs, ch), :]
        s = sin_ref[pl.ds(rs, ch), :]
        qr[pl.ds(rs, ch), :] = _rope_mxu(q_ref[pl.ds(rs, ch), :], c, s, perm_ref)
        k = _rope_mxu(k_ref[pl.ds(rs, ch), :], c, s, perm_ref)
        kr[pl.ds(rs, ch), :] = k
        for p in range(n_pairs):
            krt[i, p * 128:(p + 1) * 128, :] = k[:, p * 128:(p + 1) * 128].T
        return carry

    lax.fori_loop(0, n_lat_ch, prep, 0)
    for n in range(n_ctx_ch):
        kc = kc_ref[n * ch:(n + 1) * ch, :].astype(F32)
        for p in range(n_pairs):
            kct[n, p * 128:(p + 1) * 128, :] = kc[:, p * 128:(p + 1) * 128].T

    ri = lax.broadcasted_iota(jnp.int32, (ch, ch), 0)
    ci = lax.broadcasted_iota(jnp.int32, (ch, ch), 1)
    lag = (ri - ci).astype(F32)
    rowf = ri.astype(F32)
    colf = ci.astype(F32)
    low_lane = ci < RET_DK
    low_row = ri < RET_DK
    block_diag = jnp.where(low_lane == low_row, 1.0, 0.0)
    low, high = _lane_masks(2)

    chains = [(p, dirn) for p in range(n_pairs) for dirn in range(2)]
    cdec = []
    for c, (p, dirn) in enumerate(chains):
        lg_a = lg_ref[dirn, 2 * p]
        lg_b = lg_ref[dirn, 2 * p + 1]
        lg_lane = jnp.where(low_lane, lg_a, lg_b)
        lg_row = jnp.where(low_row, lg_a, lg_b)
        for hh, lg in enumerate((lg_a, lg_b)):
            if dirn == 0:
                keep = ri >= ci
                dm[2 * c + hh] = jnp.where(keep, jnp.exp(jnp.where(keep, lag, 0.0) * lg), 0.0)
            else:
                keep = ci >= ri
                dm[2 * c + hh] = jnp.where(keep, jnp.exp(jnp.where(keep, -lag, 0.0) * lg), 0.0)
        if dirn == 0:
            qd[c] = jnp.exp((rowf + 1.0) * lg_lane)
            kdt[c] = jnp.exp((ch - 1.0 - colf) * lg_row)
        else:
            qd[c] = jnp.exp((ch - rowf) * lg_lane)
            kdt[c] = jnp.exp(colf * lg_row)
        cdec.append(jnp.exp(ch * lg_lane[0:1, :]))

    def step(c, state, qi, ki, kti, vi):
        kb = ki.astype(BF16)
        inner_a = (_dot_nt((qi * low).astype(BF16), kb) * dm[2 * c]).astype(BF16)
        inner_b = (_dot_nt((qi * high).astype(BF16), kb) * dm[2 * c + 1]).astype(BF16)
        o = (jnp.where(low_lane, _dot(inner_a, vi), _dot(inner_b, vi))
             + _dot((qi * qd[c]).astype(BF16), state.astype(BF16)))
        state = state * cdec[c] + block_diag * _dot((kti * kdt[c]).astype(BF16), vi)
        return state, o

    states = []
    for c, (p, dirn) in enumerate(chains):
        ps = slice(p * 128, (p + 1) * 128)
        state = jnp.zeros((128, 128), F32)
        for n in (range(n_ctx_ch) if dirn == 0 else range(n_ctx_ch - 1, -1, -1)):
            rs = slice(n * ch, (n + 1) * ch)
            state, o = step(c, state, qc_ref[rs, ps].astype(F32), kc_ref[rs, ps].astype(F32), kct[n, ps, :],
                            vc_ref[rs, ps])
            if need_ctx:
                (oc_f if dirn == 0 else oc_b)[rs, ps] = o
        states.append(state)

    def lat_body(i, states):
        new = []
        for c, (p, dirn) in enumerate(chains):
            ps = slice(p * 128, (p + 1) * 128)
            n = i if dirn == 0 else n_lat_ch - 1 - i
            rs = pl.ds(pl.multiple_of(n * ch, ch), ch)
            state, o = step(c, states[c], qr[rs, ps], kr[rs, ps], krt[n, ps, :], v_ref[rs, ps])
            (o_f if dirn == 0 else o_b)[rs, ps] = o
            new.append(state)
        return tuple(new)

    lax.fori_loop(0, n_lat_ch, lat_body, tuple(states))

    def head_mean(x):
        hi = x.astype(BF16)
        lo = (x - hi.astype(F32)).astype(BF16)
        bd = bd_ref[...]
        return (_dot(hi, bd) + _dot(lo, bd)) * (1.0 / RET_DK)

    def finish(o, g):
        cen = o - head_mean(o)
        on = cen * lax.rsqrt(head_mean(cen * cen) + EPS)
        g = g.astype(F32)
        return (g * _sigmoid(g) * on).astype(BF16)

    def fin_body(i, carry):
        rs = pl.multiple_of(i * ch, ch)
        y_ref[pl.ds(rs, ch), :] = finish(o_f[pl.ds(rs, ch), :] + o_b[pl.ds(rs, ch), :], g_ref[pl.ds(rs, ch), :])
        return carry

    lax.fori_loop(0, n_lat_ch, fin_body, 0)
    if need_ctx:
        for n in range(n_ctx_ch):
            rs = slice(n * ch, (n + 1) * ch)
            yc_ref[rs, :] = finish(oc_f[rs, :] + oc_b[rs, :], gc_ref[rs, :])


def _rope_tables(seq, head_dim, width, dist):
    half = head_dim // 2
    nf = half // 2
    assert nf == dist
    inv = 1.0 / (ROPE_THETA ** (np.arange(nf, dtype=np.float32) / nf))
    t = np.arange(seq)
    rows = (t // GRID_W).astype(np.float32)[:, None] * inv[None, :]
    cols = (t % GRID_W).astype(np.float32)[:, None] * inv[None, :]
    cos = np.concatenate([np.cos(rows), np.cos(rows), np.cos(cols), np.cos(cols)], axis=-1)
    sin = np.concatenate([-np.sin(rows), np.sin(rows), -np.sin(cols), np.sin(cols)], axis=-1)
    reps = width // head_dim
    return (jnp.asarray(np.tile(cos, (1, reps)), F32), jnp.asarray(np.tile(sin, (1, reps)), F32))


def _ret_mixer(p_all, log_decay, batch, seq, n_ctx, need_ctx):
    n_lat = batch * seq
    cb = n_lat // n_ctx
    w = 256
    ch = RET_CHUNK
    log_gamma = jnp.log1p(-jnp.exp(log_decay.astype(F32)))
    cos, sin = _rope_tables(seq, RET_DK, w, 16)
    lat = lambda c: pl.BlockSpec((seq, w), lambda b, c=c: (b, c))
    ctx = lambda c: pl.BlockSpec((n_ctx, w), lambda b, c=c: (cb + b, c))
    whole = pl.BlockSpec((seq, w), lambda b: (0, 0))
    out_shape = [jax.ShapeDtypeStruct((n_lat, w), BF16)]
    out_specs = [pl.BlockSpec((seq, w), lambda b: (b, 0))]
    if need_ctx:
        out_shape.append(jax.ShapeDtypeStruct((batch * n_ctx, w), BF16))
        out_specs.append(pl.BlockSpec((n_ctx, w), lambda b: (b, 0)))
    return pl.pallas_call(
        functools.partial(_ret_kernel, need_ctx, seq, n_ctx),
        grid=(batch,),
        in_specs=[pl.BlockSpec(memory_space=pltpu.SMEM),
                  lat(RET_Q), lat(RET_K), lat(RET_V), lat(RET_G),
                  ctx(RET_Q), ctx(RET_K), ctx(RET_V), ctx(RET_G), whole, whole,
                  pl.BlockSpec((w, w), lambda b: (0, 0)), pl.BlockSpec((w, w), lambda b: (0, 0))],
        out_specs=out_specs,
        out_shape=out_shape,
        scratch_shapes=[pltpu.VMEM((seq, w), F32), pltpu.VMEM((seq, w), F32),
                        pltpu.VMEM((seq // ch, w, ch), F32), pltpu.VMEM((n_ctx // ch, w, ch), F32),
                        pltpu.VMEM((seq, w), F32), pltpu.VMEM((seq, w), F32),
                        pltpu.VMEM((n_ctx, w), F32), pltpu.VMEM((n_ctx, w), F32),
                        pltpu.VMEM((2 * RET_HEADS, ch, ch), F32),
                        pltpu.VMEM((RET_HEADS, ch, 128), F32), pltpu.VMEM((RET_HEADS, 128, ch), F32)],
        compiler_params=_params("parallel"),
        name="retention_mixer",
    )(log_gamma, p_all, p_all, p_all, p_all, p_all, p_all, p_all, p_all, cos, sin,
      _swap_matrix(w, 16), _block_ones(w, RET_DK))


def _swap_matrix(width, dist):
    i = np.arange(width)
    partner = np.where(i % (2 * dist) < dist, i + dist, i - dist)
    p = np.zeros((width, width), np.float32)
    p[partner, i] = 1.0
    return jnp.asarray(p, BF16)


def _block_ones(width, block):
    i = np.arange(width)
    return jnp.asarray((i[:, None] // block == i[None, :] // block).astype(np.float32), BF16)


def _sumsq_blocks(xf, bd_ref):
    sq = xf * xf
    hi = sq.astype(BF16)
    lo = (sq - hi.astype(F32)).astype(BF16)
    bd = bd_ref[...]
    return _dot(hi, bd) + _dot(lo, bd)


def _rope_mxu(x, cos, sin, perm_ref):
    return x.astype(F32) * cos + _dot(x, perm_ref[...]) * sin


def _softmax_pv(qm, k, v_ones):
    s = _dot_nt(qm, k)
    p = jnp.exp2((s - jnp.max(s, axis=-1, keepdims=True)).astype(BF16))
    oe = _dot(p, v_ones)
    return oe[:, :128] / oe[:, 128:]


def _lane_masks(n_parts):
    lane = lax.broadcasted_iota(jnp.int32, (1, 128), 1)
    return [jnp.where(lane // (128 // n_parts) == i, 1.0, 0.0).astype(BF16) for i in range(n_parts)]


def _gqa_kernel(n_lat_k, n_ctx, *refs):
    if n_lat_k:
        (q_ref, k_ref, v_ref, kc_ref, vc_ref, qa_ref, qb_ref, ka_ref, kb_ref, kg_ref, pq_ref, pk_ref,
         bdq_ref, bdk_ref, y_ref, kp, vx) = refs
    else:
        q_ref, kc_ref, vc_ref, qa_ref, kg_ref, bdq_ref, bdk_ref, y_ref, kp, vx = refs
    dim = GQA_DIM
    nk = n_lat_k + n_ctx

    def inv_rms(xf, bd_ref):
        return lax.rsqrt(_sumsq_blocks(xf, bd_ref) * (1.0 / dim) + EPS)

    @pl.when(pl.program_id(1) == 0)
    def _():
        def put(rs, kn, v):
            ones = jnp.ones(v.shape, BF16)
            kp[0, rs, :] = kn.astype(BF16)
            kp[1, rs, :] = pltpu.roll(kn, dim, 1).astype(BF16)
            vx[0, rs, :] = jnp.concatenate([v, ones], axis=-1)
            vx[1, rs, :] = jnp.concatenate([pltpu.roll(v.astype(F32), dim, 1).astype(BF16), ones], axis=-1)

        kc = kc_ref[...]
        kcf = kc.astype(F32)
        put(slice(n_lat_k, nk), kcf * kg_ref[...] * inv_rms(kcf, bdk_ref), vc_ref[...])
        ck = 512
        for c in range(n_lat_k // ck):
            rs = slice(c * ck, (c + 1) * ck)
            k = k_ref[rs, :]
            put(rs, _rope_mxu(k, ka_ref[rs, :], kb_ref[rs, :], pk_ref) * inv_rms(k.astype(F32), bdk_ref), v_ref[rs, :])

    q = q_ref[...]
    qf = q.astype(F32)
    qn = _rope_mxu(q, qa_ref[...], qb_ref[...], pq_ref) if n_lat_k else qf * qa_ref[...]
    qn = (qn * inv_rms(qf, bdq_ref)).astype(BF16)
    low, high = _lane_masks(2)
    outs = []
    for g in range(GQA_KV_HEADS):
        qv = qn[:, g * 128:(g + 1) * 128]
        o_low = _softmax_pv(qv * low, kp[g], vx[g])
        o_high = _softmax_pv(qv * high, kp[1 - g], vx[1 - g])
        outs.append(jnp.where(low > 0, o_low, o_high))
    y_ref[...] = jnp.concatenate(outs, axis=-1).astype(BF16)


def _gqa_mixer(p_all, q_gain, k_gain, batch, seq, n_ctx, need_ctx):
    n_lat = batch * seq
    cb = n_lat // n_ctx
    dim = GQA_DIM
    qw = GQA_HEADS * dim
    kvw = GQA_KV_HEADS * dim
    assert qw == 256 and kvw == 128

    def swapped(g):
        return g.reshape(-1, 2, 16)[:, ::-1].reshape(1, -1)

    qg = jnp.tile(q_gain.astype(F32) * (dim ** -0.5 * LOG2E), GQA_HEADS).reshape(1, qw)
    kg = jnp.tile(k_gain.astype(F32), GQA_KV_HEADS).reshape(1, kvw)
    cq, sq = _rope_tables(seq, dim, qw, 16)
    ck, sk = _rope_tables(seq, dim, kvw, 16)
    qa, qb = cq * qg, sq * swapped(qg)
    ka, kb = ck * kg, sk * swapped(kg)
    pq, pk = _swap_matrix(qw, 16), _swap_matrix(kvw, 16)
    bdq, bdk = _block_ones(qw, dim), _block_ones(kvw, dim)
    tq = 512
    nqb = seq // tq
    const = lambda shape: pl.BlockSpec(shape, lambda b, i: (0,) * len(shape))
    scratch = lambda nk: [pltpu.VMEM((2, nk, kvw), BF16), pltpu.VMEM((2, nk, 2 * kvw), BF16)]
    y_lat = pl.pallas_call(
        functools.partial(_gqa_kernel, seq, n_ctx),
        grid=(batch, nqb),
        in_specs=[pl.BlockSpec((tq, qw), lambda b, i: (b * nqb + i, GQA_Q)),
                  pl.BlockSpec((seq, kvw), lambda b, i: (b, 16)),
                  pl.BlockSpec((seq, kvw), lambda b, i: (b, 17)),
                  pl.BlockSpec((n_ctx, kvw), lambda b, i: (cb + b, 16)),
                  pl.BlockSpec((n_ctx, kvw), lambda b, i: (cb + b, 17)),
                  pl.BlockSpec((tq, qw), lambda b, i: (i, 0)),
                  pl.BlockSpec((tq, qw), lambda b, i: (i, 0)),
                  const((seq, kvw)), const((seq, kvw)), const((1, kvw)),
                  const((qw, qw)), const((kvw, kvw)), const((qw, qw)), const((kvw, kvw))],
        out_specs=pl.BlockSpec((tq, qw), lambda b, i: (b * nqb + i, 0)),
        out_shape=jax.ShapeDtypeStruct((n_lat, qw), BF16),
        scratch_shapes=scratch(seq + n_ctx),
        compiler_params=_params("parallel", "arbitrary"),
        name="gqa_mixer",
    )(p_all, p_all, p_all, p_all, p_all, qa, qb, ka, kb, kg, pq, pk, bdq, bdk)
    if not need_ctx:
        return y_lat, None
    y_ctx = pl.pallas_call(
        functools.partial(_gqa_kernel, 0, n_ctx),
        grid=(batch, 1),
        in_specs=[pl.BlockSpec((n_ctx, qw), lambda b, i: (cb + b, GQA_Q)),
                  pl.BlockSpec((n_ctx, kvw), lambda b, i: (cb + b, 16)),
                  pl.BlockSpec((n_ctx, kvw), lambda b, i: (cb + b, 17)),
                  const((1, qw)), const((1, kvw)), const((qw, qw)), const((kvw, kvw))],
        out_specs=pl.BlockSpec((n_ctx, qw), lambda b, i: (b, 0)),
        out_shape=jax.ShapeDtypeStruct((batch * n_ctx, qw), BF16),
        scratch_shapes=scratch(n_ctx),
        compiler_params=_params("parallel", "arbitrary"),
        name="gqa_mixer_ctx",
    )(p_all, p_all, p_all, qg, kg, bdq, bdk)
    return y_lat, y_ctx


def _diff_kernel(n_lat_k, n_ctx, lam_init, *refs):
    if n_lat_k:
        (q_ref, k_ref, v_ref, kc_ref, vc_ref, lp_ref, sg_ref, cq_ref, sq_ref, ck_ref, sk_ref, perm_ref, bd_ref,
         y_ref, kp, vx) = refs
    else:
        q_ref, kc_ref, vc_ref, lp_ref, sg_ref, bd_ref, y_ref, kp, vx = refs
    nk = n_lat_k + n_ctx
    n_pairs = DIFF_HEADS // 2

    @pl.when(pl.program_id(1) == 0)
    def _():
        def put(rs, k, v):
            kp[rs, :] = k
            ones = jnp.ones((v.shape[0], 128), BF16)
            for pr in range(n_pairs):
                vx[pr, rs, :] = jnp.concatenate([v[:, pr * 128:(pr + 1) * 128], ones], axis=-1)

        put(slice(n_lat_k, nk), kc_ref[...], vc_ref[...])
        ck = 512
        for c in range(n_lat_k // ck):
            rs = slice(c * ck, (c + 1) * ck)
            put(rs, _rope_mxu(k_ref[rs, :], ck_ref[rs, :], sk_ref[rs, :], perm_ref).astype(BF16), v_ref[rs, :])

    lp = lp_ref[...]
    lam = (jnp.exp(jnp.sum(lp[0:1, :] * lp[1:2, :], axis=-1, keepdims=True))
           - jnp.exp(jnp.sum(lp[2:3, :] * lp[3:4, :], axis=-1, keepdims=True)) + lam_init)

    q = q_ref[...]
    if n_lat_k:
        q = _rope_mxu(q, cq_ref[...], sq_ref[...], perm_ref).astype(BF16)
    quarter = _lane_masks(4)
    low = _lane_masks(2)[0]
    outs = []
    for pr in range(n_pairs):
        ps = slice(pr * 128, (pr + 1) * 128)
        qv = q[:, ps]
        kv = kp[:, ps]
        o_head = [_softmax_pv(qv * quarter[2 * hh], kv, vx[pr]) - lam * _softmax_pv(qv * quarter[2 * hh + 1], kv, vx[pr])
                  for hh in range(2)]
        outs.append(jnp.where(low > 0, o_head[0], o_head[1]))
    o = jnp.concatenate(outs, axis=-1)
    inv = lax.rsqrt(_sumsq_blocks(o, bd_ref) * (1.0 / DIFF_V_DIM) + EPS)
    y_ref[...] = (o * inv * sg_ref[...] * (1.0 - lam_init)).astype(BF16)


def _diff_mixer(p_all, lam_params, subln, layer_idx, batch, seq, n_ctx, need_ctx):
    n_lat = batch * seq
    cb = n_lat // n_ctx
    lam_init = 0.8 - 0.6 * math.exp(-0.3 * layer_idx)
    lp = jnp.zeros((8, 128), F32).at[:4, :DIFF_QK_DIM].set(lam_params.astype(F32))
    w = DIFF_HEADS * DIFF_V_DIM
    assert w == 256 and DIFF_HEADS * 2 * DIFF_QK_DIM == w
    sg = jnp.tile(subln.astype(F32), DIFF_HEADS).reshape(1, w)
    cq, sq = _rope_tables(seq, DIFF_QK_DIM, w, 8)
    perm = _swap_matrix(w, 8)
    bd = _block_ones(w, DIFF_V_DIM)
    tq = 512
    nqb = seq // tq
    const = lambda shape: pl.BlockSpec(shape, lambda b, i: (0,) * len(shape))
    scratch = lambda nk: [pltpu.VMEM((nk, w), BF16), pltpu.VMEM((DIFF_HEADS // 2, nk, 256), BF16)]
    y_lat = pl.pallas_call(
        functools.partial(_diff_kernel, seq, n_ctx, lam_init),
        grid=(batch, nqb),
        in_specs=[pl.BlockSpec((tq, w), lambda b, i: (b * nqb + i, DIFF_Q)),
                  pl.BlockSpec((seq, w), lambda b, i: (b, DIFF_K)),
                  pl.BlockSpec((seq, w), lambda b, i: (b, DIFF_V)),
                  pl.BlockSpec((n_ctx, w), lambda b, i: (cb + b, DIFF_K)),
                  pl.BlockSpec((n_ctx, w), lambda b, i: (cb + b, DIFF_V)),
                  const((8, 128)), const((1, w)),
                  pl.BlockSpec((tq, w), lambda b, i: (i, 0)),
                  pl.BlockSpec((tq, w), lambda b, i: (i, 0)),
                  const((seq, w)), const((seq, w)), const((w, w)), const((w, w))],
        out_specs=pl.BlockSpec((tq, w), lambda b, i: (b * nqb + i, 0)),
        out_shape=jax.ShapeDtypeStruct((n_lat, w), BF16),
        scratch_shapes=scratch(seq + n_ctx),
        compiler_params=_params("parallel", "arbitrary"),
        name="diff_mixer",
    )(p_all, p_all, p_all, p_all, p_all, lp, sg, cq, sq, cq, sq, perm, bd)
    if not need_ctx:
        return y_lat, None
    y_ctx = pl.pallas_call(
        functools.partial(_diff_kernel, 0, n_ctx, lam_init),
        grid=(batch, 1),
        in_specs=[pl.BlockSpec((n_ctx, w), lambda b, i: (cb + b, DIFF_Q)),
                  pl.BlockSpec((n_ctx, w), lambda b, i: (cb + b, DIFF_K)),
                  pl.BlockSpec((n_ctx, w), lambda b, i: (cb + b, DIFF_V)),
                  const((8, 128)), const((1, w)), const((w, w))],
        out_specs=pl.BlockSpec((n_ctx, w), lambda b, i: (b, 0)),
        out_shape=jax.ShapeDtypeStruct((batch * n_ctx, w), BF16),
        scratch_shapes=scratch(n_ctx),
        compiler_params=_params("parallel", "arbitrary"),
        name="diff_mixer_ctx",
    )(p_all, p_all, p_all, lp, sg, bd)
    return y_lat, y_ctx


def _merge_kernel(nlat_blk, has_ctx, n_parts, *refs):
    x_refs, refs = refs[:n_parts], refs[n_parts:]
    g_refs = refs[:N_BRANCH]
    y_refs = refs[N_BRANCH:2 * N_BRANCH]
    refs = refs[2 * N_BRANCH:]
    if has_ctx:
        yc_refs, refs = refs[:N_BRANCH], refs[N_BRANCH:]
        is_ctx = pl.program_id(0) >= nlat_blk
    wb_ref, wo_ref, mod_ref, gn_ref, wr_ref, xo_ref, m_ref, lg_ref = refs
    acc = None
    for n in range(N_BRANCH):
        y = y_refs[n][...]
        if has_ctx:
            y = jnp.where(is_ctx, yc_refs[n][...], y)
        term = g_refs[n][...].astype(F32) * _dot(y, wb_ref[n])
        acc = term if acc is None else acc + term
    y = _dot(acc.astype(BF16), wo_ref[...])
    x = _pick_rows(x_refs, nlat_blk) + mod_ref[2:3, :] * y
    xo_ref[...] = x
    xn = x * lax.rsqrt(jnp.mean(x * x, axis=-1, keepdims=True) + EPS) * gn_ref[...]
    m = xn * (1.0 + mod_ref[4:5, :]) + mod_ref[3:4, :]
    _store_tile_rows(m_ref, m)
    m_hi = m.astype(BF16)
    m_lo = (m - m_hi.astype(F32)).astype(BF16)
    w = wr_ref[...]
    w_hi = w.astype(BF16)
    w_lo = (w - w_hi.astype(F32)).astype(BF16)
    lg_ref[...] = _dot_nt(w_hi, m_hi) + (_dot_nt(w_hi, m_lo) + _dot_nt(w_lo, m_hi))


def _merge(x_parts, p_all, ys_lat, ys_ctx, wb_bf, wo_bf, modtab, gn, wr_t, tm, n_rows, n_lat, seq, batch):
    d = D_MODEL
    midx = _mod_index(tm, n_lat, seq, batch)
    nlat_blk = n_lat // tm
    has_ctx = ys_ctx is not None
    gate = lambda n: pl.BlockSpec((tm, d), lambda i, n=n: (i, MIX_COLS // d + n))
    y_specs = [pl.BlockSpec((tm, BRANCH_W), lambda i: (jnp.minimum(i, nlat_blk - 1), 0))] * N_BRANCH
    ys = list(ys_lat)
    if has_ctx:
        y_specs += [pl.BlockSpec((tm, BRANCH_W), lambda i: (jnp.maximum(i - nlat_blk, 0), 0))] * N_BRANCH
        ys += list(ys_ctx)
    return pl.pallas_call(
        functools.partial(_merge_kernel, nlat_blk, has_ctx, len(x_parts)),
        grid=(n_rows // tm,),
        in_specs=[*_row_blocks(x_parts, tm, n_lat),
                  gate(0), gate(1), gate(2), gate(3), *y_specs,
                  pl.BlockSpec((N_BRANCH, BRANCH_W, d), lambda i: (0, 0, 0)),
                  pl.BlockSpec((d, d), lambda i: (0, 0)),
                  pl.BlockSpec((None, 8, d), lambda i: (midx(i), 0, 0)),
                  pl.BlockSpec((1, d), lambda i: (0, 0)),
                  pl.BlockSpec((N_EXPERTS, d), lambda i: (0, 0))],
        out_specs=[pl.BlockSpec((tm, d), lambda i: (i, 0)),
                   pl.BlockSpec((tm * ROW_TILE, 128), lambda i: (i, 0)),
                   pl.BlockSpec((N_EXPERTS, tm), lambda i: (0, i))],
        out_shape=[jax.ShapeDtypeStruct((n_rows, d), F32),
                   jax.ShapeDtypeStruct((n_rows * ROW_TILE, 128), F32),
                   jax.ShapeDtypeStruct((N_EXPERTS, n_rows), F32)],
        compiler_params=_params("parallel"),
        name="merge_norm_route",
    )(*x_parts, p_all, p_all, p_all, p_all, *ys, wb_bf, wo_bf, modtab, gn.reshape(1, d), wr_t)


def _route_kernel(lg_ref, b_ref, tri_ref, idx_ref, w_ref, rank_ref, cnt_ref):
    s = _sigmoid(lg_ref[...])
    sel = s + b_ref[...]
    row = lambda a, e: a[e:e + 1, :]
    gsz = EXPERTS_PER_GROUP
    g_idx = None
    best = None
    for g in range(N_GROUPS):
        v = [row(sel, g * gsz + i) for i in range(gsz)]
        score = None
        for i in range(gsz):
            for j in range(i + 1, gsz):
                pair = v[i] + v[j]
                score = pair if score is None else jnp.maximum(score, pair)
        if g == 0:
            best, g_idx = score, jnp.zeros(score.shape, jnp.int32)
        else:
            better = score > best
            best = jnp.where(better, score, best)
            g_idx = jnp.where(better, g, g_idx)

    def in_group(a, i):
        out = row(a, i)
        for g in range(1, N_GROUPS):
            out = jnp.where(g_idx == g, row(a, g * gsz + i), out)
        return out

    v = [in_group(sel, i) for i in range(gsz)]
    sv = [in_group(s, i) for i in range(gsz)]

    def arg_first_max(vals):
        bv, bi = vals[0], jnp.zeros(vals[0].shape, jnp.int32)
        for i in range(1, gsz):
            better = vals[i] > bv
            bv = jnp.where(better, vals[i], bv)
            bi = jnp.where(better, i, bi)
        return bi

    i1 = arg_first_max(v)
    i2 = arg_first_max([jnp.where(i1 == i, -jnp.inf, v[i]) for i in range(gsz)])

    def pick(vals, idx):
        out = vals[0]
        for i in range(1, gsz):
            out = jnp.where(idx == i, vals[i], out)
        return out

    w1 = pick(sv, i1)
    w2 = pick(sv, i2)
    tot = w1 + w2
    e1 = g_idx * gsz + i1
    e2 = g_idx * gsz + i2
    idx_ref[0:1, :] = e1
    idx_ref[1:2, :] = e2
    w_ref[0:1, :] = w1 / tot
    w_ref[1:2, :] = w2 / tot

    @pl.when(pl.program_id(0) == 0)
    def _():
        cnt_ref[...] = jnp.zeros(cnt_ref.shape, cnt_ref.dtype)

    n_e, tn = s.shape
    erow = lax.broadcasted_iota(jnp.int32, (n_e, tn), 0)
    oh1 = jnp.where(erow == e1, 1.0, 0.0)
    oh2 = jnp.where(erow == e2, 1.0, 0.0)
    oh = (oh1 + oh2).astype(BF16)
    base = cnt_ref[...]
    r1, r2 = [], []
    for c in range(tn // 128):
        cs = slice(c * 128, (c + 1) * 128)
        before = base + _dot(oh[:, cs], tri_ref[...])
        r1.append(jnp.sum(oh1[:, cs] * before, axis=0, keepdims=True))
        r2.append(jnp.sum(oh2[:, cs] * before, axis=0, keepdims=True))
        base = base + jnp.sum(oh[:, cs].astype(F32), axis=1, keepdims=True)
    cnt_ref[...] = base
    rank_ref[0:1, :] = jnp.concatenate(r1, axis=-1).astype(jnp.int32)
    rank_ref[1:2, :] = jnp.concatenate(r2, axis=-1).astype(jnp.int32)


def _route(logits_t, router_bias):
    e, n = logits_t.shape
    tn = math.gcd(n, 2048)
    i = np.arange(128)
    tri = jnp.asarray((i[:, None] < i[None, :]).astype(np.float32), BF16)
    idx2, w2, rank, cnt = pl.pallas_call(
        _route_kernel,
        grid=(n // tn,),
        in_specs=[pl.BlockSpec((e, tn), lambda i: (0, i)),
                  pl.BlockSpec((e, 1), lambda i: (0, 0)),
                  pl.BlockSpec((128, 128), lambda i: (0, 0))],
        out_specs=[pl.BlockSpec((TOP_K, tn), lambda i: (0, i)),
                   pl.BlockSpec((TOP_K, tn), lambda i: (0, i)),
                   pl.BlockSpec((TOP_K, tn), lambda i: (0, i)),
                   pl.BlockSpec((e, 128), lambda i: (0, 0))],
        out_shape=[jax.ShapeDtypeStruct((TOP_K, n), jnp.int32),
                   jax.ShapeDtypeStruct((TOP_K, n), F32),
                   jax.ShapeDtypeStruct((TOP_K, n), jnp.int32),
                   jax.ShapeDtypeStruct((e, 128), F32)],
        compiler_params=_params("arbitrary"),
        name="route_top2",
    )(logits_t, router_bias.astype(F32).reshape(e, 1), tri)
    return idx2, w2, rank, cnt[:, 0].astype(jnp.int32)


def _dispatch_kernel(n_tok_steps, pos_ref, m_ref, xs_out, zrow, sem):
    i = pl.program_id(0)
    rt = ROW_TILE
    tm = m_ref.shape[0] // rt
    n = TOP_K * tm

    def wait():
        pltpu.make_async_copy(xs_out.at[pl.ds(0, n * rt)], xs_out.at[pl.ds(0, n * rt)], sem).wait()

    @pl.when(i < n_tok_steps)
    def _():
        for j in range(n):
            dst = pl.multiple_of(pos_ref[i, j], rt)
            pltpu.make_async_copy(m_ref.at[pl.ds((j % tm) * rt, rt)], xs_out.at[pl.ds(dst, rt)],
                                  sem).start(priority=j % 2)
        wait()

    @pl.when(i >= n_tok_steps)
    def _():
        zrow[...] = jnp.zeros(zrow.shape, zrow.dtype)
        for j in range(n):
            dst = pl.multiple_of(pos_ref[i, j], rt)
            pltpu.make_async_copy(zrow, xs_out.at[pl.ds(dst, rt)], sem).start(priority=j % 2)
        wait()


def _dispatch(m_tiles, pos, pad_slots, n_slots):
    rt = ROW_TILE
    n_tok_steps, n = pos.shape
    tm = n // TOP_K
    steps = jnp.concatenate([pos, pad_slots], axis=0)
    grid_spec = pltpu.PrefetchScalarGridSpec(
        num_scalar_prefetch=1,
        grid=(steps.shape[0],),
        in_specs=[pl.BlockSpec((tm * rt, 128), lambda i, pos: (jnp.minimum(i, n_tok_steps - 1), 0))],
        out_specs=pl.BlockSpec(memory_space=pl.ANY),
        scratch_shapes=[pltpu.VMEM((rt, 128), F32), pltpu.SemaphoreType.DMA(())])
    return pl.pallas_call(
        functools.partial(_dispatch_kernel, n_tok_steps),
        grid_spec=grid_spec,
        out_shape=jax.ShapeDtypeStruct((n_slots * rt, 128), F32),
        compiler_params=_params("arbitrary"),
        name="moe_dispatch",
    )(steps, m_tiles)


def _expert_kernel(be_ref, x_ref, wg_ref, wu_ref, wd_ref, y_ref):
    blk = x_ref.shape[0] // ROW_TILE
    x = _load_tile_rows(x_ref, 0, blk).astype(BF16)
    a = _dot(x, wg_ref[...])
    h = a * _sigmoid(a) * _dot(x, wu_ref[...])
    _store_tile_rows(y_ref, _dot(h.astype(BF16), wd_ref[...]))


def _experts(x_slots, blk_e, blk, wg_bf, wu_bf, wd_bf):
    rt = ROW_TILE
    d = D_MODEL
    n_slots = x_slots.shape[0] // rt
    grid_spec = pltpu.PrefetchScalarGridSpec(
        num_scalar_prefetch=1,
        grid=(n_slots // blk,),
        in_specs=[pl.BlockSpec((blk * rt, 128), lambda i, be: (i, 0)),
                  pl.BlockSpec((None, d, D_EXPERT), lambda i, be: (be[i], 0, 0)),
                  pl.BlockSpec((None, d, D_EXPERT), lambda i, be: (be[i], 0, 0)),
                  pl.BlockSpec((None, D_EXPERT, d), lambda i, be: (be[i], 0, 0))],
        out_specs=pl.BlockSpec((blk * rt, 128), lambda i, be: (i, 0)))
    return pl.pallas_call(
        _expert_kernel,
        grid_spec=grid_spec,
        out_shape=jax.ShapeDtypeStruct((n_slots * rt, 128), F32),
        compiler_params=_params("arbitrary"),
        name="expert_ffn",
    )(blk_e, x_slots, wg_bf, wu_bf, wd_bf)


def _combine_kernel(final, pos_ref, y_hbm, x_ref, w_ref, mod_ref, gf_ref, o_ref, ybuf, sem):
    i = pl.program_id(0)
    last = pl.num_programs(0) - 1
    slot = i % 2
    rt = ROW_TILE
    tm = x_ref.shape[0]
    n = TOP_K * tm

    def issue(step, dst_slot):
        for j in range(n):
            src = pl.multiple_of(pos_ref[step, j], rt)
            pltpu.make_async_copy(y_hbm.at[pl.ds(src, rt)], ybuf.at[dst_slot, pl.ds(j * rt, rt)],
                                  sem.at[dst_slot]).start(priority=j % 2)

    def wait(dst_slot):
        pltpu.make_async_copy(y_hbm.at[pl.ds(0, n * rt)], ybuf.at[dst_slot], sem.at[dst_slot]).wait()

    @pl.when(i == 0)
    def _():
        issue(0, 0)

    wait(slot)
    issue(jnp.minimum(i + 1, last), 1 - slot)
    w = w_ref[...]
    yb = ybuf.at[slot]
    f = w[:, 0:1] * _load_tile_rows(yb, 0, tm) + w[:, 1:2] * _load_tile_rows(yb, tm * rt, tm)
    x = x_ref[...] + mod_ref[5:6, :] * f
    if final:
        x = x * lax.rsqrt(jnp.mean(x * x, axis=-1, keepdims=True) + EPS) * gf_ref[...]
    o_ref[...] = x

    @pl.when(i == last)
    def _():
        wait(1 - slot)


def _combine(x_rows, y_slots, pos, w_tok, modtab, g_final, final, n_lat, seq, batch):
    n_rows, d = x_rows.shape
    n_steps, n = pos.shape
    tm = n // TOP_K
    midx = _mod_index(tm, n_lat, seq, batch)
    grid_spec = pltpu.PrefetchScalarGridSpec(
        num_scalar_prefetch=1,
        grid=(n_steps,),
        in_specs=[pl.BlockSpec(memory_space=pl.ANY),
                  pl.BlockSpec((tm, d), lambda i, pos: (i, 0)),
                  pl.BlockSpec((tm, TOP_K), lambda i, pos: (i, 0)),
                  pl.BlockSpec((None, 8, d), lambda i, pos: (midx(i), 0, 0)),
                  pl.BlockSpec((1, d), lambda i, pos: (0, 0))],
        out_specs=pl.BlockSpec((tm, d), lambda i, pos: (i, 0)),
        scratch_shapes=[pltpu.VMEM((2, n * ROW_TILE, 128), F32), pltpu.SemaphoreType.DMA((2,))])
    return pl.pallas_call(
        functools.partial(_combine_kernel, final),
        grid_spec=grid_spec,
        out_shape=jax.ShapeDtypeStruct((n_rows, d), F32),
        compiler_params=_params("arbitrary"),
        name="moe_combine",
    )(pos, y_slots, x_rows, w_tok, modtab, g_final.astype(F32).reshape(1, d))


def _dispatch_plan(idx2, rank, counts, blk, tm):
    n = idx2.shape[1]
    a = TOP_K * n
    padded = (counts + blk - 1) // blk * blk
    pad_end = jnp.cumsum(padded)
    pad_start = pad_end - padded
    start_of = jnp.zeros(idx2.shape, jnp.int32)
    for e in range(N_EXPERTS):
        start_of = jnp.where(idx2 == e, pad_start[e], start_of)
    dest = (start_of + rank).astype(jnp.int32)
    n_blocks = a // blk + N_EXPERTS
    first_slot = jnp.arange(n_blocks, dtype=jnp.int32) * blk
    blk_e = jnp.minimum(jnp.sum(pad_end[None, :] <= first_slot[:, None], axis=1), N_EXPERTS - 1).astype(jnp.int32)
    pos = dest.reshape(TOP_K, n // tm, tm).transpose(1, 0, 2).reshape(n // tm, TOP_K * tm) * ROW_TILE
    n_slots = n_blocks * blk
    starts = jnp.concatenate([pad_start + counts, pad_end[-1:]])
    lens = jnp.concatenate([padded - counts, n_slots - pad_end[-1:]])
    ends = jnp.cumsum(lens)
    j = jnp.arange(N_EXPERTS * blk, dtype=jnp.int32)
    shift = starts - (ends - lens)
    shift_of = jnp.zeros(j.shape, jnp.int32)
    for s in range(N_EXPERTS + 1):
        shift_of = jnp.where((j >= ends[s] - lens[s]) & (j < ends[s]), shift[s], shift_of)
    pad_slots = ((shift_of + j) * ROW_TILE).astype(jnp.int32)
    return pos, pad_slots.reshape(-1, TOP_K * tm), blk_e, n_slots


def kernel(x, c, ctx, c_ctx, w_mod, b_mod, g_norm1, g_norm2, w_in, na_rpb, ret_log_decay, gqa_q_gain, gqa_k_gain,
           diff_lambda, diff_subln, w_branch, w_out, w_router, router_bias, w_gate_e, w_up_e, w_down_e, g_final):
    batch, seq, d = x.shape
    n_ctx = ctx.shape[1]
    n_lat = batch * seq
    n_ctx_tot = batch * n_ctx
    n_tot = n_lat + n_ctx_tot
    tm = 1024 if n_ctx_tot % 1024 == 0 else 512

    x_parts = (x.reshape(n_lat, d), ctx.reshape(n_ctx_tot, d))

    nrow = -(-(batch + 1) // 8) * 8
    cc = jnp.zeros((nrow, d), F32).at[:batch].set(c).at[batch].set(c_ctx)
    mod = _modulation(cc, w_mod, b_mod)
    modtab = jnp.pad(mod.reshape(DEPTH, nrow, 6, d), ((0, 0), (0, 0), (0, 2), (0, 0)))

    wr_t = w_router.astype(F32).T
    out = None
    for layer in range(DEPTH):
        need_ctx = layer < DEPTH - 1
        n_rows = n_tot if need_ctx else n_lat
        p_all = _project(x_parts, g_norm1[layer], modtab[layer], w_in[layer].astype(BF16), tm, n_lat, seq, batch)
        ya = _na_mixer(p_all, na_rpb[layer], batch, seq, n_ctx, need_ctx)
        yb = _ret_mixer(p_all, ret_log_decay[layer], batch, seq, n_ctx, need_ctx)
        yc = _gqa_mixer(p_all, gqa_q_gain[layer], gqa_k_gain[layer], batch, seq, n_ctx, need_ctx)
        yd = _diff_mixer(p_all, diff_lambda[layer], diff_subln[layer], layer, batch, seq, n_ctx, need_ctx)
        ys_lat = [ya[0], yb[0], yc[0], yd[0]]
        ys_ctx = [ya[1], yb[1], yc[1], yd[1]] if need_ctx else None
        x_mid, m_all, logits_t = _merge(x_parts, p_all, ys_lat, ys_ctx, w_branch[layer].astype(BF16),
                                        w_out[layer].astype(BF16), modtab[layer], g_norm2[layer], wr_t, MERGE_TM,
                                        n_rows, n_lat, seq, batch)
        idx2, w2, rank, counts = _route(logits_t, router_bias)
        pos, pad_slots, blk_e, n_slots = _dispatch_plan(idx2, rank, counts, MOE_BLK, CMB_TM)
        x_slots = _dispatch(m_all, pos, pad_slots, n_slots)
        y_slots = _experts(x_slots, blk_e, MOE_BLK, w_gate_e[layer].astype(BF16), w_up_e[layer].astype(BF16),
                           w_down_e[layer].astype(BF16))
        out = _combine(x_mid, y_slots, pos, w2.T, modtab[layer], g_final, not need_ctx, n_lat, seq, batch)
        x_parts = (out,)
    return out.reshape(batch, seq, d)
```

```python
import functools
import math

import numpy as np
import jax
import jax.numpy as jnp
from jax import lax
from jax.experimental import pallas as pl
from jax.experimental.pallas import tpu as pltpu

F32 = jnp.float32
BF16 = jnp.bfloat16

D_MODEL = 1024
DEPTH = 2
GRID_W = 64
NA_HEADS = 4
NA_DIM = 64
NA_WIN_H = 8
NA_WIN_W = 16
RET_HEADS = 4
RET_DK = 64
RET_CHUNK = 128
GQA_HEADS = 4
GQA_KV_HEADS = 2
GQA_DIM = 64
DIFF_HEADS = 4
DIFF_QK_DIM = 32
DIFF_V_DIM = 64
N_BRANCH = 4
BRANCH_W = 256
ROPE_THETA = 10000.0
EPS = 1e-6
NEG_INF = -1e30
N_EXPERTS = 16
N_GROUPS = 4
EXPERTS_PER_GROUP = 4
TOP_K = 2
D_EXPERT = 512

MIX_COLS = 3072
IN_COLS = MIX_COLS + N_BRANCH * D_MODEL
NA_Q, NA_K, NA_V = 0, 1, 2
RET_Q, RET_K, RET_V, RET_G = 3, 4, 5, 6
GQA_Q = 7
DIFF_Q, DIFF_K, DIFF_V = 9, 10, 11

LOG2E = math.log2(math.e)
NA_GROUP_ROWS = 4
NA_BAND_ROWS = 12

VMEM_LIMIT = 56 * 1024 * 1024
MOE_BLK = 512
CMB_TM = 512
PROJ_TN = 1792
MERGE_TM = 512


ROW_TILE = D_MODEL // 128


def _store_tile_rows(ref, x):
    n = x.shape[0]
    for j in range(ROW_TILE):
        ref[pl.ds(j, n, stride=ROW_TILE), :] = x[:, j * 128:(j + 1) * 128]


def _load_tile_rows(ref, start, n):
    return jnp.concatenate([ref[pl.ds(start + j, n, stride=ROW_TILE), :] for j in range(ROW_TILE)], axis=-1)


def _dot(a, b):
    return jnp.dot(a, b, preferred_element_type=F32)


def _dot_nt(a, b):
    return lax.dot_general(a, b, (((1,), (1,)), ((), ())), preferred_element_type=F32)


def _dot_tn(a, b):
    return lax.dot_general(a, b, (((0,), (0,)), ((), ())), preferred_element_type=F32)


def _sigmoid(x):
    return 1.0 / (1.0 + jnp.exp(-x))


def _params(*sem):
    return pltpu.CompilerParams(dimension_semantics=sem, vmem_limit_bytes=VMEM_LIMIT)


def _swap_halves(x, dist):
    pieces = []
    for j in range(x.shape[-1] // 128):
        xs = x[:, j * 128:(j + 1) * 128]
        lane = lax.broadcasted_iota(jnp.int32, xs.shape, 1)
        up = pltpu.roll(xs, 128 - dist, 1)
        dn = pltpu.roll(xs, dist, 1)
        pieces.append(jnp.where((lane % (2 * dist)) < dist, up, dn))
    return pieces[0] if len(pieces) == 1 else jnp.concatenate(pieces, axis=-1)


def _rope(x, cos, sin, dist):
    return x * cos + _swap_halves(x, dist) * sin


def _mod_kernel(c_ref, w_ref, b_ref, o_ref):
    c = c_ref[...]
    a = (c * _sigmoid(c)).astype(BF16)
    o_ref[...] = _dot(a, w_ref[...].astype(BF16)) + b_ref[...]


def _modulation(cc, w_mod, b_mod):
    nrow = cc.shape[0]
    depth, d, n6 = w_mod.shape
    tn = 1536
    return pl.pallas_call(
        _mod_kernel,
        grid=(depth, n6 // tn),
        in_specs=[pl.BlockSpec((nrow, d), lambda l, j: (0, 0)),
                  pl.BlockSpec((None, d, tn), lambda l, j: (l, 0, j)),
                  pl.BlockSpec((None, 1, tn), lambda l, j: (l, 0, j))],
        out_specs=pl.BlockSpec((None, nrow, tn), lambda l, j: (l, 0, j)),
        out_shape=jax.ShapeDtypeStruct((depth, nrow, n6), F32),
        compiler_params=_params("parallel", "arbitrary"),
        name="modulation",
    )(cc, w_mod, b_mod.reshape(depth, 1, n6))


def _mix_col_scale():
    cs = np.ones((1, IN_COLS), np.float32)
    cs[0, NA_Q * 256:(NA_Q + 1) * 256] = NA_DIM ** -0.5 * LOG2E
    cs[0, DIFF_Q * 256:(DIFF_Q + 1) * 256] = DIFF_QK_DIM ** -0.5 * LOG2E
    cs[0, RET_K * 256:(RET_K + 1) * 256] = RET_DK ** -0.5
    cs[0, MIX_COLS:] = 0.0
    return jnp.asarray(cs)


def _row_blocks(parts, tm, n_lat):
    d = parts[0].shape[1]
    nlat_blk = n_lat // tm
    if len(parts) == 1:
        return [pl.BlockSpec((tm, d), lambda i, *_: (i, 0))]
    return [pl.BlockSpec((tm, d), lambda i, *_: (jnp.minimum(i, nlat_blk - 1), 0)),
            pl.BlockSpec((tm, d), lambda i, *_: (jnp.maximum(i - nlat_blk, 0), 0))]


def _pick_rows(x_refs, nlat_blk):
    if len(x_refs) == 1:
        return x_refs[0][...]
    return jnp.where(pl.program_id(0) >= nlat_blk, x_refs[1][...], x_refs[0][...])


def _proj_kernel(nlat_blk, n_parts, *refs):
    x_refs, (g_ref, mod_ref, w_ref, cs_ref, o_ref, h_ref) = refs[:n_parts], refs[n_parts:]
    j = pl.program_id(1)

    @pl.when(j == 0)
    def _():
        x = _pick_rows(x_refs, nlat_blk)
        y = x * lax.rsqrt(jnp.mean(x * x, axis=-1, keepdims=True) + EPS) * g_ref[...]
        h_ref[...] = (y * (1.0 + mod_ref[1:2, :]) + mod_ref[0:1, :]).astype(BF16)

    acc = _dot(h_ref[...], w_ref[...])
    gate = 0.5 * jnp.tanh(0.5 * acc) + 0.5
    cs = cs_ref[...]
    o_ref[...] = jnp.where(cs > 0.0, acc * cs, gate).astype(BF16)


def _mod_index(tm, n_lat, seq, batch):
    nlat_blk = n_lat // tm
    bpb = seq // tm

    def index(i):
        return jnp.where(i < nlat_blk, i // bpb, batch)
    return index


def _project(x_parts, g, modtab, w_bf, tm, n_lat, seq, batch):
    n_tot = sum(p.shape[0] for p in x_parts)
    d = x_parts[0].shape[1]
    ncol = w_bf.shape[1]
    tn = PROJ_TN
    midx = _mod_index(tm, n_lat, seq, batch)
    return pl.pallas_call(
        functools.partial(_proj_kernel, n_lat // tm, len(x_parts)),
        grid=(n_tot // tm, ncol // tn),
        in_specs=[*_row_blocks(x_parts, tm, n_lat),
                  pl.BlockSpec((1, d), lambda i, j: (0, 0)),
                  pl.BlockSpec((None, 8, d), lambda i, j: (midx(i), 0, 0)),
                  pl.BlockSpec((d, tn), lambda i, j: (0, j)),
                  pl.BlockSpec((1, tn), lambda i, j: (0, j))],
        out_specs=pl.BlockSpec((tm, tn), lambda i, j: (i, j)),
        out_shape=jax.ShapeDtypeStruct((n_tot, ncol), BF16),
        scratch_shapes=[pltpu.VMEM((tm, d), BF16)],
        compiler_params=_params("parallel", "arbitrary"),
        name="norm_project",
    )(*x_parts, g.reshape(1, d), modtab, w_bf, _mix_col_scale())


def _na_kernel(need_ctx, rows, q_ref, k_ref, v_ref, qc_ref, kc_ref, vc_ref, tb_ref, y_ref, *rest):
    gq = NA_GROUP_ROWS * GRID_W
    band = NA_BAND_ROWS * GRID_W
    n_groups = rows // NA_GROUP_ROWS

    def group_body(g, carry):
        u = jnp.clip(g * NA_GROUP_ROWS - NA_WIN_H // 2, 0, rows - NA_BAND_ROWS)
        typ = jnp.where(g == 0, 0, jnp.where(g == n_groups - 1, 2, 1))
        qs = pl.multiple_of(g * gq, gq)
        ks = pl.multiple_of(u * GRID_W, GRID_W)
        for h in range(NA_HEADS):
            hs = slice(h * NA_DIM, (h + 1) * NA_DIM)
            q = q_ref[pl.ds(qs, gq), hs]
            vc = vc_ref[:, hs]
            s_w = _dot_nt(q, k_ref[pl.ds(ks, band), hs]) + tb_ref[h, typ]
            s_c = _dot_nt(q, kc_ref[:, hs])
            m = jnp.maximum(jnp.max(s_w, axis=-1, keepdims=True), jnp.max(s_c, axis=-1, keepdims=True))
            p_w = jnp.exp2(s_w - m)
            p_c = jnp.exp2(s_c - m)
            l = jnp.sum(p_w, axis=-1, keepdims=True) + jnp.sum(p_c, axis=-1, keepdims=True)
            o = (_dot(p_w.astype(BF16), v_ref[pl.ds(ks, band), hs]) + _dot(p_c.astype(BF16), vc)) / l
            y_ref[pl.ds(qs, gq), hs] = o.astype(BF16)
        return carry

    lax.fori_loop(0, n_groups, group_body, 0)

    if need_ctx:
        yc_ref = rest[0]
        for h in range(NA_HEADS):
            hs = slice(h * NA_DIM, (h + 1) * NA_DIM)
            s = _dot_nt(qc_ref[:, hs], kc_ref[:, hs])
            p = jnp.exp2(s - jnp.max(s, axis=-1, keepdims=True))
            l = jnp.sum(p, axis=-1, keepdims=True)
            yc_ref[:, hs] = (_dot(p.astype(BF16), vc_ref[:, hs]) / l).astype(BF16)


def _na_bias_table(rpb, rows):
    assert rows % NA_GROUP_ROWS == 0 and rows >= NA_BAND_ROWS + NA_GROUP_ROWS
    n_groups = rows // NA_GROUP_ROWS
    qc = np.arange(GRID_W)[:, None]
    kc = np.arange(GRID_W)[None, :]
    win_start = np.clip(qc - NA_WIN_W // 2, 0, GRID_W - NA_WIN_W)
    col_ok = (kc >= win_start) & (kc < win_start + NA_WIN_W)
    col_idx = np.clip(kc - qc, -(NA_WIN_W - 1), NA_WIN_W - 1) + NA_WIN_W - 1
    n_col = 2 * NA_WIN_W - 1
    onehot = jnp.asarray((np.arange(n_col)[:, None] == col_idx.reshape(1, -1)).astype(np.float32))
    picked = jnp.dot(rpb.astype(F32).reshape(-1, n_col), onehot, precision=lax.Precision.HIGHEST)
    picked = picked.reshape(NA_HEADS, 2 * NA_WIN_H - 1, GRID_W, GRID_W)
    tiles = jnp.where(col_ok[None, None], picked * LOG2E, NEG_INF)
    masked = jnp.full((NA_HEADS, GRID_W, GRID_W), NEG_INF, F32)
    tables = []
    for g in (0, 1, n_groups - 1):
        u = int(np.clip(g * NA_GROUP_ROWS - NA_WIN_H // 2, 0, rows - NA_BAND_ROWS))
        per_row = []
        for a in range(NA_GROUP_ROWS):
            r = g * NA_GROUP_ROWS + a
            r0 = int(np.clip(r - NA_WIN_H // 2, 0, rows - NA_WIN_H))
            pieces = [tiles[:, kr - r + NA_WIN_H - 1] if r0 <= kr < r0 + NA_WIN_H else masked
                      for kr in range(u, u + NA_BAND_ROWS)]
            per_row.append(jnp.concatenate(pieces, axis=-1))
        tables.append(jnp.concatenate(per_row, axis=1))
    return jnp.stack(tables, axis=1)


def _na_mixer(p_all, rpb, batch, seq, n_ctx, need_ctx):
    n_lat = batch * seq
    rows = seq // GRID_W
    tb = _na_bias_table(rpb, rows)
    cb = n_lat // n_ctx
    w = 256
    lat = lambda c: pl.BlockSpec((seq, w), lambda b, c=c: (b, c))
    ctx = lambda c: pl.BlockSpec((n_ctx, w), lambda b, c=c: (cb + b, c))
    out_shape = [jax.ShapeDtypeStruct((n_lat, w), BF16)]
    out_specs = [pl.BlockSpec((seq, w), lambda b: (b, 0))]
    if need_ctx:
        out_shape.append(jax.ShapeDtypeStruct((batch * n_ctx, w), BF16))
        out_specs.append(pl.BlockSpec((n_ctx, w), lambda b: (b, 0)))
    return pl.pallas_call(
        functools.partial(_na_kernel, need_ctx, rows),
        grid=(batch,),
        in_specs=[lat(NA_Q), lat(NA_K), lat(NA_V), ctx(NA_Q), ctx(NA_K), ctx(NA_V),
                  pl.BlockSpec(tb.shape, lambda b: (0, 0, 0, 0))],
        out_specs=out_specs,
        out_shape=out_shape,
        compiler_params=_params("parallel"),
        name="na_mixer",
    )(p_all, p_all, p_all, p_all, p_all, p_all, tb)


def _ret_kernel(need_ctx, seq, n_ctx, lg_ref, q_ref, k_ref, v_ref, g_ref, qc_ref, kc_ref, vc_ref, gc_ref,
                cos_ref, sin_ref, perm_ref, bd_ref, y_ref, *rest):
    if need_ctx:
        yc_ref, qr, kr, krt, kct, o_f, o_b, oc_f, oc_b, dm, qd, kdt = rest
    else:
        qr, kr, krt, kct, o_f, o_b, oc_f, oc_b, dm, qd, kdt = rest
    ch = RET_CHUNK
    n_lat_ch = seq // ch
    n_ctx_ch = n_ctx // ch
    n_pairs = RET_HEADS // 2
    assert ch == 128 and 2 * RET_DK == 128

    def prep(i, carry):
        rs = pl.multiple_of(i * ch, ch)
        c = cos_ref[pl.ds(rs, ch), :]
        s = sin_ref[pl.ds(rs, ch), :]
        qr[pl.ds(rs, ch), :] = _rope_mxu(q_ref[pl.ds(rs, ch), :], c, s, perm_ref)
        k = _rope_mxu(k_ref[pl.ds(rs, ch), :], c, s, perm_ref)
        kr[pl.ds(rs, ch), :] = k
        for p in range(n_pairs):
            krt[i, p * 128:(p + 1) * 128, :] = k[:, p * 128:(p + 1) * 128].T
        return carry

    lax.fori_loop(0, n_lat_ch, prep, 0)
    for n in range(n_ctx_ch):
        kc = kc_ref[n * ch:(n + 1) * ch, :].astype(F32)
        for p in range(n_pairs):
            kct[n, p * 128:(p + 1) * 128, :] = kc[:, p * 128:(p + 1) * 128].T

    ri = lax.broadcasted_iota(jnp.int32, (ch, ch), 0)
    ci = lax.broadcasted_iota(jnp.int32, (ch, ch), 1)
    lag = (ri - ci).astype(F32)
    rowf = ri.astype(F32)
    colf = ci.astype(F32)
    low_lane = ci < RET_DK
    low_row = ri < RET_DK
    block_diag = jnp.where(low_lane == low_row, 1.0, 0.0)
    low, high = _lane_masks(2)

    chains = [(p, dirn) for p in range(n_pairs) for dirn in range(2)]
    cdec = []
    for c, (p, dirn) in enumerate(chains):
        lg_a = lg_ref[dirn, 2 * p]
        lg_b = lg_ref[dirn, 2 * p + 1]
        lg_lane = jnp.where(low_lane, lg_a, lg_b)
        lg_row = jnp.where(low_row, lg_a, lg_b)
        for hh, lg in enumerate((lg_a, lg_b)):
            if dirn == 0:
                keep = ri >= ci
                dm[2 * c + hh] = jnp.where(keep, jnp.exp(jnp.where(keep, lag, 0.0) * lg), 0.0)
            else:
                keep = ci >= ri
                dm[2 * c + hh] = jnp.where(keep, jnp.exp(jnp.where(keep, -lag, 0.0) * lg), 0.0)
        if dirn == 0:
            qd[c] = jnp.exp((rowf + 1.0) * lg_lane)
            kdt[c] = jnp.exp((ch - 1.0 - colf) * lg_row)
        else:
            qd[c] = jnp.exp((ch - rowf) * lg_lane)
            kdt[c] = jnp.exp(colf * lg_row)
        cdec.append(jnp.exp(ch * lg_lane[0:1, :]))

    def step(c, state, qi, ki, kti, vi):
        kb = ki.astype(BF16)
        inner_a = (_dot_nt((qi * low).astype(BF16), kb) * dm[2 * c]).astype(BF16)
        inner_b = (_dot_nt((qi * high).astype(BF16), kb) * dm[2 * c + 1]).astype(BF16)
        o = (jnp.where(low_lane, _dot(inner_a, vi), _dot(inner_b, vi))
             + _dot((qi * qd[c]).astype(BF16), state.astype(BF16)))
        state = state * cdec[c] + block_diag * _dot((kti * kdt[c]).astype(BF16), vi)
        return state, o

    states = []
    for c, (p, dirn) in enumerate(chains):
        ps = slice(p * 128, (p + 1) * 128)
        state = jnp.zeros((128, 128), F32)
        for n in (range(n_ctx_ch) if dirn == 0 else range(n_ctx_ch - 1, -1, -1)):
            rs = slice(n * ch, (n + 1) * ch)
            state, o = step(c, state, qc_ref[rs, ps].astype(F32), kc_ref[rs, ps].astype(F32), kct[n, ps, :],
                            vc_ref[rs, ps])
            if need_ctx:
                (oc_f if dirn == 0 else oc_b)[rs, ps] = o
        states.append(state)

    def lat_body(i, states):
        new = []
        for c, (p, dirn) in enumerate(chains):
            ps = slice(p * 128, (p + 1) * 128)
            n = i if dirn == 0 else n_lat_ch - 1 - i
            rs = pl.ds(pl.multiple_of(n * ch, ch), ch)
            state, o = step(c, states[c], qr[rs, ps], kr[rs, ps], krt[n, ps, :], v_ref[rs, ps])
            (o_f if dirn == 0 else o_b)[rs, ps] = o
            new.append(state)
        return tuple(new)

    lax.fori_loop(0, n_lat_ch, lat_body, tuple(states))

    def head_mean(x):
        hi = x.astype(BF16)
        lo = (x - hi.astype(F32)).astype(BF16)
        bd = bd_ref[...]
        return (_dot(hi, bd) + _dot(lo, bd)) * (1.0 / RET_DK)

    def finish(o, g):
        cen = o - head_mean(o)
        on = cen * lax.rsqrt(head_mean(cen * cen) + EPS)
        g = g.astype(F32)
        return (g * _sigmoid(g) * on).astype(BF16)

    def fin_body(i, carry):
        rs = pl.multiple_of(i * ch, ch)
        y_ref[pl.ds(rs, ch), :] = finish(o_f[pl.ds(rs, ch), :] + o_b[pl.ds(rs, ch), :], g_ref[pl.ds(rs, ch), :])
        return carry

    lax.fori_loop(0, n_lat_ch, fin_body, 0)
    if need_ctx:
        for n in range(n_ctx_ch):
            rs = slice(n * ch, (n + 1) * ch)
            yc_ref[rs, :] = finish(oc_f[rs, :] + oc_b[rs, :], gc_ref[rs, :])


def _rope_tables(seq, head_dim, width, dist):
    half = head_dim // 2
    nf = half // 2
    assert nf == dist
    inv = 1.0 / (ROPE_THETA ** (np.arange(nf, dtype=np.float32) / nf))
    t = np.arange(seq)
    rows = (t // GRID_W).astype(np.float32)[:, None] * inv[None, :]
    cols = (t % GRID_W).astype(np.float32)[:, None] * inv[None, :]
    cos = np.concatenate([np.cos(rows), np.cos(rows), np.cos(cols), np.cos(cols)], axis=-1)
    sin = np.concatenate([-np.sin(rows), np.sin(rows), -np.sin(cols), np.sin(cols)], axis=-1)
    reps = width // head_dim
    return (jnp.asarray(np.tile(cos, (1, reps)), F32), jnp.asarray(np.tile(sin, (1, reps)), F32))


def _ret_mixer(p_all, log_decay, batch, seq, n_ctx, need_ctx):
    n_lat = batch * seq
    cb = n_lat // n_ctx
    w = 256
    ch = RET_CHUNK
    log_gamma = jnp.log1p(-jnp.exp(log_decay.astype(F32)))
    cos, sin = _rope_tables(seq, RET_DK, w, 16)
    lat = lambda c: pl.BlockSpec((seq, w), lambda b, c=c: (b, c))
    ctx = lambda c: pl.BlockSpec((n_ctx, w), lambda b, c=c: (cb + b, c))
    whole = pl.BlockSpec((seq, w), lambda b: (0, 0))
    out_shape = [jax.ShapeDtypeStruct((n_lat, w), BF16)]
    out_specs = [pl.BlockSpec((seq, w), lambda b: (b, 0))]
    if need_ctx:
        out_shape.append(jax.ShapeDtypeStruct((batch * n_ctx, w), BF16))
        out_specs.append(pl.BlockSpec((n_ctx, w), lambda b: (b, 0)))
    return pl.pallas_call(
        functools.partial(_ret_kernel, need_ctx, seq, n_ctx),
        grid=(batch,),
        in_specs=[pl.BlockSpec(memory_space=pltpu.SMEM),
                  lat(RET_Q), lat(RET_K), lat(RET_V), lat(RET_G),
                  ctx(RET_Q), ctx(RET_K), ctx(RET_V), ctx(RET_G), whole, whole,
                  pl.BlockSpec((w, w), lambda b: (0, 0)), pl.BlockSpec((w, w), lambda b: (0, 0))],
        out_specs=out_specs,
        out_shape=out_shape,
        scratch_shapes=[pltpu.VMEM((seq, w), F32), pltpu.VMEM((seq, w), F32),
                        pltpu.VMEM((seq // ch, w, ch), F32), pltpu.VMEM((n_ctx // ch, w, ch), F32),
                        pltpu.VMEM((seq, w), F32), pltpu.VMEM((seq, w), F32),
                        pltpu.VMEM((n_ctx, w), F32), pltpu.VMEM((n_ctx, w), F32),
                        pltpu.VMEM((2 * RET_HEADS, ch, ch), F32),
                        pltpu.VMEM((RET_HEADS, ch, 128), F32), pltpu.VMEM((RET_HEADS, 128, ch), F32)],
        compiler_params=_params("parallel"),
        name="retention_mixer",
    )(log_gamma, p_all, p_all, p_all, p_all, p_all, p_all, p_all, p_all, cos, sin,
      _swap_matrix(w, 16), _block_ones(w, RET_DK))


def _swap_matrix(width, dist):
    i = np.arange(width)
    partner = np.where(i % (2 * dist) < dist, i + dist, i - dist)
    p = np.zeros((width, width), np.float32)
    p[partner, i] = 1.0
    return jnp.asarray(p, BF16)


def _block_ones(width, block):
    i = np.arange(width)
    return jnp.asarray((i[:, None] // block == i[None, :] // block).astype(np.float32), BF16)


def _sumsq_blocks(xf, bd_ref):
    sq = xf * xf
    hi = sq.astype(BF16)
    lo = (sq - hi.astype(F32)).astype(BF16)
    bd = bd_ref[...]
    return _dot(hi, bd) + _dot(lo, bd)


def _rope_mxu(x, cos, sin, perm_ref):
    return x.astype(F32) * cos + _dot(x, perm_ref[...]) * sin


def _softmax_pv(qm, k, v_ones):
    s = _dot_nt(qm, k)
    p = jnp.exp2((s - jnp.max(s, axis=-1, keepdims=True)).astype(BF16))
    oe = _dot(p, v_ones)
    return oe[:, :128] / oe[:, 128:]


def _lane_masks(n_parts):
    lane = lax.broadcasted_iota(jnp.int32, (1, 128), 1)
    return [jnp.where(lane // (128 // n_parts) == i, 1.0, 0.0).astype(BF16) for i in range(n_parts)]


def _gqa_kernel(n_lat_k, n_ctx, *refs):
    if n_lat_k:
        (q_ref, k_ref, v_ref, kc_ref, vc_ref, qa_ref, qb_ref, ka_ref, kb_ref, kg_ref, pq_ref, pk_ref,
         bdq_ref, bdk_ref, y_ref, kp, vx) = refs
    else:
        q_ref, kc_ref, vc_ref, qa_ref, kg_ref, bdq_ref, bdk_ref, y_ref, kp, vx = refs
    dim = GQA_DIM
    nk = n_lat_k + n_ctx

    def inv_rms(xf, bd_ref):
        return lax.rsqrt(_sumsq_blocks(xf, bd_ref) * (1.0 / dim) + EPS)

    @pl.when(pl.program_id(1) == 0)
    def _():
        def put(rs, kn, v):
            ones = jnp.ones(v.shape, BF16)
            kp[0, rs, :] = kn.astype(BF16)
            kp[1, rs, :] = pltpu.roll(kn, dim, 1).astype(BF16)
            vx[0, rs, :] = jnp.concatenate([v, ones], axis=-1)
            vx[1, rs, :] = jnp.concatenate([pltpu.roll(v.astype(F32), dim, 1).astype(BF16), ones], axis=-1)

        kc = kc_ref[...]
        kcf = kc.astype(F32)
        put(slice(n_lat_k, nk), kcf * kg_ref[...] * inv_rms(kcf, bdk_ref), vc_ref[...])
        ck = 512
        for c in range(n_lat_k // ck):
            rs = slice(c * ck, (c + 1) * ck)
            k = k_ref[rs, :]
            put(rs, _rope_mxu(k, ka_ref[rs, :], kb_ref[rs, :], pk_ref) * inv_rms(k.astype(F32), bdk_ref), v_ref[rs, :])

    q = q_ref[...]
    qf = q.astype(F32)
    qn = _rope_mxu(q, qa_ref[...], qb_ref[...], pq_ref) if n_lat_k else qf * qa_ref[...]
    qn = (qn * inv_rms(qf, bdq_ref)).astype(BF16)
    low, high = _lane_masks(2)
    outs = []
    for g in range(GQA_KV_HEADS):
        qv = qn[:, g * 128:(g + 1) * 128]
        o_low = _softmax_pv(qv * low, kp[g], vx[g])
        o_high = _softmax_pv(qv * high, kp[1 - g], vx[1 - g])
        outs.append(jnp.where(low > 0, o_low, o_high))
    y_ref[...] = jnp.concatenate(outs, axis=-1).astype(BF16)


def _gqa_mixer(p_all, q_gain, k_gain, batch, seq, n_ctx, need_ctx):
    n_lat = batch * seq
    cb = n_lat // n_ctx
    dim = GQA_DIM
    qw = GQA_HEADS * dim
    kvw = GQA_KV_HEADS * dim
    assert qw == 256 and kvw == 128

    def swapped(g):
        return g.reshape(-1, 2, 16)[:, ::-1].reshape(1, -1)

    qg = jnp.tile(q_gain.astype(F32) * (dim ** -0.5 * LOG2E), GQA_HEADS).reshape(1, qw)
    kg = jnp.tile(k_gain.astype(F32), GQA_KV_HEADS).reshape(1, kvw)
    cq, sq = _rope_tables(seq, dim, qw, 16)
    ck, sk = _rope_tables(seq, dim, kvw, 16)
    qa, qb = cq * qg, sq * swapped(qg)
    ka, kb = ck * kg, sk * swapped(kg)
    pq, pk = _swap_matrix(qw, 16), _swap_matrix(kvw, 16)
    bdq, bdk = _block_ones(qw, dim), _block_ones(kvw, dim)
    tq = 512
    nqb = seq // tq
    const = lambda shape: pl.BlockSpec(shape, lambda b, i: (0,) * len(shape))
    scratch = lambda nk: [pltpu.VMEM((2, nk, kvw), BF16), pltpu.VMEM((2, nk, 2 * kvw), BF16)]
    y_lat = pl.pallas_call(
        functools.partial(_gqa_kernel, seq, n_ctx),
        grid=(batch, nqb),
        in_specs=[pl.BlockSpec((tq, qw), lambda b, i: (b * nqb + i, GQA_Q)),
                  pl.BlockSpec((seq, kvw), lambda b, i: (b, 16)),
                  pl.BlockSpec((seq, kvw), lambda b, i: (b, 17)),
                  pl.BlockSpec((n_ctx, kvw), lambda b, i: (cb + b, 16)),
                  pl.BlockSpec((n_ctx, kvw), lambda b, i: (cb + b, 17)),
                  pl.BlockSpec((tq, qw), lambda b, i: (i, 0)),
                  pl.BlockSpec((tq, qw), lambda b, i: (i, 0)),
                  const((seq, kvw)), const((seq, kvw)), const((1, kvw)),
                  const((qw, qw)), const((kvw, kvw)), const((qw, qw)), const((kvw, kvw))],
        out_specs=pl.BlockSpec((tq, qw), lambda b, i: (b * nqb + i, 0)),
        out_shape=jax.ShapeDtypeStruct((n_lat, qw), BF16),
        scratch_shapes=scratch(seq + n_ctx),
        compiler_params=_params("parallel", "arbitrary"),
        name="gqa_mixer",
    )(p_all, p_all, p_all, p_all, p_all, qa, qb, ka, kb, kg, pq, pk, bdq, bdk)
    if not need_ctx:
        return y_lat, None
    y_ctx = pl.pallas_call(
        functools.partial(_gqa_kernel, 0, n_ctx),
        grid=(batch, 1),
        in_specs=[pl.BlockSpec((n_ctx, qw), lambda b, i: (cb + b, GQA_Q)),
                  pl.BlockSpec((n_ctx, kvw), lambda b, i: (cb + b, 16)),
                  pl.BlockSpec((n_ctx, kvw), lambda b, i: (cb + b, 17)),
                  const((1, qw)), const((1, kvw)), const((qw, qw)), const((kvw, kvw))],
        out_specs=pl.BlockSpec((n_ctx, qw), lambda b, i: (b, 0)),
        out_shape=jax.ShapeDtypeStruct((batch * n_ctx, qw), BF16),
        scratch_shapes=scratch(n_ctx),
        compiler_params=_params("parallel", "arbitrary"),
        name="gqa_mixer_ctx",
    )(p_all, p_all, p_all, qg, kg, bdq, bdk)
    return y_lat, y_ctx


def _diff_kernel(n_lat_k, n_ctx, lam_init, *refs):
    if n_lat_k:
        (q_ref, k_ref, v_ref, kc_ref, vc_ref, lp_ref, sg_ref, cq_ref, sq_ref, ck_ref, sk_ref, perm_ref, bd_ref,
         y_ref, kp, vx) = refs
    else:
        q_ref, kc_ref, vc_ref, lp_ref, sg_ref, bd_ref, y_ref, kp, vx = refs
    nk = n_lat_k + n_ctx
    n_pairs = DIFF_HEADS // 2

    @pl.when(pl.program_id(1) == 0)
    def _():
        def put(rs, k, v):
            kp[rs, :] = k
            ones = jnp.ones((v.shape[0], 128), BF16)
            for pr in range(n_pairs):
                vx[pr, rs, :] = jnp.concatenate([v[:, pr * 128:(pr + 1) * 128], ones], axis=-1)

        put(slice(n_lat_k, nk), kc_ref[...], vc_ref[...])
        ck = 512
        for c in range(n_lat_k // ck):
            rs = slice(c * ck, (c + 1) * ck)
            put(rs, _rope_mxu(k_ref[rs, :], ck_ref[rs, :], sk_ref[rs, :], perm_ref).astype(BF16), v_ref[rs, :])

    lp = lp_ref[...]
    lam = (jnp.exp(jnp.sum(lp[0:1, :] * lp[1:2, :], axis=-1, keepdims=True))
           - jnp.exp(jnp.sum(lp[2:3, :] * lp[3:4, :], axis=-1, keepdims=True)) + lam_init)

    q = q_ref[...]
    if n_lat_k:
        q = _rope_mxu(q, cq_ref[...], sq_ref[...], perm_ref).astype(BF16)
    quarter = _lane_masks(4)
    low = _lane_masks(2)[0]
    outs = []
    for pr in range(n_pairs):
        ps = slice(pr * 128, (pr + 1) * 128)
        qv = q[:, ps]
        kv = kp[:, ps]
        o_head = [_softmax_pv(qv * quarter[2 * hh], kv, vx[pr]) - lam * _softmax_pv(qv * quarter[2 * hh + 1], kv, vx[pr])
                  for hh in range(2)]
        outs.append(jnp.where(low > 0, o_head[0], o_head[1]))
    o = jnp.concatenate(outs, axis=-1)
    inv = lax.rsqrt(_sumsq_blocks(o, bd_ref) * (1.0 / DIFF_V_DIM) + EPS)
    y_ref[...] = (o * inv * sg_ref[...] * (1.0 - lam_init)).astype(BF16)


def _diff_mixer(p_all, lam_params, subln, layer_idx, batch, seq, n_ctx, need_ctx):
    n_lat = batch * seq
    cb = n_lat // n_ctx
    lam_init = 0.8 - 0.6 * math.exp(-0.3 * layer_idx)
    lp = jnp.zeros((8, 128), F32).at[:4, :DIFF_QK_DIM].set(lam_params.astype(F32))
    w = DIFF_HEADS * DIFF_V_DIM
    assert w == 256 and DIFF_HEADS * 2 * DIFF_QK_DIM == w
    sg = jnp.tile(subln.astype(F32), DIFF_HEADS).reshape(1, w)
    cq, sq = _rope_tables(seq, DIFF_QK_DIM, w, 8)
    perm = _swap_matrix(w, 8)
    bd = _block_ones(w, DIFF_V_DIM)
    tq = 512
    nqb = seq // tq
    const = lambda shape: pl.BlockSpec(shape, lambda b, i: (0,) * len(shape))
    scratch = lambda nk: [pltpu.VMEM((nk, w), BF16), pltpu.VMEM((DIFF_HEADS // 2, nk, 256), BF16)]
    y_lat = pl.pallas_call(
        functools.partial(_diff_kernel, seq, n_ctx, lam_init),
        grid=(batch, nqb),
        in_specs=[pl.BlockSpec((tq, w), lambda b, i: (b * nqb + i, DIFF_Q)),
                  pl.BlockSpec((seq, w), lambda b, i: (b, DIFF_K)),
                  pl.BlockSpec((seq, w), lambda b, i: (b, DIFF_V)),
                  pl.BlockSpec((n_ctx, w), lambda b, i: (cb + b, DIFF_K)),
                  pl.BlockSpec((n_ctx, w), lambda b, i: (cb + b, DIFF_V)),
                  const((8, 128)), const((1, w)),
                  pl.BlockSpec((tq, w), lambda b, i: (i, 0)),
                  pl.BlockSpec((tq, w), lambda b, i: (i, 0)),
                  const((seq, w)), const((seq, w)), const((w, w)), const((w, w))],
        out_specs=pl.BlockSpec((tq, w), lambda b, i: (b * nqb + i, 0)),
        out_shape=jax.ShapeDtypeStruct((n_lat, w), BF16),
        scratch_shapes=scratch(seq + n_ctx),
        compiler_params=_params("parallel", "arbitrary"),
        name="diff_mixer",
    )(p_all, p_all, p_all, p_all, p_all, lp, sg, cq, sq, cq, sq, perm, bd)
    if not need_ctx:
        return y_lat, None
    y_ctx = pl.pallas_call(
        functools.partial(_diff_kernel, 0, n_ctx, lam_init),
        grid=(batch, 1),
        in_specs=[pl.BlockSpec((n_ctx, w), lambda b, i: (cb + b, DIFF_Q)),
                  pl.BlockSpec((n_ctx, w), lambda b, i: (cb + b, DIFF_K)),
                  pl.BlockSpec((n_ctx, w), lambda b, i: (cb + b, DIFF_V)),
                  const((8, 128)), const((1, w)), const((w, w))],
        out_specs=pl.BlockSpec((n_ctx, w), lambda b, i: (b, 0)),
        out_shape=jax.ShapeDtypeStruct((batch * n_ctx, w), BF16),
        scratch_shapes=scratch(n_ctx),
        compiler_params=_params("parallel", "arbitrary"),
        name="diff_mixer_ctx",
    )(p_all, p_all, p_all, lp, sg, bd)
    return y_lat, y_ctx


def _merge_kernel(nlat_blk, has_ctx, n_parts, *refs):
    x_refs, refs = refs[:n_parts], refs[n_parts:]
    g_refs = refs[:N_BRANCH]
    y_refs = refs[N_BRANCH:2 * N_BRANCH]
    refs = refs[2 * N_BRANCH:]
    if has_ctx:
        yc_refs, refs = refs[:N_BRANCH], refs[N_BRANCH:]
        is_ctx = pl.program_id(0) >= nlat_blk
    wb_ref, wo_ref, mod_ref, gn_ref, wr_ref, xo_ref, m_ref, lg_ref = refs
    acc = None
    for n in range(N_BRANCH):
        y = y_refs[n][...]
        if has_ctx:
            y = jnp.where(is_ctx, yc_refs[n][...], y)
        term = g_refs[n][...].astype(F32) * _dot(y, wb_ref[n])
        acc = term if acc is None else acc + term
    y = _dot(acc.astype(BF16), wo_ref[...])
    x = _pick_rows(x_refs, nlat_blk) + mod_ref[2:3, :] * y
    xo_ref[...] = x
    xn = x * lax.rsqrt(jnp.mean(x * x, axis=-1, keepdims=True) + EPS) * gn_ref[...]
    m = xn * (1.0 + mod_ref[4:5, :]) + mod_ref[3:4, :]
    _store_tile_rows(m_ref, m)
    m_hi = m.astype(BF16)
    m_lo = (m - m_hi.astype(F32)).astype(BF16)
    w = wr_ref[...]
    w_hi = w.astype(BF16)
    w_lo = (w - w_hi.astype(F32)).astype(BF16)
    lg_ref[...] = _dot_nt(w_hi, m_hi) + (_dot_nt(w_hi, m_lo) + _dot_nt(w_lo, m_hi))


def _merge(x_parts, p_all, ys_lat, ys_ctx, wb_bf, wo_bf, modtab, gn, wr_t, tm, n_rows, n_lat, seq, batch):
    d = D_MODEL
    midx = _mod_index(tm, n_lat, seq, batch)
    nlat_blk = n_lat // tm
    has_ctx = ys_ctx is not None
    gate = lambda n: pl.BlockSpec((tm, d), lambda i, n=n: (i, MIX_COLS // d + n))
    y_specs = [pl.BlockSpec((tm, BRANCH_W), lambda i: (jnp.minimum(i, nlat_blk - 1), 0))] * N_BRANCH
    ys = list(ys_lat)
    if has_ctx:
        y_specs += [pl.BlockSpec((tm, BRANCH_W), lambda i: (jnp.maximum(i - nlat_blk, 0), 0))] * N_BRANCH
        ys += list(ys_ctx)
    return pl.pallas_call(
        functools.partial(_merge_kernel, nlat_blk, has_ctx, len(x_parts)),
        grid=(n_rows // tm,),
        in_specs=[*_row_blocks(x_parts, tm, n_lat),
                  gate(0), gate(1), gate(2), gate(3), *y_specs,
                  pl.BlockSpec((N_BRANCH, BRANCH_W, d), lambda i: (0, 0, 0)),
                  pl.BlockSpec((d, d), lambda i: (0, 0)),
                  pl.BlockSpec((None, 8, d), lambda i: (midx(i), 0, 0)),
                  pl.BlockSpec((1, d), lambda i: (0, 0)),
                  pl.BlockSpec((N_EXPERTS, d), lambda i: (0, 0))],
        out_specs=[pl.BlockSpec((tm, d), lambda i: (i, 0)),
                   pl.BlockSpec((tm * ROW_TILE, 128), lambda i: (i, 0)),
                   pl.BlockSpec((N_EXPERTS, tm), lambda i: (0, i))],
        out_shape=[jax.ShapeDtypeStruct((n_rows, d), F32),
                   jax.ShapeDtypeStruct((n_rows * ROW_TILE, 128), F32),
                   jax.ShapeDtypeStruct((N_EXPERTS, n_rows), F32)],
        compiler_params=_params("parallel"),
        name="merge_norm_route",
    )(*x_parts, p_all, p_all, p_all, p_all, *ys, wb_bf, wo_bf, modtab, gn.reshape(1, d), wr_t)


def _route_kernel(lg_ref, b_ref, tri_ref, idx_ref, w_ref, rank_ref, cnt_ref):
    s = _sigmoid(lg_ref[...])
    sel = s + b_ref[...]
    row = lambda a, e: a[e:e + 1, :]
    gsz = EXPERTS_PER_GROUP
    g_idx = None
    best = None
    for g in range(N_GROUPS):
        v = [row(sel, g * gsz + i) for i in range(gsz)]
        score = None
        for i in range(gsz):
            for j in range(i + 1, gsz):
                pair = v[i] + v[j]
                score = pair if score is None else jnp.maximum(score, pair)
        if g == 0:
            best, g_idx = score, jnp.zeros(score.shape, jnp.int32)
        else:
            better = score > best
            best = jnp.where(better, score, best)
            g_idx = jnp.where(better, g, g_idx)

    def in_group(a, i):
        out = row(a, i)
        for g in range(1, N_GROUPS):
            out = jnp.where(g_idx == g, row(a, g * gsz + i), out)
        return out

    v = [in_group(sel, i) for i in range(gsz)]
    sv = [in_group(s, i) for i in range(gsz)]

    def arg_first_max(vals):
        bv, bi = vals[0], jnp.zeros(vals[0].shape, jnp.int32)
        for i in range(1, gsz):
            better = vals[i] > bv
            bv = jnp.where(better, vals[i], bv)
            bi = jnp.where(better, i, bi)
        return bi

    i1 = arg_first_max(v)
    i2 = arg_first_max([jnp.where(i1 == i, -jnp.inf, v[i]) for i in range(gsz)])

    def pick(vals, idx):
        out = vals[0]
        for i in range(1, gsz):
            out = jnp.where(idx == i, vals[i], out)
        return out

    w1 = pick(sv, i1)
    w2 = pick(sv, i2)
    tot = w1 + w2
    e1 = g_idx * gsz + i1
    e2 = g_idx * gsz + i2
    idx_ref[0:1, :] = e1
    idx_ref[1:2, :] = e2
    w_ref[0:1, :] = w1 / tot
    w_ref[1:2, :] = w2 / tot

    @pl.when(pl.program_id(0) == 0)
    def _():
        cnt_ref[...] = jnp.zeros(cnt_ref.shape, cnt_ref.dtype)

    n_e, tn = s.shape
    erow = lax.broadcasted_iota(jnp.int32, (n_e, tn), 0)
    oh1 = jnp.where(erow == e1, 1.0, 0.0)
    oh2 = jnp.where(erow == e2, 1.0, 0.0)
    oh = (oh1 + oh2).astype(BF16)
    base = cnt_ref[...]
    r1, r2 = [], []
    for c in range(tn // 128):
        cs = slice(c * 128, (c + 1) * 128)
        before = base + _dot(oh[:, cs], tri_ref[...])
        r1.append(jnp.sum(oh1[:, cs] * before, axis=0, keepdims=True))
        r2.append(jnp.sum(oh2[:, cs] * before, axis=0, keepdims=True))
        base = base + jnp.sum(oh[:, cs].astype(F32), axis=1, keepdims=True)
    cnt_ref[...] = base
    rank_ref[0:1, :] = jnp.concatenate(r1, axis=-1).astype(jnp.int32)
    rank_ref[1:2, :] = jnp.concatenate(r2, axis=-1).astype(jnp.int32)


def _route(logits_t, router_bias):
    e, n = logits_t.shape
    tn = math.gcd(n, 2048)
    i = np.arange(128)
    tri = jnp.asarray((i[:, None] < i[None, :]).astype(np.float32), BF16)
    idx2, w2, rank, cnt = pl.pallas_call(
        _route_kernel,
        grid=(n // tn,),
        in_specs=[pl.BlockSpec((e, tn), lambda i: (0, i)),
                  pl.BlockSpec((e, 1), lambda i: (0, 0)),
                  pl.BlockSpec((128, 128), lambda i: (0, 0))],
        out_specs=[pl.BlockSpec((TOP_K, tn), lambda i: (0, i)),
                   pl.BlockSpec((TOP_K, tn), lambda i: (0, i)),
                   pl.BlockSpec((TOP_K, tn), lambda i: (0, i)),
                   pl.BlockSpec((e, 128), lambda i: (0, 0))],
        out_shape=[jax.ShapeDtypeStruct((TOP_K, n), jnp.int32),
                   jax.ShapeDtypeStruct((TOP_K, n), F32),
                   jax.ShapeDtypeStruct((TOP_K, n), jnp.int32),
                   jax.ShapeDtypeStruct((e, 128), F32)],
        compiler_params=_params("arbitrary"),
        name="route_top2",
    )(logits_t, router_bias.astype(F32).reshape(e, 1), tri)
    return idx2, w2, rank, cnt[:, 0].astype(jnp.int32)


def _dispatch_kernel(n_tok_steps, pos_ref, m_ref, xs_out, zrow, sem):
    i = pl.program_id(0)
    rt = ROW_TILE
    tm = m_ref.shape[0] // rt
    n = TOP_K * tm

    def wait():
        pltpu.make_async_copy(xs_out.at[pl.ds(0, n * rt)], xs_out.at[pl.ds(0, n * rt)], sem).wait()

    @pl.when(i < n_tok_steps)
    def _():
        for j in range(n):
            dst = pl.multiple_of(pos_ref[i, j], rt)
            pltpu.make_async_copy(m_ref.at[pl.ds((j % tm) * rt, rt)], xs_out.at[pl.ds(dst, rt)],
                                  sem).start(priority=j % 2)
        wait()

    @pl.when(i >= n_tok_steps)
    def _():
        zrow[...] = jnp.zeros(zrow.shape, zrow.dtype)
        for j in range(n):
            dst = pl.multiple_of(pos_ref[i, j], rt)
            pltpu.make_async_copy(zrow, xs_out.at[pl.ds(dst, rt)], sem).start(priority=j % 2)
        wait()


def _dispatch(m_tiles, pos, pad_slots, n_slots):
    rt = ROW_TILE
    n_tok_steps, n = pos.shape
    tm = n // TOP_K
    steps = jnp.concatenate([pos, pad_slots], axis=0)
    grid_spec = pltpu.PrefetchScalarGridSpec(
        num_scalar_prefetch=1,
        grid=(steps.shape[0],),
        in_specs=[pl.BlockSpec((tm * rt, 128), lambda i, pos: (jnp.minimum(i, n_tok_steps - 1), 0))],
        out_specs=pl.BlockSpec(memory_space=pl.ANY),
        scratch_shapes=[pltpu.VMEM((rt, 128), F32), pltpu.SemaphoreType.DMA(())])
    return pl.pallas_call(
        functools.partial(_dispatch_kernel, n_tok_steps),
        grid_spec=grid_spec,
        out_shape=jax.ShapeDtypeStruct((n_slots * rt, 128), F32),
        compiler_params=_params("arbitrary"),
        name="moe_dispatch",
    )(steps, m_tiles)


def _expert_kernel(be_ref, x_ref, wg_ref, wu_ref, wd_ref, y_ref):
    blk = x_ref.shape[0] // ROW_TILE
    x = _load_tile_rows(x_ref, 0, blk).astype(BF16)
    a = _dot(x, wg_ref[...])
    h = a * _sigmoid(a) * _dot(x, wu_ref[...])
    _store_tile_rows(y_ref, _dot(h.astype(BF16), wd_ref[...]))


def _experts(x_slots, blk_e, blk, wg_bf, wu_bf, wd_bf):
    rt = ROW_TILE
    d = D_MODEL
    n_slots = x_slots.shape[0] // rt
    grid_spec = pltpu.PrefetchScalarGridSpec(
        num_scalar_prefetch=1,
        grid=(n_slots // blk,),
        in_specs=[pl.BlockSpec((blk * rt, 128), lambda i, be: (i, 0)),
                  pl.BlockSpec((None, d, D_EXPERT), lambda i, be: (be[i], 0, 0)),
                  pl.BlockSpec((None, d, D_EXPERT), lambda i, be: (be[i], 0, 0)),
                  pl.BlockSpec((None, D_EXPERT, d), lambda i, be: (be[i], 0, 0))],
        out_specs=pl.BlockSpec((blk * rt, 128), lambda i, be: (i, 0)))
    return pl.pallas_call(
        _expert_kernel,
        grid_spec=grid_spec,
        out_shape=jax.ShapeDtypeStruct((n_slots * rt, 128), F32),
        compiler_params=_params("arbitrary"),
        name="expert_ffn",
    )(blk_e, x_slots, wg_bf, wu_bf, wd_bf)


def _combine_kernel(final, pos_ref, y_hbm, x_ref, w_ref, mod_ref, gf_ref, o_ref, ybuf, sem):
    i = pl.program_id(0)
    last = pl.num_programs(0) - 1
    slot = i % 2
    rt = ROW_TILE
    tm = x_ref.shape[0]
    n = TOP_K * tm

    def issue(step, dst_slot):
        for j in range(n):
            src = pl.multiple_of(pos_ref[step, j], rt)
            pltpu.make_async_copy(y_hbm.at[pl.ds(src, rt)], ybuf.at[dst_slot, pl.ds(j * rt, rt)],
                                  sem.at[dst_slot]).start(priority=j % 2)

    def wait(dst_slot):
        pltpu.make_async_copy(y_hbm.at[pl.ds(0, n * rt)], ybuf.at[dst_slot], sem.at[dst_slot]).wait()

    @pl.when(i == 0)
    def _():
        issue(0, 0)

    wait(slot)
    issue(jnp.minimum(i + 1, last), 1 - slot)
    w = w_ref[...]
    yb = ybuf.at[slot]
    f = w[:, 0:1] * _load_tile_rows(yb, 0, tm) + w[:, 1:2] * _load_tile_rows(yb, tm * rt, tm)
    x = x_ref[...] + mod_ref[5:6, :] * f
    if final:
        x = x * lax.rsqrt(jnp.mean(x * x, axis=-1, keepdims=True) + EPS) * gf_ref[...]
    o_ref[...] = x

    @pl.when(i == last)
    def _():
        wait(1 - slot)


def _combine(x_rows, y_slots, pos, w_tok, modtab, g_final, final, n_lat, seq, batch):
    n_rows, d = x_rows.shape
    n_steps, n = pos.shape
    tm = n // TOP_K
    midx = _mod_index(tm, n_lat, seq, batch)
    grid_spec = pltpu.PrefetchScalarGridSpec(
        num_scalar_prefetch=1,
        grid=(n_steps,),
        in_specs=[pl.BlockSpec(memory_space=pl.ANY),
                  pl.BlockSpec((tm, d), lambda i, pos: (i, 0)),
                  pl.BlockSpec((tm, TOP_K), lambda i, pos: (i, 0)),
                  pl.BlockSpec((None, 8, d), lambda i, pos: (midx(i), 0, 0)),
                  pl.BlockSpec((1, d), lambda i, pos: (0, 0))],
        out_specs=pl.BlockSpec((tm, d), lambda i, pos: (i, 0)),
        scratch_shapes=[pltpu.VMEM((2, n * ROW_TILE, 128), F32), pltpu.SemaphoreType.DMA((2,))])
    return pl.pallas_call(
        functools.partial(_combine_kernel, final),
        grid_spec=grid_spec,
        out_shape=jax.ShapeDtypeStruct((n_rows, d), F32),
        compiler_params=_params("arbitrary"),
        name="moe_combine",
    )(pos, y_slots, x_rows, w_tok, modtab, g_final.astype(F32).reshape(1, d))


def _combine_project_kernel(n_col_steps, pos_ref, y_hbm, x_ref, wt_ref, modp_ref, g_ref, mod_ref, w_ref, cs_ref,
                            p_ref, xo_ref, h_ref, ybuf, sem):
    i = pl.program_id(0)
    j = pl.program_id(1)
    last = pl.num_programs(0) - 1
    slot = i % 2
    rt = ROW_TILE
    tm = x_ref.shape[0]
    n = TOP_K * tm
    per_step = n // n_col_steps

    def issue(block, dst_slot, first, count):
        for jj in range(count):
            src = pl.multiple_of(pos_ref[block, first + jj], rt)
            dst = (first + jj) * rt
            if not isinstance(dst, int):
                dst = pl.multiple_of(dst, rt)
            pltpu.make_async_copy(y_hbm.at[pl.ds(src, rt)], ybuf.at[dst_slot, pl.ds(dst, rt)],
                                  sem.at[dst_slot]).start(priority=jj % 2)

    def wait(dst_slot):
        pltpu.make_async_copy(y_hbm.at[pl.ds(0, n * rt)], ybuf.at[dst_slot], sem.at[dst_slot]).wait()

    @pl.when((i == 0) & (j == 0))
    def _():
        issue(0, 0, 0, n)

    @pl.when(j == 0)
    def _():
        wait(slot)
        w = wt_ref[...]
        yb = ybuf.at[slot]
        f = w[:, 0:1] * _load_tile_rows(yb, 0, tm) + w[:, 1:2] * _load_tile_rows(yb, tm * rt, tm)
        x = x_ref[...] + modp_ref[5:6, :] * f
        xo_ref[...] = x
        y = x * lax.rsqrt(jnp.mean(x * x, axis=-1, keepdims=True) + EPS) * g_ref[...]
        h_ref[...] = (y * (1.0 + mod_ref[1:2, :]) + mod_ref[0:1, :]).astype(BF16)

    issue(jnp.minimum(i + 1, last), 1 - slot, j * per_step, per_step)
    acc = _dot(h_ref[...], w_ref[...])
    gate = 0.5 * jnp.tanh(0.5 * acc) + 0.5
    cs = cs_ref[...]
    p_ref[...] = jnp.where(cs > 0.0, acc * cs, gate).astype(BF16)

    @pl.when((i == last) & (j == n_col_steps - 1))
    def _():
        wait(1 - slot)


def _combine_project(x_rows, y_slots, pos, w_tok, modtab_prev, g, modtab, w_bf, n_lat, seq, batch):
    n_rows, d = x_rows.shape
    n_steps, n = pos.shape
    tm = n // TOP_K
    ncol = w_bf.shape[1]
    tn = PROJ_TN
    n_col_steps = ncol // tn
    assert n % n_col_steps == 0
    midx = _mod_index(tm, n_lat, seq, batch)
    grid_spec = pltpu.PrefetchScalarGridSpec(
        num_scalar_prefetch=1,
        grid=(n_steps, n_col_steps),
        in_specs=[pl.BlockSpec(memory_space=pl.ANY),
                  pl.BlockSpec((tm, d), lambda i, j, pos: (i, 0)),
                  pl.BlockSpec((tm, TOP_K), lambda i, j, pos: (i, 0)),
                  pl.BlockSpec((None, 8, d), lambda i, j, pos: (midx(i), 0, 0)),
                  pl.BlockSpec((1, d), lambda i, j, pos: (0, 0)),
                  pl.BlockSpec((None, 8, d), lambda i, j, pos: (midx(i), 0, 0)),
                  pl.BlockSpec((d, tn), lambda i, j, pos: (0, j)),
                  pl.BlockSpec((1, tn), lambda i, j, pos: (0, j))],
        out_specs=[pl.BlockSpec((tm, tn), lambda i, j, pos: (i, j)),
                   pl.BlockSpec((tm, d), lambda i, j, pos: (i, 0))],
        scratch_shapes=[pltpu.VMEM((tm, d), BF16), pltpu.VMEM((2, n * ROW_TILE, 128), F32),
                        pltpu.SemaphoreType.DMA((2,))])
    p_all, x_new = pl.pallas_call(
        functools.partial(_combine_project_kernel, n_col_steps),
        grid_spec=grid_spec,
        out_shape=[jax.ShapeDtypeStruct((n_rows, ncol), BF16), jax.ShapeDtypeStruct((n_rows, d), F32)],
        compiler_params=_params("arbitrary", "arbitrary"),
        name="combine_norm_project",
    )(pos, y_slots, x_rows, w_tok, modtab_prev, g.reshape(1, d), modtab, w_bf, _mix_col_scale())
    return p_all, x_new


def _dispatch_plan(idx2, rank, counts, blk, tm):
    n = idx2.shape[1]
    a = TOP_K * n
    padded = (counts + blk - 1) // blk * blk
    pad_end = jnp.cumsum(padded)
    pad_start = pad_end - padded
    start_of = jnp.zeros(idx2.shape, jnp.int32)
    for e in range(N_EXPERTS):
        start_of = jnp.where(idx2 == e, pad_start[e], start_of)
    dest = (start_of + rank).astype(jnp.int32)
    n_blocks = a // blk + N_EXPERTS
    first_slot = jnp.arange(n_blocks, dtype=jnp.int32) * blk
    blk_e = jnp.minimum(jnp.sum(pad_end[None, :] <= first_slot[:, None], axis=1), N_EXPERTS - 1).astype(jnp.int32)
    pos = dest.reshape(TOP_K, n // tm, tm).transpose(1, 0, 2).reshape(n // tm, TOP_K * tm) * ROW_TILE
    n_slots = n_blocks * blk
    starts = jnp.concatenate([pad_start + counts, pad_end[-1:]])
    lens = jnp.concatenate([padded - counts, n_slots - pad_end[-1:]])
    ends = jnp.cumsum(lens)
    j = jnp.arange(N_EXPERTS * blk, dtype=jnp.int32)
    shift = starts - (ends - lens)
    shift_of = jnp.zeros(j.shape, jnp.int32)
    for s in range(N_EXPERTS + 1):
        shift_of = jnp.where((j >= ends[s] - lens[s]) & (j < ends[s]), shift[s], shift_of)
    pad_slots = ((shift_of + j) * ROW_TILE).astype(jnp.int32)
    return pos, pad_slots.reshape(-1, TOP_K * tm), blk_e, n_slots


def kernel(x, c, ctx, c_ctx, w_mod, b_mod, g_norm1, g_norm2, w_in, na_rpb, ret_log_decay, gqa_q_gain, gqa_k_gain,
           diff_lambda, diff_subln, w_branch, w_out, w_router, router_bias, w_gate_e, w_up_e, w_down_e, g_final):
    batch, seq, d = x.shape
    n_ctx = ctx.shape[1]
    n_lat = batch * seq
    n_ctx_tot = batch * n_ctx
    n_tot = n_lat + n_ctx_tot
    tm = 1024 if n_ctx_tot % 1024 == 0 else 512

    x_parts = (x.reshape(n_lat, d), ctx.reshape(n_ctx_tot, d))

    nrow = -(-(batch + 1) // 8) * 8
    cc = jnp.zeros((nrow, d), F32).at[:batch].set(c).at[batch].set(c_ctx)
    mod = _modulation(cc, w_mod, b_mod)
    modtab = jnp.pad(mod.reshape(DEPTH, nrow, 6, d), ((0, 0), (0, 0), (0, 2), (0, 0)))

    wr_t = w_router.astype(F32).T
    out = None
    pending = None
    for layer in range(DEPTH):
        need_ctx = layer < DEPTH - 1
        n_rows = n_tot if need_ctx else n_lat
        if pending is None:
            p_all = _project(x_parts, g_norm1[layer], modtab[layer], w_in[layer].astype(BF16), tm, n_lat, seq, batch)
        else:
            p_all, x_new = _combine_project(*pending, g_norm1[layer], modtab[layer], w_in[layer].astype(BF16),
                                            n_lat, seq, batch)
            x_parts = (x_new,)
        ya = _na_mixer(p_all, na_rpb[layer], batch, seq, n_ctx, need_ctx)
        yb = _ret_mixer(p_all, ret_log_decay[layer], batch, seq, n_ctx, need_ctx)
        yc = _gqa_mixer(p_all, gqa_q_gain[layer], gqa_k_gain[layer], batch, seq, n_ctx, need_ctx)
        yd = _diff_mixer(p_all, diff_lambda[layer], diff_subln[layer], layer, batch, seq, n_ctx, need_ctx)
        ys_lat = [ya[0], yb[0], yc[0], yd[0]]
        ys_ctx = [ya[1], yb[1], yc[1], yd[1]] if need_ctx else None
        x_mid, m_all, logits_t = _merge(x_parts, p_all, ys_lat, ys_ctx, w_branch[layer].astype(BF16),
                                        w_out[layer].astype(BF16), modtab[layer], g_norm2[layer], wr_t, MERGE_TM,
                                        n_rows, n_lat, seq, batch)
        idx2, w2, rank, counts = _route(logits_t, router_bias)
        pos, pad_slots, blk_e, n_slots = _dispatch_plan(idx2, rank, counts, MOE_BLK, CMB_TM)
        x_slots = _dispatch(m_all, pos, pad_slots, n_slots)
        y_slots = _experts(x_slots, blk_e, MOE_BLK, w_gate_e[layer].astype(BF16), w_up_e[layer].astype(BF16),
                           w_down_e[layer].astype(BF16))
        if need_ctx:
            pending = (x_mid, y_slots, pos, w2.T, modtab[layer])
        else:
            out = _combine(x_mid, y_slots, pos, w2.T, modtab[layer], g_final, True, n_lat, seq, batch)
    return out.reshape(batch, seq, d)
```

```python
import functools
import math

import numpy as np
import jax
import jax.numpy as jnp
from jax import lax
from jax.experimental import pallas as pl
from jax.experimental.pallas import tpu as pltpu

F32 = jnp.float32
BF16 = jnp.bfloat16

D_MODEL = 1024
DEPTH = 2
GRID_W = 64
NA_HEADS = 4
NA_DIM = 64
NA_WIN_H = 8
NA_WIN_W = 16
RET_HEADS = 4
RET_DK = 64
RET_CHUNK = 128
GQA_HEADS = 4
GQA_KV_HEADS = 2
GQA_DIM = 64
DIFF_HEADS = 4
DIFF_QK_DIM = 32
DIFF_V_DIM = 64
N_BRANCH = 4
BRANCH_W = 256
ROPE_THETA = 10000.0
EPS = 1e-6
NEG_INF = -1e30
N_EXPERTS = 16
N_GROUPS = 4
EXPERTS_PER_GROUP = 4
TOP_K = 2
D_EXPERT = 512

MIX_COLS = 3072
IN_COLS = MIX_COLS + N_BRANCH * D_MODEL
NA_Q, NA_K, NA_V = 0, 1, 2
RET_Q, RET_K, RET_V, RET_G = 3, 4, 5, 6
GQA_Q = 7
DIFF_Q, DIFF_K, DIFF_V = 9, 10, 11

LOG2E = math.log2(math.e)
NA_GROUP_ROWS = 4
NA_BAND_ROWS = 12

VMEM_LIMIT = 56 * 1024 * 1024
MOE_BLK = 512
CMB_TM = 512
PROJ_TN = 1792
MERGE_TM = 512


ROW_TILE = D_MODEL // 128


def _store_tile_rows(ref, x):
    n = x.shape[0]
    for j in range(ROW_TILE):
        ref[pl.ds(j, n, stride=ROW_TILE), :] = x[:, j * 128:(j + 1) * 128]


def _load_tile_rows(ref, start, n):
    return jnp.concatenate([ref[pl.ds(start + j, n, stride=ROW_TILE), :] for j in range(ROW_TILE)], axis=-1)


def _dot(a, b):
    return jnp.dot(a, b, preferred_element_type=F32)


def _dot_nt(a, b):
    return lax.dot_general(a, b, (((1,), (1,)), ((), ())), preferred_element_type=F32)


def _dot_tn(a, b):
    return lax.dot_general(a, b, (((0,), (0,)), ((), ())), preferred_element_type=F32)


def _sigmoid(x):
    return 1.0 / (1.0 + jnp.exp(-x))


def _params(*sem):
    return pltpu.CompilerParams(dimension_semantics=sem, vmem_limit_bytes=VMEM_LIMIT)


def _swap_halves(x, dist):
    pieces = []
    for j in range(x.shape[-1] // 128):
        xs = x[:, j * 128:(j + 1) * 128]
        lane = lax.broadcasted_iota(jnp.int32, xs.shape, 1)
        up = pltpu.roll(xs, 128 - dist, 1)
        dn = pltpu.roll(xs, dist, 1)
        pieces.append(jnp.where((lane % (2 * dist)) < dist, up, dn))
    return pieces[0] if len(pieces) == 1 else jnp.concatenate(pieces, axis=-1)


def _rope(x, cos, sin, dist):
    return x * cos + _swap_halves(x, dist) * sin


def _mod_kernel(c_ref, w_ref, b_ref, o_ref):
    c = c_ref[...]
    a = (c * _sigmoid(c)).astype(BF16)
    o_ref[...] = _dot(a, w_ref[...].astype(BF16)) + b_ref[...]


def _modulation(cc, w_mod, b_mod):
    nrow = cc.shape[0]
    depth, d, n6 = w_mod.shape
    tn = 1536
    return pl.pallas_call(
        _mod_kernel,
        grid=(depth, n6 // tn),
        in_specs=[pl.BlockSpec((nrow, d), lambda l, j: (0, 0)),
                  pl.BlockSpec((None, d, tn), lambda l, j: (l, 0, j)),
                  pl.BlockSpec((None, 1, tn), lambda l, j: (l, 0, j))],
        out_specs=pl.BlockSpec((None, nrow, tn), lambda l, j: (l, 0, j)),
        out_shape=jax.ShapeDtypeStruct((depth, nrow, n6), F32),
        compiler_params=_params("parallel", "arbitrary"),
        name="modulation",
    )(cc, w_mod, b_mod.reshape(depth, 1, n6))


def _mix_col_scale():
    cs = np.ones((1, IN_COLS), np.float32)
    cs[0, NA_Q * 256:(NA_Q + 1) * 256] = NA_DIM ** -0.5 * LOG2E
    cs[0, DIFF_Q * 256:(DIFF_Q + 1) * 256] = DIFF_QK_DIM ** -0.5 * LOG2E
    cs[0, RET_K * 256:(RET_K + 1) * 256] = RET_DK ** -0.5
    cs[0, MIX_COLS:] = 0.0
    return jnp.asarray(cs)


def _row_blocks(parts, tm, n_lat):
    d = parts[0].shape[1]
    nlat_blk = n_lat // tm
    if len(parts) == 1:
        return [pl.BlockSpec((tm, d), lambda i, *_: (i, 0))]
    return [pl.BlockSpec((tm, d), lambda i, *_: (jnp.minimum(i, nlat_blk - 1), 0)),
            pl.BlockSpec((tm, d), lambda i, *_: (jnp.maximum(i - nlat_blk, 0), 0))]


def _pick_rows(x_refs, nlat_blk):
    if len(x_refs) == 1:
        return x_refs[0][...]
    return jnp.where(pl.program_id(0) >= nlat_blk, x_refs[1][...], x_refs[0][...])


def _proj_kernel(nlat_blk, n_parts, *refs):
    x_refs, (g_ref, mod_ref, w_ref, cs_ref, o_ref, h_ref) = refs[:n_parts], refs[n_parts:]
    j = pl.program_id(1)

    @pl.when(j == 0)
    def _():
        x = _pick_rows(x_refs, nlat_blk)
        y = x * lax.rsqrt(jnp.mean(x * x, axis=-1, keepdims=True) + EPS) * g_ref[...]
        h_ref[...] = (y * (1.0 + mod_ref[1:2, :]) + mod_ref[0:1, :]).astype(BF16)

    acc = _dot(h_ref[...], w_ref[...])
    gate = 0.5 * jnp.tanh(0.5 * acc) + 0.5
    cs = cs_ref[...]
    o_ref[...] = jnp.where(cs > 0.0, acc * cs, gate).astype(BF16)


def _mod_index(tm, n_lat, seq, batch):
    nlat_blk = n_lat // tm
    bpb = seq // tm

    def index(i):
        return jnp.where(i < nlat_blk, i // bpb, batch)
    return index


def _project(x_parts, g, modtab, w_bf, tm, n_lat, seq, batch):
    n_tot = sum(p.shape[0] for p in x_parts)
    d = x_parts[0].shape[1]
    ncol = w_bf.shape[1]
    tn = PROJ_TN
    midx = _mod_index(tm, n_lat, seq, batch)
    return pl.pallas_call(
        functools.partial(_proj_kernel, n_lat // tm, len(x_parts)),
        grid=(n_tot // tm, ncol // tn),
        in_specs=[*_row_blocks(x_parts, tm, n_lat),
                  pl.BlockSpec((1, d), lambda i, j: (0, 0)),
                  pl.BlockSpec((None, 8, d), lambda i, j: (midx(i), 0, 0)),
                  pl.BlockSpec((d, tn), lambda i, j: (0, j)),
                  pl.BlockSpec((1, tn), lambda i, j: (0, j))],
        out_specs=pl.BlockSpec((tm, tn), lambda i, j: (i, j)),
        out_shape=jax.ShapeDtypeStruct((n_tot, ncol), BF16),
        scratch_shapes=[pltpu.VMEM((tm, d), BF16)],
        compiler_params=_params("parallel", "arbitrary"),
        name="norm_project",
    )(*x_parts, g.reshape(1, d), modtab, w_bf, _mix_col_scale())


def _na_kernel(need_ctx, rows, q_ref, k_ref, v_ref, qc_ref, kc_ref, vc_ref, tb_ref, y_ref, *rest):
    if need_ctx:
        yc_ref, vx, vcx = rest
    else:
        vx, vcx = rest
    gq = NA_GROUP_ROWS * GRID_W
    band = NA_BAND_ROWS * GRID_W
    n_groups = rows // NA_GROUP_ROWS
    n_pairs = NA_HEADS // 2
    masks = _lane_masks(2)
    low = masks[0] > 0

    for p in range(n_pairs):
        ps = slice(p * 128, (p + 1) * 128)
        vx[p] = jnp.concatenate([v_ref[:, ps], jnp.ones((v_ref.shape[0], 128), BF16)], axis=-1)
        vcx[p] = jnp.concatenate([vc_ref[:, ps], jnp.ones((vc_ref.shape[0], 128), BF16)], axis=-1)

    def group_body(g, carry):
        u = jnp.clip(g * NA_GROUP_ROWS - NA_WIN_H // 2, 0, rows - NA_BAND_ROWS)
        typ = jnp.where(g == 0, 0, jnp.where(g == n_groups - 1, 2, 1))
        qs = pl.multiple_of(g * gq, gq)
        ks = pl.multiple_of(u * GRID_W, GRID_W)
        for p in range(n_pairs):
            ps = slice(p * 128, (p + 1) * 128)
            q = q_ref[pl.ds(qs, gq), ps]
            kb = k_ref[pl.ds(ks, band), ps]
            kc = kc_ref[:, ps]
            outs = []
            for hh in range(2):
                qm = q * masks[hh]
                s_w = _dot_nt(qm, kb) + tb_ref[2 * p + hh, typ]
                s_c = _dot_nt(qm, kc)
                m = jnp.maximum(jnp.max(s_w, axis=-1, keepdims=True), jnp.max(s_c, axis=-1, keepdims=True))
                oe = (_dot(jnp.exp2((s_w - m).astype(BF16)), vx[p, pl.ds(ks, band), :])
                      + _dot(jnp.exp2((s_c - m).astype(BF16)), vcx[p]))
                outs.append(oe[:, :128] / oe[:, 128:])
            y_ref[pl.ds(qs, gq), ps] = jnp.where(low, outs[0], outs[1]).astype(BF16)
        return carry

    lax.fori_loop(0, n_groups, group_body, 0)

    if need_ctx:
        for p in range(n_pairs):
            ps = slice(p * 128, (p + 1) * 128)
            outs = [_softmax_pv(qc_ref[:, ps] * masks[hh], kc_ref[:, ps], vcx[p]) for hh in range(2)]
            yc_ref[:, ps] = jnp.where(low, outs[0], outs[1]).astype(BF16)


def _na_bias_table(rpb, rows):
    assert rows % NA_GROUP_ROWS == 0 and rows >= NA_BAND_ROWS + NA_GROUP_ROWS
    n_groups = rows // NA_GROUP_ROWS
    qc = np.arange(GRID_W)[:, None]
    kc = np.arange(GRID_W)[None, :]
    win_start = np.clip(qc - NA_WIN_W // 2, 0, GRID_W - NA_WIN_W)
    col_ok = (kc >= win_start) & (kc < win_start + NA_WIN_W)
    col_idx = np.clip(kc - qc, -(NA_WIN_W - 1), NA_WIN_W - 1) + NA_WIN_W - 1
    n_col = 2 * NA_WIN_W - 1
    onehot = jnp.asarray((np.arange(n_col)[:, None] == col_idx.reshape(1, -1)).astype(np.float32))
    picked = jnp.dot(rpb.astype(F32).reshape(-1, n_col), onehot, precision=lax.Precision.HIGHEST)
    picked = picked.reshape(NA_HEADS, 2 * NA_WIN_H - 1, GRID_W, GRID_W)
    tiles = jnp.where(col_ok[None, None], picked * LOG2E, NEG_INF)
    masked = jnp.full((NA_HEADS, GRID_W, GRID_W), NEG_INF, F32)
    tables = []
    for g in (0, 1, n_groups - 1):
        u = int(np.clip(g * NA_GROUP_ROWS - NA_WIN_H // 2, 0, rows - NA_BAND_ROWS))
        per_row = []
        for a in range(NA_GROUP_ROWS):
            r = g * NA_GROUP_ROWS + a
            r0 = int(np.clip(r - NA_WIN_H // 2, 0, rows - NA_WIN_H))
            pieces = [tiles[:, kr - r + NA_WIN_H - 1] if r0 <= kr < r0 + NA_WIN_H else masked
                      for kr in range(u, u + NA_BAND_ROWS)]
            per_row.append(jnp.concatenate(pieces, axis=-1))
        tables.append(jnp.concatenate(per_row, axis=1))
    return jnp.stack(tables, axis=1)


def _na_mixer(p_all, rpb, batch, seq, n_ctx, need_ctx):
    n_lat = batch * seq
    rows = seq // GRID_W
    tb = _na_bias_table(rpb, rows)
    cb = n_lat // n_ctx
    w = 256
    lat = lambda c: pl.BlockSpec((seq, w), lambda b, c=c: (b, c))
    ctx = lambda c: pl.BlockSpec((n_ctx, w), lambda b, c=c: (cb + b, c))
    out_shape = [jax.ShapeDtypeStruct((n_lat, w), BF16)]
    out_specs = [pl.BlockSpec((seq, w), lambda b: (b, 0))]
    if need_ctx:
        out_shape.append(jax.ShapeDtypeStruct((batch * n_ctx, w), BF16))
        out_specs.append(pl.BlockSpec((n_ctx, w), lambda b: (b, 0)))
    return pl.pallas_call(
        functools.partial(_na_kernel, need_ctx, rows),
        grid=(batch,),
        in_specs=[lat(NA_Q), lat(NA_K), lat(NA_V), ctx(NA_Q), ctx(NA_K), ctx(NA_V),
                  pl.BlockSpec(tb.shape, lambda b: (0, 0, 0, 0))],
        out_specs=out_specs,
        out_shape=out_shape,
        scratch_shapes=[pltpu.VMEM((NA_HEADS // 2, seq, 256), BF16), pltpu.VMEM((NA_HEADS // 2, n_ctx, 256), BF16)],
        compiler_params=_params("parallel"),
        name="na_mixer",
    )(p_all, p_all, p_all, p_all, p_all, p_all, tb)


def _ret_kernel(need_ctx, seq, n_ctx, lg_ref, q_ref, k_ref, v_ref, g_ref, qc_ref, kc_ref, vc_ref, gc_ref,
                cos_ref, sin_ref, perm_ref, bd_ref, y_ref, *rest):
    if need_ctx:
        yc_ref, qr, kr, krt, kct, o_f, o_b, oc_f, oc_b, dm, qd, kdt = rest
    else:
        qr, kr, krt, kct, o_f, o_b, oc_f, oc_b, dm, qd, kdt = rest
    ch = RET_CHUNK
    n_lat_ch = seq // ch
    n_ctx_ch = n_ctx // ch
    n_pairs = RET_HEADS // 2
    assert ch == 128 and 2 * RET_DK == 128

    def prep(i, carry):
        rs = pl.multiple_of(i * ch, ch)
        c = cos_ref[pl.ds(rs, ch), :]
        s = sin_ref[pl.ds(rs, ch), :]
        qr[pl.ds(rs, ch), :] = _rope_mxu(q_ref[pl.ds(rs, ch), :], c, s, perm_ref)
        k = _rope_mxu(k_ref[pl.ds(rs, ch), :], c, s, perm_ref)
        kr[pl.ds(rs, ch), :] = k
        for p in range(n_pairs):
            krt[i, p * 128:(p + 1) * 128, :] = k[:, p * 128:(p + 1) * 128].T
        return carry

    lax.fori_loop(0, n_lat_ch, prep, 0)
    for n in range(n_ctx_ch):
        kc = kc_ref[n * ch:(n + 1) * ch, :].astype(F32)
        for p in range(n_pairs):
            kct[n, p * 128:(p + 1) * 128, :] = kc[:, p * 128:(p + 1) * 128].T

    ri = lax.broadcasted_iota(jnp.int32, (ch, ch), 0)
    ci = lax.broadcasted_iota(jnp.int32, (ch, ch), 1)
    lag = (ri - ci).astype(F32)
    rowf = ri.astype(F32)
    colf = ci.astype(F32)
    low_lane = ci < RET_DK
    low_row = ri < RET_DK
    block_diag = jnp.where(low_lane == low_row, 1.0, 0.0)
    low, high = _lane_masks(2)

    chains = [(p, dirn) for p in range(n_pairs) for dirn in range(2)]
    cdec = []
    for c, (p, dirn) in enumerate(chains):
        lg_a = lg_ref[dirn, 2 * p]
        lg_b = lg_ref[dirn, 2 * p + 1]
        lg_lane = jnp.where(low_lane, lg_a, lg_b)
        lg_row = jnp.where(low_row, lg_a, lg_b)
        for hh, lg in enumerate((lg_a, lg_b)):
            if dirn == 0:
                keep = ri >= ci
                dm[2 * c + hh] = jnp.where(keep, jnp.exp(jnp.where(keep, lag, 0.0) * lg), 0.0)
            else:
                keep = ci >= ri
                dm[2 * c + hh] = jnp.where(keep, jnp.exp(jnp.where(keep, -lag, 0.0) * lg), 0.0)
        if dirn == 0:
            qd[c] = jnp.exp((rowf + 1.0) * lg_lane)
            kdt[c] = jnp.exp((ch - 1.0 - colf) * lg_row)
        else:
            qd[c] = jnp.exp((ch - rowf) * lg_lane)
            kdt[c] = jnp.exp(colf * lg_row)
        cdec.append(jnp.exp(ch * lg_lane[0:1, :]))

    def step(c, state, qi, ki, kti, vi):
        kb = ki.astype(BF16)
        inner_a = (_dot_nt((qi * low).astype(BF16), kb) * dm[2 * c]).astype(BF16)
        inner_b = (_dot_nt((qi * high).astype(BF16), kb) * dm[2 * c + 1]).astype(BF16)
        o = (jnp.where(low_lane, _dot(inner_a, vi), _dot(inner_b, vi))
             + _dot((qi * qd[c]).astype(BF16), state.astype(BF16)))
        state = state * cdec[c] + block_diag * _dot((kti * kdt[c]).astype(BF16), vi)
        return state, o

    states = []
    for c, (p, dirn) in enumerate(chains):
        ps = slice(p * 128, (p + 1) * 128)
        state = jnp.zeros((128, 128), F32)
        for n in (range(n_ctx_ch) if dirn == 0 else range(n_ctx_ch - 1, -1, -1)):
            rs = slice(n * ch, (n + 1) * ch)
            state, o = step(c, state, qc_ref[rs, ps].astype(F32), kc_ref[rs, ps].astype(F32), kct[n, ps, :],
                            vc_ref[rs, ps])
            if need_ctx:
                (oc_f if dirn == 0 else oc_b)[rs, ps] = o
        states.append(state)

    def lat_body(i, states):
        new = []
        for c, (p, dirn) in enumerate(chains):
            ps = slice(p * 128, (p + 1) * 128)
            n = i if dirn == 0 else n_lat_ch - 1 - i
            rs = pl.ds(pl.multiple_of(n * ch, ch), ch)
            state, o = step(c, states[c], qr[rs, ps], kr[rs, ps], krt[n, ps, :], v_ref[rs, ps])
            (o_f if dirn == 0 else o_b)[rs, ps] = o
            new.append(state)
        return tuple(new)

    lax.fori_loop(0, n_lat_ch, lat_body, tuple(states))

    def head_mean(x):
        hi = x.astype(BF16)
        lo = (x - hi.astype(F32)).astype(BF16)
        bd = bd_ref[...]
        return (_dot(hi, bd) + _dot(lo, bd)) * (1.0 / RET_DK)

    def finish(o, g):
        cen = o - head_mean(o)
        on = cen * lax.rsqrt(head_mean(cen * cen) + EPS)
        g = g.astype(F32)
        return (g * _sigmoid(g) * on).astype(BF16)

    def fin_body(i, carry):
        rs = pl.multiple_of(i * ch, ch)
        y_ref[pl.ds(rs, ch), :] = finish(o_f[pl.ds(rs, ch), :] + o_b[pl.ds(rs, ch), :], g_ref[pl.ds(rs, ch), :])
        return carry

    lax.fori_loop(0, n_lat_ch, fin_body, 0)
    if need_ctx:
        for n in range(n_ctx_ch):
            rs = slice(n * ch, (n + 1) * ch)
            yc_ref[rs, :] = finish(oc_f[rs, :] + oc_b[rs, :], gc_ref[rs, :])


def _rope_tables(seq, head_dim, width, dist):
    half = head_dim // 2
    nf = half // 2
    assert nf == dist
    inv = 1.0 / (ROPE_THETA ** (np.arange(nf, dtype=np.float32) / nf))
    t = np.arange(seq)
    rows = (t // GRID_W).astype(np.float32)[:, None] * inv[None, :]
    cols = (t % GRID_W).astype(np.float32)[:, None] * inv[None, :]
    cos = np.concatenate([np.cos(rows), np.cos(rows), np.cos(cols), np.cos(cols)], axis=-1)
    sin = np.concatenate([-np.sin(rows), np.sin(rows), -np.sin(cols), np.sin(cols)], axis=-1)
    reps = width // head_dim
    return (jnp.asarray(np.tile(cos, (1, reps)), F32), jnp.asarray(np.tile(sin, (1, reps)), F32))


def _ret_mixer(p_all, log_decay, batch, seq, n_ctx, need_ctx):
    n_lat = batch * seq
    cb = n_lat // n_ctx
    w = 256
    ch = RET_CHUNK
    log_gamma = jnp.log1p(-jnp.exp(log_decay.astype(F32)))
    cos, sin = _rope_tables(seq, RET_DK, w, 16)
    lat = lambda c: pl.BlockSpec((seq, w), lambda b, c=c: (b, c))
    ctx = lambda c: pl.BlockSpec((n_ctx, w), lambda b, c=c: (cb + b, c))
    whole = pl.BlockSpec((seq, w), lambda b: (0, 0))
    out_shape = [jax.ShapeDtypeStruct((n_lat, w), BF16)]
    out_specs = [pl.BlockSpec((seq, w), lambda b: (b, 0))]
    if need_ctx:
        out_shape.append(jax.ShapeDtypeStruct((batch * n_ctx, w), BF16))
        out_specs.append(pl.BlockSpec((n_ctx, w), lambda b: (b, 0)))
    return pl.pallas_call(
        functools.partial(_ret_kernel, need_ctx, seq, n_ctx),
        grid=(batch,),
        in_specs=[pl.BlockSpec(memory_space=pltpu.SMEM),
                  lat(RET_Q), lat(RET_K), lat(RET_V), lat(RET_G),
                  ctx(RET_Q), ctx(RET_K), ctx(RET_V), ctx(RET_G), whole, whole,
                  pl.BlockSpec((w, w), lambda b: (0, 0)), pl.BlockSpec((w, w), lambda b: (0, 0))],
        out_specs=out_specs,
        out_shape=out_shape,
        scratch_shapes=[pltpu.VMEM((seq, w), F32), pltpu.VMEM((seq, w), F32),
                        pltpu.VMEM((seq // ch, w, ch), F32), pltpu.VMEM((n_ctx // ch, w, ch), F32),
                        pltpu.VMEM((seq, w), F32), pltpu.VMEM((seq, w), F32),
                        pltpu.VMEM((n_ctx, w), F32), pltpu.VMEM((n_ctx, w), F32),
                        pltpu.VMEM((2 * RET_HEADS, ch, ch), F32),
                        pltpu.VMEM((RET_HEADS, ch, 128), F32), pltpu.VMEM((RET_HEADS, 128, ch), F32)],
        compiler_params=_params("parallel"),
        name="retention_mixer",
    )(log_gamma, p_all, p_all, p_all, p_all, p_all, p_all, p_all, p_all, cos, sin,
      _swap_matrix(w, 16), _block_ones(w, RET_DK))


def _swap_matrix(width, dist):
    i = np.arange(width)
    partner = np.where(i % (2 * dist) < dist, i + dist, i - dist)
    p = np.zeros((width, width), np.float32)
    p[partner, i] = 1.0
    return jnp.asarray(p, BF16)


def _block_ones(width, block):
    i = np.arange(width)
    return jnp.asarray((i[:, None] // block == i[None, :] // block).astype(np.float32), BF16)


def _sumsq_blocks(xf, bd_ref):
    sq = xf * xf
    hi = sq.astype(BF16)
    lo = (sq - hi.astype(F32)).astype(BF16)
    bd = bd_ref[...]
    return _dot(hi, bd) + _dot(lo, bd)


def _rope_mxu(x, cos, sin, perm_ref):
    return x.astype(F32) * cos + _dot(x, perm_ref[...]) * sin


def _softmax_pv(qm, k, v_ones):
    s = _dot_nt(qm, k)
    p = jnp.exp2((s - jnp.max(s, axis=-1, keepdims=True)).astype(BF16))
    oe = _dot(p, v_ones)
    return oe[:, :128] / oe[:, 128:]


def _lane_masks(n_parts):
    lane = lax.broadcasted_iota(jnp.int32, (1, 128), 1)
    return [jnp.where(lane // (128 // n_parts) == i, 1.0, 0.0).astype(BF16) for i in range(n_parts)]


def _gqa_kernel(n_lat_k, n_ctx, *refs):
    if n_lat_k:
        (q_ref, k_ref, v_ref, kc_ref, vc_ref, qa_ref, qb_ref, ka_ref, kb_ref, kg_ref, pq_ref, pk_ref,
         bdq_ref, bdk_ref, y_ref, kp, vx) = refs
    else:
        q_ref, kc_ref, vc_ref, qa_ref, kg_ref, bdq_ref, bdk_ref, y_ref, kp, vx = refs
    dim = GQA_DIM
    nk = n_lat_k + n_ctx

    def inv_rms(xf, bd_ref):
        return lax.rsqrt(_sumsq_blocks(xf, bd_ref) * (1.0 / dim) + EPS)

    @pl.when(pl.program_id(1) == 0)
    def _():
        def put(rs, kn, v):
            ones = jnp.ones(v.shape, BF16)
            kp[0, rs, :] = kn.astype(BF16)
            kp[1, rs, :] = pltpu.roll(kn, dim, 1).astype(BF16)
            vx[0, rs, :] = jnp.concatenate([v, ones], axis=-1)
            vx[1, rs, :] = jnp.concatenate([pltpu.roll(v.astype(F32), dim, 1).astype(BF16), ones], axis=-1)

        kc = kc_ref[...]
        kcf = kc.astype(F32)
        put(slice(n_lat_k, nk), kcf * kg_ref[...] * inv_rms(kcf, bdk_ref), vc_ref[...])
        ck = 512
        for c in range(n_lat_k // ck):
            rs = slice(c * ck, (c + 1) * ck)
            k = k_ref[rs, :]
            put(rs, _rope_mxu(k, ka_ref[rs, :], kb_ref[rs, :], pk_ref) * inv_rms(k.astype(F32), bdk_ref), v_ref[rs, :])

    q = q_ref[...]
    qf = q.astype(F32)
    qn = _rope_mxu(q, qa_ref[...], qb_ref[...], pq_ref) if n_lat_k else qf * qa_ref[...]
    qn = (qn * inv_rms(qf, bdq_ref)).astype(BF16)
    low, high = _lane_masks(2)
    outs = []
    for g in range(GQA_KV_HEADS):
        qv = qn[:, g * 128:(g + 1) * 128]
        o_low = _softmax_pv(qv * low, kp[g], vx[g])
        o_high = _softmax_pv(qv * high, kp[1 - g], vx[1 - g])
        outs.append(jnp.where(low > 0, o_low, o_high))
    y_ref[...] = jnp.concatenate(outs, axis=-1).astype(BF16)


def _gqa_mixer(p_all, q_gain, k_gain, batch, seq, n_ctx, need_ctx):
    n_lat = batch * seq
    cb = n_lat // n_ctx
    dim = GQA_DIM
    qw = GQA_HEADS * dim
    kvw = GQA_KV_HEADS * dim
    assert qw == 256 and kvw == 128

    def swapped(g):
        return g.reshape(-1, 2, 16)[:, ::-1].reshape(1, -1)

    qg = jnp.tile(q_gain.astype(F32) * (dim ** -0.5 * LOG2E), GQA_HEADS).reshape(1, qw)
    kg = jnp.tile(k_gain.astype(F32), GQA_KV_HEADS).reshape(1, kvw)
    cq, sq = _rope_tables(seq, dim, qw, 16)
    ck, sk = _rope_tables(seq, dim, kvw, 16)
    qa, qb = cq * qg, sq * swapped(qg)
    ka, kb = ck * kg, sk * swapped(kg)
    pq, pk = _swap_matrix(qw, 16), _swap_matrix(kvw, 16)
    bdq, bdk = _block_ones(qw, dim), _block_ones(kvw, dim)
    tq = 512
    nqb = seq // tq
    const = lambda shape: pl.BlockSpec(shape, lambda b, i: (0,) * len(shape))
    scratch = lambda nk: [pltpu.VMEM((2, nk, kvw), BF16), pltpu.VMEM((2, nk, 2 * kvw), BF16)]
    y_lat = pl.pallas_call(
        functools.partial(_gqa_kernel, seq, n_ctx),
        grid=(batch, nqb),
        in_specs=[pl.BlockSpec((tq, qw), lambda b, i: (b * nqb + i, GQA_Q)),
                  pl.BlockSpec((seq, kvw), lambda b, i: (b, 16)),
                  pl.BlockSpec((seq, kvw), lambda b, i: (b, 17)),
                  pl.BlockSpec((n_ctx, kvw), lambda b, i: (cb + b, 16)),
                  pl.BlockSpec((n_ctx, kvw), lambda b, i: (cb + b, 17)),
                  pl.BlockSpec((tq, qw), lambda b, i: (i, 0)),
                  pl.BlockSpec((tq, qw), lambda b, i: (i, 0)),
                  const((seq, kvw)), const((seq, kvw)), const((1, kvw)),
                  const((qw, qw)), const((kvw, kvw)), const((qw, qw)), const((kvw, kvw))],
        out_specs=pl.BlockSpec((tq, qw), lambda b, i: (b * nqb + i, 0)),
        out_shape=jax.ShapeDtypeStruct((n_lat, qw), BF16),
        scratch_shapes=scratch(seq + n_ctx),
        compiler_params=_params("parallel", "arbitrary"),
        name="gqa_mixer",
    )(p_all, p_all, p_all, p_all, p_all, qa, qb, ka, kb, kg, pq, pk, bdq, bdk)
    if not need_ctx:
        return y_lat, None
    y_ctx = pl.pallas_call(
        functools.partial(_gqa_kernel, 0, n_ctx),
        grid=(batch, 1),
        in_specs=[pl.BlockSpec((n_ctx, qw), lambda b, i: (cb + b, GQA_Q)),
                  pl.BlockSpec((n_ctx, kvw), lambda b, i: (cb + b, 16)),
                  pl.BlockSpec((n_ctx, kvw), lambda b, i: (cb + b, 17)),
                  const((1, qw)), const((1, kvw)), const((qw, qw)), const((kvw, kvw))],
        out_specs=pl.BlockSpec((n_ctx, qw), lambda b, i: (b, 0)),
        out_shape=jax.ShapeDtypeStruct((batch * n_ctx, qw), BF16),
        scratch_shapes=scratch(n_ctx),
        compiler_params=_params("parallel", "arbitrary"),
        name="gqa_mixer_ctx",
    )(p_all, p_all, p_all, qg, kg, bdq, bdk)
    return y_lat, y_ctx


def _diff_kernel(n_lat_k, n_ctx, lam_init, *refs):
    if n_lat_k:
        (q_ref, k_ref, v_ref, kc_ref, vc_ref, lp_ref, sg_ref, cq_ref, sq_ref, ck_ref, sk_ref, perm_ref, bd_ref,
         y_ref, kp, vx) = refs
    else:
        q_ref, kc_ref, vc_ref, lp_ref, sg_ref, bd_ref, y_ref, kp, vx = refs
    nk = n_lat_k + n_ctx
    n_pairs = DIFF_HEADS // 2

    @pl.when(pl.program_id(1) == 0)
    def _():
        def put(rs, k, v):
            kp[rs, :] = k
            ones = jnp.ones((v.shape[0], 128), BF16)
            for pr in range(n_pairs):
                vx[pr, rs, :] = jnp.concatenate([v[:, pr * 128:(pr + 1) * 128], ones], axis=-1)

        put(slice(n_lat_k, nk), kc_ref[...], vc_ref[...])
        ck = 512
        for c in range(n_lat_k // ck):
            rs = slice(c * ck, (c + 1) * ck)
            put(rs, _rope_mxu(k_ref[rs, :], ck_ref[rs, :], sk_ref[rs, :], perm_ref).astype(BF16), v_ref[rs, :])

    lp = lp_ref[...]
    lam = (jnp.exp(jnp.sum(lp[0:1, :] * lp[1:2, :], axis=-1, keepdims=True))
           - jnp.exp(jnp.sum(lp[2:3, :] * lp[3:4, :], axis=-1, keepdims=True)) + lam_init)

    q = q_ref[...]
    if n_lat_k:
        q = _rope_mxu(q, cq_ref[...], sq_ref[...], perm_ref).astype(BF16)
    quarter = _lane_masks(4)
    low = _lane_masks(2)[0]
    outs = []
    for pr in range(n_pairs):
        ps = slice(pr * 128, (pr + 1) * 128)
        qv = q[:, ps]
        kv = kp[:, ps]
        o_head = [_softmax_pv(qv * quarter[2 * hh], kv, vx[pr]) - lam * _softmax_pv(qv * quarter[2 * hh + 1], kv, vx[pr])
                  for hh in range(2)]
        outs.append(jnp.where(low > 0, o_head[0], o_head[1]))
    o = jnp.concatenate(outs, axis=-1)
    inv = lax.rsqrt(_sumsq_blocks(o, bd_ref) * (1.0 / DIFF_V_DIM) + EPS)
    y_ref[...] = (o * inv * sg_ref[...] * (1.0 - lam_init)).astype(BF16)


def _diff_mixer(p_all, lam_params, subln, layer_idx, batch, seq, n_ctx, need_ctx):
    n_lat = batch * seq
    cb = n_lat // n_ctx
    lam_init = 0.8 - 0.6 * math.exp(-0.3 * layer_idx)
    lp = jnp.zeros((8, 128), F32).at[:4, :DIFF_QK_DIM].set(lam_params.astype(F32))
    w = DIFF_HEADS * DIFF_V_DIM
    assert w == 256 and DIFF_HEADS * 2 * DIFF_QK_DIM == w
    sg = jnp.tile(subln.astype(F32), DIFF_HEADS).reshape(1, w)
    cq, sq = _rope_tables(seq, DIFF_QK_DIM, w, 8)
    perm = _swap_matrix(w, 8)
    bd = _block_ones(w, DIFF_V_DIM)
    tq = 512
    nqb = seq // tq
    const = lambda shape: pl.BlockSpec(shape, lambda b, i: (0,) * len(shape))
    scratch = lambda nk: [pltpu.VMEM((nk, w), BF16), pltpu.VMEM((DIFF_HEADS // 2, nk, 256), BF16)]
    y_lat = pl.pallas_call(
        functools.partial(_diff_kernel, seq, n_ctx, lam_init),
        grid=(batch, nqb),
        in_specs=[pl.BlockSpec((tq, w), lambda b, i: (b * nqb + i, DIFF_Q)),
                  pl.BlockSpec((seq, w), lambda b, i: (b, DIFF_K)),
                  pl.BlockSpec((seq, w), lambda b, i: (b, DIFF_V)),
                  pl.BlockSpec((n_ctx, w), lambda b, i: (cb + b, DIFF_K)),
                  pl.BlockSpec((n_ctx, w), lambda b, i: (cb + b, DIFF_V)),
                  const((8, 128)), const((1, w)),
                  pl.BlockSpec((tq, w), lambda b, i: (i, 0)),
                  pl.BlockSpec((tq, w), lambda b, i: (i, 0)),
                  const((seq, w)), const((seq, w)), const((w, w)), const((w, w))],
        out_specs=pl.BlockSpec((tq, w), lambda b, i: (b * nqb + i, 0)),
        out_shape=jax.ShapeDtypeStruct((n_lat, w), BF16),
        scratch_shapes=scratch(seq + n_ctx),
        compiler_params=_params("parallel", "arbitrary"),
        name="diff_mixer",
    )(p_all, p_all, p_all, p_all, p_all, lp, sg, cq, sq, cq, sq, perm, bd)
    if not need_ctx:
        return y_lat, None
    y_ctx = pl.pallas_call(
        functools.partial(_diff_kernel, 0, n_ctx, lam_init),
        grid=(batch, 1),
        in_specs=[pl.BlockSpec((n_ctx, w), lambda b, i: (cb + b, DIFF_Q)),
                  pl.BlockSpec((n_ctx, w), lambda b, i: (cb + b, DIFF_K)),
                  pl.BlockSpec((n_ctx, w), lambda b, i: (cb + b, DIFF_V)),
                  const((8, 128)), const((1, w)), const((w, w))],
        out_specs=pl.BlockSpec((n_ctx, w), lambda b, i: (b, 0)),
        out_shape=jax.ShapeDtypeStruct((batch * n_ctx, w), BF16),
        scratch_shapes=scratch(n_ctx),
        compiler_params=_params("parallel", "arbitrary"),
        name="diff_mixer_ctx",
    )(p_all, p_all, p_all, lp, sg, bd)
    return y_lat, y_ctx


def _merge_kernel(nlat_blk, has_ctx, n_parts, *refs):
    x_refs, refs = refs[:n_parts], refs[n_parts:]
    g_refs = refs[:N_BRANCH]
    y_refs = refs[N_BRANCH:2 * N_BRANCH]
    refs = refs[2 * N_BRANCH:]
    if has_ctx:
        yc_refs, refs = refs[:N_BRANCH], refs[N_BRANCH:]
        is_ctx = pl.program_id(0) >= nlat_blk
    wb_ref, wo_ref, mod_ref, gn_ref, wr_ref, xo_ref, m_ref, lg_ref = refs
    acc = None
    for n in range(N_BRANCH):
        y = y_refs[n][...]
        if has_ctx:
            y = jnp.where(is_ctx, yc_refs[n][...], y)
        term = g_refs[n][...].astype(F32) * _dot(y, wb_ref[n])
        acc = term if acc is None else acc + term
    y = _dot(acc.astype(BF16), wo_ref[...])
    x = _pick_rows(x_refs, nlat_blk) + mod_ref[2:3, :] * y
    xo_ref[...] = x
    xn = x * lax.rsqrt(jnp.mean(x * x, axis=-1, keepdims=True) + EPS) * gn_ref[...]
    m = xn * (1.0 + mod_ref[4:5, :]) + mod_ref[3:4, :]
    _store_tile_rows(m_ref, m)
    m_hi = m.astype(BF16)
    m_lo = (m - m_hi.astype(F32)).astype(BF16)
    w = wr_ref[...]
    w_hi = w.astype(BF16)
    w_lo = (w - w_hi.astype(F32)).astype(BF16)
    lg_ref[...] = _dot_nt(w_hi, m_hi) + (_dot_nt(w_hi, m_lo) + _dot_nt(w_lo, m_hi))


def _merge(x_parts, p_all, ys_lat, ys_ctx, wb_bf, wo_bf, modtab, gn, wr_t, tm, n_rows, n_lat, seq, batch):
    d = D_MODEL
    midx = _mod_index(tm, n_lat, seq, batch)
    nlat_blk = n_lat // tm
    has_ctx = ys_ctx is not None
    gate = lambda n: pl.BlockSpec((tm, d), lambda i, n=n: (i, MIX_COLS // d + n))
    y_specs = [pl.BlockSpec((tm, BRANCH_W), lambda i: (jnp.minimum(i, nlat_blk - 1), 0))] * N_BRANCH
    ys = list(ys_lat)
    if has_ctx:
        y_specs += [pl.BlockSpec((tm, BRANCH_W), lambda i: (jnp.maximum(i - nlat_blk, 0), 0))] * N_BRANCH
        ys += list(ys_ctx)
    return pl.pallas_call(
        functools.partial(_merge_kernel, nlat_blk, has_ctx, len(x_parts)),
        grid=(n_rows // tm,),
        in_specs=[*_row_blocks(x_parts, tm, n_lat),
                  gate(0), gate(1), gate(2), gate(3), *y_specs,
                  pl.BlockSpec((N_BRANCH, BRANCH_W, d), lambda i: (0, 0, 0)),
                  pl.BlockSpec((d, d), lambda i: (0, 0)),
                  pl.BlockSpec((None, 8, d), lambda i: (midx(i), 0, 0)),
                  pl.BlockSpec((1, d), lambda i: (0, 0)),
                  pl.BlockSpec((N_EXPERTS, d), lambda i: (0, 0))],
        out_specs=[pl.BlockSpec((tm, d), lambda i: (i, 0)),
                   pl.BlockSpec((tm * ROW_TILE, 128), lambda i: (i, 0)),
                   pl.BlockSpec((N_EXPERTS, tm), lambda i: (0, i))],
        out_shape=[jax.ShapeDtypeStruct((n_rows, d), F32),
                   jax.ShapeDtypeStruct((n_rows * ROW_TILE, 128), F32),
                   jax.ShapeDtypeStruct((N_EXPERTS, n_rows), F32)],
        compiler_params=_params("parallel"),
        name="merge_norm_route",
    )(*x_parts, p_all, p_all, p_all, p_all, *ys, wb_bf, wo_bf, modtab, gn.reshape(1, d), wr_t)


def _route_kernel(lg_ref, b_ref, tri_ref, idx_ref, w_ref, rank_ref, cnt_ref):
    s = _sigmoid(lg_ref[...])
    sel = s + b_ref[...]
    row = lambda a, e: a[e:e + 1, :]
    gsz = EXPERTS_PER_GROUP
    g_idx = None
    best = None
    for g in range(N_GROUPS):
        v = [row(sel, g * gsz + i) for i in range(gsz)]
        score = None
        for i in range(gsz):
            for j in range(i + 1, gsz):
                pair = v[i] + v[j]
                score = pair if score is None else jnp.maximum(score, pair)
        if g == 0:
            best, g_idx = score, jnp.zeros(score.shape, jnp.int32)
        else:
            better = score > best
            best = jnp.where(better, score, best)
            g_idx = jnp.where(better, g, g_idx)

    def in_group(a, i):
        out = row(a, i)
        for g in range(1, N_GROUPS):
            out = jnp.where(g_idx == g, row(a, g * gsz + i), out)
        return out

    v = [in_group(sel, i) for i in range(gsz)]
    sv = [in_group(s, i) for i in range(gsz)]

    def arg_first_max(vals):
        bv, bi = vals[0], jnp.zeros(vals[0].shape, jnp.int32)
        for i in range(1, gsz):
            better = vals[i] > bv
            bv = jnp.where(better, vals[i], bv)
            bi = jnp.where(better, i, bi)
        return bi

    i1 = arg_first_max(v)
    i2 = arg_first_max([jnp.where(i1 == i, -jnp.inf, v[i]) for i in range(gsz)])

    def pick(vals, idx):
        out = vals[0]
        for i in range(1, gsz):
            out = jnp.where(idx == i, vals[i], out)
        return out

    w1 = pick(sv, i1)
    w2 = pick(sv, i2)
    tot = w1 + w2
    e1 = g_idx * gsz + i1
    e2 = g_idx * gsz + i2
    idx_ref[0:1, :] = e1
    idx_ref[1:2, :] = e2
    w_ref[0:1, :] = w1 / tot
    w_ref[1:2, :] = w2 / tot

    @pl.when(pl.program_id(0) == 0)
    def _():
        cnt_ref[...] = jnp.zeros(cnt_ref.shape, cnt_ref.dtype)

    n_e, tn = s.shape
    erow = lax.broadcasted_iota(jnp.int32, (n_e, tn), 0)
    oh1 = jnp.where(erow == e1, 1.0, 0.0)
    oh2 = jnp.where(erow == e2, 1.0, 0.0)
    oh = (oh1 + oh2).astype(BF16)
    base = cnt_ref[...]
    r1, r2 = [], []
    for c in range(tn // 128):
        cs = slice(c * 128, (c + 1) * 128)
        before = base + _dot(oh[:, cs], tri_ref[...])
        r1.append(jnp.sum(oh1[:, cs] * before, axis=0, keepdims=True))
        r2.append(jnp.sum(oh2[:, cs] * before, axis=0, keepdims=True))
        base = base + jnp.sum(oh[:, cs].astype(F32), axis=1, keepdims=True)
    cnt_ref[...] = base
    rank_ref[0:1, :] = jnp.concatenate(r1, axis=-1).astype(jnp.int32)
    rank_ref[1:2, :] = jnp.concatenate(r2, axis=-1).astype(jnp.int32)


def _route(logits_t, router_bias):
    e, n = logits_t.shape
    tn = math.gcd(n, 2048)
    i = np.arange(128)
    tri = jnp.asarray((i[:, None] < i[None, :]).astype(np.float32), BF16)
    idx2, w2, rank, cnt = pl.pallas_call(
        _route_kernel,
        grid=(n // tn,),
        in_specs=[pl.BlockSpec((e, tn), lambda i: (0, i)),
                  pl.BlockSpec((e, 1), lambda i: (0, 0)),
                  pl.BlockSpec((128, 128), lambda i: (0, 0))],
        out_specs=[pl.BlockSpec((TOP_K, tn), lambda i: (0, i)),
                   pl.BlockSpec((TOP_K, tn), lambda i: (0, i)),
                   pl.BlockSpec((TOP_K, tn), lambda i: (0, i)),
                   pl.BlockSpec((e, 128), lambda i: (0, 0))],
        out_shape=[jax.ShapeDtypeStruct((TOP_K, n), jnp.int32),
                   jax.ShapeDtypeStruct((TOP_K, n), F32),
                   jax.ShapeDtypeStruct((TOP_K, n), jnp.int32),
                   jax.ShapeDtypeStruct((e, 128), F32)],
        compiler_params=_params("arbitrary"),
        name="route_top2",
    )(logits_t, router_bias.astype(F32).reshape(e, 1), tri)
    return idx2, w2, rank, cnt[:, 0].astype(jnp.int32)


def _dispatch_kernel(n_tok_steps, pos_ref, m_ref, xs_out, zrow, sem):
    i = pl.program_id(0)
    rt = ROW_TILE
    tm = m_ref.shape[0] // rt
    n = TOP_K * tm

    def wait():
        pltpu.make_async_copy(xs_out.at[pl.ds(0, n * rt)], xs_out.at[pl.ds(0, n * rt)], sem).wait()

    @pl.when(i < n_tok_steps)
    def _():
        for j in range(n):
            dst = pl.multiple_of(pos_ref[i, j], rt)
            pltpu.make_async_copy(m_ref.at[pl.ds((j % tm) * rt, rt)], xs_out.at[pl.ds(dst, rt)],
                                  sem).start(priority=j % 2)
        wait()

    @pl.when(i >= n_tok_steps)
    def _():
        zrow[...] = jnp.zeros(zrow.shape, zrow.dtype)
        for j in range(n):
            dst = pl.multiple_of(pos_ref[i, j], rt)
            pltpu.make_async_copy(zrow, xs_out.at[pl.ds(dst, rt)], sem).start(priority=j % 2)
        wait()


def _dispatch(m_tiles, pos, pad_slots, n_slots):
    rt = ROW_TILE
    n_tok_steps, n = pos.shape
    tm = n // TOP_K
    steps = jnp.concatenate([pos, pad_slots], axis=0)
    grid_spec = pltpu.PrefetchScalarGridSpec(
        num_scalar_prefetch=1,
        grid=(steps.shape[0],),
        in_specs=[pl.BlockSpec((tm * rt, 128), lambda i, pos: (jnp.minimum(i, n_tok_steps - 1), 0))],
        out_specs=pl.BlockSpec(memory_space=pl.ANY),
        scratch_shapes=[pltpu.VMEM((rt, 128), F32), pltpu.SemaphoreType.DMA(())])
    return pl.pallas_call(
        functools.partial(_dispatch_kernel, n_tok_steps),
        grid_spec=grid_spec,
        out_shape=jax.ShapeDtypeStruct((n_slots * rt, 128), F32),
        compiler_params=_params("arbitrary"),
        name="moe_dispatch",
    )(steps, m_tiles)


def _expert_kernel(be_ref, x_ref, wg_ref, wu_ref, wd_ref, y_ref):
    blk = x_ref.shape[0] // ROW_TILE
    x = _load_tile_rows(x_ref, 0, blk).astype(BF16)
    a = _dot(x, wg_ref[...])
    h = a * _sigmoid(a) * _dot(x, wu_ref[...])
    _store_tile_rows(y_ref, _dot(h.astype(BF16), wd_ref[...]))


def _experts(x_slots, blk_e, blk, wg_bf, wu_bf, wd_bf):
    rt = ROW_TILE
    d = D_MODEL
    n_slots = x_slots.shape[0] // rt
    grid_spec = pltpu.PrefetchScalarGridSpec(
        num_scalar_prefetch=1,
        grid=(n_slots // blk,),
        in_specs=[pl.BlockSpec((blk * rt, 128), lambda i, be: (i, 0)),
                  pl.BlockSpec((None, d, D_EXPERT), lambda i, be: (be[i], 0, 0)),
                  pl.BlockSpec((None, d, D_EXPERT), lambda i, be: (be[i], 0, 0)),
                  pl.BlockSpec((None, D_EXPERT, d), lambda i, be: (be[i], 0, 0))],
        out_specs=pl.BlockSpec((blk * rt, 128), lambda i, be: (i, 0)))
    return pl.pallas_call(
        _expert_kernel,
        grid_spec=grid_spec,
        out_shape=jax.ShapeDtypeStruct((n_slots * rt, 128), F32),
        compiler_params=_params("arbitrary"),
        name="expert_ffn",
    )(blk_e, x_slots, wg_bf, wu_bf, wd_bf)


def _combine_kernel(final, pos_ref, y_hbm, x_ref, w_ref, mod_ref, gf_ref, o_ref, ybuf, sem):
    i = pl.program_id(0)
    last = pl.num_programs(0) - 1
    slot = i % 2
    rt = ROW_TILE
    tm = x_ref.shape[0]
    n = TOP_K * tm

    def issue(step, dst_slot):
        for j in range(n):
            src = pl.multiple_of(pos_ref[step, j], rt)
            pltpu.make_async_copy(y_hbm.at[pl.ds(src, rt)], ybuf.at[dst_slot, pl.ds(j * rt, rt)],
                                  sem.at[dst_slot]).start(priority=j % 2)

    def wait(dst_slot):
        pltpu.make_async_copy(y_hbm.at[pl.ds(0, n * rt)], ybuf.at[dst_slot], sem.at[dst_slot]).wait()

    @pl.when(i == 0)
    def _():
        issue(0, 0)

    wait(slot)
    issue(jnp.minimum(i + 1, last), 1 - slot)
    w = w_ref[...]
    yb = ybuf.at[slot]
    f = w[:, 0:1] * _load_tile_rows(yb, 0, tm) + w[:, 1:2] * _load_tile_rows(yb, tm * rt, tm)
    x = x_ref[...] + mod_ref[5:6, :] * f
    if final:
        x = x * lax.rsqrt(jnp.mean(x * x, axis=-1, keepdims=True) + EPS) * gf_ref[...]
    o_ref[...] = x

    @pl.when(i == last)
    def _():
        wait(1 - slot)


def _combine(x_rows, y_slots, pos, w_tok, modtab, g_final, final, n_lat, seq, batch):
    n_rows, d = x_rows.shape
    n_steps, n = pos.shape
    tm = n // TOP_K
    midx = _mod_index(tm, n_lat, seq, batch)
    grid_spec = pltpu.PrefetchScalarGridSpec(
        num_scalar_prefetch=1,
        grid=(n_steps,),
        in_specs=[pl.BlockSpec(memory_space=pl.ANY),
                  pl.BlockSpec((tm, d), lambda i, pos: (i, 0)),
                  pl.BlockSpec((tm, TOP_K), lambda i, pos: (i, 0)),
                  pl.BlockSpec((None, 8, d), lambda i, pos: (midx(i), 0, 0)),
                  pl.BlockSpec((1, d), lambda i, pos: (0, 0))],
        out_specs=pl.BlockSpec((tm, d), lambda i, pos: (i, 0)),
        scratch_shapes=[pltpu.VMEM((2, n * ROW_TILE, 128), F32), pltpu.SemaphoreType.DMA((2,))])
    return pl.pallas_call(
        functools.partial(_combine_kernel, final),
        grid_spec=grid_spec,
        out_shape=jax.ShapeDtypeStruct((n_rows, d), F32),
        compiler_params=_params("arbitrary"),
        name="moe_combine",
    )(pos, y_slots, x_rows, w_tok, modtab, g_final.astype(F32).reshape(1, d))


def _combine_project_kernel(n_col_steps, pos_ref, y_hbm, x_ref, wt_ref, modp_ref, g_ref, mod_ref, w_ref, cs_ref,
                            p_ref, xo_ref, h_ref, ybuf, sem):
    i = pl.program_id(0)
    j = pl.program_id(1)
    last = pl.num_programs(0) - 1
    slot = i % 2
    rt = ROW_TILE
    tm = x_ref.shape[0]
    per_step = ybuf.shape[2] // rt
    per_k = n_col_steps // TOP_K

    def issue(block, step, dst_slot):
        for jj in range(per_step):
            src = pl.multiple_of(pos_ref[block * n_col_steps + step, jj], rt)
            pltpu.make_async_copy(y_hbm.at[pl.ds(src, rt)], ybuf.at[dst_slot, step, pl.ds(jj * rt, rt)],
                                  sem.at[dst_slot]).start(priority=jj % 2)

    def wait(dst_slot):
        for step in range(n_col_steps):
            pltpu.make_async_copy(y_hbm.at[pl.ds(0, per_step * rt)], ybuf.at[dst_slot, step],
                                  sem.at[dst_slot]).wait()

    @pl.when((i == 0) & (j == 0))
    def _():
        for step in range(n_col_steps):
            issue(0, step, 0)

    @pl.when(j == 0)
    def _():
        wait(slot)
        w = wt_ref[...]
        yk = [jnp.concatenate([_load_tile_rows(ybuf.at[slot, k * per_k + c], 0, per_step) for c in range(per_k)],
                              axis=0) for k in range(TOP_K)]
        f = w[:, 0:1] * yk[0] + w[:, 1:2] * yk[1]
        x = x_ref[...] + modp_ref[5:6, :] * f
        xo_ref[...] = x
        y = x * lax.rsqrt(jnp.mean(x * x, axis=-1, keepdims=True) + EPS) * g_ref[...]
        h_ref[...] = (y * (1.0 + mod_ref[1:2, :]) + mod_ref[0:1, :]).astype(BF16)

    issue(jnp.minimum(i + 1, last), j, 1 - slot)
    acc = _dot(h_ref[...], w_ref[...])
    gate = 0.5 * jnp.tanh(0.5 * acc) + 0.5
    cs = cs_ref[...]
    p_ref[...] = jnp.where(cs > 0.0, acc * cs, gate).astype(BF16)

    @pl.when((i == last) & (j == n_col_steps - 1))
    def _():
        wait(1 - slot)


def _combine_project(x_rows, y_slots, pos, w_tok, modtab_prev, g, modtab, w_bf, n_lat, seq, batch):
    n_rows, d = x_rows.shape
    n_steps, n = pos.shape
    tm = n // TOP_K
    ncol = w_bf.shape[1]
    tn = PROJ_TN
    n_col_steps = ncol // tn
    assert n_col_steps % TOP_K == 0 and n % n_col_steps == 0
    per_step = n // n_col_steps
    pos = pos.reshape(n_steps * n_col_steps, per_step)
    midx = _mod_index(tm, n_lat, seq, batch)
    grid_spec = pltpu.PrefetchScalarGridSpec(
        num_scalar_prefetch=1,
        grid=(n_steps, n_col_steps),
        in_specs=[pl.BlockSpec(memory_space=pl.ANY),
                  pl.BlockSpec((tm, d), lambda i, j, pos: (i, 0)),
                  pl.BlockSpec((tm, TOP_K), lambda i, j, pos: (i, 0)),
                  pl.BlockSpec((None, 8, d), lambda i, j, pos: (midx(i), 0, 0)),
                  pl.BlockSpec((1, d), lambda i, j, pos: (0, 0)),
                  pl.BlockSpec((None, 8, d), lambda i, j, pos: (midx(i), 0, 0)),
                  pl.BlockSpec((d, tn), lambda i, j, pos: (0, j)),
                  pl.BlockSpec((1, tn), lambda i, j, pos: (0, j))],
        out_specs=[pl.BlockSpec((tm, tn), lambda i, j, pos: (i, j)),
                   pl.BlockSpec((tm, d), lambda i, j, pos: (i, 0))],
        scratch_shapes=[pltpu.VMEM((tm, d), BF16), pltpu.VMEM((2, n_col_steps, per_step * ROW_TILE, 128), F32),
                        pltpu.SemaphoreType.DMA((2,))])
    p_all, x_new = pl.pallas_call(
        functools.partial(_combine_project_kernel, n_col_steps),
        grid_spec=grid_spec,
        out_shape=[jax.ShapeDtypeStruct((n_rows, ncol), BF16), jax.ShapeDtypeStruct((n_rows, d), F32)],
        compiler_params=_params("arbitrary", "arbitrary"),
        name="combine_norm_project",
    )(pos, y_slots, x_rows, w_tok, modtab_prev, g.reshape(1, d), modtab, w_bf, _mix_col_scale())
    return p_all, x_new


def _dispatch_plan(idx2, rank, counts, blk, tm):
    n = idx2.shape[1]
    a = TOP_K * n
    padded = (counts + blk - 1) // blk * blk
    pad_end = jnp.cumsum(padded)
    pad_start = pad_end - padded
    start_of = jnp.zeros(idx2.shape, jnp.int32)
    for e in range(N_EXPERTS):
        start_of = jnp.where(idx2 == e, pad_start[e], start_of)
    dest = (start_of + rank).astype(jnp.int32)
    n_blocks = a // blk + N_EXPERTS
    first_slot = jnp.arange(n_blocks, dtype=jnp.int32) * blk
    blk_e = jnp.minimum(jnp.sum(pad_end[None, :] <= first_slot[:, None], axis=1), N_EXPERTS - 1).astype(jnp.int32)
    pos = dest.reshape(TOP_K, n // tm, tm).transpose(1, 0, 2).reshape(n // tm, TOP_K * tm) * ROW_TILE
    n_slots = n_blocks * blk
    starts = jnp.concatenate([pad_start + counts, pad_end[-1:]])
    lens = jnp.concatenate([padded - counts, n_slots - pad_end[-1:]])
    ends = jnp.cumsum(lens)
    j = jnp.arange(N_EXPERTS * blk, dtype=jnp.int32)
    shift = starts - (ends - lens)
    shift_of = jnp.zeros(j.shape, jnp.int32)
    for s in range(N_EXPERTS + 1):
        shift_of = jnp.where((j >= ends[s] - lens[s]) & (j < ends[s]), shift[s], shift_of)
    pad_slots = ((shift_of + j) * ROW_TILE).astype(jnp.int32)
    return pos, pad_slots.reshape(-1, TOP_K * tm), blk_e, n_slots


def kernel(x, c, ctx, c_ctx, w_mod, b_mod, g_norm1, g_norm2, w_in, na_rpb, ret_log_decay, gqa_q_gain, gqa_k_gain,
           diff_lambda, diff_subln, w_branch, w_out, w_router, router_bias, w_gate_e, w_up_e, w_down_e, g_final):
    batch, seq, d = x.shape
    n_ctx = ctx.shape[1]
    n_lat = batch * seq
    n_ctx_tot = batch * n_ctx
    n_tot = n_lat + n_ctx_tot
    tm = 1024 if n_ctx_tot % 1024 == 0 else 512

    x_parts = (x.reshape(n_lat, d), ctx.reshape(n_ctx_tot, d))

    nrow = -(-(batch + 1) // 8) * 8
    cc = jnp.zeros((nrow, d), F32).at[:batch].set(c).at[batch].set(c_ctx)
    mod = _modulation(cc, w_mod, b_mod)
    modtab = jnp.pad(mod.reshape(DEPTH, nrow, 6, d), ((0, 0), (0, 0), (0, 2), (0, 0)))

    wr_t = w_router.astype(F32).T
    out = None
    pending = None
    for layer in range(DEPTH):
        need_ctx = layer < DEPTH - 1
        n_rows = n_tot if need_ctx else n_lat
        if pending is None:
            p_all = _project(x_parts, g_norm1[layer], modtab[layer], w_in[layer].astype(BF16), tm, n_lat, seq, batch)
        else:
            p_all, x_new = _combine_project(*pending, g_norm1[layer], modtab[layer], w_in[layer].astype(BF16),
                                            n_lat, seq, batch)
            x_parts = (x_new,)
        ya = _na_mixer(p_all, na_rpb[layer], batch, seq, n_ctx, need_ctx)
        yb = _ret_mixer(p_all, ret_log_decay[layer], batch, seq, n_ctx, need_ctx)
        yc = _gqa_mixer(p_all, gqa_q_gain[layer], gqa_k_gain[layer], batch, seq, n_ctx, need_ctx)
        yd = _diff_mixer(p_all, diff_lambda[layer], diff_subln[layer], layer, batch, seq, n_ctx, need_ctx)
        ys_lat = [ya[0], yb[0], yc[0], yd[0]]
        ys_ctx = [ya[1], yb[1], yc[1], yd[1]] if need_ctx else None
        x_mid, m_all, logits_t = _merge(x_parts, p_all, ys_lat, ys_ctx, w_branch[layer].astype(BF16),
                                        w_out[layer].astype(BF16), modtab[layer], g_norm2[layer], wr_t, MERGE_TM,
                                        n_rows, n_lat, seq, batch)
        idx2, w2, rank, counts = _route(logits_t, router_bias)
        pos, pad_slots, blk_e, n_slots = _dispatch_plan(idx2, rank, counts, MOE_BLK, CMB_TM)
        x_slots = _dispatch(m_all, pos, pad_slots, n_slots)
        y_slots = _experts(x_slots, blk_e, MOE_BLK, w_gate_e[layer].astype(BF16), w_up_e[layer].astype(BF16),
                           w_down_e[layer].astype(BF16))
        if need_ctx:
            pending = (x_mid, y_slots, pos, w2.T, modtab[layer])
        else:
            out = _combine(x_mid, y_slots, pos, w2.T, modtab[layer], g_final, True, n_lat, seq, batch)
    return out.reshape(batch, seq, d)
```

```python
import functools
import math

import numpy as np
import jax
import jax.numpy as jnp
from jax import lax
from jax.experimental import pallas as pl
from jax.experimental.pallas import tpu as pltpu

F32 = jnp.float32
BF16 = jnp.bfloat16

D_MODEL = 1024
DEPTH = 2
GRID_W = 64
NA_HEADS = 4
NA_DIM = 64
NA_WIN_H = 8
NA_WIN_W = 16
RET_HEADS = 4
RET_DK = 64
RET_CHUNK = 128
GQA_HEADS = 4
GQA_KV_HEADS = 2
GQA_DIM = 64
DIFF_HEADS = 4
DIFF_QK_DIM = 32
DIFF_V_DIM = 64
N_BRANCH = 4
BRANCH_W = 256
ROPE_THETA = 10000.0
EPS = 1e-6
NEG_INF = -1e30
N_EXPERTS = 16
N_GROUPS = 4
EXPERTS_PER_GROUP = 4
TOP_K = 2
D_EXPERT = 512

MIX_COLS = 3072
IN_COLS = MIX_COLS + N_BRANCH * D_MODEL
NA_Q, NA_K, NA_V = 0, 1, 2
RET_Q, RET_K, RET_V, RET_G = 3, 4, 5, 6
GQA_Q = 7
DIFF_Q, DIFF_K, DIFF_V = 9, 10, 11

LOG2E = math.log2(math.e)
NA_GROUP_ROWS = 4
NA_BAND_ROWS = 12

VMEM_LIMIT = 56 * 1024 * 1024
MOE_BLK = 512
CMB_TM = 512
PROJ_TN = 1792
MERGE_TM = 512


ROW_TILE = D_MODEL // 128


def _store_tile_rows(ref, x):
    n = x.shape[0]
    for j in range(ROW_TILE):
        ref[pl.ds(j, n, stride=ROW_TILE), :] = x[:, j * 128:(j + 1) * 128]


def _load_tile_rows(ref, start, n):
    return jnp.concatenate([ref[pl.ds(start + j, n, stride=ROW_TILE), :] for j in range(ROW_TILE)], axis=-1)


def _dot(a, b):
    return jnp.dot(a, b, preferred_element_type=F32)


def _dot_nt(a, b):
    return lax.dot_general(a, b, (((1,), (1,)), ((), ())), preferred_element_type=F32)


def _sigmoid(x):
    return 1.0 / (1.0 + jnp.exp(-x))


def _params(*sem):
    return pltpu.CompilerParams(dimension_semantics=sem, vmem_limit_bytes=VMEM_LIMIT)


def _mod_kernel(c_ref, w_ref, b_ref, o_ref):
    c = c_ref[...]
    a = (c * _sigmoid(c)).astype(BF16)
    o_ref[...] = _dot(a, w_ref[...].astype(BF16)) + b_ref[...]


def _modulation(cc, w_mod, b_mod):
    nrow = cc.shape[0]
    depth, d, n6 = w_mod.shape
    tn = 1536
    return pl.pallas_call(
        _mod_kernel,
        grid=(depth, n6 // tn),
        in_specs=[pl.BlockSpec((nrow, d), lambda l, j: (0, 0)),
                  pl.BlockSpec((None, d, tn), lambda l, j: (l, 0, j)),
                  pl.BlockSpec((None, 1, tn), lambda l, j: (l, 0, j))],
        out_specs=pl.BlockSpec((None, nrow, tn), lambda l, j: (l, 0, j)),
        out_shape=jax.ShapeDtypeStruct((depth, nrow, n6), F32),
        compiler_params=_params("parallel", "arbitrary"),
        name="modulation",
    )(cc, w_mod, b_mod.reshape(depth, 1, n6))


def _mix_col_scale():
    cs = np.ones((1, IN_COLS), np.float32)
    cs[0, NA_Q * 256:(NA_Q + 1) * 256] = NA_DIM ** -0.5 * LOG2E
    cs[0, DIFF_Q * 256:(DIFF_Q + 1) * 256] = DIFF_QK_DIM ** -0.5 * LOG2E
    cs[0, RET_K * 256:(RET_K + 1) * 256] = RET_DK ** -0.5
    cs[0, MIX_COLS:] = 0.0
    return jnp.asarray(cs)


def _row_blocks(parts, tm, n_lat):
    d = parts[0].shape[1]
    nlat_blk = n_lat // tm
    if len(parts) == 1:
        return [pl.BlockSpec((tm, d), lambda i, *_: (i, 0))]
    return [pl.BlockSpec((tm, d), lambda i, *_: (jnp.minimum(i, nlat_blk - 1), 0)),
            pl.BlockSpec((tm, d), lambda i, *_: (jnp.maximum(i - nlat_blk, 0), 0))]


def _pick_rows(x_refs, nlat_blk):
    if len(x_refs) == 1:
        return x_refs[0][...]
    return jnp.where(pl.program_id(0) >= nlat_blk, x_refs[1][...], x_refs[0][...])


def _proj_kernel(nlat_blk, n_parts, *refs):
    x_refs, (g_ref, mod_ref, w_ref, cs_ref, o_ref, h_ref) = refs[:n_parts], refs[n_parts:]
    j = pl.program_id(1)

    @pl.when(j == 0)
    def _():
        x = _pick_rows(x_refs, nlat_blk)
        y = x * lax.rsqrt(jnp.mean(x * x, axis=-1, keepdims=True) + EPS) * g_ref[...]
        h_ref[...] = (y * (1.0 + mod_ref[1:2, :]) + mod_ref[0:1, :]).astype(BF16)

    acc = _dot(h_ref[...], w_ref[...])
    gate = 0.5 * jnp.tanh(0.5 * acc) + 0.5
    cs = cs_ref[...]
    o_ref[...] = jnp.where(cs > 0.0, acc * cs, gate).astype(BF16)


def _mod_index(tm, n_lat, seq, batch):
    nlat_blk = n_lat // tm
    bpb = seq // tm

    def index(i):
        return jnp.where(i < nlat_blk, i // bpb, batch)
    return index


def _project(x_parts, g, modtab, w_bf, tm, n_lat, seq, batch):
    n_tot = sum(p.shape[0] for p in x_parts)
    d = x_parts[0].shape[1]
    ncol = w_bf.shape[1]
    tn = PROJ_TN
    midx = _mod_index(tm, n_lat, seq, batch)
    return pl.pallas_call(
        functools.partial(_proj_kernel, n_lat // tm, len(x_parts)),
        grid=(n_tot // tm, ncol // tn),
        in_specs=[*_row_blocks(x_parts, tm, n_lat),
                  pl.BlockSpec((1, d), lambda i, j: (0, 0)),
                  pl.BlockSpec((None, 8, d), lambda i, j: (midx(i), 0, 0)),
                  pl.BlockSpec((d, tn), lambda i, j: (0, j)),
                  pl.BlockSpec((1, tn), lambda i, j: (0, j))],
        out_specs=pl.BlockSpec((tm, tn), lambda i, j: (i, j)),
        out_shape=jax.ShapeDtypeStruct((n_tot, ncol), BF16),
        scratch_shapes=[pltpu.VMEM((tm, d), BF16)],
        compiler_params=_params("parallel", "arbitrary"),
        name="norm_project",
    )(*x_parts, g.reshape(1, d), modtab, w_bf, _mix_col_scale())


def _na_kernel(need_ctx, rows, q_ref, k_ref, v_ref, qc_ref, kc_ref, vc_ref, tb_ref, y_ref, *rest):
    if need_ctx:
        yc_ref, vx, vcx = rest
    else:
        vx, vcx = rest
    gq = NA_GROUP_ROWS * GRID_W
    band = NA_BAND_ROWS * GRID_W
    n_groups = rows // NA_GROUP_ROWS
    n_pairs = NA_HEADS // 2
    masks = _lane_masks(2)
    low = masks[0] > 0

    for p in range(n_pairs):
        ps = slice(p * 128, (p + 1) * 128)
        vx[p] = jnp.concatenate([v_ref[:, ps], jnp.ones((v_ref.shape[0], 128), BF16)], axis=-1)
        vcx[p] = jnp.concatenate([vc_ref[:, ps], jnp.ones((vc_ref.shape[0], 128), BF16)], axis=-1)

    def group_body(g, carry):
        u = jnp.clip(g * NA_GROUP_ROWS - NA_WIN_H // 2, 0, rows - NA_BAND_ROWS)
        typ = jnp.where(g == 0, 0, jnp.where(g == n_groups - 1, 2, 1))
        qs = pl.multiple_of(g * gq, gq)
        ks = pl.multiple_of(u * GRID_W, GRID_W)
        for p in range(n_pairs):
            ps = slice(p * 128, (p + 1) * 128)
            q = q_ref[pl.ds(qs, gq), ps]
            kb = k_ref[pl.ds(ks, band), ps]
            kc = kc_ref[:, ps]
            outs = []
            for hh in range(2):
                qm = q * masks[hh]
                s_w = _dot_nt(qm, kb) + tb_ref[2 * p + hh, typ]
                s_c = _dot_nt(qm, kc)
                m = jnp.maximum(jnp.max(s_w, axis=-1, keepdims=True), jnp.max(s_c, axis=-1, keepdims=True))
                oe = (_dot(jnp.exp2((s_w - m).astype(BF16)), vx[p, pl.ds(ks, band), :])
                      + _dot(jnp.exp2((s_c - m).astype(BF16)), vcx[p]))
                outs.append(oe[:, :128] / oe[:, 128:])
            y_ref[pl.ds(qs, gq), ps] = jnp.where(low, outs[0], outs[1]).astype(BF16)
        return carry

    lax.fori_loop(0, n_groups, group_body, 0)

    if need_ctx:
        for p in range(n_pairs):
            ps = slice(p * 128, (p + 1) * 128)
            outs = [_softmax_pv(qc_ref[:, ps] * masks[hh], kc_ref[:, ps], vcx[p]) for hh in range(2)]
            yc_ref[:, ps] = jnp.where(low, outs[0], outs[1]).astype(BF16)


def _na_bias_table(rpb, rows):
    assert rows % NA_GROUP_ROWS == 0 and rows >= NA_BAND_ROWS + NA_GROUP_ROWS
    n_groups = rows // NA_GROUP_ROWS
    qc = np.arange(GRID_W)[:, None]
    kc = np.arange(GRID_W)[None, :]
    win_start = np.clip(qc - NA_WIN_W // 2, 0, GRID_W - NA_WIN_W)
    col_ok = (kc >= win_start) & (kc < win_start + NA_WIN_W)
    col_idx = np.clip(kc - qc, -(NA_WIN_W - 1), NA_WIN_W - 1) + NA_WIN_W - 1
    n_col = 2 * NA_WIN_W - 1
    onehot = jnp.asarray((np.arange(n_col)[:, None] == col_idx.reshape(1, -1)).astype(np.float32))
    picked = jnp.dot(rpb.astype(F32).reshape(-1, n_col), onehot, precision=lax.Precision.HIGHEST)
    picked = picked.reshape(NA_HEADS, 2 * NA_WIN_H - 1, GRID_W, GRID_W)
    tiles = jnp.where(col_ok[None, None], picked * LOG2E, NEG_INF)
    masked = jnp.full((NA_HEADS, GRID_W, GRID_W), NEG_INF, F32)
    tables = []
    for g in (0, 1, n_groups - 1):
        u = int(np.clip(g * NA_GROUP_ROWS - NA_WIN_H // 2, 0, rows - NA_BAND_ROWS))
        per_row = []
        for a in range(NA_GROUP_ROWS):
            r = g * NA_GROUP_ROWS + a
            r0 = int(np.clip(r - NA_WIN_H // 2, 0, rows - NA_WIN_H))
            pieces = [tiles[:, kr - r + NA_WIN_H - 1] if r0 <= kr < r0 + NA_WIN_H else masked
                      for kr in range(u, u + NA_BAND_ROWS)]
            per_row.append(jnp.concatenate(pieces, axis=-1))
        tables.append(jnp.concatenate(per_row, axis=1))
    return jnp.stack(tables, axis=1)


def _na_mixer(p_all, rpb, batch, seq, n_ctx, need_ctx):
    n_lat = batch * seq
    rows = seq // GRID_W
    tb = _na_bias_table(rpb, rows)
    cb = n_lat // n_ctx
    w = 256
    lat = lambda c: pl.BlockSpec((seq, w), lambda b, c=c: (b, c))
    ctx = lambda c: pl.BlockSpec((n_ctx, w), lambda b, c=c: (cb + b, c))
    out_shape = [jax.ShapeDtypeStruct((n_lat, w), BF16)]
    out_specs = [pl.BlockSpec((seq, w), lambda b: (b, 0))]
    if need_ctx:
        out_shape.append(jax.ShapeDtypeStruct((batch * n_ctx, w), BF16))
        out_specs.append(pl.BlockSpec((n_ctx, w), lambda b: (b, 0)))
    return pl.pallas_call(
        functools.partial(_na_kernel, need_ctx, rows),
        grid=(batch,),
        in_specs=[lat(NA_Q), lat(NA_K), lat(NA_V), ctx(NA_Q), ctx(NA_K), ctx(NA_V),
                  pl.BlockSpec(tb.shape, lambda b: (0, 0, 0, 0))],
        out_specs=out_specs,
        out_shape=out_shape,
        scratch_shapes=[pltpu.VMEM((NA_HEADS // 2, seq, 256), BF16), pltpu.VMEM((NA_HEADS // 2, n_ctx, 256), BF16)],
        compiler_params=_params("parallel"),
        name="na_mixer",
    )(p_all, p_all, p_all, p_all, p_all, p_all, tb)


def _ret_kernel(need_ctx, seq, n_ctx, lg_ref, q_ref, k_ref, v_ref, g_ref, qc_ref, kc_ref, vc_ref, gc_ref,
                cos_ref, sin_ref, perm_ref, bd_ref, y_ref, *rest):
    if need_ctx:
        yc_ref, qr, kr, krt, kct, o_f, o_b, oc_f, oc_b, dm, qd, kdt = rest
    else:
        qr, kr, krt, kct, o_f, o_b, oc_f, oc_b, dm, qd, kdt = rest
    ch = RET_CHUNK
    n_lat_ch = seq // ch
    n_ctx_ch = n_ctx // ch
    n_pairs = RET_HEADS // 2
    assert ch == 128 and 2 * RET_DK == 128

    def prep(i, carry):
        rs = pl.multiple_of(i * ch, ch)
        c = cos_ref[pl.ds(rs, ch), :]
        s = sin_ref[pl.ds(rs, ch), :]
        qr[pl.ds(rs, ch), :] = _rope_mxu(q_ref[pl.ds(rs, ch), :], c, s, perm_ref)
        k = _rope_mxu(k_ref[pl.ds(rs, ch), :], c, s, perm_ref)
        kr[pl.ds(rs, ch), :] = k
        for p in range(n_pairs):
            krt[i, p * 128:(p + 1) * 128, :] = k[:, p * 128:(p + 1) * 128].T
        return carry

    lax.fori_loop(0, n_lat_ch, prep, 0, unroll=2)
    for n in range(n_ctx_ch):
        kc = kc_ref[n * ch:(n + 1) * ch, :].astype(F32)
        for p in range(n_pairs):
            kct[n, p * 128:(p + 1) * 128, :] = kc[:, p * 128:(p + 1) * 128].T

    ri = lax.broadcasted_iota(jnp.int32, (ch, ch), 0)
    ci = lax.broadcasted_iota(jnp.int32, (ch, ch), 1)
    lag = (ri - ci).astype(F32)
    rowf = ri.astype(F32)
    colf = ci.astype(F32)
    low_lane = ci < RET_DK
    low_row = ri < RET_DK
    block_diag = jnp.where(low_lane == low_row, 1.0, 0.0)
    low, high = _lane_masks(2)

    chains = [(p, dirn) for p in range(n_pairs) for dirn in range(2)]
    cdec = []
    for c, (p, dirn) in enumerate(chains):
        lg_a = lg_ref[dirn, 2 * p]
        lg_b = lg_ref[dirn, 2 * p + 1]
        lg_lane = jnp.where(low_lane, lg_a, lg_b)
        lg_row = jnp.where(low_row, lg_a, lg_b)
        for hh, lg in enumerate((lg_a, lg_b)):
            if dirn == 0:
                keep = ri >= ci
                dm[2 * c + hh] = jnp.where(keep, jnp.exp(jnp.where(keep, lag, 0.0) * lg), 0.0)
            else:
                keep = ci >= ri
                dm[2 * c + hh] = jnp.where(keep, jnp.exp(jnp.where(keep, -lag, 0.0) * lg), 0.0)
        if dirn == 0:
            qd[c] = jnp.exp((rowf + 1.0) * lg_lane)
            kdt[c] = jnp.exp((ch - 1.0 - colf) * lg_row)
        else:
            qd[c] = jnp.exp((ch - rowf) * lg_lane)
            kdt[c] = jnp.exp(colf * lg_row)
        cdec.append(jnp.exp(ch * lg_lane[0:1, :]))

    def step(c, state, qi, ki, kti, vi):
        kb = ki.astype(BF16)
        inner_a = (_dot_nt((qi * low).astype(BF16), kb) * dm[2 * c]).astype(BF16)
        inner_b = (_dot_nt((qi * high).astype(BF16), kb) * dm[2 * c + 1]).astype(BF16)
        o = (jnp.where(low_lane, _dot(inner_a, vi), _dot(inner_b, vi))
             + _dot((qi * qd[c]).astype(BF16), state.astype(BF16)))
        state = state * cdec[c] + block_diag * _dot((kti * kdt[c]).astype(BF16), vi)
        return state, o

    states = []
    for c, (p, dirn) in enumerate(chains):
        ps = slice(p * 128, (p + 1) * 128)
        state = jnp.zeros((128, 128), F32)
        for n in (range(n_ctx_ch) if dirn == 0 else range(n_ctx_ch - 1, -1, -1)):
            rs = slice(n * ch, (n + 1) * ch)
            state, o = step(c, state, qc_ref[rs, ps].astype(F32), kc_ref[rs, ps].astype(F32), kct[n, ps, :],
                            vc_ref[rs, ps])
            if need_ctx:
                (oc_f if dirn == 0 else oc_b)[rs, ps] = o
        states.append(state)

    def lat_body(i, states):
        new = []
        for c, (p, dirn) in enumerate(chains):
            ps = slice(p * 128, (p + 1) * 128)
            n = i if dirn == 0 else n_lat_ch - 1 - i
            rs = pl.ds(pl.multiple_of(n * ch, ch), ch)
            state, o = step(c, states[c], qr[rs, ps], kr[rs, ps], krt[n, ps, :], v_ref[rs, ps])
            (o_f if dirn == 0 else o_b)[rs, ps] = o
            new.append(state)
        return tuple(new)

    lax.fori_loop(0, n_lat_ch, lat_body, tuple(states), unroll=2)

    def head_mean(x):
        hi = x.astype(BF16)
        lo = (x - hi.astype(F32)).astype(BF16)
        bd = bd_ref[...]
        return (_dot(hi, bd) + _dot(lo, bd)) * (1.0 / RET_DK)

    def finish(o, g):
        cen = o - head_mean(o)
        on = cen * lax.rsqrt(head_mean(cen * cen) + EPS)
        g = g.astype(F32)
        return (g * _sigmoid(g) * on).astype(BF16)

    def fin_body(i, carry):
        rs = pl.multiple_of(i * ch, ch)
        y_ref[pl.ds(rs, ch), :] = finish(o_f[pl.ds(rs, ch), :] + o_b[pl.ds(rs, ch), :], g_ref[pl.ds(rs, ch), :])
        return carry

    lax.fori_loop(0, n_lat_ch, fin_body, 0, unroll=2)
    if need_ctx:
        for n in range(n_ctx_ch):
            rs = slice(n * ch, (n + 1) * ch)
            yc_ref[rs, :] = finish(oc_f[rs, :] + oc_b[rs, :], gc_ref[rs, :])


def _rope_tables(seq, head_dim, width, dist):
    half = head_dim // 2
    nf = half // 2
    assert nf == dist
    inv = 1.0 / (ROPE_THETA ** (np.arange(nf, dtype=np.float32) / nf))
    t = np.arange(seq)
    rows = (t // GRID_W).astype(np.float32)[:, None] * inv[None, :]
    cols = (t % GRID_W).astype(np.float32)[:, None] * inv[None, :]
    cos = np.concatenate([np.cos(rows), np.cos(rows), np.cos(cols), np.cos(cols)], axis=-1)
    sin = np.concatenate([-np.sin(rows), np.sin(rows), -np.sin(cols), np.sin(cols)], axis=-1)
    reps = width // head_dim
    return (jnp.asarray(np.tile(cos, (1, reps)), F32), jnp.asarray(np.tile(sin, (1, reps)), F32))


def _ret_mixer(p_all, log_decay, batch, seq, n_ctx, need_ctx):
    n_lat = batch * seq
    cb = n_lat // n_ctx
    w = 256
    ch = RET_CHUNK
    log_gamma = jnp.log1p(-jnp.exp(log_decay.astype(F32)))
    cos, sin = _rope_tables(seq, RET_DK, w, 16)
    lat = lambda c: pl.BlockSpec((seq, w), lambda b, c=c: (b, c))
    ctx = lambda c: pl.BlockSpec((n_ctx, w), lambda b, c=c: (cb + b, c))
    whole = pl.BlockSpec((seq, w), lambda b: (0, 0))
    out_shape = [jax.ShapeDtypeStruct((n_lat, w), BF16)]
    out_specs = [pl.BlockSpec((seq, w), lambda b: (b, 0))]
    if need_ctx:
        out_shape.append(jax.ShapeDtypeStruct((batch * n_ctx, w), BF16))
        out_specs.append(pl.BlockSpec((n_ctx, w), lambda b: (b, 0)))
    return pl.pallas_call(
        functools.partial(_ret_kernel, need_ctx, seq, n_ctx),
        grid=(batch,),
        in_specs=[pl.BlockSpec(memory_space=pltpu.SMEM),
                  lat(RET_Q), lat(RET_K), lat(RET_V), lat(RET_G),
                  ctx(RET_Q), ctx(RET_K), ctx(RET_V), ctx(RET_G), whole, whole,
                  pl.BlockSpec((w, w), lambda b: (0, 0)), pl.BlockSpec((w, w), lambda b: (0, 0))],
        out_specs=out_specs,
        out_shape=out_shape,
        scratch_shapes=[pltpu.VMEM((seq, w), F32), pltpu.VMEM((seq, w), F32),
                        pltpu.VMEM((seq // ch, w, ch), F32), pltpu.VMEM((n_ctx // ch, w, ch), F32),
                        pltpu.VMEM((seq, w), F32), pltpu.VMEM((seq, w), F32),
                        pltpu.VMEM((n_ctx, w), F32), pltpu.VMEM((n_ctx, w), F32),
                        pltpu.VMEM((2 * RET_HEADS, ch, ch), F32),
                        pltpu.VMEM((RET_HEADS, ch, 128), F32), pltpu.VMEM((RET_HEADS, 128, ch), F32)],
        compiler_params=_params("parallel"),
        name="retention_mixer",
    )(log_gamma, p_all, p_all, p_all, p_all, p_all, p_all, p_all, p_all, cos, sin,
      _swap_matrix(w, 16), _block_ones(w, RET_DK))


def _swap_matrix(width, dist):
    i = np.arange(width)
    partner = np.where(i % (2 * dist) < dist, i + dist, i - dist)
    p = np.zeros((width, width), np.float32)
    p[partner, i] = 1.0
    return jnp.asarray(p, BF16)


def _block_ones(width, block):
    i = np.arange(width)
    return jnp.asarray((i[:, None] // block == i[None, :] // block).astype(np.float32), BF16)


def _sumsq_blocks(xf, bd_ref):
    sq = xf * xf
    hi = sq.astype(BF16)
    lo = (sq - hi.astype(F32)).astype(BF16)
    bd = bd_ref[...]
    return _dot(hi, bd) + _dot(lo, bd)


def _rope_mxu(x, cos, sin, perm_ref):
    return x.astype(F32) * cos + _dot(x, perm_ref[...]) * sin


def _softmax_pv(qm, k, v_ones):
    s = _dot_nt(qm, k)
    p = jnp.exp2((s - jnp.max(s, axis=-1, keepdims=True)).astype(BF16))
    oe = _dot(p, v_ones)
    return oe[:, :128] / oe[:, 128:]


def _lane_masks(n_parts):
    lane = lax.broadcasted_iota(jnp.int32, (1, 128), 1)
    return [jnp.where(lane // (128 // n_parts) == i, 1.0, 0.0).astype(BF16) for i in range(n_parts)]


def _gqa_kernel(n_lat_k, n_ctx, *refs):
    if n_lat_k:
        (q_ref, k_ref, v_ref, kc_ref, vc_ref, qa_ref, qb_ref, ka_ref, kb_ref, kg_ref, pq_ref, pk_ref,
         bdq_ref, bdk_ref, y_ref, kp, vx) = refs
    else:
        q_ref, kc_ref, vc_ref, qa_ref, kg_ref, bdq_ref, bdk_ref, y_ref, kp, vx = refs
    dim = GQA_DIM
    nk = n_lat_k + n_ctx

    def inv_rms(xf, bd_ref):
        return lax.rsqrt(_sumsq_blocks(xf, bd_ref) * (1.0 / dim) + EPS)

    @pl.when(pl.program_id(1) == 0)
    def _():
        def put(rs, kn, v):
            ones = jnp.ones(v.shape, BF16)
            kp[0, rs, :] = kn.astype(BF16)
            kp[1, rs, :] = pltpu.roll(kn, dim, 1).astype(BF16)
            vx[0, rs, :] = jnp.concatenate([v, ones], axis=-1)
            vx[1, rs, :] = jnp.concatenate([pltpu.roll(v.astype(F32), dim, 1).astype(BF16), ones], axis=-1)

        kc = kc_ref[...]
        kcf = kc.astype(F32)
        put(slice(n_lat_k, nk), kcf * kg_ref[...] * inv_rms(kcf, bdk_ref), vc_ref[...])
        ck = 512
        for c in range(n_lat_k // ck):
            rs = slice(c * ck, (c + 1) * ck)
            k = k_ref[rs, :]
            put(rs, _rope_mxu(k, ka_ref[rs, :], kb_ref[rs, :], pk_ref) * inv_rms(k.astype(F32), bdk_ref), v_ref[rs, :])

    q = q_ref[...]
    qf = q.astype(F32)
    qn = _rope_mxu(q, qa_ref[...], qb_ref[...], pq_ref) if n_lat_k else qf * qa_ref[...]
    qn = (qn * inv_rms(qf, bdq_ref)).astype(BF16)
    low, high = _lane_masks(2)
    outs = []
    for g in range(GQA_KV_HEADS):
        qv = qn[:, g * 128:(g + 1) * 128]
        o_low = _softmax_pv(qv * low, kp[g], vx[g])
        o_high = _softmax_pv(qv * high, kp[1 - g], vx[1 - g])
        outs.append(jnp.where(low > 0, o_low, o_high))
    y_ref[...] = jnp.concatenate(outs, axis=-1).astype(BF16)


def _gqa_mixer(p_all, q_gain, k_gain, batch, seq, n_ctx, need_ctx):
    n_lat = batch * seq
    cb = n_lat // n_ctx
    dim = GQA_DIM
    qw = GQA_HEADS * dim
    kvw = GQA_KV_HEADS * dim
    assert qw == 256 and kvw == 128

    def swapped(g):
        return g.reshape(-1, 2, 16)[:, ::-1].reshape(1, -1)

    qg = jnp.tile(q_gain.astype(F32) * (dim ** -0.5 * LOG2E), GQA_HEADS).reshape(1, qw)
    kg = jnp.tile(k_gain.astype(F32), GQA_KV_HEADS).reshape(1, kvw)
    cq, sq = _rope_tables(seq, dim, qw, 16)
    ck, sk = _rope_tables(seq, dim, kvw, 16)
    qa, qb = cq * qg, sq * swapped(qg)
    ka, kb = ck * kg, sk * swapped(kg)
    pq, pk = _swap_matrix(qw, 16), _swap_matrix(kvw, 16)
    bdq, bdk = _block_ones(qw, dim), _block_ones(kvw, dim)
    tq = 512
    nqb = seq // tq
    const = lambda shape: pl.BlockSpec(shape, lambda b, i: (0,) * len(shape))
    scratch = lambda nk: [pltpu.VMEM((2, nk, kvw), BF16), pltpu.VMEM((2, nk, 2 * kvw), BF16)]
    y_lat = pl.pallas_call(
        functools.partial(_gqa_kernel, seq, n_ctx),
        grid=(batch, nqb),
        in_specs=[pl.BlockSpec((tq, qw), lambda b, i: (b * nqb + i, GQA_Q)),
                  pl.BlockSpec((seq, kvw), lambda b, i: (b, 16)),
                  pl.BlockSpec((seq, kvw), lambda b, i: (b, 17)),
                  pl.BlockSpec((n_ctx, kvw), lambda b, i: (cb + b, 16)),
                  pl.BlockSpec((n_ctx, kvw), lambda b, i: (cb + b, 17)),
                  pl.BlockSpec((tq, qw), lambda b, i: (i, 0)),
                  pl.BlockSpec((tq, qw), lambda b, i: (i, 0)),
                  const((seq, kvw)), const((seq, kvw)), const((1, kvw)),
                  const((qw, qw)), const((kvw, kvw)), const((qw, qw)), const((kvw, kvw))],
        out_specs=pl.BlockSpec((tq, qw), lambda b, i: (b * nqb + i, 0)),
        out_shape=jax.ShapeDtypeStruct((n_lat, qw), BF16),
        scratch_shapes=scratch(seq + n_ctx),
        compiler_params=_params("parallel", "arbitrary"),
        name="gqa_mixer",
    )(p_all, p_all, p_all, p_all, p_all, qa, qb, ka, kb, kg, pq, pk, bdq, bdk)
    if not need_ctx:
        return y_lat, None
    y_ctx = pl.pallas_call(
        functools.partial(_gqa_kernel, 0, n_ctx),
        grid=(batch, 1),
        in_specs=[pl.BlockSpec((n_ctx, qw), lambda b, i: (cb + b, GQA_Q)),
                  pl.BlockSpec((n_ctx, kvw), lambda b, i: (cb + b, 16)),
                  pl.BlockSpec((n_ctx, kvw), lambda b, i: (cb + b, 17)),
                  const((1, qw)), const((1, kvw)), const((qw, qw)), const((kvw, kvw))],
        out_specs=pl.BlockSpec((n_ctx, qw), lambda b, i: (b, 0)),
        out_shape=jax.ShapeDtypeStruct((batch * n_ctx, qw), BF16),
        scratch_shapes=scratch(n_ctx),
        compiler_params=_params("parallel", "arbitrary"),
        name="gqa_mixer_ctx",
    )(p_all, p_all, p_all, qg, kg, bdq, bdk)
    return y_lat, y_ctx


def _diff_kernel(n_lat_k, n_ctx, lam_init, *refs):
    if n_lat_k:
        (q_ref, k_ref, v_ref, kc_ref, vc_ref, lp_ref, sg_ref, cq_ref, sq_ref, ck_ref, sk_ref, perm_ref, bd_ref,
         y_ref, kp, vx) = refs
    else:
        q_ref, kc_ref, vc_ref, lp_ref, sg_ref, bd_ref, y_ref, kp, vx = refs
    nk = n_lat_k + n_ctx
    n_pairs = DIFF_HEADS // 2

    @pl.when(pl.program_id(1) == 0)
    def _():
        def put(rs, k, v):
            kp[rs, :] = k
            ones = jnp.ones((v.shape[0], 128), BF16)
            for pr in range(n_pairs):
                vx[pr, rs, :] = jnp.concatenate([v[:, pr * 128:(pr + 1) * 128], ones], axis=-1)

        put(slice(n_lat_k, nk), kc_ref[...], vc_ref[...])
        ck = 512
        for c in range(n_lat_k // ck):
            rs = slice(c * ck, (c + 1) * ck)
            put(rs, _rope_mxu(k_ref[rs, :], ck_ref[rs, :], sk_ref[rs, :], perm_ref).astype(BF16), v_ref[rs, :])

    lp = lp_ref[...]
    lam = (jnp.exp(jnp.sum(lp[0:1, :] * lp[1:2, :], axis=-1, keepdims=True))
           - jnp.exp(jnp.sum(lp[2:3, :] * lp[3:4, :], axis=-1, keepdims=True)) + lam_init)

    q = q_ref[...]
    if n_lat_k:
        q = _rope_mxu(q, cq_ref[...], sq_ref[...], perm_ref).astype(BF16)
    quarter = _lane_masks(4)
    low = _lane_masks(2)[0]
    outs = []
    for pr in range(n_pairs):
        ps = slice(pr * 128, (pr + 1) * 128)
        qv = q[:, ps]
        kv = kp[:, ps]
        o_head = [_softmax_pv(qv * quarter[2 * hh], kv, vx[pr]) - lam * _softmax_pv(qv * quarter[2 * hh + 1], kv, vx[pr])
                  for hh in range(2)]
        outs.append(jnp.where(low > 0, o_head[0], o_head[1]))
    o = jnp.concatenate(outs, axis=-1)
    inv = lax.rsqrt(_sumsq_blocks(o, bd_ref) * (1.0 / DIFF_V_DIM) + EPS)
    y_ref[...] = (o * inv * sg_ref[...] * (1.0 - lam_init)).astype(BF16)


def _diff_mixer(p_all, lam_params, subln, layer_idx, batch, seq, n_ctx, need_ctx):
    n_lat = batch * seq
    cb = n_lat // n_ctx
    lam_init = 0.8 - 0.6 * math.exp(-0.3 * layer_idx)
    lp = jnp.zeros((8, 128), F32).at[:4, :DIFF_QK_DIM].set(lam_params.astype(F32))
    w = DIFF_HEADS * DIFF_V_DIM
    assert w == 256 and DIFF_HEADS * 2 * DIFF_QK_DIM == w
    sg = jnp.tile(subln.astype(F32), DIFF_HEADS).reshape(1, w)
    cq, sq = _rope_tables(seq, DIFF_QK_DIM, w, 8)
    perm = _swap_matrix(w, 8)
    bd = _block_ones(w, DIFF_V_DIM)
    tq = 512
    nqb = seq // tq
    const = lambda shape: pl.BlockSpec(shape, lambda b, i: (0,) * len(shape))
    scratch = lambda nk: [pltpu.VMEM((nk, w), BF16), pltpu.VMEM((DIFF_HEADS // 2, nk, 256), BF16)]
    y_lat = pl.pallas_call(
        functools.partial(_diff_kernel, seq, n_ctx, lam_init),
        grid=(batch, nqb),
        in_specs=[pl.BlockSpec((tq, w), lambda b, i: (b * nqb + i, DIFF_Q)),
                  pl.BlockSpec((seq, w), lambda b, i: (b, DIFF_K)),
                  pl.BlockSpec((seq, w), lambda b, i: (b, DIFF_V)),
                  pl.BlockSpec((n_ctx, w), lambda b, i: (cb + b, DIFF_K)),
                  pl.BlockSpec((n_ctx, w), lambda b, i: (cb + b, DIFF_V)),
                  const((8, 128)), const((1, w)),
                  pl.BlockSpec((tq, w), lambda b, i: (i, 0)),
                  pl.BlockSpec((tq, w), lambda b, i: (i, 0)),
                  const((seq, w)), const((seq, w)), const((w, w)), const((w, w))],
        out_specs=pl.BlockSpec((tq, w), lambda b, i: (b * nqb + i, 0)),
        out_shape=jax.ShapeDtypeStruct((n_lat, w), BF16),
        scratch_shapes=scratch(seq + n_ctx),
        compiler_params=_params("parallel", "arbitrary"),
        name="diff_mixer",
    )(p_all, p_all, p_all, p_all, p_all, lp, sg, cq, sq, cq, sq, perm, bd)
    if not need_ctx:
        return y_lat, None
    y_ctx = pl.pallas_call(
        functools.partial(_diff_kernel, 0, n_ctx, lam_init),
        grid=(batch, 1),
        in_specs=[pl.BlockSpec((n_ctx, w), lambda b, i: (cb + b, DIFF_Q)),
                  pl.BlockSpec((n_ctx, w), lambda b, i: (cb + b, DIFF_K)),
                  pl.BlockSpec((n_ctx, w), lambda b, i: (cb + b, DIFF_V)),
                  const((8, 128)), const((1, w)), const((w, w))],
        out_specs=pl.BlockSpec((n_ctx, w), lambda b, i: (b, 0)),
        out_shape=jax.ShapeDtypeStruct((batch * n_ctx, w), BF16),
        scratch_shapes=scratch(n_ctx),
        compiler_params=_params("parallel", "arbitrary"),
        name="diff_mixer_ctx",
    )(p_all, p_all, p_all, lp, sg, bd)
    return y_lat, y_ctx


def _merge_kernel(nlat_blk, has_ctx, n_parts, *refs):
    x_refs, refs = refs[:n_parts], refs[n_parts:]
    g_refs = refs[:N_BRANCH]
    y_refs = refs[N_BRANCH:2 * N_BRANCH]
    refs = refs[2 * N_BRANCH:]
    if has_ctx:
        yc_refs, refs = refs[:N_BRANCH], refs[N_BRANCH:]
        is_ctx = pl.program_id(0) >= nlat_blk
    wb_ref, wo_ref, mod_ref, gn_ref, wr_ref, xo_ref, m_ref, lg_ref = refs
    acc = None
    for n in range(N_BRANCH):
        y = y_refs[n][...]
        if has_ctx:
            y = jnp.where(is_ctx, yc_refs[n][...], y)
        term = g_refs[n][...].astype(F32) * _dot(y, wb_ref[n])
        acc = term if acc is None else acc + term
    y = _dot(acc.astype(BF16), wo_ref[...])
    x = _pick_rows(x_refs, nlat_blk) + mod_ref[2:3, :] * y
    xo_ref[...] = x
    xn = x * lax.rsqrt(jnp.mean(x * x, axis=-1, keepdims=True) + EPS) * gn_ref[...]
    m = xn * (1.0 + mod_ref[4:5, :]) + mod_ref[3:4, :]
    _store_tile_rows(m_ref, m)
    m_hi = m.astype(BF16)
    m_lo = (m - m_hi.astype(F32)).astype(BF16)
    w = wr_ref[...]
    w_hi = w.astype(BF16)
    w_lo = (w - w_hi.astype(F32)).astype(BF16)
    both = _dot_nt(jnp.concatenate([w_hi, w_lo], axis=0), m_hi)
    lg_ref[...] = both[:N_EXPERTS] + (_dot_nt(w_hi, m_lo) + both[N_EXPERTS:])


def _merge(x_parts, p_all, ys_lat, ys_ctx, wb_bf, wo_bf, modtab, gn, wr_t, tm, n_rows, n_lat, seq, batch):
    d = D_MODEL
    midx = _mod_index(tm, n_lat, seq, batch)
    nlat_blk = n_lat // tm
    has_ctx = ys_ctx is not None
    gate = lambda n: pl.BlockSpec((tm, d), lambda i, n=n: (i, MIX_COLS // d + n))
    y_specs = [pl.BlockSpec((tm, BRANCH_W), lambda i: (jnp.minimum(i, nlat_blk - 1), 0))] * N_BRANCH
    ys = list(ys_lat)
    if has_ctx:
        y_specs += [pl.BlockSpec((tm, BRANCH_W), lambda i: (jnp.maximum(i - nlat_blk, 0), 0))] * N_BRANCH
        ys += list(ys_ctx)
    return pl.pallas_call(
        functools.partial(_merge_kernel, nlat_blk, has_ctx, len(x_parts)),
        grid=(n_rows // tm,),
        in_specs=[*_row_blocks(x_parts, tm, n_lat),
                  gate(0), gate(1), gate(2), gate(3), *y_specs,
                  pl.BlockSpec((N_BRANCH, BRANCH_W, d), lambda i: (0, 0, 0)),
                  pl.BlockSpec((d, d), lambda i: (0, 0)),
                  pl.BlockSpec((None, 8, d), lambda i: (midx(i), 0, 0)),
                  pl.BlockSpec((1, d), lambda i: (0, 0)),
                  pl.BlockSpec((N_EXPERTS, d), lambda i: (0, 0))],
        out_specs=[pl.BlockSpec((tm, d), lambda i: (i, 0)),
                   pl.BlockSpec((tm * ROW_TILE, 128), lambda i: (i, 0)),
                   pl.BlockSpec((N_EXPERTS, tm), lambda i: (0, i))],
        out_shape=[jax.ShapeDtypeStruct((n_rows, d), F32),
                   jax.ShapeDtypeStruct((n_rows * ROW_TILE, 128), F32),
                   jax.ShapeDtypeStruct((N_EXPERTS, n_rows), F32)],
        compiler_params=_params("parallel"),
        name="merge_norm_route",
    )(*x_parts, p_all, p_all, p_all, p_all, *ys, wb_bf, wo_bf, modtab, gn.reshape(1, d), wr_t)


def _route_kernel(lg_ref, b_ref, tri_ref, idx_ref, w_ref, rank_ref, cnt_ref):
    s = _sigmoid(lg_ref[...])
    sel = s + b_ref[...]
    row = lambda a, e: a[e:e + 1, :]
    gsz = EXPERTS_PER_GROUP
    g_idx = None
    best = None
    for g in range(N_GROUPS):
        v = [row(sel, g * gsz + i) for i in range(gsz)]
        score = None
        for i in range(gsz):
            for j in range(i + 1, gsz):
                pair = v[i] + v[j]
                score = pair if score is None else jnp.maximum(score, pair)
        if g == 0:
            best, g_idx = score, jnp.zeros(score.shape, jnp.int32)
        else:
            better = score > best
            best = jnp.where(better, score, best)
            g_idx = jnp.where(better, g, g_idx)

    def in_group(a, i):
        out = row(a, i)
        for g in range(1, N_GROUPS):
            out = jnp.where(g_idx == g, row(a, g * gsz + i), out)
        return out

    v = [in_group(sel, i) for i in range(gsz)]
    sv = [in_group(s, i) for i in range(gsz)]

    def arg_first_max(vals):
        bv, bi = vals[0], jnp.zeros(vals[0].shape, jnp.int32)
        for i in range(1, gsz):
            better = vals[i] > bv
            bv = jnp.where(better, vals[i], bv)
            bi = jnp.where(better, i, bi)
        return bi

    i1 = arg_first_max(v)
    i2 = arg_first_max([jnp.where(i1 == i, -jnp.inf, v[i]) for i in range(gsz)])

    def pick(vals, idx):
        out = vals[0]
        for i in range(1, gsz):
            out = jnp.where(idx == i, vals[i], out)
        return out

    w1 = pick(sv, i1)
    w2 = pick(sv, i2)
    tot = w1 + w2
    e1 = g_idx * gsz + i1
    e2 = g_idx * gsz + i2
    idx_ref[0:1, :] = e1
    idx_ref[1:2, :] = e2
    w_ref[0:1, :] = w1 / tot
    w_ref[1:2, :] = w2 / tot

    @pl.when(pl.program_id(0) == 0)
    def _():
        cnt_ref[...] = jnp.zeros(cnt_ref.shape, cnt_ref.dtype)

    n_e, tn = s.shape
    erow = lax.broadcasted_iota(jnp.int32, (n_e, tn), 0)
    oh1 = jnp.where(erow == e1, 1.0, 0.0)
    oh2 = jnp.where(erow == e2, 1.0, 0.0)
    oh = (oh1 + oh2).astype(BF16)
    base = cnt_ref[...]
    r1, r2 = [], []
    for c in range(tn // 128):
        cs = slice(c * 128, (c + 1) * 128)
        before = base + _dot(oh[:, cs], tri_ref[...])
        r1.append(jnp.sum(oh1[:, cs] * before, axis=0, keepdims=True))
        r2.append(jnp.sum(oh2[:, cs] * before, axis=0, keepdims=True))
        base = base + jnp.sum(oh[:, cs].astype(F32), axis=1, keepdims=True)
    cnt_ref[...] = base
    rank_ref[0:1, :] = jnp.concatenate(r1, axis=-1).astype(jnp.int32)
    rank_ref[1:2, :] = jnp.concatenate(r2, axis=-1).astype(jnp.int32)


def _route(logits_t, router_bias):
    e, n = logits_t.shape
    tn = math.gcd(n, 2048)
    i = np.arange(128)
    tri = jnp.asarray((i[:, None] < i[None, :]).astype(np.float32), BF16)
    idx2, w2, rank, cnt = pl.pallas_call(
        _route_kernel,
        grid=(n // tn,),
        in_specs=[pl.BlockSpec((e, tn), lambda i: (0, i)),
                  pl.BlockSpec((e, 1), lambda i: (0, 0)),
                  pl.BlockSpec((128, 128), lambda i: (0, 0))],
        out_specs=[pl.BlockSpec((TOP_K, tn), lambda i: (0, i)),
                   pl.BlockSpec((TOP_K, tn), lambda i: (0, i)),
                   pl.BlockSpec((TOP_K, tn), lambda i: (0, i)),
                   pl.BlockSpec((e, 128), lambda i: (0, 0))],
        out_shape=[jax.ShapeDtypeStruct((TOP_K, n), jnp.int32),
                   jax.ShapeDtypeStruct((TOP_K, n), F32),
                   jax.ShapeDtypeStruct((TOP_K, n), jnp.int32),
                   jax.ShapeDtypeStruct((e, 128), F32)],
        compiler_params=_params("arbitrary"),
        name="route_top2",
    )(logits_t, router_bias.astype(F32).reshape(e, 1), tri)
    return idx2, w2, rank, cnt[:, 0].astype(jnp.int32)


def _dispatch_kernel(n_tok_steps, pos_ref, m_ref, xs_out, zrow, sem):
    i = pl.program_id(0)
    rt = ROW_TILE
    tm = m_ref.shape[0] // rt
    n = TOP_K * tm

    def wait():
        pltpu.make_async_copy(xs_out.at[pl.ds(0, n * rt)], xs_out.at[pl.ds(0, n * rt)], sem).wait()

    @pl.when(i < n_tok_steps)
    def _():
        for j in range(n):
            dst = pl.multiple_of(pos_ref[i, j], rt)
            pltpu.make_async_copy(m_ref.at[pl.ds((j % tm) * rt, rt)], xs_out.at[pl.ds(dst, rt)],
                                  sem).start(priority=j % 2)
        wait()

    @pl.when(i >= n_tok_steps)
    def _():
        zrow[...] = jnp.zeros(zrow.shape, zrow.dtype)
        for j in range(n):
            dst = pl.multiple_of(pos_ref[i, j], rt)
            pltpu.make_async_copy(zrow, xs_out.at[pl.ds(dst, rt)], sem).start(priority=j % 2)
        wait()


def _dispatch(m_tiles, pos, pad_slots, n_slots):
    rt = ROW_TILE
    n_tok_steps, n = pos.shape
    tm = n // TOP_K
    steps = jnp.concatenate([pos, pad_slots], axis=0)
    grid_spec = pltpu.PrefetchScalarGridSpec(
        num_scalar_prefetch=1,
        grid=(steps.shape[0],),
        in_specs=[pl.BlockSpec((tm * rt, 128), lambda i, pos: (jnp.minimum(i, n_tok_steps - 1), 0))],
        out_specs=pl.BlockSpec(memory_space=pl.ANY),
        scratch_shapes=[pltpu.VMEM((rt, 128), F32), pltpu.SemaphoreType.DMA(())])
    return pl.pallas_call(
        functools.partial(_dispatch_kernel, n_tok_steps),
        grid_spec=grid_spec,
        out_shape=jax.ShapeDtypeStruct((n_slots * rt, 128), F32),
        compiler_params=_params("arbitrary"),
        name="moe_dispatch",
    )(steps, m_tiles)


def _expert_kernel(be_ref, x_ref, wg_ref, wu_ref, wd_ref, y_ref):
    blk = x_ref.shape[0] // ROW_TILE
    x = _load_tile_rows(x_ref, 0, blk).astype(BF16)
    a = _dot(x, wg_ref[...].astype(BF16))
    h = a * _sigmoid(a) * _dot(x, wu_ref[...].astype(BF16))
    _store_tile_rows(y_ref, _dot(h.astype(BF16), wd_ref[...].astype(BF16)))


def _experts(x_slots, blk_e, blk, w_gate, w_up, w_down):
    rt = ROW_TILE
    d = D_MODEL
    n_slots = x_slots.shape[0] // rt
    grid_spec = pltpu.PrefetchScalarGridSpec(
        num_scalar_prefetch=1,
        grid=(n_slots // blk,),
        in_specs=[pl.BlockSpec((blk * rt, 128), lambda i, be: (i, 0)),
                  pl.BlockSpec((None, d, D_EXPERT), lambda i, be: (be[i], 0, 0)),
                  pl.BlockSpec((None, d, D_EXPERT), lambda i, be: (be[i], 0, 0)),
                  pl.BlockSpec((None, D_EXPERT, d), lambda i, be: (be[i], 0, 0))],
        out_specs=pl.BlockSpec((blk * rt, 128), lambda i, be: (i, 0)))
    return pl.pallas_call(
        _expert_kernel,
        grid_spec=grid_spec,
        out_shape=jax.ShapeDtypeStruct((n_slots * rt, 128), F32),
        compiler_params=_params("arbitrary"),
        name="expert_ffn",
    )(blk_e, x_slots, w_gate, w_up, w_down)


def _combine_kernel(pos_ref, y_hbm, x_ref, w_ref, mod_ref, gf_ref, o_ref, ybuf, sem):
    i = pl.program_id(0)
    last = pl.num_programs(0) - 1
    slot = i % 2
    rt = ROW_TILE
    tm = x_ref.shape[0]
    n = TOP_K * tm

    def issue(step, dst_slot):
        for j in range(n):
            src = pl.multiple_of(pos_ref[step, j], rt)
            pltpu.make_async_copy(y_hbm.at[pl.ds(src, rt)], ybuf.at[dst_slot, pl.ds(j * rt, rt)],
                                  sem.at[dst_slot]).start(priority=j % 2)

    def wait(dst_slot):
        pltpu.make_async_copy(y_hbm.at[pl.ds(0, n * rt)], ybuf.at[dst_slot], sem.at[dst_slot]).wait()

    @pl.when(i == 0)
    def _():
        issue(0, 0)

    wait(slot)
    issue(jnp.minimum(i + 1, last), 1 - slot)
    w = w_ref[...]
    yb = ybuf.at[slot]
    f = w[:, 0:1] * _load_tile_rows(yb, 0, tm) + w[:, 1:2] * _load_tile_rows(yb, tm * rt, tm)
    x = x_ref[...] + mod_ref[5:6, :] * f
    o_ref[...] = x * lax.rsqrt(jnp.mean(x * x, axis=-1, keepdims=True) + EPS) * gf_ref[...]

    @pl.when(i == last)
    def _():
        wait(1 - slot)


def _combine(x_rows, y_slots, pos, w_tok, modtab, g_final, n_lat, seq, batch):
    n_rows, d = x_rows.shape
    n_steps, n = pos.shape
    tm = n // TOP_K
    midx = _mod_index(tm, n_lat, seq, batch)
    grid_spec = pltpu.PrefetchScalarGridSpec(
        num_scalar_prefetch=1,
        grid=(n_steps,),
        in_specs=[pl.BlockSpec(memory_space=pl.ANY),
                  pl.BlockSpec((tm, d), lambda i, pos: (i, 0)),
                  pl.BlockSpec((tm, TOP_K), lambda i, pos: (i, 0)),
                  pl.BlockSpec((None, 8, d), lambda i, pos: (midx(i), 0, 0)),
                  pl.BlockSpec((1, d), lambda i, pos: (0, 0))],
        out_specs=pl.BlockSpec((tm, d), lambda i, pos: (i, 0)),
        scratch_shapes=[pltpu.VMEM((2, n * ROW_TILE, 128), F32), pltpu.SemaphoreType.DMA((2,))])
    return pl.pallas_call(
        _combine_kernel,
        grid_spec=grid_spec,
        out_shape=jax.ShapeDtypeStruct((n_rows, d), F32),
        compiler_params=_params("arbitrary"),
        name="moe_combine",
    )(pos, y_slots, x_rows, w_tok, modtab, g_final.astype(F32).reshape(1, d))


def _combine_project_kernel(n_col_steps, pos_ref, y_hbm, x_ref, wt_ref, modp_ref, g_ref, mod_ref, w_ref, cs_ref,
                            p_ref, xo_ref, h_ref, ybuf, sem):
    i = pl.program_id(0)
    j = pl.program_id(1)
    last = pl.num_programs(0) - 1
    slot = i % 2
    rt = ROW_TILE
    tm = x_ref.shape[0]
    per_step = ybuf.shape[2] // rt
    per_k = n_col_steps // TOP_K

    def issue(block, step, dst_slot):
        for jj in range(per_step):
            src = pl.multiple_of(pos_ref[block * n_col_steps + step, jj], rt)
            pltpu.make_async_copy(y_hbm.at[pl.ds(src, rt)], ybuf.at[dst_slot, step, pl.ds(jj * rt, rt)],
                                  sem.at[dst_slot]).start(priority=jj % 2)

    def wait(dst_slot):
        for step in range(n_col_steps):
            pltpu.make_async_copy(y_hbm.at[pl.ds(0, per_step * rt)], ybuf.at[dst_slot, step],
                                  sem.at[dst_slot]).wait()

    @pl.when((i == 0) & (j == 0))
    def _():
        for step in range(n_col_steps):
            issue(0, step, 0)

    @pl.when(j == 0)
    def _():
        wait(slot)
        w = wt_ref[...]
        yk = [jnp.concatenate([_load_tile_rows(ybuf.at[slot, k * per_k + c], 0, per_step) for c in range(per_k)],
                              axis=0) for k in range(TOP_K)]
        f = w[:, 0:1] * yk[0] + w[:, 1:2] * yk[1]
        x = x_ref[...] + modp_ref[5:6, :] * f
        xo_ref[...] = x
        y = x * lax.rsqrt(jnp.mean(x * x, axis=-1, keepdims=True) + EPS) * g_ref[...]
        h_ref[...] = (y * (1.0 + mod_ref[1:2, :]) + mod_ref[0:1, :]).astype(BF16)

    issue(jnp.minimum(i + 1, last), j, 1 - slot)
    acc = _dot(h_ref[...], w_ref[...])
    gate = 0.5 * jnp.tanh(0.5 * acc) + 0.5
    cs = cs_ref[...]
    p_ref[...] = jnp.where(cs > 0.0, acc * cs, gate).astype(BF16)

    @pl.when((i == last) & (j == n_col_steps - 1))
    def _():
        wait(1 - slot)


def _combine_project(x_rows, y_slots, pos, w_tok, modtab_prev, g, modtab, w_bf, n_lat, seq, batch):
    n_rows, d = x_rows.shape
    n_steps, n = pos.shape
    tm = n // TOP_K
    ncol = w_bf.shape[1]
    tn = PROJ_TN
    n_col_steps = ncol // tn
    assert n_col_steps % TOP_K == 0 and n % n_col_steps == 0
    per_step = n // n_col_steps
    pos = pos.reshape(n_steps * n_col_steps, per_step)
    midx = _mod_index(tm, n_lat, seq, batch)
    grid_spec = pltpu.PrefetchScalarGridSpec(
        num_scalar_prefetch=1,
        grid=(n_steps, n_col_steps),
        in_specs=[pl.BlockSpec(memory_space=pl.ANY),
                  pl.BlockSpec((tm, d), lambda i, j, pos: (i, 0)),
                  pl.BlockSpec((tm, TOP_K), lambda i, j, pos: (i, 0)),
                  pl.BlockSpec((None, 8, d), lambda i, j, pos: (midx(i), 0, 0)),
                  pl.BlockSpec((1, d), lambda i, j, pos: (0, 0)),
                  pl.BlockSpec((None, 8, d), lambda i, j, pos: (midx(i), 0, 0)),
                  pl.BlockSpec((d, tn), lambda i, j, pos: (0, j)),
                  pl.BlockSpec((1, tn), lambda i, j, pos: (0, j))],
        out_specs=[pl.BlockSpec((tm, tn), lambda i, j, pos: (i, j)),
                   pl.BlockSpec((tm, d), lambda i, j, pos: (i, 0))],
        scratch_shapes=[pltpu.VMEM((tm, d), BF16), pltpu.VMEM((2, n_col_steps, per_step * ROW_TILE, 128), F32),
                        pltpu.SemaphoreType.DMA((2,))])
    p_all, x_new = pl.pallas_call(
        functools.partial(_combine_project_kernel, n_col_steps),
        grid_spec=grid_spec,
        out_shape=[jax.ShapeDtypeStruct((n_rows, ncol), BF16), jax.ShapeDtypeStruct((n_rows, d), F32)],
        compiler_params=_params("arbitrary", "arbitrary"),
        name="combine_norm_project",
    )(pos, y_slots, x_rows, w_tok, modtab_prev, g.reshape(1, d), modtab, w_bf, _mix_col_scale())
    return p_all, x_new


def _dispatch_plan(idx2, rank, counts, blk, tm):
    n = idx2.shape[1]
    a = TOP_K * n
    padded = (counts + blk - 1) // blk * blk
    pad_end = jnp.cumsum(padded)
    pad_start = pad_end - padded
    start_of = jnp.zeros(idx2.shape, jnp.int32)
    for e in range(N_EXPERTS):
        start_of = jnp.where(idx2 == e, pad_start[e], start_of)
    dest = (start_of + rank).astype(jnp.int32)
    n_blocks = a // blk + N_EXPERTS
    first_slot = jnp.arange(n_blocks, dtype=jnp.int32) * blk
    blk_e = jnp.minimum(jnp.sum(pad_end[None, :] <= first_slot[:, None], axis=1), N_EXPERTS - 1).astype(jnp.int32)
    pos = dest.reshape(TOP_K, n // tm, tm).transpose(1, 0, 2).reshape(n // tm, TOP_K * tm) * ROW_TILE
    n_slots = n_blocks * blk
    starts = jnp.concatenate([pad_start + counts, pad_end[-1:]])
    lens = jnp.concatenate([padded - counts, n_slots - pad_end[-1:]])
    ends = jnp.cumsum(lens)
    j = jnp.arange(N_EXPERTS * blk, dtype=jnp.int32)
    shift = starts - (ends - lens)
    shift_of = jnp.zeros(j.shape, jnp.int32)
    for s in range(N_EXPERTS + 1):
        shift_of = jnp.where((j >= ends[s] - lens[s]) & (j < ends[s]), shift[s], shift_of)
    pad_slots = ((shift_of + j) * ROW_TILE).astype(jnp.int32)
    return pos, pad_slots.reshape(-1, TOP_K * tm), blk_e, n_slots


def kernel(x, c, ctx, c_ctx, w_mod, b_mod, g_norm1, g_norm2, w_in, na_rpb, ret_log_decay, gqa_q_gain, gqa_k_gain,
           diff_lambda, diff_subln, w_branch, w_out, w_router, router_bias, w_gate_e, w_up_e, w_down_e, g_final):
    batch, seq, d = x.shape
    n_ctx = ctx.shape[1]
    n_lat = batch * seq
    n_ctx_tot = batch * n_ctx
    n_tot = n_lat + n_ctx_tot
    tm = 1024 if n_ctx_tot % 1024 == 0 else 512

    x_parts = (x.reshape(n_lat, d), ctx.reshape(n_ctx_tot, d))

    nrow = -(-(batch + 1) // 8) * 8
    cc = jnp.zeros((nrow, d), F32).at[:batch].set(c).at[batch].set(c_ctx)
    mod = _modulation(cc, w_mod, b_mod)
    modtab = jnp.pad(mod.reshape(DEPTH, nrow, 6, d), ((0, 0), (0, 0), (0, 2), (0, 0)))

    wr_t = w_router.astype(F32).T
    out = None
    pending = None
    for layer in range(DEPTH):
        need_ctx = layer < DEPTH - 1
        n_rows = n_tot if need_ctx else n_lat
        if pending is None:
            p_all = _project(x_parts, g_norm1[layer], modtab[layer], w_in[layer].astype(BF16), tm, n_lat, seq, batch)
        else:
            p_all, x_new = _combine_project(*pending, g_norm1[layer], modtab[layer], w_in[layer].astype(BF16),
                                            n_lat, seq, batch)
            x_parts = (x_new,)
        ya = _na_mixer(p_all, na_rpb[layer], batch, seq, n_ctx, need_ctx)
        yb = _ret_mixer(p_all, ret_log_decay[layer], batch, seq, n_ctx, need_ctx)
        yc = _gqa_mixer(p_all, gqa_q_gain[layer], gqa_k_gain[layer], batch, seq, n_ctx, need_ctx)
        yd = _diff_mixer(p_all, diff_lambda[layer], diff_subln[layer], layer, batch, seq, n_ctx, need_ctx)
        ys_lat = [ya[0], yb[0], yc[0], yd[0]]
        ys_ctx = [ya[1], yb[1], yc[1], yd[1]] if need_ctx else None
        x_mid, m_all, logits_t = _merge(x_parts, p_all, ys_lat, ys_ctx, w_branch[layer].astype(BF16),
                                        w_out[layer].astype(BF16), modtab[layer], g_norm2[layer], wr_t, MERGE_TM,
                                        n_rows, n_lat, seq, batch)
        idx2, w2, rank, counts = _route(logits_t, router_bias)
        pos, pad_slots, blk_e, n_slots = _dispatch_plan(idx2, rank, counts, MOE_BLK, CMB_TM)
        x_slots = _dispatch(m_all, pos, pad_slots, n_slots)
        y_slots = _experts(x_slots, blk_e, MOE_BLK, w_gate_e[layer], w_up_e[layer], w_down_e[layer])
        if need_ctx:
            pending = (x_mid, y_slots, pos, w2.T, modtab[layer])
        else:
            out = _combine(x_mid, y_slots, pos, w2.T, modtab[layer], g_final, n_lat, seq, batch)
    return out.reshape(batch, seq, d)
```

```python
import functools
import math

import numpy as np
import jax
import jax.numpy as jnp
from jax import lax
from jax.experimental import pallas as pl
from jax.experimental.pallas import tpu as pltpu

F32 = jnp.float32
BF16 = jnp.bfloat16

D_MODEL = 1024
DEPTH = 2
GRID_W = 64
NA_HEADS = 4
NA_DIM = 64
NA_WIN_H = 8
NA_WIN_W = 16
RET_HEADS = 4
RET_DK = 64
RET_CHUNK = 128
GQA_HEADS = 4
GQA_KV_HEADS = 2
GQA_DIM = 64
DIFF_HEADS = 4
DIFF_QK_DIM = 32
DIFF_V_DIM = 64
N_BRANCH = 4
BRANCH_W = 256
ROPE_THETA = 10000.0
EPS = 1e-6
NEG_INF = -1e30
N_EXPERTS = 16
N_GROUPS = 4
EXPERTS_PER_GROUP = 4
TOP_K = 2
D_EXPERT = 512

MIX_COLS = 3072
IN_COLS = MIX_COLS + N_BRANCH * D_MODEL
NA_Q, NA_K, NA_V = 0, 1, 2
RET_Q, RET_K, RET_V, RET_G = 3, 4, 5, 6
GQA_Q = 7
DIFF_Q, DIFF_K, DIFF_V = 9, 10, 11

LOG2E = math.log2(math.e)
NA_GROUP_ROWS = 4
NA_BAND_ROWS = 12

VMEM_LIMIT = 56 * 1024 * 1024
MOE_BLK = 512
CMB_TM = 512
PROJ_TN = 1792
MERGE_TM = 512


ROW_TILE = D_MODEL // 128


def _store_tile_rows(ref, x):
    n = x.shape[0]
    for j in range(ROW_TILE):
        ref[pl.ds(j, n, stride=ROW_TILE), :] = x[:, j * 128:(j + 1) * 128]


def _load_tile_rows(ref, start, n):
    return jnp.concatenate([ref[pl.ds(start + j, n, stride=ROW_TILE), :] for j in range(ROW_TILE)], axis=-1)


def _dot(a, b):
    return jnp.dot(a, b, preferred_element_type=F32)


def _dot_nt(a, b):
    return lax.dot_general(a, b, (((1,), (1,)), ((), ())), preferred_element_type=F32)


def _sigmoid(x):
    return 1.0 / (1.0 + jnp.exp(-x))


def _params(*sem):
    return pltpu.CompilerParams(dimension_semantics=sem, vmem_limit_bytes=VMEM_LIMIT)


def _mod_kernel(c_ref, w_ref, b_ref, o_ref):
    c = c_ref[...]
    a = (c * _sigmoid(c)).astype(BF16)
    o_ref[...] = _dot(a, w_ref[...].astype(BF16)) + b_ref[...]


def _modulation(cc, w_mod, b_mod):
    nrow = cc.shape[0]
    depth, d, n6 = w_mod.shape
    tn = 1536
    return pl.pallas_call(
        _mod_kernel,
        grid=(depth, n6 // tn),
        in_specs=[pl.BlockSpec((nrow, d), lambda l, j: (0, 0)),
                  pl.BlockSpec((None, d, tn), lambda l, j: (l, 0, j)),
                  pl.BlockSpec((None, 1, tn), lambda l, j: (l, 0, j))],
        out_specs=pl.BlockSpec((None, nrow, tn), lambda l, j: (l, 0, j)),
        out_shape=jax.ShapeDtypeStruct((depth, nrow, n6), F32),
        compiler_params=_params("parallel", "arbitrary"),
        name="modulation",
    )(cc, w_mod, b_mod.reshape(depth, 1, n6))


def _mix_col_scale():
    cs = np.ones((1, IN_COLS), np.float32)
    cs[0, NA_Q * 256:(NA_Q + 1) * 256] = NA_DIM ** -0.5 * LOG2E
    cs[0, DIFF_Q * 256:(DIFF_Q + 1) * 256] = DIFF_QK_DIM ** -0.5 * LOG2E
    cs[0, RET_K * 256:(RET_K + 1) * 256] = RET_DK ** -0.5
    cs[0, MIX_COLS:] = 0.0
    return jnp.asarray(cs)


def _row_blocks(parts, tm, n_lat):
    d = parts[0].shape[1]
    nlat_blk = n_lat // tm
    if len(parts) == 1:
        return [pl.BlockSpec((tm, d), lambda i, *_: (i, 0))]
    return [pl.BlockSpec((tm, d), lambda i, *_: (jnp.minimum(i, nlat_blk - 1), 0)),
            pl.BlockSpec((tm, d), lambda i, *_: (jnp.maximum(i - nlat_blk, 0), 0))]


def _pick_rows(x_refs, nlat_blk):
    if len(x_refs) == 1:
        return x_refs[0][...]
    return jnp.where(pl.program_id(0) >= nlat_blk, x_refs[1][...], x_refs[0][...])


def _proj_kernel(nlat_blk, n_parts, *refs):
    x_refs, (g_ref, mod_ref, w_ref, cs_ref, o_ref, h_ref) = refs[:n_parts], refs[n_parts:]
    j = pl.program_id(1)

    @pl.when(j == 0)
    def _():
        x = _pick_rows(x_refs, nlat_blk)
        y = x * lax.rsqrt(jnp.mean(x * x, axis=-1, keepdims=True) + EPS) * g_ref[...]
        h_ref[...] = (y * (1.0 + mod_ref[1:2, :]) + mod_ref[0:1, :]).astype(BF16)

    acc = _dot(h_ref[...], w_ref[...])
    gate = 0.5 * jnp.tanh(0.5 * acc) + 0.5
    cs = cs_ref[...]
    o_ref[...] = jnp.where(cs > 0.0, acc * cs, gate).astype(BF16)


def _mod_index(tm, n_lat, seq, batch):
    nlat_blk = n_lat // tm
    bpb = seq // tm

    def index(i):
        return jnp.where(i < nlat_blk, i // bpb, batch)
    return index


def _project(x_parts, g, modtab, w_bf, tm, n_lat, seq, batch):
    n_tot = sum(p.shape[0] for p in x_parts)
    d = x_parts[0].shape[1]
    ncol = w_bf.shape[1]
    tn = PROJ_TN
    midx = _mod_index(tm, n_lat, seq, batch)
    return pl.pallas_call(
        functools.partial(_proj_kernel, n_lat // tm, len(x_parts)),
        grid=(n_tot // tm, ncol // tn),
        in_specs=[*_row_blocks(x_parts, tm, n_lat),
                  pl.BlockSpec((1, d), lambda i, j: (0, 0)),
                  pl.BlockSpec((None, 8, d), lambda i, j: (midx(i), 0, 0)),
                  pl.BlockSpec((d, tn), lambda i, j: (0, j)),
                  pl.BlockSpec((1, tn), lambda i, j: (0, j))],
        out_specs=pl.BlockSpec((tm, tn), lambda i, j: (i, j)),
        out_shape=jax.ShapeDtypeStruct((n_tot, ncol), BF16),
        scratch_shapes=[pltpu.VMEM((tm, d), BF16)],
        compiler_params=_params("parallel", "arbitrary"),
        name="norm_project",
    )(*x_parts, g.reshape(1, d), modtab, w_bf, _mix_col_scale())


def _na_kernel(need_ctx, rows, q_ref, k_ref, v_ref, qc_ref, kc_ref, vc_ref, tb_ref, y_ref, *rest):
    if need_ctx:
        yc_ref, vx, vcx = rest
    else:
        vx, vcx = rest
    gq = NA_GROUP_ROWS * GRID_W
    band = NA_BAND_ROWS * GRID_W
    n_groups = rows // NA_GROUP_ROWS
    n_pairs = NA_HEADS // 2
    masks = _lane_masks(2)
    low = masks[0] > 0

    for p in range(n_pairs):
        ps = slice(p * 128, (p + 1) * 128)
        vx[p] = jnp.concatenate([v_ref[:, ps], jnp.ones((v_ref.shape[0], 128), BF16)], axis=-1)
        vcx[p] = jnp.concatenate([vc_ref[:, ps], jnp.ones((vc_ref.shape[0], 128), BF16)], axis=-1)

    def group_body(g, carry):
        u = jnp.clip(g * NA_GROUP_ROWS - NA_WIN_H // 2, 0, rows - NA_BAND_ROWS)
        typ = jnp.where(g == 0, 0, jnp.where(g == n_groups - 1, 2, 1))
        qs = pl.multiple_of(g * gq, gq)
        ks = pl.multiple_of(u * GRID_W, GRID_W)
        for p in range(n_pairs):
            ps = slice(p * 128, (p + 1) * 128)
            q = q_ref[pl.ds(qs, gq), ps]
            kb = k_ref[pl.ds(ks, band), ps]
            kc = kc_ref[:, ps]
            outs = []
            for hh in range(2):
                qm = q * masks[hh]
                s_w = _dot_nt(qm, kb) + tb_ref[2 * p + hh, typ]
                s_c = _dot_nt(qm, kc)
                m = jnp.maximum(jnp.max(s_w, axis=-1, keepdims=True), jnp.max(s_c, axis=-1, keepdims=True))
                oe = (_dot(jnp.exp2((s_w - m).astype(BF16)), vx[p, pl.ds(ks, band), :])
                      + _dot(jnp.exp2((s_c - m).astype(BF16)), vcx[p]))
                outs.append(oe[:, :128] / oe[:, 128:])
            y_ref[pl.ds(qs, gq), ps] = jnp.where(low, outs[0], outs[1]).astype(BF16)
        return carry

    lax.fori_loop(0, n_groups, group_body, 0)

    if need_ctx:
        for p in range(n_pairs):
            ps = slice(p * 128, (p + 1) * 128)
            outs = [_softmax_pv(qc_ref[:, ps] * masks[hh], kc_ref[:, ps], vcx[p]) for hh in range(2)]
            yc_ref[:, ps] = jnp.where(low, outs[0], outs[1]).astype(BF16)


def _na_bias_table(rpb, rows):
    assert rows % NA_GROUP_ROWS == 0 and rows >= NA_BAND_ROWS + NA_GROUP_ROWS
    n_groups = rows // NA_GROUP_ROWS
    qc = np.arange(GRID_W)[:, None]
    kc = np.arange(GRID_W)[None, :]
    win_start = np.clip(qc - NA_WIN_W // 2, 0, GRID_W - NA_WIN_W)
    col_ok = (kc >= win_start) & (kc < win_start + NA_WIN_W)
    col_idx = np.clip(kc - qc, -(NA_WIN_W - 1), NA_WIN_W - 1) + NA_WIN_W - 1
    n_col = 2 * NA_WIN_W - 1
    onehot = jnp.asarray((np.arange(n_col)[:, None] == col_idx.reshape(1, -1)).astype(np.float32))
    picked = jnp.dot(rpb.astype(F32).reshape(-1, n_col), onehot, precision=lax.Precision.HIGHEST)
    picked = picked.reshape(NA_HEADS, 2 * NA_WIN_H - 1, GRID_W, GRID_W)
    tiles = jnp.where(col_ok[None, None], picked * LOG2E, NEG_INF)
    masked = jnp.full((NA_HEADS, GRID_W, GRID_W), NEG_INF, F32)
    tables = []
    for g in (0, 1, n_groups - 1):
        u = int(np.clip(g * NA_GROUP_ROWS - NA_WIN_H // 2, 0, rows - NA_BAND_ROWS))
        per_row = []
        for a in range(NA_GROUP_ROWS):
            r = g * NA_GROUP_ROWS + a
            r0 = int(np.clip(r - NA_WIN_H // 2, 0, rows - NA_WIN_H))
            pieces = [tiles[:, kr - r + NA_WIN_H - 1] if r0 <= kr < r0 + NA_WIN_H else masked
                      for kr in range(u, u + NA_BAND_ROWS)]
            per_row.append(jnp.concatenate(pieces, axis=-1))
        tables.append(jnp.concatenate(per_row, axis=1))
    return jnp.stack(tables, axis=1)


def _na_mixer(p_all, rpb, batch, seq, n_ctx, need_ctx):
    n_lat = batch * seq
    rows = seq // GRID_W
    tb = _na_bias_table(rpb, rows)
    cb = n_lat // n_ctx
    w = 256
    lat = lambda c: pl.BlockSpec((seq, w), lambda b, c=c: (b, c))
    ctx = lambda c: pl.BlockSpec((n_ctx, w), lambda b, c=c: (cb + b, c))
    out_shape = [jax.ShapeDtypeStruct((n_lat, w), BF16)]
    out_specs = [pl.BlockSpec((seq, w), lambda b: (b, 0))]
    if need_ctx:
        out_shape.append(jax.ShapeDtypeStruct((batch * n_ctx, w), BF16))
        out_specs.append(pl.BlockSpec((n_ctx, w), lambda b: (b, 0)))
    return pl.pallas_call(
        functools.partial(_na_kernel, need_ctx, rows),
        grid=(batch,),
        in_specs=[lat(NA_Q), lat(NA_K), lat(NA_V), ctx(NA_Q), ctx(NA_K), ctx(NA_V),
                  pl.BlockSpec(tb.shape, lambda b: (0, 0, 0, 0))],
        out_specs=out_specs,
        out_shape=out_shape,
        scratch_shapes=[pltpu.VMEM((NA_HEADS // 2, seq, 256), BF16), pltpu.VMEM((NA_HEADS // 2, n_ctx, 256), BF16)],
        compiler_params=_params("parallel"),
        name="na_mixer",
    )(p_all, p_all, p_all, p_all, p_all, p_all, tb)


def _ret_kernel(need_ctx, seq, n_ctx, lg_ref, q_ref, k_ref, v_ref, g_ref, qc_ref, kc_ref, vc_ref, gc_ref,
                cos_ref, sin_ref, perm_ref, bd_ref, y_ref, *rest):
    if need_ctx:
        yc_ref, qr, kr, krt, kct, o_f, o_b, oc_f, oc_b, dm, qd, kdt = rest
    else:
        qr, kr, krt, kct, o_f, o_b, oc_f, oc_b, dm, qd, kdt = rest
    ch = RET_CHUNK
    n_lat_ch = seq // ch
    n_ctx_ch = n_ctx // ch
    n_pairs = RET_HEADS // 2
    assert ch == 128 and 2 * RET_DK == 128

    def prep(i, carry):
        rs = pl.multiple_of(i * ch, ch)
        c = cos_ref[pl.ds(rs, ch), :]
        s = sin_ref[pl.ds(rs, ch), :]
        qr[pl.ds(rs, ch), :] = _rope_mxu(q_ref[pl.ds(rs, ch), :], c, s, perm_ref)
        k = _rope_mxu(k_ref[pl.ds(rs, ch), :], c, s, perm_ref)
        kr[pl.ds(rs, ch), :] = k
        for p in range(n_pairs):
            krt[i, p * 128:(p + 1) * 128, :] = k[:, p * 128:(p + 1) * 128].T
        return carry

    lax.fori_loop(0, n_lat_ch, prep, 0, unroll=2)
    for n in range(n_ctx_ch):
        kc = kc_ref[n * ch:(n + 1) * ch, :].astype(F32)
        for p in range(n_pairs):
            kct[n, p * 128:(p + 1) * 128, :] = kc[:, p * 128:(p + 1) * 128].T

    ri = lax.broadcasted_iota(jnp.int32, (ch, ch), 0)
    ci = lax.broadcasted_iota(jnp.int32, (ch, ch), 1)
    lag = (ri - ci).astype(F32)
    rowf = ri.astype(F32)
    colf = ci.astype(F32)
    low_lane = ci < RET_DK
    low_row = ri < RET_DK
    block_diag = jnp.where(low_lane == low_row, 1.0, 0.0)
    low, high = _lane_masks(2)

    chains = [(p, dirn) for p in range(n_pairs) for dirn in range(2)]
    cdec = []
    for c, (p, dirn) in enumerate(chains):
        lg_a = lg_ref[dirn, 2 * p]
        lg_b = lg_ref[dirn, 2 * p + 1]
        lg_lane = jnp.where(low_lane, lg_a, lg_b)
        lg_row = jnp.where(low_row, lg_a, lg_b)
        for hh, lg in enumerate((lg_a, lg_b)):
            if dirn == 0:
                keep = ri >= ci
                dm[2 * c + hh] = jnp.where(keep, jnp.exp(jnp.where(keep, lag, 0.0) * lg), 0.0)
            else:
                keep = ci >= ri
                dm[2 * c + hh] = jnp.where(keep, jnp.exp(jnp.where(keep, -lag, 0.0) * lg), 0.0)
        if dirn == 0:
            qd[c] = jnp.exp((rowf + 1.0) * lg_lane)
            kdt[c] = jnp.exp((ch - 1.0 - colf) * lg_row)
        else:
            qd[c] = jnp.exp((ch - rowf) * lg_lane)
            kdt[c] = jnp.exp(colf * lg_row)
        cdec.append(jnp.exp(ch * lg_lane[0:1, :]))

    def step(c, state, qi, ki, kti, vi):
        kb = ki.astype(BF16)
        inner_a = (_dot_nt((qi * low).astype(BF16), kb) * dm[2 * c]).astype(BF16)
        inner_b = (_dot_nt((qi * high).astype(BF16), kb) * dm[2 * c + 1]).astype(BF16)
        o = (jnp.where(low_lane, _dot(inner_a, vi), _dot(inner_b, vi))
             + _dot((qi * qd[c]).astype(BF16), state.astype(BF16)))
        state = state * cdec[c] + block_diag * _dot((kti * kdt[c]).astype(BF16), vi)
        return state, o

    states = []
    for c, (p, dirn) in enumerate(chains):
        ps = slice(p * 128, (p + 1) * 128)
        state = jnp.zeros((128, 128), F32)
        for n in (range(n_ctx_ch) if dirn == 0 else range(n_ctx_ch - 1, -1, -1)):
            rs = slice(n * ch, (n + 1) * ch)
            state, o = step(c, state, qc_ref[rs, ps].astype(F32), kc_ref[rs, ps].astype(F32), kct[n, ps, :],
                            vc_ref[rs, ps])
            if need_ctx:
                (oc_f if dirn == 0 else oc_b)[rs, ps] = o
        states.append(state)

    def lat_body(i, states):
        new = []
        for c, (p, dirn) in enumerate(chains):
            ps = slice(p * 128, (p + 1) * 128)
            n = i if dirn == 0 else n_lat_ch - 1 - i
            rs = pl.ds(pl.multiple_of(n * ch, ch), ch)
            state, o = step(c, states[c], qr[rs, ps], kr[rs, ps], krt[n, ps, :], v_ref[rs, ps])
            (o_f if dirn == 0 else o_b)[rs, ps] = o
            new.append(state)
        return tuple(new)

    lax.fori_loop(0, n_lat_ch, lat_body, tuple(states), unroll=2)

    def head_mean(x):
        hi = x.astype(BF16)
        lo = (x - hi.astype(F32)).astype(BF16)
        bd = bd_ref[...]
        return (_dot(hi, bd) + _dot(lo, bd)) * (1.0 / RET_DK)

    def finish(o, g):
        cen = o - head_mean(o)
        on = cen * lax.rsqrt(head_mean(cen * cen) + EPS)
        g = g.astype(F32)
        return (g * _sigmoid(g) * on).astype(BF16)

    def fin_body(i, carry):
        rs = pl.multiple_of(i * ch, ch)
        y_ref[pl.ds(rs, ch), :] = finish(o_f[pl.ds(rs, ch), :] + o_b[pl.ds(rs, ch), :], g_ref[pl.ds(rs, ch), :])
        return carry

    lax.fori_loop(0, n_lat_ch, fin_body, 0, unroll=2)
    if need_ctx:
        for n in range(n_ctx_ch):
            rs = slice(n * ch, (n + 1) * ch)
            yc_ref[rs, :] = finish(oc_f[rs, :] + oc_b[rs, :], gc_ref[rs, :])


def _rope_tables(seq, head_dim, width, dist):
    half = head_dim // 2
    nf = half // 2
    assert nf == dist
    inv = 1.0 / (ROPE_THETA ** (np.arange(nf, dtype=np.float32) / nf))
    t = np.arange(seq)
    rows = (t // GRID_W).astype(np.float32)[:, None] * inv[None, :]
    cols = (t % GRID_W).astype(np.float32)[:, None] * inv[None, :]
    cos = np.concatenate([np.cos(rows), np.cos(rows), np.cos(cols), np.cos(cols)], axis=-1)
    sin = np.concatenate([-np.sin(rows), np.sin(rows), -np.sin(cols), np.sin(cols)], axis=-1)
    reps = width // head_dim
    return (jnp.asarray(np.tile(cos, (1, reps)), F32), jnp.asarray(np.tile(sin, (1, reps)), F32))


def _ret_mixer(p_all, log_decay, batch, seq, n_ctx, need_ctx):
    n_lat = batch * seq
    cb = n_lat // n_ctx
    w = 256
    ch = RET_CHUNK
    log_gamma = jnp.log1p(-jnp.exp(log_decay.astype(F32)))
    cos, sin = _rope_tables(seq, RET_DK, w, 16)
    lat = lambda c: pl.BlockSpec((seq, w), lambda b, c=c: (b, c))
    ctx = lambda c: pl.BlockSpec((n_ctx, w), lambda b, c=c: (cb + b, c))
    whole = pl.BlockSpec((seq, w), lambda b: (0, 0))
    out_shape = [jax.ShapeDtypeStruct((n_lat, w), BF16)]
    out_specs = [pl.BlockSpec((seq, w), lambda b: (b, 0))]
    if need_ctx:
        out_shape.append(jax.ShapeDtypeStruct((batch * n_ctx, w), BF16))
        out_specs.append(pl.BlockSpec((n_ctx, w), lambda b: (b, 0)))
    return pl.pallas_call(
        functools.partial(_ret_kernel, need_ctx, seq, n_ctx),
        grid=(batch,),
        in_specs=[pl.BlockSpec(memory_space=pltpu.SMEM),
                  lat(RET_Q), lat(RET_K), lat(RET_V), lat(RET_G),
                  ctx(RET_Q), ctx(RET_K), ctx(RET_V), ctx(RET_G), whole, whole,
                  pl.BlockSpec((w, w), lambda b: (0, 0)), pl.BlockSpec((w, w), lambda b: (0, 0))],
        out_specs=out_specs,
        out_shape=out_shape,
        scratch_shapes=[pltpu.VMEM((seq, w), F32), pltpu.VMEM((seq, w), F32),
                        pltpu.VMEM((seq // ch, w, ch), F32), pltpu.VMEM((n_ctx // ch, w, ch), F32),
                        pltpu.VMEM((seq, w), F32), pltpu.VMEM((seq, w), F32),
                        pltpu.VMEM((n_ctx, w), F32), pltpu.VMEM((n_ctx, w), F32),
                        pltpu.VMEM((2 * RET_HEADS, ch, ch), F32),
                        pltpu.VMEM((RET_HEADS, ch, 128), F32), pltpu.VMEM((RET_HEADS, 128, ch), F32)],
        compiler_params=_params("parallel"),
        name="retention_mixer",
    )(log_gamma, p_all, p_all, p_all, p_all, p_all, p_all, p_all, p_all, cos, sin,
      _swap_matrix(w, 16), _block_ones(w, RET_DK))


def _swap_matrix(width, dist):
    i = np.arange(width)
    partner = np.where(i % (2 * dist) < dist, i + dist, i - dist)
    p = np.zeros((width, width), np.float32)
    p[partner, i] = 1.0
    return jnp.asarray(p, BF16)


def _block_ones(width, block):
    i = np.arange(width)
    return jnp.asarray((i[:, None] // block == i[None, :] // block).astype(np.float32), BF16)


def _sumsq_blocks(xf, bd_ref):
    sq = xf * xf
    hi = sq.astype(BF16)
    lo = (sq - hi.astype(F32)).astype(BF16)
    bd = bd_ref[...]
    return _dot(hi, bd) + _dot(lo, bd)


def _rope_mxu(x, cos, sin, perm_ref):
    return x.astype(F32) * cos + _dot(x, perm_ref[...]) * sin


def _softmax_pv(qm, k, v_ones):
    s = _dot_nt(qm, k)
    p = jnp.exp2((s - jnp.max(s, axis=-1, keepdims=True)).astype(BF16))
    oe = _dot(p, v_ones)
    return oe[:, :128] / oe[:, 128:]


def _lane_masks(n_parts):
    lane = lax.broadcasted_iota(jnp.int32, (1, 128), 1)
    return [jnp.where(lane // (128 // n_parts) == i, 1.0, 0.0).astype(BF16) for i in range(n_parts)]


def _gqa_kernel(n_lat_k, n_ctx, *refs):
    if n_lat_k:
        (q_ref, k_ref, v_ref, kc_ref, vc_ref, qa_ref, qb_ref, ka_ref, kb_ref, kg_ref, pq_ref, pk_ref,
         bdq_ref, bdk_ref, y_ref, kp, vx) = refs
    else:
        q_ref, kc_ref, vc_ref, qa_ref, kg_ref, bdq_ref, bdk_ref, y_ref, kp, vx = refs
    dim = GQA_DIM
    nk = n_lat_k + n_ctx

    def inv_rms(xf, bd_ref):
        return lax.rsqrt(_sumsq_blocks(xf, bd_ref) * (1.0 / dim) + EPS)

    @pl.when(pl.program_id(1) == 0)
    def _():
        def put(rs, kn, v):
            ones = jnp.ones(v.shape, BF16)
            kp[0, rs, :] = kn.astype(BF16)
            kp[1, rs, :] = pltpu.roll(kn, dim, 1).astype(BF16)
            vx[0, rs, :] = jnp.concatenate([v, ones], axis=-1)
            vx[1, rs, :] = jnp.concatenate([pltpu.roll(v.astype(F32), dim, 1).astype(BF16), ones], axis=-1)

        kc = kc_ref[...]
        kcf = kc.astype(F32)
        put(slice(n_lat_k, nk), kcf * kg_ref[...] * inv_rms(kcf, bdk_ref), vc_ref[...])
        ck = 512
        for c in range(n_lat_k // ck):
            rs = slice(c * ck, (c + 1) * ck)
            k = k_ref[rs, :]
            put(rs, _rope_mxu(k, ka_ref[rs, :], kb_ref[rs, :], pk_ref) * inv_rms(k.astype(F32), bdk_ref), v_ref[rs, :])

    q = q_ref[...]
    qf = q.astype(F32)
    qn = _rope_mxu(q, qa_ref[...], qb_ref[...], pq_ref) if n_lat_k else qf * qa_ref[...]
    qn = (qn * inv_rms(qf, bdq_ref)).astype(BF16)
    low, high = _lane_masks(2)
    outs = []
    for g in range(GQA_KV_HEADS):
        qv = qn[:, g * 128:(g + 1) * 128]
        o_low = _softmax_pv(qv * low, kp[g], vx[g])
        o_high = _softmax_pv(qv * high, kp[1 - g], vx[1 - g])
        outs.append(jnp.where(low > 0, o_low, o_high))
    y_ref[...] = jnp.concatenate(outs, axis=-1).astype(BF16)


def _gqa_mixer(p_all, q_gain, k_gain, batch, seq, n_ctx, need_ctx):
    n_lat = batch * seq
    cb = n_lat // n_ctx
    dim = GQA_DIM
    qw = GQA_HEADS * dim
    kvw = GQA_KV_HEADS * dim
    assert qw == 256 and kvw == 128

    def swapped(g):
        return g.reshape(-1, 2, 16)[:, ::-1].reshape(1, -1)

    qg = jnp.tile(q_gain.astype(F32) * (dim ** -0.5 * LOG2E), GQA_HEADS).reshape(1, qw)
    kg = jnp.tile(k_gain.astype(F32), GQA_KV_HEADS).reshape(1, kvw)
    cq, sq = _rope_tables(seq, dim, qw, 16)
    ck, sk = _rope_tables(seq, dim, kvw, 16)
    qa, qb = cq * qg, sq * swapped(qg)
    ka, kb = ck * kg, sk * swapped(kg)
    pq, pk = _swap_matrix(qw, 16), _swap_matrix(kvw, 16)
    bdq, bdk = _block_ones(qw, dim), _block_ones(kvw, dim)
    tq = 512
    nqb = seq // tq
    const = lambda shape: pl.BlockSpec(shape, lambda b, i: (0,) * len(shape))
    scratch = lambda nk: [pltpu.VMEM((2, nk, kvw), BF16), pltpu.VMEM((2, nk, 2 * kvw), BF16)]
    y_lat = pl.pallas_call(
        functools.partial(_gqa_kernel, seq, n_ctx),
        grid=(batch, nqb),
        in_specs=[pl.BlockSpec((tq, qw), lambda b, i: (b * nqb + i, GQA_Q)),
                  pl.BlockSpec((seq, kvw), lambda b, i: (b, 16)),
                  pl.BlockSpec((seq, kvw), lambda b, i: (b, 17)),
                  pl.BlockSpec((n_ctx, kvw), lambda b, i: (cb + b, 16)),
                  pl.BlockSpec((n_ctx, kvw), lambda b, i: (cb + b, 17)),
                  pl.BlockSpec((tq, qw), lambda b, i: (i, 0)),
                  pl.BlockSpec((tq, qw), lambda b, i: (i, 0)),
                  const((seq, kvw)), const((seq, kvw)), const((1, kvw)),
                  const((qw, qw)), const((kvw, kvw)), const((qw, qw)), const((kvw, kvw))],
        out_specs=pl.BlockSpec((tq, qw), lambda b, i: (b * nqb + i, 0)),
        out_shape=jax.ShapeDtypeStruct((n_lat, qw), BF16),
        scratch_shapes=scratch(seq + n_ctx),
        compiler_params=_params("parallel", "arbitrary"),
        name="gqa_mixer",
    )(p_all, p_all, p_all, p_all, p_all, qa, qb, ka, kb, kg, pq, pk, bdq, bdk)
    if not need_ctx:
        return y_lat, None
    y_ctx = pl.pallas_call(
        functools.partial(_gqa_kernel, 0, n_ctx),
        grid=(batch, 1),
        in_specs=[pl.BlockSpec((n_ctx, qw), lambda b, i: (cb + b, GQA_Q)),
                  pl.BlockSpec((n_ctx, kvw), lambda b, i: (cb + b, 16)),
                  pl.BlockSpec((n_ctx, kvw), lambda b, i: (cb + b, 17)),
                  const((1, qw)), const((1, kvw)), const((qw, qw)), const((kvw, kvw))],
        out_specs=pl.BlockSpec((n_ctx, qw), lambda b, i: (b, 0)),
        out_shape=jax.ShapeDtypeStruct((batch * n_ctx, qw), BF16),
        scratch_shapes=scratch(n_ctx),
        compiler_params=_params("parallel", "arbitrary"),
        name="gqa_mixer_ctx",
    )(p_all, p_all, p_all, qg, kg, bdq, bdk)
    return y_lat, y_ctx


def _diff_kernel(n_lat_k, n_ctx, lam_init, *refs):
    if n_lat_k:
        (q_ref, k_ref, v_ref, kc_ref, vc_ref, lp_ref, sg_ref, cq_ref, sq_ref, ck_ref, sk_ref, perm_ref, bd_ref,
         y_ref, kp, vx) = refs
    else:
        q_ref, kc_ref, vc_ref, lp_ref, sg_ref, bd_ref, y_ref, kp, vx = refs
    nk = n_lat_k + n_ctx
    n_pairs = DIFF_HEADS // 2

    @pl.when(pl.program_id(1) == 0)
    def _():
        def put(rs, k, v):
            kp[rs, :] = k
            ones = jnp.ones((v.shape[0], 128), BF16)
            for pr in range(n_pairs):
                vx[pr, rs, :] = jnp.concatenate([v[:, pr * 128:(pr + 1) * 128], ones], axis=-1)

        put(slice(n_lat_k, nk), kc_ref[...], vc_ref[...])
        ck = 512
        for c in range(n_lat_k // ck):
            rs = slice(c * ck, (c + 1) * ck)
            put(rs, _rope_mxu(k_ref[rs, :], ck_ref[rs, :], sk_ref[rs, :], perm_ref).astype(BF16), v_ref[rs, :])

    lp = lp_ref[...]
    lam = (jnp.exp(jnp.sum(lp[0:1, :] * lp[1:2, :], axis=-1, keepdims=True))
           - jnp.exp(jnp.sum(lp[2:3, :] * lp[3:4, :], axis=-1, keepdims=True)) + lam_init)

    q = q_ref[...]
    if n_lat_k:
        q = _rope_mxu(q, cq_ref[...], sq_ref[...], perm_ref).astype(BF16)
    quarter = _lane_masks(4)
    low = _lane_masks(2)[0]
    outs = []
    for pr in range(n_pairs):
        ps = slice(pr * 128, (pr + 1) * 128)
        qv = q[:, ps]
        kv = kp[:, ps]
        o_head = [_softmax_pv(qv * quarter[2 * hh], kv, vx[pr]) - lam * _softmax_pv(qv * quarter[2 * hh + 1], kv, vx[pr])
                  for hh in range(2)]
        outs.append(jnp.where(low > 0, o_head[0], o_head[1]))
    o = jnp.concatenate(outs, axis=-1)
    inv = lax.rsqrt(_sumsq_blocks(o, bd_ref) * (1.0 / DIFF_V_DIM) + EPS)
    y_ref[...] = (o * inv * sg_ref[...] * (1.0 - lam_init)).astype(BF16)


def _diff_mixer(p_all, lam_params, subln, layer_idx, batch, seq, n_ctx, need_ctx):
    n_lat = batch * seq
    cb = n_lat // n_ctx
    lam_init = 0.8 - 0.6 * math.exp(-0.3 * layer_idx)
    lp = jnp.zeros((8, 128), F32).at[:4, :DIFF_QK_DIM].set(lam_params.astype(F32))
    w = DIFF_HEADS * DIFF_V_DIM
    assert w == 256 and DIFF_HEADS * 2 * DIFF_QK_DIM == w
    sg = jnp.tile(subln.astype(F32), DIFF_HEADS).reshape(1, w)
    cq, sq = _rope_tables(seq, DIFF_QK_DIM, w, 8)
    perm = _swap_matrix(w, 8)
    bd = _block_ones(w, DIFF_V_DIM)
    tq = 512
    nqb = seq // tq
    const = lambda shape: pl.BlockSpec(shape, lambda b, i: (0,) * len(shape))
    scratch = lambda nk: [pltpu.VMEM((nk, w), BF16), pltpu.VMEM((DIFF_HEADS // 2, nk, 256), BF16)]
    y_lat = pl.pallas_call(
        functools.partial(_diff_kernel, seq, n_ctx, lam_init),
        grid=(batch, nqb),
        in_specs=[pl.BlockSpec((tq, w), lambda b, i: (b * nqb + i, DIFF_Q)),
                  pl.BlockSpec((seq, w), lambda b, i: (b, DIFF_K)),
                  pl.BlockSpec((seq, w), lambda b, i: (b, DIFF_V)),
                  pl.BlockSpec((n_ctx, w), lambda b, i: (cb + b, DIFF_K)),
                  pl.BlockSpec((n_ctx, w), lambda b, i: (cb + b, DIFF_V)),
                  const((8, 128)), const((1, w)),
                  pl.BlockSpec((tq, w), lambda b, i: (i, 0)),
                  pl.BlockSpec((tq, w), lambda b, i: (i, 0)),
                  const((seq, w)), const((seq, w)), const((w, w)), const((w, w))],
        out_specs=pl.BlockSpec((tq, w), lambda b, i: (b * nqb + i, 0)),
        out_shape=jax.ShapeDtypeStruct((n_lat, w), BF16),
        scratch_shapes=scratch(seq + n_ctx),
        compiler_params=_params("parallel", "arbitrary"),
        name="diff_mixer",
    )(p_all, p_all, p_all, p_all, p_all, lp, sg, cq, sq, cq, sq, perm, bd)
    if not need_ctx:
        return y_lat, None
    y_ctx = pl.pallas_call(
        functools.partial(_diff_kernel, 0, n_ctx, lam_init),
        grid=(batch, 1),
        in_specs=[pl.BlockSpec((n_ctx, w), lambda b, i: (cb + b, DIFF_Q)),
                  pl.BlockSpec((n_ctx, w), lambda b, i: (cb + b, DIFF_K)),
                  pl.BlockSpec((n_ctx, w), lambda b, i: (cb + b, DIFF_V)),
                  const((8, 128)), const((1, w)), const((w, w))],
        out_specs=pl.BlockSpec((n_ctx, w), lambda b, i: (b, 0)),
        out_shape=jax.ShapeDtypeStruct((batch * n_ctx, w), BF16),
        scratch_shapes=scratch(n_ctx),
        compiler_params=_params("parallel", "arbitrary"),
        name="diff_mixer_ctx",
    )(p_all, p_all, p_all, lp, sg, bd)
    return y_lat, y_ctx


def _merge_kernel(nlat_blk, has_ctx, n_parts, *refs):
    x_refs, refs = refs[:n_parts], refs[n_parts:]
    g_refs = refs[:N_BRANCH]
    y_refs = refs[N_BRANCH:2 * N_BRANCH]
    refs = refs[2 * N_BRANCH:]
    if has_ctx:
        yc_refs, refs = refs[:N_BRANCH], refs[N_BRANCH:]
        is_ctx = pl.program_id(0) >= nlat_blk
    wb_ref, wo_ref, mod_ref, gn_ref, wr_ref, xo_ref, m_ref, lg_ref = refs
    acc = None
    for n in range(N_BRANCH):
        y = y_refs[n][...]
        if has_ctx:
            y = jnp.where(is_ctx, yc_refs[n][...], y)
        term = g_refs[n][...].astype(F32) * _dot(y, wb_ref[n])
        acc = term if acc is None else acc + term
    y = _dot(acc.astype(BF16), wo_ref[...])
    x = _pick_rows(x_refs, nlat_blk) + mod_ref[2:3, :] * y
    xo_ref[...] = x
    xn = x * lax.rsqrt(jnp.mean(x * x, axis=-1, keepdims=True) + EPS) * gn_ref[...]
    m = xn * (1.0 + mod_ref[4:5, :]) + mod_ref[3:4, :]
    _store_tile_rows(m_ref, m)
    m_hi = m.astype(BF16)
    m_lo = (m - m_hi.astype(F32)).astype(BF16)
    w = wr_ref[...]
    w_hi = w.astype(BF16)
    w_lo = (w - w_hi.astype(F32)).astype(BF16)
    both = _dot_nt(jnp.concatenate([w_hi, w_lo], axis=0), m_hi)
    lg_ref[...] = both[:N_EXPERTS] + (_dot_nt(w_hi, m_lo) + both[N_EXPERTS:])


def _merge(x_parts, p_all, ys_lat, ys_ctx, wb_bf, wo_bf, modtab, gn, wr_t, tm, n_rows, n_lat, seq, batch):
    d = D_MODEL
    midx = _mod_index(tm, n_lat, seq, batch)
    nlat_blk = n_lat // tm
    has_ctx = ys_ctx is not None
    gate = lambda n: pl.BlockSpec((tm, d), lambda i, n=n: (i, MIX_COLS // d + n))
    y_specs = [pl.BlockSpec((tm, BRANCH_W), lambda i: (jnp.minimum(i, nlat_blk - 1), 0))] * N_BRANCH
    ys = list(ys_lat)
    if has_ctx:
        y_specs += [pl.BlockSpec((tm, BRANCH_W), lambda i: (jnp.maximum(i - nlat_blk, 0), 0))] * N_BRANCH
        ys += list(ys_ctx)
    return pl.pallas_call(
        functools.partial(_merge_kernel, nlat_blk, has_ctx, len(x_parts)),
        grid=(n_rows // tm,),
        in_specs=[*_row_blocks(x_parts, tm, n_lat),
                  gate(0), gate(1), gate(2), gate(3), *y_specs,
                  pl.BlockSpec((N_BRANCH, BRANCH_W, d), lambda i: (0, 0, 0)),
                  pl.BlockSpec((d, d), lambda i: (0, 0)),
                  pl.BlockSpec((None, 8, d), lambda i: (midx(i), 0, 0)),
                  pl.BlockSpec((1, d), lambda i: (0, 0)),
                  pl.BlockSpec((N_EXPERTS, d), lambda i: (0, 0))],
        out_specs=[pl.BlockSpec((tm, d), lambda i: (i, 0)),
                   pl.BlockSpec((tm * ROW_TILE, 128), lambda i: (i, 0)),
                   pl.BlockSpec((N_EXPERTS, tm), lambda i: (0, i))],
        out_shape=[jax.ShapeDtypeStruct((n_rows, d), F32),
                   jax.ShapeDtypeStruct((n_rows * ROW_TILE, 128), F32),
                   jax.ShapeDtypeStruct((N_EXPERTS, n_rows), F32)],
        compiler_params=_params("parallel"),
        name="merge_norm_route",
    )(*x_parts, p_all, p_all, p_all, p_all, *ys, wb_bf, wo_bf, modtab, gn.reshape(1, d), wr_t)


def _route_kernel(lg_ref, b_ref, tri_ref, idx_ref, w_ref, rank_ref, cnt_ref):
    s = _sigmoid(lg_ref[...])
    sel = s + b_ref[...]
    row = lambda a, e: a[e:e + 1, :]
    gsz = EXPERTS_PER_GROUP
    g_idx = None
    best = None
    for g in range(N_GROUPS):
        v = [row(sel, g * gsz + i) for i in range(gsz)]
        score = None
        for i in range(gsz):
            for j in range(i + 1, gsz):
                pair = v[i] + v[j]
                score = pair if score is None else jnp.maximum(score, pair)
        if g == 0:
            best, g_idx = score, jnp.zeros(score.shape, jnp.int32)
        else:
            better = score > best
            best = jnp.where(better, score, best)
            g_idx = jnp.where(better, g, g_idx)

    def in_group(a, i):
        out = row(a, i)
        for g in range(1, N_GROUPS):
            out = jnp.where(g_idx == g, row(a, g * gsz + i), out)
        return out

    v = [in_group(sel, i) for i in range(gsz)]
    sv = [in_group(s, i) for i in range(gsz)]

    def arg_first_max(vals):
        bv, bi = vals[0], jnp.zeros(vals[0].shape, jnp.int32)
        for i in range(1, gsz):
            better = vals[i] > bv
            bv = jnp.where(better, vals[i], bv)
            bi = jnp.where(better, i, bi)
        return bi

    i1 = arg_first_max(v)
    i2 = arg_first_max([jnp.where(i1 == i, -jnp.inf, v[i]) for i in range(gsz)])

    def pick(vals, idx):
        out = vals[0]
        for i in range(1, gsz):
            out = jnp.where(idx == i, vals[i], out)
        return out

    w1 = pick(sv, i1)
    w2 = pick(sv, i2)
    tot = w1 + w2
    e1 = g_idx * gsz + i1
    e2 = g_idx * gsz + i2
    idx_ref[0:1, :] = e1
    idx_ref[1:2, :] = e2
    w_ref[0:1, :] = w1 / tot
    w_ref[1:2, :] = w2 / tot

    @pl.when(pl.program_id(0) == 0)
    def _():
        cnt_ref[...] = jnp.zeros(cnt_ref.shape, cnt_ref.dtype)

    n_e, tn = s.shape
    erow = lax.broadcasted_iota(jnp.int32, (n_e, tn), 0)
    oh1 = jnp.where(erow == e1, 1.0, 0.0)
    oh2 = jnp.where(erow == e2, 1.0, 0.0)
    oh = (oh1 + oh2).astype(BF16)
    base = cnt_ref[...]
    r1, r2 = [], []
    for c in range(tn // 128):
        cs = slice(c * 128, (c + 1) * 128)
        before = base + _dot(oh[:, cs], tri_ref[...])
        r1.append(jnp.sum(oh1[:, cs] * before, axis=0, keepdims=True))
        r2.append(jnp.sum(oh2[:, cs] * before, axis=0, keepdims=True))
        base = base + jnp.sum(oh[:, cs].astype(F32), axis=1, keepdims=True)
    cnt_ref[...] = base
    rank_ref[0:1, :] = jnp.concatenate(r1, axis=-1).astype(jnp.int32)
    rank_ref[1:2, :] = jnp.concatenate(r2, axis=-1).astype(jnp.int32)


def _route(logits_t, router_bias):
    e, n = logits_t.shape
    tn = math.gcd(n, 2048)
    i = np.arange(128)
    tri = jnp.asarray((i[:, None] < i[None, :]).astype(np.float32), BF16)
    idx2, w2, rank, cnt = pl.pallas_call(
        _route_kernel,
        grid=(n // tn,),
        in_specs=[pl.BlockSpec((e, tn), lambda i: (0, i)),
                  pl.BlockSpec((e, 1), lambda i: (0, 0)),
                  pl.BlockSpec((128, 128), lambda i: (0, 0))],
        out_specs=[pl.BlockSpec((TOP_K, tn), lambda i: (0, i)),
                   pl.BlockSpec((TOP_K, tn), lambda i: (0, i)),
                   pl.BlockSpec((TOP_K, tn), lambda i: (0, i)),
                   pl.BlockSpec((e, 128), lambda i: (0, 0))],
        out_shape=[jax.ShapeDtypeStruct((TOP_K, n), jnp.int32),
                   jax.ShapeDtypeStruct((TOP_K, n), F32),
                   jax.ShapeDtypeStruct((TOP_K, n), jnp.int32),
                   jax.ShapeDtypeStruct((e, 128), F32)],
        compiler_params=_params("arbitrary"),
        name="route_top2",
    )(logits_t, router_bias.astype(F32).reshape(e, 1), tri)
    return idx2, w2, rank, cnt[:, 0].astype(jnp.int32)


def _dispatch_kernel(n_tok_steps, pos_ref, m_ref, xs_out, zrow, sem):
    i = pl.program_id(0)
    rt = ROW_TILE
    tm = m_ref.shape[0] // rt
    n = TOP_K * tm

    def wait():
        pltpu.make_async_copy(xs_out.at[pl.ds(0, n * rt)], xs_out.at[pl.ds(0, n * rt)], sem).wait()

    @pl.when(i < n_tok_steps)
    def _():
        for j in range(n):
            dst = pl.multiple_of(pos_ref[i, j], rt)
            pltpu.make_async_copy(m_ref.at[pl.ds((j % tm) * rt, rt)], xs_out.at[pl.ds(dst, rt)],
                                  sem).start(priority=j % 2)
        wait()

    @pl.when(i >= n_tok_steps)
    def _():
        zrow[...] = jnp.zeros(zrow.shape, zrow.dtype)
        for j in range(n):
            dst = pl.multiple_of(pos_ref[i, j], rt)
            pltpu.make_async_copy(zrow, xs_out.at[pl.ds(dst, rt)], sem).start(priority=j % 2)
        wait()


def _dispatch(m_tiles, pos, pad_slots, n_slots):
    rt = ROW_TILE
    n_tok_steps, n = pos.shape
    tm = n // TOP_K
    steps = jnp.concatenate([pos, pad_slots], axis=0)
    grid_spec = pltpu.PrefetchScalarGridSpec(
        num_scalar_prefetch=1,
        grid=(steps.shape[0],),
        in_specs=[pl.BlockSpec((tm * rt, 128), lambda i, pos: (jnp.minimum(i, n_tok_steps - 1), 0))],
        out_specs=pl.BlockSpec(memory_space=pl.ANY),
        scratch_shapes=[pltpu.VMEM((rt, 128), F32), pltpu.SemaphoreType.DMA(())])
    return pl.pallas_call(
        functools.partial(_dispatch_kernel, n_tok_steps),
        grid_spec=grid_spec,
        out_shape=jax.ShapeDtypeStruct((n_slots * rt, 128), F32),
        compiler_params=_params("arbitrary"),
        name="moe_dispatch",
    )(steps, m_tiles)


def _expert_kernel(be_ref, x_ref, wg_ref, wu_ref, wd_ref, y_ref):
    blk = x_ref.shape[0] // ROW_TILE
    x = _load_tile_rows(x_ref, 0, blk).astype(BF16)
    a = _dot(x, wg_ref[...].astype(BF16))
    h = a * _sigmoid(a) * _dot(x, wu_ref[...].astype(BF16))
    _store_tile_rows(y_ref, _dot(h.astype(BF16), wd_ref[...].astype(BF16)))


def _experts(x_slots, blk_e, blk, layer, w_gate, w_up, w_down):
    rt = ROW_TILE
    d = D_MODEL
    n_slots = x_slots.shape[0] // rt
    grid_spec = pltpu.PrefetchScalarGridSpec(
        num_scalar_prefetch=1,
        grid=(n_slots // blk,),
        in_specs=[pl.BlockSpec((blk * rt, 128), lambda i, be: (i, 0)),
                  pl.BlockSpec((None, None, d, D_EXPERT), lambda i, be: (layer, be[i], 0, 0)),
                  pl.BlockSpec((None, None, d, D_EXPERT), lambda i, be: (layer, be[i], 0, 0)),
                  pl.BlockSpec((None, None, D_EXPERT, d), lambda i, be: (layer, be[i], 0, 0))],
        out_specs=pl.BlockSpec((blk * rt, 128), lambda i, be: (i, 0)))
    return pl.pallas_call(
        _expert_kernel,
        grid_spec=grid_spec,
        out_shape=jax.ShapeDtypeStruct((n_slots * rt, 128), F32),
        compiler_params=_params("arbitrary"),
        name="expert_ffn",
    )(blk_e, x_slots, w_gate, w_up, w_down)


def _combine_kernel(pos_ref, y_hbm, x_ref, w_ref, mod_ref, gf_ref, o_ref, ybuf, sem):
    i = pl.program_id(0)
    last = pl.num_programs(0) - 1
    slot = i % 2
    rt = ROW_TILE
    tm = x_ref.shape[0]
    n = TOP_K * tm

    def issue(step, dst_slot):
        for j in range(n):
            src = pl.multiple_of(pos_ref[step, j], rt)
            pltpu.make_async_copy(y_hbm.at[pl.ds(src, rt)], ybuf.at[dst_slot, pl.ds(j * rt, rt)],
                                  sem.at[dst_slot]).start(priority=j % 2)

    def wait(dst_slot):
        pltpu.make_async_copy(y_hbm.at[pl.ds(0, n * rt)], ybuf.at[dst_slot], sem.at[dst_slot]).wait()

    @pl.when(i == 0)
    def _():
        issue(0, 0)

    wait(slot)
    issue(jnp.minimum(i + 1, last), 1 - slot)
    w = w_ref[...]
    yb = ybuf.at[slot]
    f = w[:, 0:1] * _load_tile_rows(yb, 0, tm) + w[:, 1:2] * _load_tile_rows(yb, tm * rt, tm)
    x = x_ref[...] + mod_ref[5:6, :] * f
    o_ref[...] = x * lax.rsqrt(jnp.mean(x * x, axis=-1, keepdims=True) + EPS) * gf_ref[...]

    @pl.when(i == last)
    def _():
        wait(1 - slot)


def _combine(x_rows, y_slots, pos, w_tok, modtab, g_final, n_lat, seq, batch):
    n_rows, d = x_rows.shape
    n_steps, n = pos.shape
    tm = n // TOP_K
    midx = _mod_index(tm, n_lat, seq, batch)
    grid_spec = pltpu.PrefetchScalarGridSpec(
        num_scalar_prefetch=1,
        grid=(n_steps,),
        in_specs=[pl.BlockSpec(memory_space=pl.ANY),
                  pl.BlockSpec((tm, d), lambda i, pos: (i, 0)),
                  pl.BlockSpec((tm, TOP_K), lambda i, pos: (i, 0)),
                  pl.BlockSpec((None, 8, d), lambda i, pos: (midx(i), 0, 0)),
                  pl.BlockSpec((1, d), lambda i, pos: (0, 0))],
        out_specs=pl.BlockSpec((tm, d), lambda i, pos: (i, 0)),
        scratch_shapes=[pltpu.VMEM((2, n * ROW_TILE, 128), F32), pltpu.SemaphoreType.DMA((2,))])
    return pl.pallas_call(
        _combine_kernel,
        grid_spec=grid_spec,
        out_shape=jax.ShapeDtypeStruct((n_rows, d), F32),
        compiler_params=_params("arbitrary"),
        name="moe_combine",
    )(pos, y_slots, x_rows, w_tok, modtab, g_final.astype(F32).reshape(1, d))


def _combine_project_kernel(n_col_steps, pos_ref, y_hbm, x_ref, wt_ref, modp_ref, g_ref, mod_ref, w_ref, cs_ref,
                            p_ref, xo_ref, h_ref, ybuf, sem):
    i = pl.program_id(0)
    j = pl.program_id(1)
    last = pl.num_programs(0) - 1
    rt = ROW_TILE
    per_step = ybuf.shape[1] // rt
    per_k = n_col_steps // TOP_K

    def issue(block, step):
        for jj in range(per_step):
            src = pl.multiple_of(pos_ref[block * n_col_steps + step, jj], rt)
            pltpu.make_async_copy(y_hbm.at[pl.ds(src, rt)], ybuf.at[step, pl.ds(jj * rt, rt)],
                                  sem).start(priority=jj % 2)

    def wait():
        for step in range(n_col_steps):
            pltpu.make_async_copy(y_hbm.at[pl.ds(0, per_step * rt)], ybuf.at[step], sem).wait()

    @pl.when((i == 0) & (j == 0))
    def _():
        for step in range(n_col_steps):
            issue(0, step)

    @pl.when(j == 0)
    def _():
        wait()
        w = wt_ref[...]
        yk = [jnp.concatenate([_load_tile_rows(ybuf.at[k * per_k + c], 0, per_step) for c in range(per_k)],
                              axis=0) for k in range(TOP_K)]
        f = w[:, 0:1] * yk[0] + w[:, 1:2] * yk[1]
        x = x_ref[...] + modp_ref[5:6, :] * f
        xo_ref[...] = x
        y = x * lax.rsqrt(jnp.mean(x * x, axis=-1, keepdims=True) + EPS) * g_ref[...]
        h_ref[...] = (y * (1.0 + mod_ref[1:2, :]) + mod_ref[0:1, :]).astype(BF16)

    issue(jnp.minimum(i + 1, last), j)
    acc = _dot(h_ref[...], w_ref[...])
    gate = 0.5 * jnp.tanh(0.5 * acc) + 0.5
    cs = cs_ref[...]
    p_ref[...] = jnp.where(cs > 0.0, acc * cs, gate).astype(BF16)

    @pl.when((i == last) & (j == n_col_steps - 1))
    def _():
        wait()


def _combine_project(x_rows, y_slots, pos, w_tok, modtab_prev, g, modtab, w_bf, n_lat, seq, batch):
    n_rows, d = x_rows.shape
    ncol = w_bf.shape[1]
    tn = PROJ_TN
    n_col_steps = ncol // tn
    per_k = n_col_steps // TOP_K
    per_step = pos.shape[1] // TOP_K
    tm = per_k * per_step
    n_steps = n_rows // tm
    assert n_col_steps == TOP_K * per_k and n_rows % tm == 0
    pos = pos.reshape(n_steps, per_k, TOP_K, per_step).transpose(0, 2, 1, 3).reshape(n_steps * n_col_steps, per_step)
    midx = _mod_index(tm, n_lat, seq, batch)
    grid_spec = pltpu.PrefetchScalarGridSpec(
        num_scalar_prefetch=1,
        grid=(n_steps, n_col_steps),
        in_specs=[pl.BlockSpec(memory_space=pl.ANY),
                  pl.BlockSpec((tm, d), lambda i, j, pos: (i, 0)),
                  pl.BlockSpec((tm, TOP_K), lambda i, j, pos: (i, 0)),
                  pl.BlockSpec((None, 8, d), lambda i, j, pos: (midx(i), 0, 0)),
                  pl.BlockSpec((1, d), lambda i, j, pos: (0, 0)),
                  pl.BlockSpec((None, 8, d), lambda i, j, pos: (midx(i), 0, 0)),
                  pl.BlockSpec((d, tn), lambda i, j, pos: (0, j)),
                  pl.BlockSpec((1, tn), lambda i, j, pos: (0, j))],
        out_specs=[pl.BlockSpec((tm, tn), lambda i, j, pos: (i, j)),
                   pl.BlockSpec((tm, d), lambda i, j, pos: (i, 0))],
        scratch_shapes=[pltpu.VMEM((tm, d), BF16), pltpu.VMEM((n_col_steps, per_step * ROW_TILE, 128), F32),
                        pltpu.SemaphoreType.DMA(())])
    p_all, x_new = pl.pallas_call(
        functools.partial(_combine_project_kernel, n_col_steps),
        grid_spec=grid_spec,
        out_shape=[jax.ShapeDtypeStruct((n_rows, ncol), BF16), jax.ShapeDtypeStruct((n_rows, d), F32)],
        compiler_params=_params("arbitrary", "arbitrary"),
        name="combine_norm_project",
    )(pos, y_slots, x_rows, w_tok, modtab_prev, g.reshape(1, d), modtab, w_bf, _mix_col_scale())
    return p_all, x_new


def _dispatch_plan(idx2, rank, counts, blk, tm):
    n = idx2.shape[1]
    a = TOP_K * n
    padded = (counts + blk - 1) // blk * blk
    pad_end = jnp.cumsum(padded)
    pad_start = pad_end - padded
    start_of = jnp.zeros(idx2.shape, jnp.int32)
    for e in range(N_EXPERTS):
        start_of = jnp.where(idx2 == e, pad_start[e], start_of)
    dest = (start_of + rank).astype(jnp.int32)
    n_blocks = a // blk + N_EXPERTS
    first_slot = jnp.arange(n_blocks, dtype=jnp.int32) * blk
    blk_e = jnp.minimum(jnp.sum(pad_end[None, :] <= first_slot[:, None], axis=1), N_EXPERTS - 1).astype(jnp.int32)
    pos = dest.reshape(TOP_K, n // tm, tm).transpose(1, 0, 2).reshape(n // tm, TOP_K * tm) * ROW_TILE
    n_slots = n_blocks * blk
    starts = jnp.concatenate([pad_start + counts, pad_end[-1:]])
    lens = jnp.concatenate([padded - counts, n_slots - pad_end[-1:]])
    ends = jnp.cumsum(lens)
    j = jnp.arange(N_EXPERTS * blk, dtype=jnp.int32)
    shift = starts - (ends - lens)
    shift_of = jnp.zeros(j.shape, jnp.int32)
    for s in range(N_EXPERTS + 1):
        shift_of = jnp.where((j >= ends[s] - lens[s]) & (j < ends[s]), shift[s], shift_of)
    pad_slots = ((shift_of + j) * ROW_TILE).astype(jnp.int32)
    return pos, pad_slots.reshape(-1, TOP_K * tm), blk_e, n_slots


def kernel(x, c, ctx, c_ctx, w_mod, b_mod, g_norm1, g_norm2, w_in, na_rpb, ret_log_decay, gqa_q_gain, gqa_k_gain,
           diff_lambda, diff_subln, w_branch, w_out, w_router, router_bias, w_gate_e, w_up_e, w_down_e, g_final):
    batch, seq, d = x.shape
    n_ctx = ctx.shape[1]
    n_lat = batch * seq
    n_ctx_tot = batch * n_ctx
    n_tot = n_lat + n_ctx_tot
    tm = 1024 if n_ctx_tot % 1024 == 0 else 512

    x_parts = (x.reshape(n_lat, d), ctx.reshape(n_ctx_tot, d))

    nrow = -(-(batch + 1) // 8) * 8
    cc = jnp.zeros((nrow, d), F32).at[:batch].set(c).at[batch].set(c_ctx)
    mod = _modulation(cc, w_mod, b_mod)
    modtab = jnp.pad(mod.reshape(DEPTH, nrow, 6, d), ((0, 0), (0, 0), (0, 2), (0, 0)))

    wr_t = w_router.astype(F32).T
    out = None
    pending = None
    for layer in range(DEPTH):
        need_ctx = layer < DEPTH - 1
        n_rows = n_tot if need_ctx else n_lat
        if pending is None:
            p_all = _project(x_parts, g_norm1[layer], modtab[layer], w_in[layer].astype(BF16), tm, n_lat, seq, batch)
        else:
            p_all, x_new = _combine_project(*pending, g_norm1[layer], modtab[layer], w_in[layer].astype(BF16),
                                            n_lat, seq, batch)
            x_parts = (x_new,)
        ya = _na_mixer(p_all, na_rpb[layer], batch, seq, n_ctx, need_ctx)
        yb = _ret_mixer(p_all, ret_log_decay[layer], batch, seq, n_ctx, need_ctx)
        yc = _gqa_mixer(p_all, gqa_q_gain[layer], gqa_k_gain[layer], batch, seq, n_ctx, need_ctx)
        yd = _diff_mixer(p_all, diff_lambda[layer], diff_subln[layer], layer, batch, seq, n_ctx, need_ctx)
        ys_lat = [ya[0], yb[0], yc[0], yd[0]]
        ys_ctx = [ya[1], yb[1], yc[1], yd[1]] if need_ctx else None
        x_mid, m_all, logits_t = _merge(x_parts, p_all, ys_lat, ys_ctx, w_branch[layer].astype(BF16),
                                        w_out[layer].astype(BF16), modtab[layer], g_norm2[layer], wr_t, MERGE_TM,
                                        n_rows, n_lat, seq, batch)
        idx2, w2, rank, counts = _route(logits_t, router_bias)
        pos, pad_slots, blk_e, n_slots = _dispatch_plan(idx2, rank, counts, MOE_BLK, CMB_TM)
        x_slots = _dispatch(m_all, pos, pad_slots, n_slots)
        y_slots = _experts(x_slots, blk_e, MOE_BLK, layer, w_gate_e, w_up_e, w_down_e)
        if need_ctx:
            pending = (x_mid, y_slots, pos, w2.T, modtab[layer])
        else:
            out = _combine(x_mid, y_slots, pos, w2.T, modtab[layer], g_final, n_lat, seq, batch)
    return out.reshape(batch, seq, d)
```

```python
import functools
import math

import numpy as np
import jax
import jax.numpy as jnp
from jax import lax
from jax.experimental import pallas as pl
from jax.experimental.pallas import tpu as pltpu

F32 = jnp.float32
BF16 = jnp.bfloat16

D_MODEL = 1024
DEPTH = 2
GRID_W = 64
NA_HEADS = 4
NA_DIM = 64
NA_WIN_H = 8
NA_WIN_W = 16
RET_HEADS = 4
RET_DK = 64
RET_CHUNK = 128
GQA_HEADS = 4
GQA_KV_HEADS = 2
GQA_DIM = 64
DIFF_HEADS = 4
DIFF_QK_DIM = 32
DIFF_V_DIM = 64
N_BRANCH = 4
BRANCH_W = 256
ROPE_THETA = 10000.0
EPS = 1e-6
NEG_INF = -1e30
N_EXPERTS = 16
N_GROUPS = 4
EXPERTS_PER_GROUP = 4
TOP_K = 2
D_EXPERT = 512

MIX_COLS = 3072
IN_COLS = MIX_COLS + N_BRANCH * D_MODEL
NA_Q, NA_K, NA_V = 0, 1, 2
RET_Q, RET_K, RET_V, RET_G = 3, 4, 5, 6
GQA_Q = 7
DIFF_Q, DIFF_K, DIFF_V = 9, 10, 11

LOG2E = math.log2(math.e)
NA_GROUP_ROWS = 4
NA_BAND_ROWS = 12

VMEM_LIMIT = 56 * 1024 * 1024
MOE_BLK = 512
CMB_TM = 512
DISPATCH_FOLD = 2
PROJ_TN = 1792
MERGE_TM = 512


ROW_TILE = D_MODEL // 128


def _store_tile_rows(ref, x):
    n = x.shape[0]
    for j in range(ROW_TILE):
        ref[pl.ds(j, n, stride=ROW_TILE), :] = x[:, j * 128:(j + 1) * 128]


def _load_tile_rows(ref, start, n):
    return jnp.concatenate([ref[pl.ds(start + j, n, stride=ROW_TILE), :] for j in range(ROW_TILE)], axis=-1)


def _dot(a, b):
    return jnp.dot(a, b, preferred_element_type=F32)


def _dot_nt(a, b):
    return lax.dot_general(a, b, (((1,), (1,)), ((), ())), preferred_element_type=F32)


def _sigmoid(x):
    return 1.0 / (1.0 + jnp.exp(-x))


def _params(*sem):
    return pltpu.CompilerParams(dimension_semantics=sem, vmem_limit_bytes=VMEM_LIMIT)


def _mod_kernel(c_ref, w_ref, b_ref, o_ref):
    c = c_ref[...]
    a = (c * _sigmoid(c)).astype(BF16)
    o_ref[...] = _dot(a, w_ref[...].astype(BF16)) + b_ref[...]


def _modulation(cc, w_mod, b_mod):
    nrow = cc.shape[0]
    depth, d, n6 = w_mod.shape
    tn = 1536
    return pl.pallas_call(
        _mod_kernel,
        grid=(depth, n6 // tn),
        in_specs=[pl.BlockSpec((nrow, d), lambda l, j: (0, 0)),
                  pl.BlockSpec((None, d, tn), lambda l, j: (l, 0, j)),
                  pl.BlockSpec((None, 1, tn), lambda l, j: (l, 0, j))],
        out_specs=pl.BlockSpec((None, nrow, tn), lambda l, j: (l, 0, j)),
        out_shape=jax.ShapeDtypeStruct((depth, nrow, n6), F32),
        compiler_params=_params("parallel", "arbitrary"),
        name="modulation",
    )(cc, w_mod, b_mod.reshape(depth, 1, n6))


def _mix_col_scale():
    cs = np.ones((1, IN_COLS), np.float32)
    cs[0, NA_Q * 256:(NA_Q + 1) * 256] = NA_DIM ** -0.5 * LOG2E
    cs[0, DIFF_Q * 256:(DIFF_Q + 1) * 256] = DIFF_QK_DIM ** -0.5 * LOG2E
    cs[0, RET_K * 256:(RET_K + 1) * 256] = RET_DK ** -0.5
    cs[0, MIX_COLS:] = 0.0
    return jnp.asarray(cs)


def _row_blocks(parts, tm, n_lat):
    d = parts[0].shape[1]
    nlat_blk = n_lat // tm
    if len(parts) == 1:
        return [pl.BlockSpec((tm, d), lambda i, *_: (i, 0))]
    return [pl.BlockSpec((tm, d), lambda i, *_: (jnp.minimum(i, nlat_blk - 1), 0)),
            pl.BlockSpec((tm, d), lambda i, *_: (jnp.maximum(i - nlat_blk, 0), 0))]


def _pick_rows(x_refs, nlat_blk):
    if len(x_refs) == 1:
        return x_refs[0][...]
    return jnp.where(pl.program_id(0) >= nlat_blk, x_refs[1][...], x_refs[0][...])


def _proj_kernel(nlat_blk, n_parts, *refs):
    x_refs, (g_ref, mod_ref, w_ref, cs_ref, o_ref, h_ref) = refs[:n_parts], refs[n_parts:]
    j = pl.program_id(1)

    @pl.when(j == 0)
    def _():
        x = _pick_rows(x_refs, nlat_blk)
        y = x * lax.rsqrt(jnp.mean(x * x, axis=-1, keepdims=True) + EPS) * g_ref[...]
        h_ref[...] = (y * (1.0 + mod_ref[1:2, :]) + mod_ref[0:1, :]).astype(BF16)

    acc = _dot(h_ref[...], w_ref[...])
    gate = 0.5 * jnp.tanh(0.5 * acc) + 0.5
    cs = cs_ref[...]
    o_ref[...] = jnp.where(cs > 0.0, acc * cs, gate).astype(BF16)


def _mod_index(tm, n_lat, seq, batch):
    nlat_blk = n_lat // tm
    bpb = seq // tm

    def index(i):
        return jnp.where(i < nlat_blk, i // bpb, batch)
    return index


def _project(x_parts, g, modtab, w_bf, tm, n_lat, seq, batch):
    n_tot = sum(p.shape[0] for p in x_parts)
    d = x_parts[0].shape[1]
    ncol = w_bf.shape[1]
    tn = PROJ_TN
    midx = _mod_index(tm, n_lat, seq, batch)
    return pl.pallas_call(
        functools.partial(_proj_kernel, n_lat // tm, len(x_parts)),
        grid=(n_tot // tm, ncol // tn),
        in_specs=[*_row_blocks(x_parts, tm, n_lat),
                  pl.BlockSpec((1, d), lambda i, j: (0, 0)),
                  pl.BlockSpec((None, 8, d), lambda i, j: (midx(i), 0, 0)),
                  pl.BlockSpec((d, tn), lambda i, j: (0, j)),
                  pl.BlockSpec((1, tn), lambda i, j: (0, j))],
        out_specs=pl.BlockSpec((tm, tn), lambda i, j: (i, j)),
        out_shape=jax.ShapeDtypeStruct((n_tot, ncol), BF16),
        scratch_shapes=[pltpu.VMEM((tm, d), BF16)],
        compiler_params=_params("parallel", "arbitrary"),
        name="norm_project",
    )(*x_parts, g.reshape(1, d), modtab, w_bf, _mix_col_scale())


def _na_kernel(need_ctx, rows, q_ref, k_ref, v_ref, qc_ref, kc_ref, vc_ref, tb_ref, y_ref, *rest):
    if need_ctx:
        yc_ref, vx, vcx = rest
    else:
        vx, vcx = rest
    gq = NA_GROUP_ROWS * GRID_W
    band = NA_BAND_ROWS * GRID_W
    n_groups = rows // NA_GROUP_ROWS
    n_pairs = NA_HEADS // 2
    masks = _lane_masks(2)
    low = masks[0] > 0

    for p in range(n_pairs):
        ps = slice(p * 128, (p + 1) * 128)
        vx[p] = jnp.concatenate([v_ref[:, ps], jnp.ones((v_ref.shape[0], 128), BF16)], axis=-1)
        vcx[p] = jnp.concatenate([vc_ref[:, ps], jnp.ones((vc_ref.shape[0], 128), BF16)], axis=-1)

    def group_body(g, carry):
        u = jnp.clip(g * NA_GROUP_ROWS - NA_WIN_H // 2, 0, rows - NA_BAND_ROWS)
        typ = jnp.where(g == 0, 0, jnp.where(g == n_groups - 1, 2, 1))
        qs = pl.multiple_of(g * gq, gq)
        ks = pl.multiple_of(u * GRID_W, GRID_W)
        for p in range(n_pairs):
            ps = slice(p * 128, (p + 1) * 128)
            q = q_ref[pl.ds(qs, gq), ps]
            kb = k_ref[pl.ds(ks, band), ps]
            kc = kc_ref[:, ps]
            outs = []
            for hh in range(2):
                qm = q * masks[hh]
                s_w = _dot_nt(qm, kb) + tb_ref[2 * p + hh, typ]
                s_c = _dot_nt(qm, kc)
                m = jnp.maximum(jnp.max(s_w, axis=-1, keepdims=True), jnp.max(s_c, axis=-1, keepdims=True))
                oe = (_dot(jnp.exp2((s_w - m).astype(BF16)), vx[p, pl.ds(ks, band), :])
                      + _dot(jnp.exp2((s_c - m).astype(BF16)), vcx[p]))
                outs.append(oe[:, :128] / oe[:, 128:])
            y_ref[pl.ds(qs, gq), ps] = jnp.where(low, outs[0], outs[1]).astype(BF16)
        return carry

    lax.fori_loop(0, n_groups, group_body, 0)

    if need_ctx:
        for p in range(n_pairs):
            ps = slice(p * 128, (p + 1) * 128)
            outs = [_softmax_pv(qc_ref[:, ps] * masks[hh], kc_ref[:, ps], vcx[p]) for hh in range(2)]
            yc_ref[:, ps] = jnp.where(low, outs[0], outs[1]).astype(BF16)


def _na_bias_table(rpb, rows):
    assert rows % NA_GROUP_ROWS == 0 and rows >= NA_BAND_ROWS + NA_GROUP_ROWS
    n_groups = rows // NA_GROUP_ROWS
    qc = np.arange(GRID_W)[:, None]
    kc = np.arange(GRID_W)[None, :]
    win_start = np.clip(qc - NA_WIN_W // 2, 0, GRID_W - NA_WIN_W)
    col_ok = (kc >= win_start) & (kc < win_start + NA_WIN_W)
    col_idx = np.clip(kc - qc, -(NA_WIN_W - 1), NA_WIN_W - 1) + NA_WIN_W - 1
    n_col = 2 * NA_WIN_W - 1
    onehot = jnp.asarray((np.arange(n_col)[:, None] == col_idx.reshape(1, -1)).astype(np.float32))
    picked = jnp.dot(rpb.astype(F32).reshape(-1, n_col), onehot, precision=lax.Precision.HIGHEST)
    picked = picked.reshape(NA_HEADS, 2 * NA_WIN_H - 1, GRID_W, GRID_W)
    tiles = jnp.where(col_ok[None, None], picked * LOG2E, NEG_INF)
    masked = jnp.full((NA_HEADS, GRID_W, GRID_W), NEG_INF, F32)
    tables = []
    for g in (0, 1, n_groups - 1):
        u = int(np.clip(g * NA_GROUP_ROWS - NA_WIN_H // 2, 0, rows - NA_BAND_ROWS))
        per_row = []
        for a in range(NA_GROUP_ROWS):
            r = g * NA_GROUP_ROWS + a
            r0 = int(np.clip(r - NA_WIN_H // 2, 0, rows - NA_WIN_H))
            pieces = [tiles[:, kr - r + NA_WIN_H - 1] if r0 <= kr < r0 + NA_WIN_H else masked
                      for kr in range(u, u + NA_BAND_ROWS)]
            per_row.append(jnp.concatenate(pieces, axis=-1))
        tables.append(jnp.concatenate(per_row, axis=1))
    return jnp.stack(tables, axis=1)


def _na_mixer(p_all, rpb, batch, seq, n_ctx, need_ctx):
    n_lat = batch * seq
    rows = seq // GRID_W
    tb = _na_bias_table(rpb, rows)
    cb = n_lat // n_ctx
    w = 256
    lat = lambda c: pl.BlockSpec((seq, w), lambda b, c=c: (b, c))
    ctx = lambda c: pl.BlockSpec((n_ctx, w), lambda b, c=c: (cb + b, c))
    out_shape = [jax.ShapeDtypeStruct((n_lat, w), BF16)]
    out_specs = [pl.BlockSpec((seq, w), lambda b: (b, 0))]
    if need_ctx:
        out_shape.append(jax.ShapeDtypeStruct((batch * n_ctx, w), BF16))
        out_specs.append(pl.BlockSpec((n_ctx, w), lambda b: (b, 0)))
    return pl.pallas_call(
        functools.partial(_na_kernel, need_ctx, rows),
        grid=(batch,),
        in_specs=[lat(NA_Q), lat(NA_K), lat(NA_V), ctx(NA_Q), ctx(NA_K), ctx(NA_V),
                  pl.BlockSpec(tb.shape, lambda b: (0, 0, 0, 0))],
        out_specs=out_specs,
        out_shape=out_shape,
        scratch_shapes=[pltpu.VMEM((NA_HEADS // 2, seq, 256), BF16), pltpu.VMEM((NA_HEADS // 2, n_ctx, 256), BF16)],
        compiler_params=_params("parallel"),
        name="na_mixer",
    )(p_all, p_all, p_all, p_all, p_all, p_all, tb)


def _ret_kernel(need_ctx, seq, n_ctx, lg_ref, q_ref, k_ref, v_ref, g_ref, qc_ref, kc_ref, vc_ref, gc_ref,
                cos_ref, sin_ref, perm_ref, bd_ref, y_ref, *rest):
    if need_ctx:
        yc_ref, qr, kr, krt, kct, o_f, o_b, oc_f, oc_b, dm, qd, kdt = rest
    else:
        qr, kr, krt, kct, o_f, o_b, oc_f, oc_b, dm, qd, kdt = rest
    ch = RET_CHUNK
    n_lat_ch = seq // ch
    n_ctx_ch = n_ctx // ch
    n_pairs = RET_HEADS // 2
    assert ch == 128 and 2 * RET_DK == 128

    def prep(i, carry):
        rs = pl.multiple_of(i * ch, ch)
        c = cos_ref[pl.ds(rs, ch), :]
        s = sin_ref[pl.ds(rs, ch), :]
        qr[pl.ds(rs, ch), :] = _rope_mxu(q_ref[pl.ds(rs, ch), :], c, s, perm_ref)
        k = _rope_mxu(k_ref[pl.ds(rs, ch), :], c, s, perm_ref)
        kr[pl.ds(rs, ch), :] = k
        for p in range(n_pairs):
            krt[i, p * 128:(p + 1) * 128, :] = k[:, p * 128:(p + 1) * 128].T
        return carry

    lax.fori_loop(0, n_lat_ch, prep, 0, unroll=2)
    for n in range(n_ctx_ch):
        kc = kc_ref[n * ch:(n + 1) * ch, :].astype(F32)
        for p in range(n_pairs):
            kct[n, p * 128:(p + 1) * 128, :] = kc[:, p * 128:(p + 1) * 128].T

    ri = lax.broadcasted_iota(jnp.int32, (ch, ch), 0)
    ci = lax.broadcasted_iota(jnp.int32, (ch, ch), 1)
    lag = (ri - ci).astype(F32)
    rowf = ri.astype(F32)
    colf = ci.astype(F32)
    low_lane = ci < RET_DK
    low_row = ri < RET_DK
    block_diag = jnp.where(low_lane == low_row, 1.0, 0.0)
    low, high = _lane_masks(2)

    chains = [(p, dirn) for p in range(n_pairs) for dirn in range(2)]
    cdec = []
    for c, (p, dirn) in enumerate(chains):
        lg_a = lg_ref[dirn, 2 * p]
        lg_b = lg_ref[dirn, 2 * p + 1]
        lg_lane = jnp.where(low_lane, lg_a, lg_b)
        lg_row = jnp.where(low_row, lg_a, lg_b)
        for hh, lg in enumerate((lg_a, lg_b)):
            if dirn == 0:
                keep = ri >= ci
                dm[2 * c + hh] = jnp.where(keep, jnp.exp(jnp.where(keep, lag, 0.0) * lg), 0.0)
            else:
                keep = ci >= ri
                dm[2 * c + hh] = jnp.where(keep, jnp.exp(jnp.where(keep, -lag, 0.0) * lg), 0.0)
        if dirn == 0:
            qd[c] = jnp.exp((rowf + 1.0) * lg_lane)
            kdt[c] = jnp.exp((ch - 1.0 - colf) * lg_row)
        else:
            qd[c] = jnp.exp((ch - rowf) * lg_lane)
            kdt[c] = jnp.exp(colf * lg_row)
        cdec.append(jnp.exp(ch * lg_lane[0:1, :]))

    def step(c, state, qi, ki, kti, vi):
        kb = ki.astype(BF16)
        inner_a = (_dot_nt((qi * low).astype(BF16), kb) * dm[2 * c]).astype(BF16)
        inner_b = (_dot_nt((qi * high).astype(BF16), kb) * dm[2 * c + 1]).astype(BF16)
        o = (jnp.where(low_lane, _dot(inner_a, vi), _dot(inner_b, vi))
             + _dot((qi * qd[c]).astype(BF16), state.astype(BF16)))
        state = state * cdec[c] + block_diag * _dot((kti * kdt[c]).astype(BF16), vi)
        return state, o

    states = []
    for c, (p, dirn) in enumerate(chains):
        ps = slice(p * 128, (p + 1) * 128)
        state = jnp.zeros((128, 128), F32)
        for n in (range(n_ctx_ch) if dirn == 0 else range(n_ctx_ch - 1, -1, -1)):
            rs = slice(n * ch, (n + 1) * ch)
            state, o = step(c, state, qc_ref[rs, ps].astype(F32), kc_ref[rs, ps].astype(F32), kct[n, ps, :],
                            vc_ref[rs, ps])
            if need_ctx:
                (oc_f if dirn == 0 else oc_b)[rs, ps] = o
        states.append(state)

    def lat_body(i, states):
        new = []
        for c, (p, dirn) in enumerate(chains):
            ps = slice(p * 128, (p + 1) * 128)
            n = i if dirn == 0 else n_lat_ch - 1 - i
            rs = pl.ds(pl.multiple_of(n * ch, ch), ch)
            state, o = step(c, states[c], qr[rs, ps], kr[rs, ps], krt[n, ps, :], v_ref[rs, ps])
            (o_f if dirn == 0 else o_b)[rs, ps] = o
            new.append(state)
        return tuple(new)

    lax.fori_loop(0, n_lat_ch, lat_body, tuple(states), unroll=2)

    def head_mean(x):
        hi = x.astype(BF16)
        lo = (x - hi.astype(F32)).astype(BF16)
        bd = bd_ref[...]
        return (_dot(hi, bd) + _dot(lo, bd)) * (1.0 / RET_DK)

    def finish(o, g):
        cen = o - head_mean(o)
        on = cen * lax.rsqrt(head_mean(cen * cen) + EPS)
        g = g.astype(F32)
        return (g * _sigmoid(g) * on).astype(BF16)

    def fin_body(i, carry):
        rs = pl.multiple_of(i * ch, ch)
        y_ref[pl.ds(rs, ch), :] = finish(o_f[pl.ds(rs, ch), :] + o_b[pl.ds(rs, ch), :], g_ref[pl.ds(rs, ch), :])
        return carry

    lax.fori_loop(0, n_lat_ch, fin_body, 0, unroll=2)
    if need_ctx:
        for n in range(n_ctx_ch):
            rs = slice(n * ch, (n + 1) * ch)
            yc_ref[rs, :] = finish(oc_f[rs, :] + oc_b[rs, :], gc_ref[rs, :])


def _rope_tables(seq, head_dim, width, dist):
    half = head_dim // 2
    nf = half // 2
    assert nf == dist
    inv = 1.0 / (ROPE_THETA ** (np.arange(nf, dtype=np.float32) / nf))
    t = np.arange(seq)
    rows = (t // GRID_W).astype(np.float32)[:, None] * inv[None, :]
    cols = (t % GRID_W).astype(np.float32)[:, None] * inv[None, :]
    cos = np.concatenate([np.cos(rows), np.cos(rows), np.cos(cols), np.cos(cols)], axis=-1)
    sin = np.concatenate([-np.sin(rows), np.sin(rows), -np.sin(cols), np.sin(cols)], axis=-1)
    reps = width // head_dim
    return (jnp.asarray(np.tile(cos, (1, reps)), F32), jnp.asarray(np.tile(sin, (1, reps)), F32))


def _ret_mixer(p_all, log_decay, batch, seq, n_ctx, need_ctx):
    n_lat = batch * seq
    cb = n_lat // n_ctx
    w = 256
    ch = RET_CHUNK
    log_gamma = jnp.log1p(-jnp.exp(log_decay.astype(F32)))
    cos, sin = _rope_tables(seq, RET_DK, w, 16)
    lat = lambda c: pl.BlockSpec((seq, w), lambda b, c=c: (b, c))
    ctx = lambda c: pl.BlockSpec((n_ctx, w), lambda b, c=c: (cb + b, c))
    whole = pl.BlockSpec((seq, w), lambda b: (0, 0))
    out_shape = [jax.ShapeDtypeStruct((n_lat, w), BF16)]
    out_specs = [pl.BlockSpec((seq, w), lambda b: (b, 0))]
    if need_ctx:
        out_shape.append(jax.ShapeDtypeStruct((batch * n_ctx, w), BF16))
        out_specs.append(pl.BlockSpec((n_ctx, w), lambda b: (b, 0)))
    return pl.pallas_call(
        functools.partial(_ret_kernel, need_ctx, seq, n_ctx),
        grid=(batch,),
        in_specs=[pl.BlockSpec(memory_space=pltpu.SMEM),
                  lat(RET_Q), lat(RET_K), lat(RET_V), lat(RET_G),
                  ctx(RET_Q), ctx(RET_K), ctx(RET_V), ctx(RET_G), whole, whole,
                  pl.BlockSpec((w, w), lambda b: (0, 0)), pl.BlockSpec((w, w), lambda b: (0, 0))],
        out_specs=out_specs,
        out_shape=out_shape,
        scratch_shapes=[pltpu.VMEM((seq, w), F32), pltpu.VMEM((seq, w), F32),
                        pltpu.VMEM((seq // ch, w, ch), F32), pltpu.VMEM((n_ctx // ch, w, ch), F32),
                        pltpu.VMEM((seq, w), F32), pltpu.VMEM((seq, w), F32),
                        pltpu.VMEM((n_ctx, w), F32), pltpu.VMEM((n_ctx, w), F32),
                        pltpu.VMEM((2 * RET_HEADS, ch, ch), F32),
                        pltpu.VMEM((RET_HEADS, ch, 128), F32), pltpu.VMEM((RET_HEADS, 128, ch), F32)],
        compiler_params=_params("parallel"),
        name="retention_mixer",
    )(log_gamma, p_all, p_all, p_all, p_all, p_all, p_all, p_all, p_all, cos, sin,
      _swap_matrix(w, 16), _block_ones(w, RET_DK))


def _swap_matrix(width, dist):
    i = np.arange(width)
    partner = np.where(i % (2 * dist) < dist, i + dist, i - dist)
    p = np.zeros((width, width), np.float32)
    p[partner, i] = 1.0
    return jnp.asarray(p, BF16)


def _block_ones(width, block):
    i = np.arange(width)
    return jnp.asarray((i[:, None] // block == i[None, :] // block).astype(np.float32), BF16)


def _sumsq_blocks(xf, bd_ref):
    sq = xf * xf
    hi = sq.astype(BF16)
    lo = (sq - hi.astype(F32)).astype(BF16)
    bd = bd_ref[...]
    return _dot(hi, bd) + _dot(lo, bd)


def _rope_mxu(x, cos, sin, perm_ref):
    return x.astype(F32) * cos + _dot(x, perm_ref[...]) * sin


def _softmax_pv(qm, k, v_ones):
    s = _dot_nt(qm, k)
    p = jnp.exp2((s - jnp.max(s, axis=-1, keepdims=True)).astype(BF16))
    oe = _dot(p, v_ones)
    return oe[:, :128] / oe[:, 128:]


def _lane_masks(n_parts):
    lane = lax.broadcasted_iota(jnp.int32, (1, 128), 1)
    return [jnp.where(lane // (128 // n_parts) == i, 1.0, 0.0).astype(BF16) for i in range(n_parts)]


def _gqa_kernel(n_lat_k, n_ctx, *refs):
    if n_lat_k:
        (q_ref, k_ref, v_ref, kc_ref, vc_ref, qa_ref, qb_ref, ka_ref, kb_ref, kg_ref, pq_ref, pk_ref,
         bdq_ref, bdk_ref, y_ref, kp, vx) = refs
    else:
        q_ref, kc_ref, vc_ref, qa_ref, kg_ref, bdq_ref, bdk_ref, y_ref, kp, vx = refs
    dim = GQA_DIM
    nk = n_lat_k + n_ctx

    def inv_rms(xf, bd_ref):
        return lax.rsqrt(_sumsq_blocks(xf, bd_ref) * (1.0 / dim) + EPS)

    @pl.when(pl.program_id(1) == 0)
    def _():
        def put(rs, kn, v):
            ones = jnp.ones(v.shape, BF16)
            kp[0, rs, :] = kn.astype(BF16)
            kp[1, rs, :] = pltpu.roll(kn, dim, 1).astype(BF16)
            vx[0, rs, :] = jnp.concatenate([v, ones], axis=-1)
            vx[1, rs, :] = jnp.concatenate([pltpu.roll(v.astype(F32), dim, 1).astype(BF16), ones], axis=-1)

        kc = kc_ref[...]
        kcf = kc.astype(F32)
        put(slice(n_lat_k, nk), kcf * kg_ref[...] * inv_rms(kcf, bdk_ref), vc_ref[...])
        ck = 512
        for c in range(n_lat_k // ck):
            rs = slice(c * ck, (c + 1) * ck)
            k = k_ref[rs, :]
            put(rs, _rope_mxu(k, ka_ref[rs, :], kb_ref[rs, :], pk_ref) * inv_rms(k.astype(F32), bdk_ref), v_ref[rs, :])

    q = q_ref[...]
    qf = q.astype(F32)
    qn = _rope_mxu(q, qa_ref[...], qb_ref[...], pq_ref) if n_lat_k else qf * qa_ref[...]
    qn = (qn * inv_rms(qf, bdq_ref)).astype(BF16)
    low, high = _lane_masks(2)
    outs = []
    for g in range(GQA_KV_HEADS):
        qv = qn[:, g * 128:(g + 1) * 128]
        o_low = _softmax_pv(qv * low, kp[g], vx[g])
        o_high = _softmax_pv(qv * high, kp[1 - g], vx[1 - g])
        outs.append(jnp.where(low > 0, o_low, o_high))
    y_ref[...] = jnp.concatenate(outs, axis=-1).astype(BF16)


def _gqa_mixer(p_all, q_gain, k_gain, batch, seq, n_ctx, need_ctx):
    n_lat = batch * seq
    cb = n_lat // n_ctx
    dim = GQA_DIM
    qw = GQA_HEADS * dim
    kvw = GQA_KV_HEADS * dim
    assert qw == 256 and kvw == 128

    def swapped(g):
        return g.reshape(-1, 2, 16)[:, ::-1].reshape(1, -1)

    qg = jnp.tile(q_gain.astype(F32) * (dim ** -0.5 * LOG2E), GQA_HEADS).reshape(1, qw)
    kg = jnp.tile(k_gain.astype(F32), GQA_KV_HEADS).reshape(1, kvw)
    cq, sq = _rope_tables(seq, dim, qw, 16)
    ck, sk = _rope_tables(seq, dim, kvw, 16)
    qa, qb = cq * qg, sq * swapped(qg)
    ka, kb = ck * kg, sk * swapped(kg)
    pq, pk = _swap_matrix(qw, 16), _swap_matrix(kvw, 16)
    bdq, bdk = _block_ones(qw, dim), _block_ones(kvw, dim)
    tq = 512
    nqb = seq // tq
    const = lambda shape: pl.BlockSpec(shape, lambda b, i: (0,) * len(shape))
    scratch = lambda nk: [pltpu.VMEM((2, nk, kvw), BF16), pltpu.VMEM((2, nk, 2 * kvw), BF16)]
    y_lat = pl.pallas_call(
        functools.partial(_gqa_kernel, seq, n_ctx),
        grid=(batch, nqb),
        in_specs=[pl.BlockSpec((tq, qw), lambda b, i: (b * nqb + i, GQA_Q)),
                  pl.BlockSpec((seq, kvw), lambda b, i: (b, 16)),
                  pl.BlockSpec((seq, kvw), lambda b, i: (b, 17)),
                  pl.BlockSpec((n_ctx, kvw), lambda b, i: (cb + b, 16)),
                  pl.BlockSpec((n_ctx, kvw), lambda b, i: (cb + b, 17)),
                  pl.BlockSpec((tq, qw), lambda b, i: (i, 0)),
                  pl.BlockSpec((tq, qw), lambda b, i: (i, 0)),
                  const((seq, kvw)), const((seq, kvw)), const((1, kvw)),
                  const((qw, qw)), const((kvw, kvw)), const((qw, qw)), const((kvw, kvw))],
        out_specs=pl.BlockSpec((tq, qw), lambda b, i: (b * nqb + i, 0)),
        out_shape=jax.ShapeDtypeStruct((n_lat, qw), BF16),
        scratch_shapes=scratch(seq + n_ctx),
        compiler_params=_params("parallel", "arbitrary"),
        name="gqa_mixer",
    )(p_all, p_all, p_all, p_all, p_all, qa, qb, ka, kb, kg, pq, pk, bdq, bdk)
    if not need_ctx:
        return y_lat, None
    y_ctx = pl.pallas_call(
        functools.partial(_gqa_kernel, 0, n_ctx),
        grid=(batch, 1),
        in_specs=[pl.BlockSpec((n_ctx, qw), lambda b, i: (cb + b, GQA_Q)),
                  pl.BlockSpec((n_ctx, kvw), lambda b, i: (cb + b, 16)),
                  pl.BlockSpec((n_ctx, kvw), lambda b, i: (cb + b, 17)),
                  const((1, qw)), const((1, kvw)), const((qw, qw)), const((kvw, kvw))],
        out_specs=pl.BlockSpec((n_ctx, qw), lambda b, i: (b, 0)),
        out_shape=jax.ShapeDtypeStruct((batch * n_ctx, qw), BF16),
        scratch_shapes=scratch(n_ctx),
        compiler_params=_params("parallel", "arbitrary"),
        name="gqa_mixer_ctx",
    )(p_all, p_all, p_all, qg, kg, bdq, bdk)
    return y_lat, y_ctx


def _diff_kernel(n_lat_k, n_ctx, lam_init, *refs):
    if n_lat_k:
        (q_ref, k_ref, v_ref, kc_ref, vc_ref, lp_ref, sg_ref, cq_ref, sq_ref, ck_ref, sk_ref, perm_ref, bd_ref,
         y_ref, kp, vx) = refs
    else:
        q_ref, kc_ref, vc_ref, lp_ref, sg_ref, bd_ref, y_ref, kp, vx = refs
    nk = n_lat_k + n_ctx
    n_pairs = DIFF_HEADS // 2

    @pl.when(pl.program_id(1) == 0)
    def _():
        def put(rs, k, v):
            kp[rs, :] = k
            ones = jnp.ones((v.shape[0], 128), BF16)
            for pr in range(n_pairs):
                vx[pr, rs, :] = jnp.concatenate([v[:, pr * 128:(pr + 1) * 128], ones], axis=-1)

        put(slice(n_lat_k, nk), kc_ref[...], vc_ref[...])
        ck = 512
        for c in range(n_lat_k // ck):
            rs = slice(c * ck, (c + 1) * ck)
            put(rs, _rope_mxu(k_ref[rs, :], ck_ref[rs, :], sk_ref[rs, :], perm_ref).astype(BF16), v_ref[rs, :])

    lp = lp_ref[...]
    lam = (jnp.exp(jnp.sum(lp[0:1, :] * lp[1:2, :], axis=-1, keepdims=True))
           - jnp.exp(jnp.sum(lp[2:3, :] * lp[3:4, :], axis=-1, keepdims=True)) + lam_init)

    q = q_ref[...]
    if n_lat_k:
        q = _rope_mxu(q, cq_ref[...], sq_ref[...], perm_ref).astype(BF16)
    quarter = _lane_masks(4)
    low = _lane_masks(2)[0]
    outs = []
    for pr in range(n_pairs):
        ps = slice(pr * 128, (pr + 1) * 128)
        qv = q[:, ps]
        kv = kp[:, ps]
        o_head = [_softmax_pv(qv * quarter[2 * hh], kv, vx[pr]) - lam * _softmax_pv(qv * quarter[2 * hh + 1], kv, vx[pr])
                  for hh in range(2)]
        outs.append(jnp.where(low > 0, o_head[0], o_head[1]))
    o = jnp.concatenate(outs, axis=-1)
    inv = lax.rsqrt(_sumsq_blocks(o, bd_ref) * (1.0 / DIFF_V_DIM) + EPS)
    y_ref[...] = (o * inv * sg_ref[...] * (1.0 - lam_init)).astype(BF16)


def _diff_mixer(p_all, lam_params, subln, layer_idx, batch, seq, n_ctx, need_ctx):
    n_lat = batch * seq
    cb = n_lat // n_ctx
    lam_init = 0.8 - 0.6 * math.exp(-0.3 * layer_idx)
    lp = jnp.zeros((8, 128), F32).at[:4, :DIFF_QK_DIM].set(lam_params.astype(F32))
    w = DIFF_HEADS * DIFF_V_DIM
    assert w == 256 and DIFF_HEADS * 2 * DIFF_QK_DIM == w
    sg = jnp.tile(subln.astype(F32), DIFF_HEADS).reshape(1, w)
    cq, sq = _rope_tables(seq, DIFF_QK_DIM, w, 8)
    perm = _swap_matrix(w, 8)
    bd = _block_ones(w, DIFF_V_DIM)
    tq = 512
    nqb = seq // tq
    const = lambda shape: pl.BlockSpec(shape, lambda b, i: (0,) * len(shape))
    scratch = lambda nk: [pltpu.VMEM((nk, w), BF16), pltpu.VMEM((DIFF_HEADS // 2, nk, 256), BF16)]
    y_lat = pl.pallas_call(
        functools.partial(_diff_kernel, seq, n_ctx, lam_init),
        grid=(batch, nqb),
        in_specs=[pl.BlockSpec((tq, w), lambda b, i: (b * nqb + i, DIFF_Q)),
                  pl.BlockSpec((seq, w), lambda b, i: (b, DIFF_K)),
                  pl.BlockSpec((seq, w), lambda b, i: (b, DIFF_V)),
                  pl.BlockSpec((n_ctx, w), lambda b, i: (cb + b, DIFF_K)),
                  pl.BlockSpec((n_ctx, w), lambda b, i: (cb + b, DIFF_V)),
                  const((8, 128)), const((1, w)),
                  pl.BlockSpec((tq, w), lambda b, i: (i, 0)),
                  pl.BlockSpec((tq, w), lambda b, i: (i, 0)),
                  const((seq, w)), const((seq, w)), const((w, w)), const((w, w))],
        out_specs=pl.BlockSpec((tq, w), lambda b, i: (b * nqb + i, 0)),
        out_shape=jax.ShapeDtypeStruct((n_lat, w), BF16),
        scratch_shapes=scratch(seq + n_ctx),
        compiler_params=_params("parallel", "arbitrary"),
        name="diff_mixer",
    )(p_all, p_all, p_all, p_all, p_all, lp, sg, cq, sq, cq, sq, perm, bd)
    if not need_ctx:
        return y_lat, None
    y_ctx = pl.pallas_call(
        functools.partial(_diff_kernel, 0, n_ctx, lam_init),
        grid=(batch, 1),
        in_specs=[pl.BlockSpec((n_ctx, w), lambda b, i: (cb + b, DIFF_Q)),
                  pl.BlockSpec((n_ctx, w), lambda b, i: (cb + b, DIFF_K)),
                  pl.BlockSpec((n_ctx, w), lambda b, i: (cb + b, DIFF_V)),
                  const((8, 128)), const((1, w)), const((w, w))],
        out_specs=pl.BlockSpec((n_ctx, w), lambda b, i: (b, 0)),
        out_shape=jax.ShapeDtypeStruct((batch * n_ctx, w), BF16),
        scratch_shapes=scratch(n_ctx),
        compiler_params=_params("parallel", "arbitrary"),
        name="diff_mixer_ctx",
    )(p_all, p_all, p_all, lp, sg, bd)
    return y_lat, y_ctx


def _merge_kernel(nlat_blk, has_ctx, n_parts, *refs):
    x_refs, refs = refs[:n_parts], refs[n_parts:]
    g_refs = refs[:N_BRANCH]
    y_refs = refs[N_BRANCH:2 * N_BRANCH]
    refs = refs[2 * N_BRANCH:]
    if has_ctx:
        yc_refs, refs = refs[:N_BRANCH], refs[N_BRANCH:]
        is_ctx = pl.program_id(0) >= nlat_blk
    wb_ref, wo_ref, mod_ref, gn_ref, wr_ref, xo_ref, m_ref, lg_ref = refs
    acc = None
    for n in range(N_BRANCH):
        y = y_refs[n][...]
        if has_ctx:
            y = jnp.where(is_ctx, yc_refs[n][...], y)
        term = g_refs[n][...].astype(F32) * _dot(y, wb_ref[n])
        acc = term if acc is None else acc + term
    y = _dot(acc.astype(BF16), wo_ref[...])
    x = _pick_rows(x_refs, nlat_blk) + mod_ref[2:3, :] * y
    xo_ref[...] = x
    xn = x * lax.rsqrt(jnp.mean(x * x, axis=-1, keepdims=True) + EPS) * gn_ref[...]
    m = xn * (1.0 + mod_ref[4:5, :]) + mod_ref[3:4, :]
    _store_tile_rows(m_ref, m)
    m_hi = m.astype(BF16)
    m_lo = (m - m_hi.astype(F32)).astype(BF16)
    w = wr_ref[...]
    w_hi = w.astype(BF16)
    w_lo = (w - w_hi.astype(F32)).astype(BF16)
    both = _dot_nt(jnp.concatenate([w_hi, w_lo], axis=0), m_hi)
    lg_ref[...] = both[:N_EXPERTS] + (_dot_nt(w_hi, m_lo) + both[N_EXPERTS:])


def _merge(x_parts, p_all, ys_lat, ys_ctx, wb_bf, wo_bf, modtab, gn, wr_t, tm, n_rows, n_lat, seq, batch):
    d = D_MODEL
    midx = _mod_index(tm, n_lat, seq, batch)
    nlat_blk = n_lat // tm
    has_ctx = ys_ctx is not None
    gate = lambda n: pl.BlockSpec((tm, d), lambda i, n=n: (i, MIX_COLS // d + n))
    y_specs = [pl.BlockSpec((tm, BRANCH_W), lambda i: (jnp.minimum(i, nlat_blk - 1), 0))] * N_BRANCH
    ys = list(ys_lat)
    if has_ctx:
        y_specs += [pl.BlockSpec((tm, BRANCH_W), lambda i: (jnp.maximum(i - nlat_blk, 0), 0))] * N_BRANCH
        ys += list(ys_ctx)
    return pl.pallas_call(
        functools.partial(_merge_kernel, nlat_blk, has_ctx, len(x_parts)),
        grid=(n_rows // tm,),
        in_specs=[*_row_blocks(x_parts, tm, n_lat),
                  gate(0), gate(1), gate(2), gate(3), *y_specs,
                  pl.BlockSpec((N_BRANCH, BRANCH_W, d), lambda i: (0, 0, 0)),
                  pl.BlockSpec((d, d), lambda i: (0, 0)),
                  pl.BlockSpec((None, 8, d), lambda i: (midx(i), 0, 0)),
                  pl.BlockSpec((1, d), lambda i: (0, 0)),
                  pl.BlockSpec((N_EXPERTS, d), lambda i: (0, 0))],
        out_specs=[pl.BlockSpec((tm, d), lambda i: (i, 0)),
                   pl.BlockSpec((tm * ROW_TILE, 128), lambda i: (i, 0)),
                   pl.BlockSpec((N_EXPERTS, tm), lambda i: (0, i))],
        out_shape=[jax.ShapeDtypeStruct((n_rows, d), F32),
                   jax.ShapeDtypeStruct((n_rows * ROW_TILE, 128), F32),
                   jax.ShapeDtypeStruct((N_EXPERTS, n_rows), F32)],
        compiler_params=_params("parallel"),
        name="merge_norm_route",
    )(*x_parts, p_all, p_all, p_all, p_all, *ys, wb_bf, wo_bf, modtab, gn.reshape(1, d), wr_t)


def _route_kernel(lg_ref, b_ref, tri_ref, idx_ref, w_ref, rank_ref, cnt_ref):
    s = _sigmoid(lg_ref[...])
    sel = s + b_ref[...]
    row = lambda a, e: a[e:e + 1, :]
    gsz = EXPERTS_PER_GROUP
    g_idx = None
    best = None
    for g in range(N_GROUPS):
        v = [row(sel, g * gsz + i) for i in range(gsz)]
        score = None
        for i in range(gsz):
            for j in range(i + 1, gsz):
                pair = v[i] + v[j]
                score = pair if score is None else jnp.maximum(score, pair)
        if g == 0:
            best, g_idx = score, jnp.zeros(score.shape, jnp.int32)
        else:
            better = score > best
            best = jnp.where(better, score, best)
            g_idx = jnp.where(better, g, g_idx)

    def in_group(a, i):
        out = row(a, i)
        for g in range(1, N_GROUPS):
            out = jnp.where(g_idx == g, row(a, g * gsz + i), out)
        return out

    v = [in_group(sel, i) for i in range(gsz)]
    sv = [in_group(s, i) for i in range(gsz)]

    def arg_first_max(vals):
        bv, bi = vals[0], jnp.zeros(vals[0].shape, jnp.int32)
        for i in range(1, gsz):
            better = vals[i] > bv
            bv = jnp.where(better, vals[i], bv)
            bi = jnp.where(better, i, bi)
        return bi

    i1 = arg_first_max(v)
    i2 = arg_first_max([jnp.where(i1 == i, -jnp.inf, v[i]) for i in range(gsz)])

    def pick(vals, idx):
        out = vals[0]
        for i in range(1, gsz):
            out = jnp.where(idx == i, vals[i], out)
        return out

    w1 = pick(sv, i1)
    w2 = pick(sv, i2)
    tot = w1 + w2
    e1 = g_idx * gsz + i1
    e2 = g_idx * gsz + i2
    idx_ref[0:1, :] = e1
    idx_ref[1:2, :] = e2
    w_ref[0:1, :] = w1 / tot
    w_ref[1:2, :] = w2 / tot

    @pl.when(pl.program_id(0) == 0)
    def _():
        cnt_ref[...] = jnp.zeros(cnt_ref.shape, cnt_ref.dtype)

    n_e, tn = s.shape
    erow = lax.broadcasted_iota(jnp.int32, (n_e, tn), 0)
    oh1 = jnp.where(erow == e1, 1.0, 0.0)
    oh2 = jnp.where(erow == e2, 1.0, 0.0)
    oh = (oh1 + oh2).astype(BF16)
    base = cnt_ref[...]
    r1, r2 = [], []
    for c in range(tn // 128):
        cs = slice(c * 128, (c + 1) * 128)
        before = base + _dot(oh[:, cs], tri_ref[...])
        r1.append(jnp.sum(oh1[:, cs] * before, axis=0, keepdims=True))
        r2.append(jnp.sum(oh2[:, cs] * before, axis=0, keepdims=True))
        base = base + jnp.sum(oh[:, cs].astype(F32), axis=1, keepdims=True)
    cnt_ref[...] = base
    rank_ref[0:1, :] = jnp.concatenate(r1, axis=-1).astype(jnp.int32)
    rank_ref[1:2, :] = jnp.concatenate(r2, axis=-1).astype(jnp.int32)


def _route(logits_t, router_bias):
    e, n = logits_t.shape
    tn = math.gcd(n, 2048)
    i = np.arange(128)
    tri = jnp.asarray((i[:, None] < i[None, :]).astype(np.float32), BF16)
    idx2, w2, rank, cnt = pl.pallas_call(
        _route_kernel,
        grid=(n // tn,),
        in_specs=[pl.BlockSpec((e, tn), lambda i: (0, i)),
                  pl.BlockSpec((e, 1), lambda i: (0, 0)),
                  pl.BlockSpec((128, 128), lambda i: (0, 0))],
        out_specs=[pl.BlockSpec((TOP_K, tn), lambda i: (0, i)),
                   pl.BlockSpec((TOP_K, tn), lambda i: (0, i)),
                   pl.BlockSpec((TOP_K, tn), lambda i: (0, i)),
                   pl.BlockSpec((e, 128), lambda i: (0, 0))],
        out_shape=[jax.ShapeDtypeStruct((TOP_K, n), jnp.int32),
                   jax.ShapeDtypeStruct((TOP_K, n), F32),
                   jax.ShapeDtypeStruct((TOP_K, n), jnp.int32),
                   jax.ShapeDtypeStruct((e, 128), F32)],
        compiler_params=_params("arbitrary"),
        name="route_top2",
    )(logits_t, router_bias.astype(F32).reshape(e, 1), tri)
    return idx2, w2, rank, cnt[:, 0].astype(jnp.int32)


def _dispatch_kernel(n_tok_steps, tm, pos_ref, m_ref, xs_out, zrow, sem):
    i = pl.program_id(0)
    rt = ROW_TILE
    n = pos_ref.shape[1]

    def wait():
        pltpu.make_async_copy(xs_out.at[pl.ds(0, n * rt)], xs_out.at[pl.ds(0, n * rt)], sem).wait()

    @pl.when(i < n_tok_steps)
    def _():
        for j in range(n):
            part, within = divmod(j, TOP_K * tm)
            row = part * tm + within % tm
            dst = pl.multiple_of(pos_ref[i, j], rt)
            pltpu.make_async_copy(m_ref.at[pl.ds(row * rt, rt)], xs_out.at[pl.ds(dst, rt)],
                                  sem).start(priority=j % 2)
        wait()

    @pl.when(i >= n_tok_steps)
    def _():
        zrow[...] = jnp.zeros(zrow.shape, zrow.dtype)
        for j in range(n):
            dst = pl.multiple_of(pos_ref[i, j], rt)
            pltpu.make_async_copy(zrow, xs_out.at[pl.ds(dst, rt)], sem).start(priority=j % 2)
        wait()


def _dispatch(m_tiles, pos, pad_slots, n_slots):
    rt = ROW_TILE
    fold = DISPATCH_FOLD
    tm = pos.shape[1] // TOP_K
    assert pos.shape[0] % fold == 0 and pad_slots.shape[0] % fold == 0
    n_tok_steps = pos.shape[0] // fold
    steps = jnp.concatenate([pos, pad_slots], axis=0).reshape(-1, fold * pos.shape[1])
    grid_spec = pltpu.PrefetchScalarGridSpec(
        num_scalar_prefetch=1,
        grid=(steps.shape[0],),
        in_specs=[pl.BlockSpec((fold * tm * rt, 128), lambda i, pos: (jnp.minimum(i, n_tok_steps - 1), 0))],
        out_specs=pl.BlockSpec(memory_space=pl.ANY),
        scratch_shapes=[pltpu.VMEM((rt, 128), F32), pltpu.SemaphoreType.DMA(())])
    return pl.pallas_call(
        functools.partial(_dispatch_kernel, n_tok_steps, tm),
        grid_spec=grid_spec,
        out_shape=jax.ShapeDtypeStruct((n_slots * rt, 128), F32),
        compiler_params=_params("arbitrary"),
        name="moe_dispatch",
    )(steps, m_tiles)


def _expert_kernel(be_ref, x_ref, wg_ref, wu_ref, wd_ref, y_ref):
    blk = x_ref.shape[0] // ROW_TILE
    x = _load_tile_rows(x_ref, 0, blk).astype(BF16)
    a = _dot(x, wg_ref[...].astype(BF16))
    h = a * _sigmoid(a) * _dot(x, wu_ref[...].astype(BF16))
    _store_tile_rows(y_ref, _dot(h.astype(BF16), wd_ref[...].astype(BF16)))


def _experts(x_slots, blk_e, blk, layer, w_gate, w_up, w_down):
    rt = ROW_TILE
    d = D_MODEL
    n_slots = x_slots.shape[0] // rt
    grid_spec = pltpu.PrefetchScalarGridSpec(
        num_scalar_prefetch=1,
        grid=(n_slots // blk,),
        in_specs=[pl.BlockSpec((blk * rt, 128), lambda i, be: (i, 0)),
                  pl.BlockSpec((None, None, d, D_EXPERT), lambda i, be: (layer, be[i], 0, 0)),
                  pl.BlockSpec((None, None, d, D_EXPERT), lambda i, be: (layer, be[i], 0, 0)),
                  pl.BlockSpec((None, None, D_EXPERT, d), lambda i, be: (layer, be[i], 0, 0))],
        out_specs=pl.BlockSpec((blk * rt, 128), lambda i, be: (i, 0)))
    return pl.pallas_call(
        _expert_kernel,
        grid_spec=grid_spec,
        out_shape=jax.ShapeDtypeStruct((n_slots * rt, 128), F32),
        compiler_params=_params("arbitrary"),
        name="expert_ffn",
    )(blk_e, x_slots, w_gate, w_up, w_down)


def _combine_kernel(pos_ref, y_hbm, x_ref, w_ref, mod_ref, gf_ref, o_ref, ybuf, sem):
    i = pl.program_id(0)
    last = pl.num_programs(0) - 1
    slot = i % 2
    rt = ROW_TILE
    tm = x_ref.shape[0]
    n = TOP_K * tm

    def issue(step, dst_slot):
        for j in range(n):
            src = pl.multiple_of(pos_ref[step, j], rt)
            pltpu.make_async_copy(y_hbm.at[pl.ds(src, rt)], ybuf.at[dst_slot, pl.ds(j * rt, rt)],
                                  sem.at[dst_slot]).start(priority=j % 2)

    def wait(dst_slot):
        pltpu.make_async_copy(y_hbm.at[pl.ds(0, n * rt)], ybuf.at[dst_slot], sem.at[dst_slot]).wait()

    @pl.when(i == 0)
    def _():
        issue(0, 0)

    wait(slot)
    issue(jnp.minimum(i + 1, last), 1 - slot)
    w = w_ref[...]
    yb = ybuf.at[slot]
    f = w[:, 0:1] * _load_tile_rows(yb, 0, tm) + w[:, 1:2] * _load_tile_rows(yb, tm * rt, tm)
    x = x_ref[...] + mod_ref[5:6, :] * f
    o_ref[...] = x * lax.rsqrt(jnp.mean(x * x, axis=-1, keepdims=True) + EPS) * gf_ref[...]

    @pl.when(i == last)
    def _():
        wait(1 - slot)


def _combine(x_rows, y_slots, pos, w_tok, modtab, g_final, n_lat, seq, batch):
    n_rows, d = x_rows.shape
    n_steps, n = pos.shape
    tm = n // TOP_K
    midx = _mod_index(tm, n_lat, seq, batch)
    grid_spec = pltpu.PrefetchScalarGridSpec(
        num_scalar_prefetch=1,
        grid=(n_steps,),
        in_specs=[pl.BlockSpec(memory_space=pl.ANY),
                  pl.BlockSpec((tm, d), lambda i, pos: (i, 0)),
                  pl.BlockSpec((tm, TOP_K), lambda i, pos: (i, 0)),
                  pl.BlockSpec((None, 8, d), lambda i, pos: (midx(i), 0, 0)),
                  pl.BlockSpec((1, d), lambda i, pos: (0, 0))],
        out_specs=pl.BlockSpec((tm, d), lambda i, pos: (i, 0)),
        scratch_shapes=[pltpu.VMEM((2, n * ROW_TILE, 128), F32), pltpu.SemaphoreType.DMA((2,))])
    return pl.pallas_call(
        _combine_kernel,
        grid_spec=grid_spec,
        out_shape=jax.ShapeDtypeStruct((n_rows, d), F32),
        compiler_params=_params("arbitrary"),
        name="moe_combine",
    )(pos, y_slots, x_rows, w_tok, modtab, g_final.astype(F32).reshape(1, d))


def _combine_project_kernel(n_col_steps, pos_ref, y_hbm, x_ref, wt_ref, modp_ref, g_ref, mod_ref, w_ref, cs_ref,
                            p_ref, xo_ref, h_ref, ybuf, sem):
    i = pl.program_id(0)
    j = pl.program_id(1)
    last = pl.num_programs(0) - 1
    rt = ROW_TILE
    per_step = ybuf.shape[1] // rt
    per_k = n_col_steps // TOP_K

    def issue(block, step):
        for jj in range(per_step):
            src = pl.multiple_of(pos_ref[block * n_col_steps + step, jj], rt)
            pltpu.make_async_copy(y_hbm.at[pl.ds(src, rt)], ybuf.at[step, pl.ds(jj * rt, rt)],
                                  sem).start(priority=jj % 2)

    def wait():
        for step in range(n_col_steps):
            pltpu.make_async_copy(y_hbm.at[pl.ds(0, per_step * rt)], ybuf.at[step], sem).wait()

    @pl.when((i == 0) & (j == 0))
    def _():
        for step in range(n_col_steps):
            issue(0, step)

    @pl.when(j == 0)
    def _():
        wait()
        w = wt_ref[...]
        yk = [jnp.concatenate([_load_tile_rows(ybuf.at[k * per_k + c], 0, per_step) for c in range(per_k)],
                              axis=0) for k in range(TOP_K)]
        f = w[:, 0:1] * yk[0] + w[:, 1:2] * yk[1]
        x = x_ref[...] + modp_ref[5:6, :] * f
        xo_ref[...] = x
        y = x * lax.rsqrt(jnp.mean(x * x, axis=-1, keepdims=True) + EPS) * g_ref[...]
        h_ref[...] = (y * (1.0 + mod_ref[1:2, :]) + mod_ref[0:1, :]).astype(BF16)

    issue(jnp.minimum(i + 1, last), j)
    acc = _dot(h_ref[...], w_ref[...])
    gate = 0.5 * jnp.tanh(0.5 * acc) + 0.5
    cs = cs_ref[...]
    p_ref[...] = jnp.where(cs > 0.0, acc * cs, gate).astype(BF16)

    @pl.when((i == last) & (j == n_col_steps - 1))
    def _():
        wait()


def _combine_project(x_rows, y_slots, pos, w_tok, modtab_prev, g, modtab, w_bf, n_lat, seq, batch):
    n_rows, d = x_rows.shape
    ncol = w_bf.shape[1]
    tn = PROJ_TN
    n_col_steps = ncol // tn
    per_k = n_col_steps // TOP_K
    per_step = pos.shape[1] // TOP_K
    tm = per_k * per_step
    n_steps = n_rows // tm
    assert n_col_steps == TOP_K * per_k and n_rows % tm == 0
    pos = pos.reshape(n_steps, per_k, TOP_K, per_step).transpose(0, 2, 1, 3).reshape(n_steps * n_col_steps, per_step)
    midx = _mod_index(tm, n_lat, seq, batch)
    grid_spec = pltpu.PrefetchScalarGridSpec(
        num_scalar_prefetch=1,
        grid=(n_steps, n_col_steps),
        in_specs=[pl.BlockSpec(memory_space=pl.ANY),
                  pl.BlockSpec((tm, d), lambda i, j, pos: (i, 0)),
                  pl.BlockSpec((tm, TOP_K), lambda i, j, pos: (i, 0)),
                  pl.BlockSpec((None, 8, d), lambda i, j, pos: (midx(i), 0, 0)),
                  pl.BlockSpec((1, d), lambda i, j, pos: (0, 0)),
                  pl.BlockSpec((None, 8, d), lambda i, j, pos: (midx(i), 0, 0)),
                  pl.BlockSpec((d, tn), lambda i, j, pos: (0, j)),
                  pl.BlockSpec((1, tn), lambda i, j, pos: (0, j))],
        out_specs=[pl.BlockSpec((tm, tn), lambda i, j, pos: (i, j)),
                   pl.BlockSpec((tm, d), lambda i, j, pos: (i, 0))],
        scratch_shapes=[pltpu.VMEM((tm, d), BF16), pltpu.VMEM((n_col_steps, per_step * ROW_TILE, 128), F32),
                        pltpu.SemaphoreType.DMA(())])
    p_all, x_new = pl.pallas_call(
        functools.partial(_combine_project_kernel, n_col_steps),
        grid_spec=grid_spec,
        out_shape=[jax.ShapeDtypeStruct((n_rows, ncol), BF16), jax.ShapeDtypeStruct((n_rows, d), F32)],
        compiler_params=_params("arbitrary", "arbitrary"),
        name="combine_norm_project",
    )(pos, y_slots, x_rows, w_tok, modtab_prev, g.reshape(1, d), modtab, w_bf, _mix_col_scale())
    return p_all, x_new


def _dispatch_plan(idx2, rank, counts, blk, tm):
    n = idx2.shape[1]
    a = TOP_K * n
    padded = (counts + blk - 1) // blk * blk
    pad_end = jnp.cumsum(padded)
    pad_start = pad_end - padded
    start_of = jnp.zeros(idx2.shape, jnp.int32)
    for e in range(N_EXPERTS):
        start_of = jnp.where(idx2 == e, pad_start[e], start_of)
    dest = (start_of + rank).astype(jnp.int32)
    n_blocks = a // blk + N_EXPERTS
    first_slot = jnp.arange(n_blocks, dtype=jnp.int32) * blk
    blk_e = jnp.minimum(jnp.sum(pad_end[None, :] <= first_slot[:, None], axis=1), N_EXPERTS - 1).astype(jnp.int32)
    pos = dest.reshape(TOP_K, n // tm, tm).transpose(1, 0, 2).reshape(n // tm, TOP_K * tm) * ROW_TILE
    n_slots = n_blocks * blk
    starts = jnp.concatenate([pad_start + counts, pad_end[-1:]])
    lens = jnp.concatenate([padded - counts, n_slots - pad_end[-1:]])
    ends = jnp.cumsum(lens)
    j = jnp.arange(N_EXPERTS * blk, dtype=jnp.int32)
    shift = starts - (ends - lens)
    shift_of = jnp.zeros(j.shape, jnp.int32)
    for s in range(N_EXPERTS + 1):
        shift_of = jnp.where((j >= ends[s] - lens[s]) & (j < ends[s]), shift[s], shift_of)
    pad_slots = ((shift_of + j) * ROW_TILE).astype(jnp.int32)
    return pos, pad_slots.reshape(-1, TOP_K * tm), blk_e, n_slots


def kernel(x, c, ctx, c_ctx, w_mod, b_mod, g_norm1, g_norm2, w_in, na_rpb, ret_log_decay, gqa_q_gain, gqa_k_gain,
           diff_lambda, diff_subln, w_branch, w_out, w_router, router_bias, w_gate_e, w_up_e, w_down_e, g_final):
    batch, seq, d = x.shape
    n_ctx = ctx.shape[1]
    n_lat = batch * seq
    n_ctx_tot = batch * n_ctx
    n_tot = n_lat + n_ctx_tot
    tm = 1024 if n_ctx_tot % 1024 == 0 else 512

    x_parts = (x.reshape(n_lat, d), ctx.reshape(n_ctx_tot, d))

    nrow = -(-(batch + 1) // 8) * 8
    cc = jnp.zeros((nrow, d), F32).at[:batch].set(c).at[batch].set(c_ctx)
    mod = _modulation(cc, w_mod, b_mod)
    modtab = jnp.pad(mod.reshape(DEPTH, nrow, 6, d), ((0, 0), (0, 0), (0, 2), (0, 0)))

    wr_t = w_router.astype(F32).T
    out = None
    pending = None
    for layer in range(DEPTH):
        need_ctx = layer < DEPTH - 1
        n_rows = n_tot if need_ctx else n_lat
        if pending is None:
            p_all = _project(x_parts, g_norm1[layer], modtab[layer], w_in[layer].astype(BF16), tm, n_lat, seq, batch)
        else:
            p_all, x_new = _combine_project(*pending, g_norm1[layer], modtab[layer], w_in[layer].astype(BF16),
                                            n_lat, seq, batch)
            x_parts = (x_new,)
        ya = _na_mixer(p_all, na_rpb[layer], batch, seq, n_ctx, need_ctx)
        yb = _ret_mixer(p_all, ret_log_decay[layer], batch, seq, n_ctx, need_ctx)
        yc = _gqa_mixer(p_all, gqa_q_gain[layer], gqa_k_gain[layer], batch, seq, n_ctx, need_ctx)
        yd = _diff_mixer(p_all, diff_lambda[layer], diff_subln[layer], layer, batch, seq, n_ctx, need_ctx)
        ys_lat = [ya[0], yb[0], yc[0], yd[0]]
        ys_ctx = [ya[1], yb[1], yc[1], yd[1]] if need_ctx else None
        x_mid, m_all, logits_t = _merge(x_parts, p_all, ys_lat, ys_ctx, w_branch[layer].astype(BF16),
                                        w_out[layer].astype(BF16), modtab[layer], g_norm2[layer], wr_t, MERGE_TM,
                                        n_rows, n_lat, seq, batch)
        idx2, w2, rank, counts = _route(logits_t, router_bias)
        pos, pad_slots, blk_e, n_slots = _dispatch_plan(idx2, rank, counts, MOE_BLK, CMB_TM)
        x_slots = _dispatch(m_all, pos, pad_slots, n_slots)
        y_slots = _experts(x_slots, blk_e, MOE_BLK, layer, w_gate_e, w_up_e, w_down_e)
        if need_ctx:
            pending = (x_mid, y_slots, pos, w2.T, modtab[layer])
        else:
            out = _combine(x_mid, y_slots, pos, w2.T, modtab[layer], g_final, n_lat, seq, batch)
    return out.reshape(batch, seq, d)
```

```python
import functools
import math

import numpy as np
import jax
import jax.numpy as jnp
from jax import lax
from jax.experimental import pallas as pl
from jax.experimental.pallas import tpu as pltpu

F32 = jnp.float32
BF16 = jnp.bfloat16

D_MODEL = 1024
DEPTH = 2
GRID_W = 64
NA_HEADS = 4
NA_DIM = 64
NA_WIN_H = 8
NA_WIN_W = 16
RET_HEADS = 4
RET_DK = 64
RET_CHUNK = 128
GQA_HEADS = 4
GQA_KV_HEADS = 2
GQA_DIM = 64
DIFF_HEADS = 4
DIFF_QK_DIM = 32
DIFF_V_DIM = 64
N_BRANCH = 4
BRANCH_W = 256
ROPE_THETA = 10000.0
EPS = 1e-6
NEG_INF = -1e30
N_EXPERTS = 16
N_GROUPS = 4
EXPERTS_PER_GROUP = 4
TOP_K = 2
D_EXPERT = 512

MIX_COLS = 3072
IN_COLS = MIX_COLS + N_BRANCH * D_MODEL
NA_Q, NA_K, NA_V = 0, 1, 2
RET_Q, RET_K, RET_V, RET_G = 3, 4, 5, 6
GQA_Q = 7
DIFF_Q, DIFF_K, DIFF_V = 9, 10, 11

LOG2E = math.log2(math.e)
NA_GROUP_ROWS = 4
NA_BAND_ROWS = 12

VMEM_LIMIT = 56 * 1024 * 1024
MOE_BLK = 512
CMB_TM = 512
DISPATCH_FOLD = 1
PROJ_TN = 1792
MERGE_TM = 512


ROW_TILE = D_MODEL // 128


def _store_tile_rows(ref, x):
    n = x.shape[0]
    for j in range(ROW_TILE):
        ref[pl.ds(j, n, stride=ROW_TILE), :] = x[:, j * 128:(j + 1) * 128]


def _load_tile_rows(ref, start, n):
    return jnp.concatenate([ref[pl.ds(start + j, n, stride=ROW_TILE), :] for j in range(ROW_TILE)], axis=-1)


def _dot(a, b):
    return jnp.dot(a, b, preferred_element_type=F32)


def _dot_nt(a, b):
    return lax.dot_general(a, b, (((1,), (1,)), ((), ())), preferred_element_type=F32)


def _sigmoid(x):
    return 1.0 / (1.0 + jnp.exp(-x))


def _params(*sem):
    return pltpu.CompilerParams(dimension_semantics=sem, vmem_limit_bytes=VMEM_LIMIT)


def _mod_kernel(c_ref, w_ref, b_ref, o_ref):
    c = c_ref[...]
    a = (c * _sigmoid(c)).astype(BF16)
    o_ref[...] = _dot(a, w_ref[...].astype(BF16)) + b_ref[...]


def _modulation(cc, w_mod, b_mod):
    nrow = cc.shape[0]
    depth, d, n6 = w_mod.shape
    tn = 1536
    return pl.pallas_call(
        _mod_kernel,
        grid=(depth, n6 // tn),
        in_specs=[pl.BlockSpec((nrow, d), lambda l, j: (0, 0)),
                  pl.BlockSpec((None, d, tn), lambda l, j: (l, 0, j)),
                  pl.BlockSpec((None, 1, tn), lambda l, j: (l, 0, j))],
        out_specs=pl.BlockSpec((None, nrow, tn), lambda l, j: (l, 0, j)),
        out_shape=jax.ShapeDtypeStruct((depth, nrow, n6), F32),
        compiler_params=_params("parallel", "arbitrary"),
        name="modulation",
    )(cc, w_mod, b_mod.reshape(depth, 1, n6))


def _mix_col_scale():
    cs = np.ones((1, IN_COLS), np.float32)
    cs[0, NA_Q * 256:(NA_Q + 1) * 256] = NA_DIM ** -0.5 * LOG2E
    cs[0, DIFF_Q * 256:(DIFF_Q + 1) * 256] = DIFF_QK_DIM ** -0.5 * LOG2E
    cs[0, RET_K * 256:(RET_K + 1) * 256] = RET_DK ** -0.5
    cs[0, MIX_COLS:] = 0.0
    return jnp.asarray(cs)


def _row_blocks(parts, tm, n_lat):
    d = parts[0].shape[1]
    nlat_blk = n_lat // tm
    if len(parts) == 1:
        return [pl.BlockSpec((tm, d), lambda i, *_: (i, 0))]
    return [pl.BlockSpec((tm, d), lambda i, *_: (jnp.minimum(i, nlat_blk - 1), 0)),
            pl.BlockSpec((tm, d), lambda i, *_: (jnp.maximum(i - nlat_blk, 0), 0))]


def _pick_rows(x_refs, nlat_blk):
    if len(x_refs) == 1:
        return x_refs[0][...]
    return jnp.where(pl.program_id(0) >= nlat_blk, x_refs[1][...], x_refs[0][...])


def _proj_kernel(nlat_blk, n_parts, *refs):
    x_refs, (g_ref, mod_ref, w_ref, cs_ref, o_ref, h_ref) = refs[:n_parts], refs[n_parts:]
    j = pl.program_id(1)

    @pl.when(j == 0)
    def _():
        x = _pick_rows(x_refs, nlat_blk)
        y = x * lax.rsqrt(jnp.mean(x * x, axis=-1, keepdims=True) + EPS) * g_ref[...]
        h_ref[...] = (y * (1.0 + mod_ref[1:2, :]) + mod_ref[0:1, :]).astype(BF16)

    acc = _dot(h_ref[...], w_ref[...])
    gate = 0.5 * jnp.tanh(0.5 * acc) + 0.5
    cs = cs_ref[...]
    o_ref[...] = jnp.where(cs > 0.0, acc * cs, gate).astype(BF16)


def _mod_index(tm, n_lat, seq, batch):
    nlat_blk = n_lat // tm
    bpb = seq // tm

    def index(i):
        return jnp.where(i < nlat_blk, i // bpb, batch)
    return index


def _project(x_parts, g, modtab, w_bf, tm, n_lat, seq, batch):
    n_tot = sum(p.shape[0] for p in x_parts)
    d = x_parts[0].shape[1]
    ncol = w_bf.shape[1]
    tn = PROJ_TN
    midx = _mod_index(tm, n_lat, seq, batch)
    return pl.pallas_call(
        functools.partial(_proj_kernel, n_lat // tm, len(x_parts)),
        grid=(n_tot // tm, ncol // tn),
        in_specs=[*_row_blocks(x_parts, tm, n_lat),
                  pl.BlockSpec((1, d), lambda i, j: (0, 0)),
                  pl.BlockSpec((None, 8, d), lambda i, j: (midx(i), 0, 0)),
                  pl.BlockSpec((d, tn), lambda i, j: (0, j)),
                  pl.BlockSpec((1, tn), lambda i, j: (0, j))],
        out_specs=pl.BlockSpec((tm, tn), lambda i, j: (i, j)),
        out_shape=jax.ShapeDtypeStruct((n_tot, ncol), BF16),
        scratch_shapes=[pltpu.VMEM((tm, d), BF16)],
        compiler_params=_params("parallel", "arbitrary"),
        name="norm_project",
    )(*x_parts, g.reshape(1, d), modtab, w_bf, _mix_col_scale())


def _na_kernel(need_ctx, rows, q_ref, k_ref, v_ref, qc_ref, kc_ref, vc_ref, tb_ref, y_ref, *rest):
    if need_ctx:
        yc_ref, vx, vcx = rest
    else:
        vx, vcx = rest
    gq = NA_GROUP_ROWS * GRID_W
    band = NA_BAND_ROWS * GRID_W
    n_groups = rows // NA_GROUP_ROWS
    n_pairs = NA_HEADS // 2
    masks = _lane_masks(2)
    low = masks[0] > 0

    for p in range(n_pairs):
        ps = slice(p * 128, (p + 1) * 128)
        vx[p] = jnp.concatenate([v_ref[:, ps], jnp.ones((v_ref.shape[0], 128), BF16)], axis=-1)
        vcx[p] = jnp.concatenate([vc_ref[:, ps], jnp.ones((vc_ref.shape[0], 128), BF16)], axis=-1)

    def group_body(g, carry):
        u = jnp.clip(g * NA_GROUP_ROWS - NA_WIN_H // 2, 0, rows - NA_BAND_ROWS)
        typ = jnp.where(g == 0, 0, jnp.where(g == n_groups - 1, 2, 1))
        qs = pl.multiple_of(g * gq, gq)
        ks = pl.multiple_of(u * GRID_W, GRID_W)
        for p in range(n_pairs):
            ps = slice(p * 128, (p + 1) * 128)
            q = q_ref[pl.ds(qs, gq), ps]
            kb = k_ref[pl.ds(ks, band), ps]
            kc = kc_ref[:, ps]
            outs = []
            for hh in range(2):
                qm = q * masks[hh]
                s_w = _dot_nt(qm, kb) + tb_ref[2 * p + hh, typ]
                s_c = _dot_nt(qm, kc)
                m = jnp.maximum(jnp.max(s_w, axis=-1, keepdims=True), jnp.max(s_c, axis=-1, keepdims=True))
                oe = (_dot(jnp.exp2((s_w - m).astype(BF16)), vx[p, pl.ds(ks, band), :])
                      + _dot(jnp.exp2((s_c - m).astype(BF16)), vcx[p]))
                outs.append(oe[:, :128] / oe[:, 128:])
            y_ref[pl.ds(qs, gq), ps] = jnp.where(low, outs[0], outs[1]).astype(BF16)
        return carry

    lax.fori_loop(0, n_groups, group_body, 0)

    if need_ctx:
        for p in range(n_pairs):
            ps = slice(p * 128, (p + 1) * 128)
            outs = [_softmax_pv(qc_ref[:, ps] * masks[hh], kc_ref[:, ps], vcx[p]) for hh in range(2)]
            yc_ref[:, ps] = jnp.where(low, outs[0], outs[1]).astype(BF16)


def _na_bias_table(rpb, rows):
    assert rows % NA_GROUP_ROWS == 0 and rows >= NA_BAND_ROWS + NA_GROUP_ROWS
    n_groups = rows // NA_GROUP_ROWS
    qc = np.arange(GRID_W)[:, None]
    kc = np.arange(GRID_W)[None, :]
    win_start = np.clip(qc - NA_WIN_W // 2, 0, GRID_W - NA_WIN_W)
    col_ok = (kc >= win_start) & (kc < win_start + NA_WIN_W)
    col_idx = np.clip(kc - qc, -(NA_WIN_W - 1), NA_WIN_W - 1) + NA_WIN_W - 1
    n_col = 2 * NA_WIN_W - 1
    onehot = jnp.asarray((np.arange(n_col)[:, None] == col_idx.reshape(1, -1)).astype(np.float32))
    picked = jnp.dot(rpb.astype(F32).reshape(-1, n_col), onehot, precision=lax.Precision.HIGHEST)
    picked = picked.reshape(NA_HEADS, 2 * NA_WIN_H - 1, GRID_W, GRID_W)
    tiles = jnp.where(col_ok[None, None], picked * LOG2E, NEG_INF)
    masked = jnp.full((NA_HEADS, GRID_W, GRID_W), NEG_INF, F32)
    tables = []
    for g in (0, 1, n_groups - 1):
        u = int(np.clip(g * NA_GROUP_ROWS - NA_WIN_H // 2, 0, rows - NA_BAND_ROWS))
        per_row = []
        for a in range(NA_GROUP_ROWS):
            r = g * NA_GROUP_ROWS + a
            r0 = int(np.clip(r - NA_WIN_H // 2, 0, rows - NA_WIN_H))
            pieces = [tiles[:, kr - r + NA_WIN_H - 1] if r0 <= kr < r0 + NA_WIN_H else masked
                      for kr in range(u, u + NA_BAND_ROWS)]
            per_row.append(jnp.concatenate(pieces, axis=-1))
        tables.append(jnp.concatenate(per_row, axis=1))
    return jnp.stack(tables, axis=1)


def _na_mixer(p_all, rpb, batch, seq, n_ctx, need_ctx):
    n_lat = batch * seq
    rows = seq // GRID_W
    tb = _na_bias_table(rpb, rows)
    cb = n_lat // n_ctx
    w = 256
    lat = lambda c: pl.BlockSpec((seq, w), lambda b, c=c: (b, c))
    ctx = lambda c: pl.BlockSpec((n_ctx, w), lambda b, c=c: (cb + b, c))
    out_shape = [jax.ShapeDtypeStruct((n_lat, w), BF16)]
    out_specs = [pl.BlockSpec((seq, w), lambda b: (b, 0))]
    if need_ctx:
        out_shape.append(jax.ShapeDtypeStruct((batch * n_ctx, w), BF16))
        out_specs.append(pl.BlockSpec((n_ctx, w), lambda b: (b, 0)))
    return pl.pallas_call(
        functools.partial(_na_kernel, need_ctx, rows),
        grid=(batch,),
        in_specs=[lat(NA_Q), lat(NA_K), lat(NA_V), ctx(NA_Q), ctx(NA_K), ctx(NA_V),
                  pl.BlockSpec(tb.shape, lambda b: (0, 0, 0, 0))],
        out_specs=out_specs,
        out_shape=out_shape,
        scratch_shapes=[pltpu.VMEM((NA_HEADS // 2, seq, 256), BF16), pltpu.VMEM((NA_HEADS // 2, n_ctx, 256), BF16)],
        compiler_params=_params("parallel"),
        name="na_mixer",
    )(p_all, p_all, p_all, p_all, p_all, p_all, tb)


def _ret_kernel(need_ctx, seq, n_ctx, lg_ref, q_ref, k_ref, v_ref, g_ref, qc_ref, kc_ref, vc_ref, gc_ref,
                cos_ref, sin_ref, perm_ref, bd_ref, y_ref, *rest):
    if need_ctx:
        yc_ref, qr, kr, krt, kct, o_f, o_b, oc_f, oc_b, dm, qd, kdt = rest
    else:
        qr, kr, krt, kct, o_f, o_b, oc_f, oc_b, dm, qd, kdt = rest
    ch = RET_CHUNK
    n_lat_ch = seq // ch
    n_ctx_ch = n_ctx // ch
    n_pairs = RET_HEADS // 2
    assert ch == 128 and 2 * RET_DK == 128

    def prep(i, carry):
        rs = pl.multiple_of(i * ch, ch)
        c = cos_ref[pl.ds(rs, ch), :]
        s = sin_ref[pl.ds(rs, ch), :]
        qr[pl.ds(rs, ch), :] = _rope_mxu(q_ref[pl.ds(rs, ch), :], c, s, perm_ref)
        k = _rope_mxu(k_ref[pl.ds(rs, ch), :], c, s, perm_ref)
        kr[pl.ds(rs, ch), :] = k
        for p in range(n_pairs):
            krt[i, p * 128:(p + 1) * 128, :] = k[:, p * 128:(p + 1) * 128].T
        return carry

    lax.fori_loop(0, n_lat_ch, prep, 0, unroll=2)
    for n in range(n_ctx_ch):
        kc = kc_ref[n * ch:(n + 1) * ch, :].astype(F32)
        for p in range(n_pairs):
            kct[n, p * 128:(p + 1) * 128, :] = kc[:, p * 128:(p + 1) * 128].T

    ri = lax.broadcasted_iota(jnp.int32, (ch, ch), 0)
    ci = lax.broadcasted_iota(jnp.int32, (ch, ch), 1)
    lag = (ri - ci).astype(F32)
    rowf = ri.astype(F32)
    colf = ci.astype(F32)
    low_lane = ci < RET_DK
    low_row = ri < RET_DK
    block_diag = jnp.where(low_lane == low_row, 1.0, 0.0)
    low, high = _lane_masks(2)

    chains = [(p, dirn) for p in range(n_pairs) for dirn in range(2)]
    cdec = []
    for c, (p, dirn) in enumerate(chains):
        lg_a = lg_ref[dirn, 2 * p]
        lg_b = lg_ref[dirn, 2 * p + 1]
        lg_lane = jnp.where(low_lane, lg_a, lg_b)
        lg_row = jnp.where(low_row, lg_a, lg_b)
        for hh, lg in enumerate((lg_a, lg_b)):
            if dirn == 0:
                keep = ri >= ci
                dm[2 * c + hh] = jnp.where(keep, jnp.exp(jnp.where(keep, lag, 0.0) * lg), 0.0)
            else:
                keep = ci >= ri
                dm[2 * c + hh] = jnp.where(keep, jnp.exp(jnp.where(keep, -lag, 0.0) * lg), 0.0)
        if dirn == 0:
            qd[c] = jnp.exp((rowf + 1.0) * lg_lane)
            kdt[c] = jnp.exp((ch - 1.0 - colf) * lg_row)
        else:
            qd[c] = jnp.exp((ch - rowf) * lg_lane)
            kdt[c] = jnp.exp(colf * lg_row)
        cdec.append(jnp.exp(ch * lg_lane[0:1, :]))

    def step(c, state, qi, ki, kti, vi):
        kb = ki.astype(BF16)
        inner_a = (_dot_nt((qi * low).astype(BF16), kb) * dm[2 * c]).astype(BF16)
        inner_b = (_dot_nt((qi * high).astype(BF16), kb) * dm[2 * c + 1]).astype(BF16)
        o = (jnp.where(low_lane, _dot(inner_a, vi), _dot(inner_b, vi))
             + _dot((qi * qd[c]).astype(BF16), state.astype(BF16)))
        state = state * cdec[c] + block_diag * _dot((kti * kdt[c]).astype(BF16), vi)
        return state, o

    states = []
    for c, (p, dirn) in enumerate(chains):
        ps = slice(p * 128, (p + 1) * 128)
        state = jnp.zeros((128, 128), F32)
        for n in (range(n_ctx_ch) if dirn == 0 else range(n_ctx_ch - 1, -1, -1)):
            rs = slice(n * ch, (n + 1) * ch)
            state, o = step(c, state, qc_ref[rs, ps].astype(F32), kc_ref[rs, ps].astype(F32), kct[n, ps, :],
                            vc_ref[rs, ps])
            if need_ctx:
                (oc_f if dirn == 0 else oc_b)[rs, ps] = o
        states.append(state)

    def lat_body(i, states):
        new = []
        for c, (p, dirn) in enumerate(chains):
            ps = slice(p * 128, (p + 1) * 128)
            n = i if dirn == 0 else n_lat_ch - 1 - i
            rs = pl.ds(pl.multiple_of(n * ch, ch), ch)
            state, o = step(c, states[c], qr[rs, ps], kr[rs, ps], krt[n, ps, :], v_ref[rs, ps])
            (o_f if dirn == 0 else o_b)[rs, ps] = o
            new.append(state)
        return tuple(new)

    lax.fori_loop(0, n_lat_ch, lat_body, tuple(states), unroll=2)

    def head_mean(x):
        hi = x.astype(BF16)
        lo = (x - hi.astype(F32)).astype(BF16)
        bd = bd_ref[...]
        return (_dot(hi, bd) + _dot(lo, bd)) * (1.0 / RET_DK)

    def finish(o, g):
        cen = o - head_mean(o)
        on = cen * lax.rsqrt(head_mean(cen * cen) + EPS)
        g = g.astype(F32)
        return (g * _sigmoid(g) * on).astype(BF16)

    def fin_body(i, carry):
        rs = pl.multiple_of(i * ch, ch)
        y_ref[pl.ds(rs, ch), :] = finish(o_f[pl.ds(rs, ch), :] + o_b[pl.ds(rs, ch), :], g_ref[pl.ds(rs, ch), :])
        return carry

    lax.fori_loop(0, n_lat_ch, fin_body, 0, unroll=2)
    if need_ctx:
        for n in range(n_ctx_ch):
            rs = slice(n * ch, (n + 1) * ch)
            yc_ref[rs, :] = finish(oc_f[rs, :] + oc_b[rs, :], gc_ref[rs, :])


def _rope_tables(seq, head_dim, width, dist):
    half = head_dim // 2
    nf = half // 2
    assert nf == dist
    inv = 1.0 / (ROPE_THETA ** (np.arange(nf, dtype=np.float32) / nf))
    t = np.arange(seq)
    rows = (t // GRID_W).astype(np.float32)[:, None] * inv[None, :]
    cols = (t % GRID_W).astype(np.float32)[:, None] * inv[None, :]
    cos = np.concatenate([np.cos(rows), np.cos(rows), np.cos(cols), np.cos(cols)], axis=-1)
    sin = np.concatenate([-np.sin(rows), np.sin(rows), -np.sin(cols), np.sin(cols)], axis=-1)
    reps = width // head_dim
    return (jnp.asarray(np.tile(cos, (1, reps)), F32), jnp.asarray(np.tile(sin, (1, reps)), F32))


def _ret_mixer(p_all, log_decay, batch, seq, n_ctx, need_ctx):
    n_lat = batch * seq
    cb = n_lat // n_ctx
    w = 256
    ch = RET_CHUNK
    log_gamma = jnp.log1p(-jnp.exp(log_decay.astype(F32)))
    cos, sin = _rope_tables(seq, RET_DK, w, 16)
    lat = lambda c: pl.BlockSpec((seq, w), lambda b, c=c: (b, c))
    ctx = lambda c: pl.BlockSpec((n_ctx, w), lambda b, c=c: (cb + b, c))
    whole = pl.BlockSpec((seq, w), lambda b: (0, 0))
    out_shape = [jax.ShapeDtypeStruct((n_lat, w), BF16)]
    out_specs = [pl.BlockSpec((seq, w), lambda b: (b, 0))]
    if need_ctx:
        out_shape.append(jax.ShapeDtypeStruct((batch * n_ctx, w), BF16))
        out_specs.append(pl.BlockSpec((n_ctx, w), lambda b: (b, 0)))
    return pl.pallas_call(
        functools.partial(_ret_kernel, need_ctx, seq, n_ctx),
        grid=(batch,),
        in_specs=[pl.BlockSpec(memory_space=pltpu.SMEM),
                  lat(RET_Q), lat(RET_K), lat(RET_V), lat(RET_G),
                  ctx(RET_Q), ctx(RET_K), ctx(RET_V), ctx(RET_G), whole, whole,
                  pl.BlockSpec((w, w), lambda b: (0, 0)), pl.BlockSpec((w, w), lambda b: (0, 0))],
        out_specs=out_specs,
        out_shape=out_shape,
        scratch_shapes=[pltpu.VMEM((seq, w), F32), pltpu.VMEM((seq, w), F32),
                        pltpu.VMEM((seq // ch, w, ch), F32), pltpu.VMEM((n_ctx // ch, w, ch), F32),
                        pltpu.VMEM((seq, w), F32), pltpu.VMEM((seq, w), F32),
                        pltpu.VMEM((n_ctx, w), F32), pltpu.VMEM((n_ctx, w), F32),
                        pltpu.VMEM((2 * RET_HEADS, ch, ch), F32),
                        pltpu.VMEM((RET_HEADS, ch, 128), F32), pltpu.VMEM((RET_HEADS, 128, ch), F32)],
        compiler_params=_params("parallel"),
        name="retention_mixer",
    )(log_gamma, p_all, p_all, p_all, p_all, p_all, p_all, p_all, p_all, cos, sin,
      _swap_matrix(w, 16), _block_ones(w, RET_DK))


def _swap_matrix(width, dist):
    i = np.arange(width)
    partner = np.where(i % (2 * dist) < dist, i + dist, i - dist)
    p = np.zeros((width, width), np.float32)
    p[partner, i] = 1.0
    return jnp.asarray(p, BF16)


def _block_ones(width, block):
    i = np.arange(width)
    return jnp.asarray((i[:, None] // block == i[None, :] // block).astype(np.float32), BF16)


def _sumsq_blocks(xf, bd_ref):
    sq = xf * xf
    hi = sq.astype(BF16)
    lo = (sq - hi.astype(F32)).astype(BF16)
    bd = bd_ref[...]
    return _dot(hi, bd) + _dot(lo, bd)


def _rope_mxu(x, cos, sin, perm_ref):
    return x.astype(F32) * cos + _dot(x, perm_ref[...]) * sin


def _softmax_pv(qm, k, v_ones):
    s = _dot_nt(qm, k)
    p = jnp.exp2((s - jnp.max(s, axis=-1, keepdims=True)).astype(BF16))
    oe = _dot(p, v_ones)
    return oe[:, :128] / oe[:, 128:]


def _lane_masks(n_parts):
    lane = lax.broadcasted_iota(jnp.int32, (1, 128), 1)
    return [jnp.where(lane // (128 // n_parts) == i, 1.0, 0.0).astype(BF16) for i in range(n_parts)]


def _gqa_kernel(n_lat_k, n_ctx, *refs):
    if n_lat_k:
        (q_ref, k_ref, v_ref, kc_ref, vc_ref, qa_ref, qb_ref, ka_ref, kb_ref, kg_ref, pq_ref, pk_ref,
         bdq_ref, bdk_ref, y_ref, kp, vx) = refs
    else:
        q_ref, kc_ref, vc_ref, qa_ref, kg_ref, bdq_ref, bdk_ref, y_ref, kp, vx = refs
    dim = GQA_DIM
    nk = n_lat_k + n_ctx

    def inv_rms(xf, bd_ref):
        return lax.rsqrt(_sumsq_blocks(xf, bd_ref) * (1.0 / dim) + EPS)

    @pl.when(pl.program_id(1) == 0)
    def _():
        def put(rs, kn, v):
            ones = jnp.ones(v.shape, BF16)
            kp[0, rs, :] = kn.astype(BF16)
            kp[1, rs, :] = pltpu.roll(kn, dim, 1).astype(BF16)
            vx[0, rs, :] = jnp.concatenate([v, ones], axis=-1)
            vx[1, rs, :] = jnp.concatenate([pltpu.roll(v.astype(F32), dim, 1).astype(BF16), ones], axis=-1)

        kc = kc_ref[...]
        kcf = kc.astype(F32)
        put(slice(n_lat_k, nk), kcf * kg_ref[...] * inv_rms(kcf, bdk_ref), vc_ref[...])
        ck = 512
        for c in range(n_lat_k // ck):
            rs = slice(c * ck, (c + 1) * ck)
            k = k_ref[rs, :]
            put(rs, _rope_mxu(k, ka_ref[rs, :], kb_ref[rs, :], pk_ref) * inv_rms(k.astype(F32), bdk_ref), v_ref[rs, :])

    q = q_ref[...]
    qf = q.astype(F32)
    qn = _rope_mxu(q, qa_ref[...], qb_ref[...], pq_ref) if n_lat_k else qf * qa_ref[...]
    qn = (qn * inv_rms(qf, bdq_ref)).astype(BF16)
    low, high = _lane_masks(2)
    outs = []
    for g in range(GQA_KV_HEADS):
        qv = qn[:, g * 128:(g + 1) * 128]
        o_low = _softmax_pv(qv * low, kp[g], vx[g])
        o_high = _softmax_pv(qv * high, kp[1 - g], vx[1 - g])
        outs.append(jnp.where(low > 0, o_low, o_high))
    y_ref[...] = jnp.concatenate(outs, axis=-1).astype(BF16)


def _gqa_mixer(p_all, q_gain, k_gain, batch, seq, n_ctx, need_ctx):
    n_lat = batch * seq
    cb = n_lat // n_ctx
    dim = GQA_DIM
    qw = GQA_HEADS * dim
    kvw = GQA_KV_HEADS * dim
    assert qw == 256 and kvw == 128

    def swapped(g):
        return g.reshape(-1, 2, 16)[:, ::-1].reshape(1, -1)

    qg = jnp.tile(q_gain.astype(F32) * (dim ** -0.5 * LOG2E), GQA_HEADS).reshape(1, qw)
    kg = jnp.tile(k_gain.astype(F32), GQA_KV_HEADS).reshape(1, kvw)
    cq, sq = _rope_tables(seq, dim, qw, 16)
    ck, sk = _rope_tables(seq, dim, kvw, 16)
    qa, qb = cq * qg, sq * swapped(qg)
    ka, kb = ck * kg, sk * swapped(kg)
    pq, pk = _swap_matrix(qw, 16), _swap_matrix(kvw, 16)
    bdq, bdk = _block_ones(qw, dim), _block_ones(kvw, dim)
    tq = 512
    nqb = seq // tq
    const = lambda shape: pl.BlockSpec(shape, lambda b, i: (0,) * len(shape))
    scratch = lambda nk: [pltpu.VMEM((2, nk, kvw), BF16), pltpu.VMEM((2, nk, 2 * kvw), BF16)]
    y_lat = pl.pallas_call(
        functools.partial(_gqa_kernel, seq, n_ctx),
        grid=(batch, nqb),
        in_specs=[pl.BlockSpec((tq, qw), lambda b, i: (b * nqb + i, GQA_Q)),
                  pl.BlockSpec((seq, kvw), lambda b, i: (b, 16)),
                  pl.BlockSpec((seq, kvw), lambda b, i: (b, 17)),
                  pl.BlockSpec((n_ctx, kvw), lambda b, i: (cb + b, 16)),
                  pl.BlockSpec((n_ctx, kvw), lambda b, i: (cb + b, 17)),
                  pl.BlockSpec((tq, qw), lambda b, i: (i, 0)),
                  pl.BlockSpec((tq, qw), lambda b, i: (i, 0)),
                  const((seq, kvw)), const((seq, kvw)), const((1, kvw)),
                  const((qw, qw)), const((kvw, kvw)), const((qw, qw)), const((kvw, kvw))],
        out_specs=pl.BlockSpec((tq, qw), lambda b, i: (b * nqb + i, 0)),
        out_shape=jax.ShapeDtypeStruct((n_lat, qw), BF16),
        scratch_shapes=scratch(seq + n_ctx),
        compiler_params=_params("parallel", "arbitrary"),
        name="gqa_mixer",
    )(p_all, p_all, p_all, p_all, p_all, qa, qb, ka, kb, kg, pq, pk, bdq, bdk)
    if not need_ctx:
        return y_lat, None
    y_ctx = pl.pallas_call(
        functools.partial(_gqa_kernel, 0, n_ctx),
        grid=(batch, 1),
        in_specs=[pl.BlockSpec((n_ctx, qw), lambda b, i: (cb + b, GQA_Q)),
                  pl.BlockSpec((n_ctx, kvw), lambda b, i: (cb + b, 16)),
                  pl.BlockSpec((n_ctx, kvw), lambda b, i: (cb + b, 17)),
                  const((1, qw)), const((1, kvw)), const((qw, qw)), const((kvw, kvw))],
        out_specs=pl.BlockSpec((n_ctx, qw), lambda b, i: (b, 0)),
        out_shape=jax.ShapeDtypeStruct((batch * n_ctx, qw), BF16),
        scratch_shapes=scratch(n_ctx),
        compiler_params=_params("parallel", "arbitrary"),
        name="gqa_mixer_ctx",
    )(p_all, p_all, p_all, qg, kg, bdq, bdk)
    return y_lat, y_ctx


def _diff_kernel(n_lat_k, n_ctx, lam_init, *refs):
    if n_lat_k:
        (q_ref, k_ref, v_ref, kc_ref, vc_ref, lp_ref, sg_ref, cq_ref, sq_ref, ck_ref, sk_ref, perm_ref, bd_ref,
         y_ref, kp, vx) = refs
    else:
        q_ref, kc_ref, vc_ref, lp_ref, sg_ref, bd_ref, y_ref, kp, vx = refs
    nk = n_lat_k + n_ctx
    n_pairs = DIFF_HEADS // 2

    @pl.when(pl.program_id(1) == 0)
    def _():
        def put(rs, k, v):
            kp[rs, :] = k
            ones = jnp.ones((v.shape[0], 128), BF16)
            for pr in range(n_pairs):
                vx[pr, rs, :] = jnp.concatenate([v[:, pr * 128:(pr + 1) * 128], ones], axis=-1)

        put(slice(n_lat_k, nk), kc_ref[...], vc_ref[...])
        ck = 512
        for c in range(n_lat_k // ck):
            rs = slice(c * ck, (c + 1) * ck)
            put(rs, _rope_mxu(k_ref[rs, :], ck_ref[rs, :], sk_ref[rs, :], perm_ref).astype(BF16), v_ref[rs, :])

    lp = lp_ref[...]
    lam = (jnp.exp(jnp.sum(lp[0:1, :] * lp[1:2, :], axis=-1, keepdims=True))
           - jnp.exp(jnp.sum(lp[2:3, :] * lp[3:4, :], axis=-1, keepdims=True)) + lam_init)

    q = q_ref[...]
    if n_lat_k:
        q = _rope_mxu(q, cq_ref[...], sq_ref[...], perm_ref).astype(BF16)
    quarter = _lane_masks(4)
    low = _lane_masks(2)[0]
    outs = []
    for pr in range(n_pairs):
        ps = slice(pr * 128, (pr + 1) * 128)
        qv = q[:, ps]
        kv = kp[:, ps]
        o_head = [_softmax_pv(qv * quarter[2 * hh], kv, vx[pr]) - lam * _softmax_pv(qv * quarter[2 * hh + 1], kv, vx[pr])
                  for hh in range(2)]
        outs.append(jnp.where(low > 0, o_head[0], o_head[1]))
    o = jnp.concatenate(outs, axis=-1)
    inv = lax.rsqrt(_sumsq_blocks(o, bd_ref) * (1.0 / DIFF_V_DIM) + EPS)
    y_ref[...] = (o * inv * sg_ref[...] * (1.0 - lam_init)).astype(BF16)


def _diff_mixer(p_all, lam_params, subln, layer_idx, batch, seq, n_ctx, need_ctx):
    n_lat = batch * seq
    cb = n_lat // n_ctx
    lam_init = 0.8 - 0.6 * math.exp(-0.3 * layer_idx)
    lp = jnp.zeros((8, 128), F32).at[:4, :DIFF_QK_DIM].set(lam_params.astype(F32))
    w = DIFF_HEADS * DIFF_V_DIM
    assert w == 256 and DIFF_HEADS * 2 * DIFF_QK_DIM == w
    sg = jnp.tile(subln.astype(F32), DIFF_HEADS).reshape(1, w)
    cq, sq = _rope_tables(seq, DIFF_QK_DIM, w, 8)
    perm = _swap_matrix(w, 8)
    bd = _block_ones(w, DIFF_V_DIM)
    tq = 512
    nqb = seq // tq
    const = lambda shape: pl.BlockSpec(shape, lambda b, i: (0,) * len(shape))
    scratch = lambda nk: [pltpu.VMEM((nk, w), BF16), pltpu.VMEM((DIFF_HEADS // 2, nk, 256), BF16)]
    y_lat = pl.pallas_call(
        functools.partial(_diff_kernel, seq, n_ctx, lam_init),
        grid=(batch, nqb),
        in_specs=[pl.BlockSpec((tq, w), lambda b, i: (b * nqb + i, DIFF_Q)),
                  pl.BlockSpec((seq, w), lambda b, i: (b, DIFF_K)),
                  pl.BlockSpec((seq, w), lambda b, i: (b, DIFF_V)),
                  pl.BlockSpec((n_ctx, w), lambda b, i: (cb + b, DIFF_K)),
                  pl.BlockSpec((n_ctx, w), lambda b, i: (cb + b, DIFF_V)),
                  const((8, 128)), const((1, w)),
                  pl.BlockSpec((tq, w), lambda b, i: (i, 0)),
                  pl.BlockSpec((tq, w), lambda b, i: (i, 0)),
                  const((seq, w)), const((seq, w)), const((w, w)), const((w, w))],
        out_specs=pl.BlockSpec((tq, w), lambda b, i: (b * nqb + i, 0)),
        out_shape=jax.ShapeDtypeStruct((n_lat, w), BF16),
        scratch_shapes=scratch(seq + n_ctx),
        compiler_params=_params("parallel", "arbitrary"),
        name="diff_mixer",
    )(p_all, p_all, p_all, p_all, p_all, lp, sg, cq, sq, cq, sq, perm, bd)
    if not need_ctx:
        return y_lat, None
    y_ctx = pl.pallas_call(
        functools.partial(_diff_kernel, 0, n_ctx, lam_init),
        grid=(batch, 1),
        in_specs=[pl.BlockSpec((n_ctx, w), lambda b, i: (cb + b, DIFF_Q)),
                  pl.BlockSpec((n_ctx, w), lambda b, i: (cb + b, DIFF_K)),
                  pl.BlockSpec((n_ctx, w), lambda b, i: (cb + b, DIFF_V)),
                  const((8, 128)), const((1, w)), const((w, w))],
        out_specs=pl.BlockSpec((n_ctx, w), lambda b, i: (b, 0)),
        out_shape=jax.ShapeDtypeStruct((batch * n_ctx, w), BF16),
        scratch_shapes=scratch(n_ctx),
        compiler_params=_params("parallel", "arbitrary"),
        name="diff_mixer_ctx",
    )(p_all, p_all, p_all, lp, sg, bd)
    return y_lat, y_ctx


def _merge_kernel(nlat_blk, has_ctx, n_parts, *refs):
    x_refs, refs = refs[:n_parts], refs[n_parts:]
    g_refs = refs[:N_BRANCH]
    y_refs = refs[N_BRANCH:2 * N_BRANCH]
    refs = refs[2 * N_BRANCH:]
    if has_ctx:
        yc_refs, refs = refs[:N_BRANCH], refs[N_BRANCH:]
        is_ctx = pl.program_id(0) >= nlat_blk
    wb_ref, wo_ref, mod_ref, gn_ref, wr_ref, xo_ref, m_ref, lg_ref = refs
    acc = None
    for n in range(N_BRANCH):
        y = y_refs[n][...]
        if has_ctx:
            y = jnp.where(is_ctx, yc_refs[n][...], y)
        term = g_refs[n][...].astype(F32) * _dot(y, wb_ref[n])
        acc = term if acc is None else acc + term
    y = _dot(acc.astype(BF16), wo_ref[...])
    x = _pick_rows(x_refs, nlat_blk) + mod_ref[2:3, :] * y
    xo_ref[...] = x
    xn = x * lax.rsqrt(jnp.mean(x * x, axis=-1, keepdims=True) + EPS) * gn_ref[...]
    m = xn * (1.0 + mod_ref[4:5, :]) + mod_ref[3:4, :]
    _store_tile_rows(m_ref, m)
    m_hi = m.astype(BF16)
    m_lo = (m - m_hi.astype(F32)).astype(BF16)
    w = wr_ref[...]
    w_hi = w.astype(BF16)
    w_lo = (w - w_hi.astype(F32)).astype(BF16)
    both = _dot_nt(jnp.concatenate([w_hi, w_lo], axis=0), m_hi)
    lg_ref[...] = both[:N_EXPERTS] + (_dot_nt(w_hi, m_lo) + both[N_EXPERTS:])


def _merge(x_parts, p_all, ys_lat, ys_ctx, wb_bf, wo_bf, modtab, gn, wr_t, tm, n_rows, n_lat, seq, batch):
    d = D_MODEL
    midx = _mod_index(tm, n_lat, seq, batch)
    nlat_blk = n_lat // tm
    has_ctx = ys_ctx is not None
    gate = lambda n: pl.BlockSpec((tm, d), lambda i, n=n: (i, MIX_COLS // d + n))
    y_specs = [pl.BlockSpec((tm, BRANCH_W), lambda i: (jnp.minimum(i, nlat_blk - 1), 0))] * N_BRANCH
    ys = list(ys_lat)
    if has_ctx:
        y_specs += [pl.BlockSpec((tm, BRANCH_W), lambda i: (jnp.maximum(i - nlat_blk, 0), 0))] * N_BRANCH
        ys += list(ys_ctx)
    return pl.pallas_call(
        functools.partial(_merge_kernel, nlat_blk, has_ctx, len(x_parts)),
        grid=(n_rows // tm,),
        in_specs=[*_row_blocks(x_parts, tm, n_lat),
                  gate(0), gate(1), gate(2), gate(3), *y_specs,
                  pl.BlockSpec((N_BRANCH, BRANCH_W, d), lambda i: (0, 0, 0)),
                  pl.BlockSpec((d, d), lambda i: (0, 0)),
                  pl.BlockSpec((None, 8, d), lambda i: (midx(i), 0, 0)),
                  pl.BlockSpec((1, d), lambda i: (0, 0)),
                  pl.BlockSpec((N_EXPERTS, d), lambda i: (0, 0))],
        out_specs=[pl.BlockSpec((tm, d), lambda i: (i, 0)),
                   pl.BlockSpec((tm * ROW_TILE, 128), lambda i: (i, 0)),
                   pl.BlockSpec((N_EXPERTS, tm), lambda i: (0, i))],
        out_shape=[jax.ShapeDtypeStruct((n_rows, d), F32),
                   jax.ShapeDtypeStruct((n_rows * ROW_TILE, 128), F32),
                   jax.ShapeDtypeStruct((N_EXPERTS, n_rows), F32)],
        compiler_params=_params("parallel"),
        name="merge_norm_route",
    )(*x_parts, p_all, p_all, p_all, p_all, *ys, wb_bf, wo_bf, modtab, gn.reshape(1, d), wr_t)


def _route_kernel(lg_ref, b_ref, tri_ref, idx_ref, w_ref, rank_ref, cnt_ref):
    s = _sigmoid(lg_ref[...])
    sel = s + b_ref[...]
    row = lambda a, e: a[e:e + 1, :]
    gsz = EXPERTS_PER_GROUP
    g_idx = None
    best = None
    for g in range(N_GROUPS):
        v = [row(sel, g * gsz + i) for i in range(gsz)]
        score = None
        for i in range(gsz):
            for j in range(i + 1, gsz):
                pair = v[i] + v[j]
                score = pair if score is None else jnp.maximum(score, pair)
        if g == 0:
            best, g_idx = score, jnp.zeros(score.shape, jnp.int32)
        else:
            better = score > best
            best = jnp.where(better, score, best)
            g_idx = jnp.where(better, g, g_idx)

    def in_group(a, i):
        out = row(a, i)
        for g in range(1, N_GROUPS):
            out = jnp.where(g_idx == g, row(a, g * gsz + i), out)
        return out

    v = [in_group(sel, i) for i in range(gsz)]
    sv = [in_group(s, i) for i in range(gsz)]

    def arg_first_max(vals):
        bv, bi = vals[0], jnp.zeros(vals[0].shape, jnp.int32)
        for i in range(1, gsz):
            better = vals[i] > bv
            bv = jnp.where(better, vals[i], bv)
            bi = jnp.where(better, i, bi)
        return bi

    i1 = arg_first_max(v)
    i2 = arg_first_max([jnp.where(i1 == i, -jnp.inf, v[i]) for i in range(gsz)])

    def pick(vals, idx):
        out = vals[0]
        for i in range(1, gsz):
            out = jnp.where(idx == i, vals[i], out)
        return out

    w1 = pick(sv, i1)
    w2 = pick(sv, i2)
    tot = w1 + w2
    e1 = g_idx * gsz + i1
    e2 = g_idx * gsz + i2
    idx_ref[0:1, :] = e1
    idx_ref[1:2, :] = e2
    w_ref[0:1, :] = w1 / tot
    w_ref[1:2, :] = w2 / tot

    @pl.when(pl.program_id(0) == 0)
    def _():
        cnt_ref[...] = jnp.zeros(cnt_ref.shape, cnt_ref.dtype)

    n_e, tn = s.shape
    erow = lax.broadcasted_iota(jnp.int32, (n_e, tn), 0)
    oh1 = jnp.where(erow == e1, 1.0, 0.0)
    oh2 = jnp.where(erow == e2, 1.0, 0.0)
    oh = (oh1 + oh2).astype(BF16)
    base = cnt_ref[...]
    r1, r2 = [], []
    for c in range(tn // 128):
        cs = slice(c * 128, (c + 1) * 128)
        before = base + _dot(oh[:, cs], tri_ref[...])
        r1.append(jnp.sum(oh1[:, cs] * before, axis=0, keepdims=True))
        r2.append(jnp.sum(oh2[:, cs] * before, axis=0, keepdims=True))
        base = base + jnp.sum(oh[:, cs].astype(F32), axis=1, keepdims=True)
    cnt_ref[...] = base
    rank_ref[0:1, :] = jnp.concatenate(r1, axis=-1).astype(jnp.int32)
    rank_ref[1:2, :] = jnp.concatenate(r2, axis=-1).astype(jnp.int32)


def _route(logits_t, router_bias):
    e, n = logits_t.shape
    tn = math.gcd(n, 2048)
    i = np.arange(128)
    tri = jnp.asarray((i[:, None] < i[None, :]).astype(np.float32), BF16)
    idx2, w2, rank, cnt = pl.pallas_call(
        _route_kernel,
        grid=(n // tn,),
        in_specs=[pl.BlockSpec((e, tn), lambda i: (0, i)),
                  pl.BlockSpec((e, 1), lambda i: (0, 0)),
                  pl.BlockSpec((128, 128), lambda i: (0, 0))],
        out_specs=[pl.BlockSpec((TOP_K, tn), lambda i: (0, i)),
                   pl.BlockSpec((TOP_K, tn), lambda i: (0, i)),
                   pl.BlockSpec((TOP_K, tn), lambda i: (0, i)),
                   pl.BlockSpec((e, 128), lambda i: (0, 0))],
        out_shape=[jax.ShapeDtypeStruct((TOP_K, n), jnp.int32),
                   jax.ShapeDtypeStruct((TOP_K, n), F32),
                   jax.ShapeDtypeStruct((TOP_K, n), jnp.int32),
                   jax.ShapeDtypeStruct((e, 128), F32)],
        compiler_params=_params("arbitrary"),
        name="route_top2",
    )(logits_t, router_bias.astype(F32).reshape(e, 1), tri)
    return idx2, w2, rank, cnt[:, 0].astype(jnp.int32)


def _dispatch_kernel(n_tok_steps, tm, pos_ref, m_ref, xs_out, zrow, sem):
    i = pl.program_id(0)
    rt = ROW_TILE
    n = pos_ref.shape[1]

    def wait():
        pltpu.make_async_copy(xs_out.at[pl.ds(0, n * rt)], xs_out.at[pl.ds(0, n * rt)], sem).wait()

    @pl.when(i < n_tok_steps)
    def _():
        for j in range(n):
            part, within = divmod(j, TOP_K * tm)
            row = part * tm + within % tm
            dst = pl.multiple_of(pos_ref[i, j], rt)
            pltpu.make_async_copy(m_ref.at[pl.ds(row * rt, rt)], xs_out.at[pl.ds(dst, rt)],
                                  sem).start(priority=j % 2)
        wait()

    @pl.when(i >= n_tok_steps)
    def _():
        zrow[...] = jnp.zeros(zrow.shape, zrow.dtype)
        for j in range(n):
            dst = pl.multiple_of(pos_ref[i, j], rt)
            pltpu.make_async_copy(zrow, xs_out.at[pl.ds(dst, rt)], sem).start(priority=j % 2)
        wait()


def _dispatch(m_tiles, pos, pad_slots, n_slots):
    rt = ROW_TILE
    fold = DISPATCH_FOLD
    tm = pos.shape[1] // TOP_K
    assert pos.shape[0] % fold == 0 and pad_slots.shape[0] % fold == 0
    n_tok_steps = pos.shape[0] // fold
    steps = jnp.concatenate([pos, pad_slots], axis=0).reshape(-1, fold * pos.shape[1])
    grid_spec = pltpu.PrefetchScalarGridSpec(
        num_scalar_prefetch=1,
        grid=(steps.shape[0],),
        in_specs=[pl.BlockSpec((fold * tm * rt, 128), lambda i, pos: (jnp.minimum(i, n_tok_steps - 1), 0))],
        out_specs=pl.BlockSpec(memory_space=pl.ANY),
        scratch_shapes=[pltpu.VMEM((rt, 128), F32), pltpu.SemaphoreType.DMA(())])
    return pl.pallas_call(
        functools.partial(_dispatch_kernel, n_tok_steps, tm),
        grid_spec=grid_spec,
        out_shape=jax.ShapeDtypeStruct((n_slots * rt, 128), F32),
        compiler_params=_params("arbitrary"),
        name="moe_dispatch",
    )(steps, m_tiles)


def _expert_kernel(be_ref, x_ref, wg_ref, wu_ref, wd_ref, y_ref):
    blk = x_ref.shape[0] // ROW_TILE
    x = _load_tile_rows(x_ref, 0, blk).astype(BF16)
    a = _dot(x, wg_ref[...].astype(BF16))
    h = a * _sigmoid(a) * _dot(x, wu_ref[...].astype(BF16))
    _store_tile_rows(y_ref, _dot(h.astype(BF16), wd_ref[...].astype(BF16)))


def _experts(x_slots, blk_e, blk, layer, w_gate, w_up, w_down):
    rt = ROW_TILE
    d = D_MODEL
    n_slots = x_slots.shape[0] // rt
    grid_spec = pltpu.PrefetchScalarGridSpec(
        num_scalar_prefetch=1,
        grid=(n_slots // blk,),
        in_specs=[pl.BlockSpec((blk * rt, 128), lambda i, be: (i, 0)),
                  pl.BlockSpec((None, None, d, D_EXPERT), lambda i, be: (layer, be[i], 0, 0)),
                  pl.BlockSpec((None, None, d, D_EXPERT), lambda i, be: (layer, be[i], 0, 0)),
                  pl.BlockSpec((None, None, D_EXPERT, d), lambda i, be: (layer, be[i], 0, 0))],
        out_specs=pl.BlockSpec((blk * rt, 128), lambda i, be: (i, 0)))
    return pl.pallas_call(
        _expert_kernel,
        grid_spec=grid_spec,
        out_shape=jax.ShapeDtypeStruct((n_slots * rt, 128), F32),
        compiler_params=_params("arbitrary"),
        name="expert_ffn",
    )(blk_e, x_slots, w_gate, w_up, w_down)


def _combine_kernel(pos_ref, y_hbm, x_ref, w_ref, mod_ref, gf_ref, o_ref, ybuf, sem):
    i = pl.program_id(0)
    last = pl.num_programs(0) - 1
    slot = i % 2
    rt = ROW_TILE
    tm = x_ref.shape[0]
    n = TOP_K * tm

    def issue(step, dst_slot):
        for j in range(n):
            src = pl.multiple_of(pos_ref[step, j], rt)
            pltpu.make_async_copy(y_hbm.at[pl.ds(src, rt)], ybuf.at[dst_slot, pl.ds(j * rt, rt)],
                                  sem.at[dst_slot]).start(priority=j % 2)

    def wait(dst_slot):
        pltpu.make_async_copy(y_hbm.at[pl.ds(0, n * rt)], ybuf.at[dst_slot], sem.at[dst_slot]).wait()

    @pl.when(i == 0)
    def _():
        issue(0, 0)

    wait(slot)
    issue(jnp.minimum(i + 1, last), 1 - slot)
    w = w_ref[...]
    yb = ybuf.at[slot]
    f = w[:, 0:1] * _load_tile_rows(yb, 0, tm) + w[:, 1:2] * _load_tile_rows(yb, tm * rt, tm)
    x = x_ref[...] + mod_ref[5:6, :] * f
    o_ref[...] = x * lax.rsqrt(jnp.mean(x * x, axis=-1, keepdims=True) + EPS) * gf_ref[...]

    @pl.when(i == last)
    def _():
        wait(1 - slot)


def _combine(x_rows, y_slots, pos, w_tok, modtab, g_final, n_lat, seq, batch):
    n_rows, d = x_rows.shape
    n_steps, n = pos.shape
    tm = n // TOP_K
    midx = _mod_index(tm, n_lat, seq, batch)
    grid_spec = pltpu.PrefetchScalarGridSpec(
        num_scalar_prefetch=1,
        grid=(n_steps,),
        in_specs=[pl.BlockSpec(memory_space=pl.ANY),
                  pl.BlockSpec((tm, d), lambda i, pos: (i, 0)),
                  pl.BlockSpec((tm, TOP_K), lambda i, pos: (i, 0)),
                  pl.BlockSpec((None, 8, d), lambda i, pos: (midx(i), 0, 0)),
                  pl.BlockSpec((1, d), lambda i, pos: (0, 0))],
        out_specs=pl.BlockSpec((tm, d), lambda i, pos: (i, 0)),
        scratch_shapes=[pltpu.VMEM((2, n * ROW_TILE, 128), F32), pltpu.SemaphoreType.DMA((2,))])
    return pl.pallas_call(
        _combine_kernel,
        grid_spec=grid_spec,
        out_shape=jax.ShapeDtypeStruct((n_rows, d), F32),
        compiler_params=_params("arbitrary"),
        name="moe_combine",
    )(pos, y_slots, x_rows, w_tok, modtab, g_final.astype(F32).reshape(1, d))


def _combine_project_kernel(n_col_steps, pos_ref, y_hbm, x_ref, wt_ref, modp_ref, g_ref, mod_ref, w_ref, cs_ref,
                            p_ref, xo_ref, h_ref, ybuf, sem):
    i = pl.program_id(0)
    j = pl.program_id(1)
    last = pl.num_programs(0) - 1
    rt = ROW_TILE
    per_step = ybuf.shape[1] // rt
    per_k = n_col_steps // TOP_K

    def issue(block, step):
        for jj in range(per_step):
            src = pl.multiple_of(pos_ref[block * n_col_steps + step, jj], rt)
            pltpu.make_async_copy(y_hbm.at[pl.ds(src, rt)], ybuf.at[step, pl.ds(jj * rt, rt)],
                                  sem).start(priority=jj % 2)

    def wait():
        for step in range(n_col_steps):
            pltpu.make_async_copy(y_hbm.at[pl.ds(0, per_step * rt)], ybuf.at[step], sem).wait()

    @pl.when((i == 0) & (j == 0))
    def _():
        for step in range(n_col_steps):
            issue(0, step)

    @pl.when(j == 0)
    def _():
        wait()
        w = wt_ref[...]
        yk = [jnp.concatenate([_load_tile_rows(ybuf.at[k * per_k + c], 0, per_step) for c in range(per_k)],
                              axis=0) for k in range(TOP_K)]
        f = w[:, 0:1] * yk[0] + w[:, 1:2] * yk[1]
        x = x_ref[...] + modp_ref[5:6, :] * f
        xo_ref[...] = x
        y = x * lax.rsqrt(jnp.mean(x * x, axis=-1, keepdims=True) + EPS) * g_ref[...]
        h_ref[...] = (y * (1.0 + mod_ref[1:2, :]) + mod_ref[0:1, :]).astype(BF16)

    issue(jnp.minimum(i + 1, last), j)
    acc = _dot(h_ref[...], w_ref[...])
    gate = 0.5 * jnp.tanh(0.5 * acc) + 0.5
    cs = cs_ref[...]
    p_ref[...] = jnp.where(cs > 0.0, acc * cs, gate).astype(BF16)

    @pl.when((i == last) & (j == n_col_steps - 1))
    def _():
        wait()


def _combine_project(x_rows, y_slots, pos, w_tok, modtab_prev, g, modtab, w_bf, n_lat, seq, batch):
    n_rows, d = x_rows.shape
    ncol = w_bf.shape[1]
    tn = PROJ_TN
    n_col_steps = ncol // tn
    per_k = n_col_steps // TOP_K
    per_step = pos.shape[1] // TOP_K
    tm = per_k * per_step
    n_steps = n_rows // tm
    assert n_col_steps == TOP_K * per_k and n_rows % tm == 0
    pos = pos.reshape(n_steps, per_k, TOP_K, per_step).transpose(0, 2, 1, 3).reshape(n_steps * n_col_steps, per_step)
    midx = _mod_index(tm, n_lat, seq, batch)
    grid_spec = pltpu.PrefetchScalarGridSpec(
        num_scalar_prefetch=1,
        grid=(n_steps, n_col_steps),
        in_specs=[pl.BlockSpec(memory_space=pl.ANY),
                  pl.BlockSpec((tm, d), lambda i, j, pos: (i, 0)),
                  pl.BlockSpec((tm, TOP_K), lambda i, j, pos: (i, 0)),
                  pl.BlockSpec((None, 8, d), lambda i, j, pos: (midx(i), 0, 0)),
                  pl.BlockSpec((1, d), lambda i, j, pos: (0, 0)),
                  pl.BlockSpec((None, 8, d), lambda i, j, pos: (midx(i), 0, 0)),
                  pl.BlockSpec((d, tn), lambda i, j, pos: (0, j)),
                  pl.BlockSpec((1, tn), lambda i, j, pos: (0, j))],
        out_specs=[pl.BlockSpec((tm, tn), lambda i, j, pos: (i, j)),
                   pl.BlockSpec((tm, d), lambda i, j, pos: (i, 0))],
        scratch_shapes=[pltpu.VMEM((tm, d), BF16), pltpu.VMEM((n_col_steps, per_step * ROW_TILE, 128), F32),
                        pltpu.SemaphoreType.DMA(())])
    p_all, x_new = pl.pallas_call(
        functools.partial(_combine_project_kernel, n_col_steps),
        grid_spec=grid_spec,
        out_shape=[jax.ShapeDtypeStruct((n_rows, ncol), BF16), jax.ShapeDtypeStruct((n_rows, d), F32)],
        compiler_params=_params("arbitrary", "arbitrary"),
        name="combine_norm_project",
    )(pos, y_slots, x_rows, w_tok, modtab_prev, g.reshape(1, d), modtab, w_bf, _mix_col_scale())
    return p_all, x_new


def _dispatch_plan(idx2, rank, counts, blk, tm):
    n = idx2.shape[1]
    a = TOP_K * n
    padded = (counts + blk - 1) // blk * blk
    pad_end = jnp.cumsum(padded)
    pad_start = pad_end - padded
    start_of = jnp.zeros(idx2.shape, jnp.int32)
    for e in range(N_EXPERTS):
        start_of = jnp.where(idx2 == e, pad_start[e], start_of)
    dest = (start_of + rank).astype(jnp.int32)
    n_blocks = a // blk + N_EXPERTS
    first_slot = jnp.arange(n_blocks, dtype=jnp.int32) * blk
    blk_e = jnp.minimum(jnp.sum(pad_end[None, :] <= first_slot[:, None], axis=1), N_EXPERTS - 1).astype(jnp.int32)
    pos = dest.reshape(TOP_K, n // tm, tm).transpose(1, 0, 2).reshape(n // tm, TOP_K * tm) * ROW_TILE
    n_slots = n_blocks * blk
    starts = jnp.concatenate([pad_start + counts, pad_end[-1:]])
    lens = jnp.concatenate([padded - counts, n_slots - pad_end[-1:]])
    ends = jnp.cumsum(lens)
    j = jnp.arange(N_EXPERTS * blk, dtype=jnp.int32)
    shift = starts - (ends - lens)
    shift_of = jnp.zeros(j.shape, jnp.int32)
    for s in range(N_EXPERTS + 1):
        shift_of = jnp.where((j >= ends[s] - lens[s]) & (j < ends[s]), shift[s], shift_of)
    pad_slots = ((shift_of + j) * ROW_TILE).astype(jnp.int32)
    return pos, pad_slots.reshape(-1, TOP_K * tm), blk_e, n_slots


def kernel(x, c, ctx, c_ctx, w_mod, b_mod, g_norm1, g_norm2, w_in, na_rpb, ret_log_decay, gqa_q_gain, gqa_k_gain,
           diff_lambda, diff_subln, w_branch, w_out, w_router, router_bias, w_gate_e, w_up_e, w_down_e, g_final):
    batch, seq, d = x.shape
    n_ctx = ctx.shape[1]
    n_lat = batch * seq
    n_ctx_tot = batch * n_ctx
    n_tot = n_lat + n_ctx_tot
    tm = 1024 if n_ctx_tot % 1024 == 0 else 512

    x_parts = (x.reshape(n_lat, d), ctx.reshape(n_ctx_tot, d))

    nrow = -(-(batch + 1) // 8) * 8
    cc = jnp.zeros((nrow, d), F32).at[:batch].set(c).at[batch].set(c_ctx)
    mod = _modulation(cc, w_mod, b_mod)
    modtab = jnp.pad(mod.reshape(DEPTH, nrow, 6, d), ((0, 0), (0, 0), (0, 2), (0, 0)))

    wr_t = w_router.astype(F32).T
    out = None
    pending = None
    for layer in range(DEPTH):
        need_ctx = layer < DEPTH - 1
        n_rows = n_tot if need_ctx else n_lat
        if pending is None:
            p_all = _project(x_parts, g_norm1[layer], modtab[layer], w_in[layer].astype(BF16), tm, n_lat, seq, batch)
        else:
            p_all, x_new = _combine_project(*pending, g_norm1[layer], modtab[layer], w_in[layer].astype(BF16),
                                            n_lat, seq, batch)
            x_parts = (x_new,)
        ya = _na_mixer(p_all, na_rpb[layer], batch, seq, n_ctx, need_ctx)
        yb = _ret_mixer(p_all, ret_log_decay[layer], batch, seq, n_ctx, need_ctx)
        yc = _gqa_mixer(p_all, gqa_q_gain[layer], gqa_k_gain[layer], batch, seq, n_ctx, need_ctx)
        yd = _diff_mixer(p_all, diff_lambda[layer], diff_subln[layer], layer, batch, seq, n_ctx, need_ctx)
        ys_lat = [ya[0], yb[0], yc[0], yd[0]]
        ys_ctx = [ya[1], yb[1], yc[1], yd[1]] if need_ctx else None
        x_mid, m_all, logits_t = _merge(x_parts, p_all, ys_lat, ys_ctx, w_branch[layer].astype(BF16),
                                        w_out[layer].astype(BF16), modtab[layer], g_norm2[layer], wr_t, MERGE_TM,
                                        n_rows, n_lat, seq, batch)
        idx2, w2, rank, counts = _route(logits_t, router_bias)
        pos, pad_slots, blk_e, n_slots = _dispatch_plan(idx2, rank, counts, MOE_BLK, CMB_TM)
        x_slots = _dispatch(m_all, pos, pad_slots, n_slots)
        y_slots = _experts(x_slots, blk_e, MOE_BLK, layer, w_gate_e, w_up_e, w_down_e)
        if need_ctx:
            pending = (x_mid, y_slots, pos, w2.T, modtab[layer])
        else:
            out = _combine(x_mid, y_slots, pos, w2.T, modtab[layer], g_final, n_lat, seq, batch)
    return out.reshape(batch, seq, d)
```

```python
import functools
import math

import numpy as np
import jax
import jax.numpy as jnp
from jax import lax
from jax.experimental import pallas as pl
from jax.experimental.pallas import tpu as pltpu

F32 = jnp.float32
BF16 = jnp.bfloat16

D_MODEL = 1024
DEPTH = 2
GRID_W = 64
NA_HEADS = 4
NA_DIM = 64
NA_WIN_H = 8
NA_WIN_W = 16
RET_HEADS = 4
RET_DK = 64
RET_CHUNK = 128
GQA_HEADS = 4
GQA_KV_HEADS = 2
GQA_DIM = 64
DIFF_HEADS = 4
DIFF_QK_DIM = 32
DIFF_V_DIM = 64
N_BRANCH = 4
BRANCH_W = 256
ROPE_THETA = 10000.0
EPS = 1e-6
NEG_INF = -1e30
N_EXPERTS = 16
N_GROUPS = 4
EXPERTS_PER_GROUP = 4
TOP_K = 2
D_EXPERT = 512

MIX_COLS = 3072
IN_COLS = MIX_COLS + N_BRANCH * D_MODEL
NA_Q, NA_K, NA_V = 0, 1, 2
RET_Q, RET_K, RET_V, RET_G = 3, 4, 5, 6
GQA_Q = 7
DIFF_Q, DIFF_K, DIFF_V = 9, 10, 11

LOG2E = math.log2(math.e)
NA_GROUP_ROWS = 4
NA_BAND_ROWS = 12

VMEM_LIMIT = 56 * 1024 * 1024
MOE_BLK = 512
CMB_TM = 512
DISPATCH_FOLD = 1
PROJ_TN = 1792
MERGE_TM = 512


ROW_TILE = D_MODEL // 128


def _store_tile_rows(ref, x):
    n = x.shape[0]
    for j in range(ROW_TILE):
        ref[pl.ds(j, n, stride=ROW_TILE), :] = x[:, j * 128:(j + 1) * 128]


def _load_tile_rows(ref, start, n):
    return jnp.concatenate([ref[pl.ds(start + j, n, stride=ROW_TILE), :] for j in range(ROW_TILE)], axis=-1)


def _dot(a, b):
    return jnp.dot(a, b, preferred_element_type=F32)


def _dot_nt(a, b):
    return lax.dot_general(a, b, (((1,), (1,)), ((), ())), preferred_element_type=F32)


def _sigmoid(x):
    return 1.0 / (1.0 + jnp.exp(-x))


def _params(*sem):
    return pltpu.CompilerParams(dimension_semantics=sem, vmem_limit_bytes=VMEM_LIMIT)


def _mod_kernel(c_ref, w_ref, b_ref, o_ref):
    c = c_ref[...]
    a = (c * _sigmoid(c)).astype(BF16)
    o_ref[...] = _dot(a, w_ref[...].astype(BF16)) + b_ref[...]


def _modulation(cc, w_mod, b_mod):
    nrow = cc.shape[0]
    depth, d, n6 = w_mod.shape
    tn = 1536
    return pl.pallas_call(
        _mod_kernel,
        grid=(depth, n6 // tn),
        in_specs=[pl.BlockSpec((nrow, d), lambda l, j: (0, 0)),
                  pl.BlockSpec((None, d, tn), lambda l, j: (l, 0, j)),
                  pl.BlockSpec((None, 1, tn), lambda l, j: (l, 0, j))],
        out_specs=pl.BlockSpec((None, nrow, tn), lambda l, j: (l, 0, j)),
        out_shape=jax.ShapeDtypeStruct((depth, nrow, n6), F32),
        compiler_params=_params("parallel", "arbitrary"),
        name="modulation",
    )(cc, w_mod, b_mod.reshape(depth, 1, n6))


def _mix_col_scale():
    cs = np.ones((1, IN_COLS), np.float32)
    cs[0, NA_Q * 256:(NA_Q + 1) * 256] = NA_DIM ** -0.5 * LOG2E
    cs[0, DIFF_Q * 256:(DIFF_Q + 1) * 256] = DIFF_QK_DIM ** -0.5 * LOG2E
    cs[0, RET_K * 256:(RET_K + 1) * 256] = RET_DK ** -0.5
    cs[0, MIX_COLS:] = 0.0
    return jnp.asarray(cs)


def _row_blocks(parts, tm, n_lat):
    d = parts[0].shape[1]
    nlat_blk = n_lat // tm
    if len(parts) == 1:
        return [pl.BlockSpec((tm, d), lambda i, *_: (i, 0))]
    return [pl.BlockSpec((tm, d), lambda i, *_: (jnp.minimum(i, nlat_blk - 1), 0)),
            pl.BlockSpec((tm, d), lambda i, *_: (jnp.maximum(i - nlat_blk, 0), 0))]


def _pick_rows(x_refs, nlat_blk):
    if len(x_refs) == 1:
        return x_refs[0][...]
    return jnp.where(pl.program_id(0) >= nlat_blk, x_refs[1][...], x_refs[0][...])


def _proj_kernel(nlat_blk, n_parts, *refs):
    x_refs, (g_ref, mod_ref, w_ref, cs_ref, o_ref, h_ref) = refs[:n_parts], refs[n_parts:]
    j = pl.program_id(1)

    @pl.when(j == 0)
    def _():
        x = _pick_rows(x_refs, nlat_blk)
        y = x * lax.rsqrt(jnp.mean(x * x, axis=-1, keepdims=True) + EPS) * g_ref[...]
        h_ref[...] = (y * (1.0 + mod_ref[1:2, :]) + mod_ref[0:1, :]).astype(BF16)

    acc = _dot(h_ref[...], w_ref[...])
    gate = 0.5 * jnp.tanh(0.5 * acc) + 0.5
    cs = cs_ref[...]
    o_ref[...] = jnp.where(cs > 0.0, acc * cs, gate).astype(BF16)


def _mod_index(tm, n_lat, seq, batch):
    nlat_blk = n_lat // tm
    bpb = seq // tm

    def index(i):
        return jnp.where(i < nlat_blk, i // bpb, batch)
    return index


def _project(x_parts, g, modtab, w_bf, tm, n_lat, seq, batch):
    n_tot = sum(p.shape[0] for p in x_parts)
    d = x_parts[0].shape[1]
    ncol = w_bf.shape[1]
    tn = PROJ_TN
    midx = _mod_index(tm, n_lat, seq, batch)
    return pl.pallas_call(
        functools.partial(_proj_kernel, n_lat // tm, len(x_parts)),
        grid=(n_tot // tm, ncol // tn),
        in_specs=[*_row_blocks(x_parts, tm, n_lat),
                  pl.BlockSpec((1, d), lambda i, j: (0, 0)),
                  pl.BlockSpec((None, 8, d), lambda i, j: (midx(i), 0, 0)),
                  pl.BlockSpec((d, tn), lambda i, j: (0, j)),
                  pl.BlockSpec((1, tn), lambda i, j: (0, j))],
        out_specs=pl.BlockSpec((tm, tn), lambda i, j: (i, j)),
        out_shape=jax.ShapeDtypeStruct((n_tot, ncol), BF16),
        scratch_shapes=[pltpu.VMEM((tm, d), BF16)],
        compiler_params=_params("parallel", "arbitrary"),
        name="norm_project",
    )(*x_parts, g.reshape(1, d), modtab, w_bf, _mix_col_scale())


def _na_kernel(need_ctx, rows, q_ref, k_ref, v_ref, qc_ref, kc_ref, vc_ref, tb_ref, y_ref, *rest):
    if need_ctx:
        yc_ref, vx, vcx = rest
    else:
        vx, vcx = rest
    gq = NA_GROUP_ROWS * GRID_W
    band = NA_BAND_ROWS * GRID_W
    n_groups = rows // NA_GROUP_ROWS
    n_pairs = NA_HEADS // 2
    masks = _lane_masks(2)
    low = masks[0] > 0

    for p in range(n_pairs):
        ps = slice(p * 128, (p + 1) * 128)
        vx[p] = jnp.concatenate([v_ref[:, ps], jnp.ones((v_ref.shape[0], 128), BF16)], axis=-1)
        vcx[p] = jnp.concatenate([vc_ref[:, ps], jnp.ones((vc_ref.shape[0], 128), BF16)], axis=-1)

    def group_body(g, carry):
        u = jnp.clip(g * NA_GROUP_ROWS - NA_WIN_H // 2, 0, rows - NA_BAND_ROWS)
        typ = jnp.where(g == 0, 0, jnp.where(g == n_groups - 1, 2, 1))
        qs = pl.multiple_of(g * gq, gq)
        ks = pl.multiple_of(u * GRID_W, GRID_W)
        for p in range(n_pairs):
            ps = slice(p * 128, (p + 1) * 128)
            q = q_ref[pl.ds(qs, gq), ps]
            kb = k_ref[pl.ds(ks, band), ps]
            kc = kc_ref[:, ps]
            outs = []
            for hh in range(2):
                qm = q * masks[hh]
                s_w = _dot_nt(qm, kb) + tb_ref[2 * p + hh, typ]
                s_c = _dot_nt(qm, kc)
                m = jnp.maximum(jnp.max(s_w, axis=-1, keepdims=True), jnp.max(s_c, axis=-1, keepdims=True))
                oe = (_dot(jnp.exp2((s_w - m).astype(BF16)), vx[p, pl.ds(ks, band), :])
                      + _dot(jnp.exp2((s_c - m).astype(BF16)), vcx[p]))
                outs.append(oe[:, :128] / oe[:, 128:])
            y_ref[pl.ds(qs, gq), ps] = jnp.where(low, outs[0], outs[1]).astype(BF16)
        return carry

    lax.fori_loop(0, n_groups, group_body, 0, unroll=2)

    if need_ctx:
        for p in range(n_pairs):
            ps = slice(p * 128, (p + 1) * 128)
            outs = [_softmax_pv(qc_ref[:, ps] * masks[hh], kc_ref[:, ps], vcx[p]) for hh in range(2)]
            yc_ref[:, ps] = jnp.where(low, outs[0], outs[1]).astype(BF16)


def _na_bias_table(rpb, rows):
    assert rows % NA_GROUP_ROWS == 0 and rows >= NA_BAND_ROWS + NA_GROUP_ROWS
    n_groups = rows // NA_GROUP_ROWS
    qc = np.arange(GRID_W)[:, None]
    kc = np.arange(GRID_W)[None, :]
    win_start = np.clip(qc - NA_WIN_W // 2, 0, GRID_W - NA_WIN_W)
    col_ok = (kc >= win_start) & (kc < win_start + NA_WIN_W)
    col_idx = np.clip(kc - qc, -(NA_WIN_W - 1), NA_WIN_W - 1) + NA_WIN_W - 1
    n_col = 2 * NA_WIN_W - 1
    onehot = jnp.asarray((np.arange(n_col)[:, None] == col_idx.reshape(1, -1)).astype(np.float32))
    picked = jnp.dot(rpb.astype(F32).reshape(-1, n_col), onehot, precision=lax.Precision.HIGHEST)
    picked = picked.reshape(NA_HEADS, 2 * NA_WIN_H - 1, GRID_W, GRID_W)
    tiles = jnp.where(col_ok[None, None], picked * LOG2E, NEG_INF)
    masked = jnp.full((NA_HEADS, GRID_W, GRID_W), NEG_INF, F32)
    tables = []
    for g in (0, 1, n_groups - 1):
        u = int(np.clip(g * NA_GROUP_ROWS - NA_WIN_H // 2, 0, rows - NA_BAND_ROWS))
        per_row = []
        for a in range(NA_GROUP_ROWS):
            r = g * NA_GROUP_ROWS + a
            r0 = int(np.clip(r - NA_WIN_H // 2, 0, rows - NA_WIN_H))
            pieces = [tiles[:, kr - r + NA_WIN_H - 1] if r0 <= kr < r0 + NA_WIN_H else masked
                      for kr in range(u, u + NA_BAND_ROWS)]
            per_row.append(jnp.concatenate(pieces, axis=-1))
        tables.append(jnp.concatenate(per_row, axis=1))
    return jnp.stack(tables, axis=1)


def _na_mixer(p_all, rpb, batch, seq, n_ctx, need_ctx):
    n_lat = batch * seq
    rows = seq // GRID_W
    tb = _na_bias_table(rpb, rows)
    cb = n_lat // n_ctx
    w = 256
    lat = lambda c: pl.BlockSpec((seq, w), lambda b, c=c: (b, c))
    ctx = lambda c: pl.BlockSpec((n_ctx, w), lambda b, c=c: (cb + b, c))
    out_shape = [jax.ShapeDtypeStruct((n_lat, w), BF16)]
    out_specs = [pl.BlockSpec((seq, w), lambda b: (b, 0))]
    if need_ctx:
        out_shape.append(jax.ShapeDtypeStruct((batch * n_ctx, w), BF16))
        out_specs.append(pl.BlockSpec((n_ctx, w), lambda b: (b, 0)))
    return pl.pallas_call(
        functools.partial(_na_kernel, need_ctx, rows),
        grid=(batch,),
        in_specs=[lat(NA_Q), lat(NA_K), lat(NA_V), ctx(NA_Q), ctx(NA_K), ctx(NA_V),
                  pl.BlockSpec(tb.shape, lambda b: (0, 0, 0, 0))],
        out_specs=out_specs,
        out_shape=out_shape,
        scratch_shapes=[pltpu.VMEM((NA_HEADS // 2, seq, 256), BF16), pltpu.VMEM((NA_HEADS // 2, n_ctx, 256), BF16)],
        compiler_params=_params("parallel"),
        name="na_mixer",
    )(p_all, p_all, p_all, p_all, p_all, p_all, tb)


def _ret_kernel(need_ctx, seq, n_ctx, lg_ref, q_ref, k_ref, v_ref, g_ref, qc_ref, kc_ref, vc_ref, gc_ref,
                cos_ref, sin_ref, perm_ref, bd_ref, y_ref, *rest):
    if need_ctx:
        yc_ref, qr, kr, krt, kct, o_f, o_b, oc_f, oc_b, dm, qd, kdt = rest
    else:
        qr, kr, krt, kct, o_f, o_b, oc_f, oc_b, dm, qd, kdt = rest
    ch = RET_CHUNK
    n_lat_ch = seq // ch
    n_ctx_ch = n_ctx // ch
    n_pairs = RET_HEADS // 2
    assert ch == 128 and 2 * RET_DK == 128

    def prep(i, carry):
        rs = pl.multiple_of(i * ch, ch)
        c = cos_ref[pl.ds(rs, ch), :]
        s = sin_ref[pl.ds(rs, ch), :]
        qr[pl.ds(rs, ch), :] = _rope_mxu(q_ref[pl.ds(rs, ch), :], c, s, perm_ref)
        k = _rope_mxu(k_ref[pl.ds(rs, ch), :], c, s, perm_ref)
        kr[pl.ds(rs, ch), :] = k
        for p in range(n_pairs):
            krt[i, p * 128:(p + 1) * 128, :] = k[:, p * 128:(p + 1) * 128].T
        return carry

    lax.fori_loop(0, n_lat_ch, prep, 0, unroll=2)
    for n in range(n_ctx_ch):
        kc = kc_ref[n * ch:(n + 1) * ch, :].astype(F32)
        for p in range(n_pairs):
            kct[n, p * 128:(p + 1) * 128, :] = kc[:, p * 128:(p + 1) * 128].T

    ri = lax.broadcasted_iota(jnp.int32, (ch, ch), 0)
    ci = lax.broadcasted_iota(jnp.int32, (ch, ch), 1)
    lag = (ri - ci).astype(F32)
    rowf = ri.astype(F32)
    colf = ci.astype(F32)
    low_lane = ci < RET_DK
    low_row = ri < RET_DK
    block_diag = jnp.where(low_lane == low_row, 1.0, 0.0)
    low, high = _lane_masks(2)

    chains = [(p, dirn) for p in range(n_pairs) for dirn in range(2)]
    cdec = []
    for c, (p, dirn) in enumerate(chains):
        lg_a = lg_ref[dirn, 2 * p]
        lg_b = lg_ref[dirn, 2 * p + 1]
        lg_lane = jnp.where(low_lane, lg_a, lg_b)
        lg_row = jnp.where(low_row, lg_a, lg_b)
        for hh, lg in enumerate((lg_a, lg_b)):
            if dirn == 0:
                keep = ri >= ci
                dm[2 * c + hh] = jnp.where(keep, jnp.exp(jnp.where(keep, lag, 0.0) * lg), 0.0)
            else:
                keep = ci >= ri
                dm[2 * c + hh] = jnp.where(keep, jnp.exp(jnp.where(keep, -lag, 0.0) * lg), 0.0)
        if dirn == 0:
            qd[c] = jnp.exp((rowf + 1.0) * lg_lane)
            kdt[c] = jnp.exp((ch - 1.0 - colf) * lg_row)
        else:
            qd[c] = jnp.exp((ch - rowf) * lg_lane)
            kdt[c] = jnp.exp(colf * lg_row)
        cdec.append(jnp.exp(ch * lg_lane[0:1, :]))

    def step(c, state, qi, ki, kti, vi):
        kb = ki.astype(BF16)
        inner_a = (_dot_nt((qi * low).astype(BF16), kb) * dm[2 * c]).astype(BF16)
        inner_b = (_dot_nt((qi * high).astype(BF16), kb) * dm[2 * c + 1]).astype(BF16)
        o = (jnp.where(low_lane, _dot(inner_a, vi), _dot(inner_b, vi))
             + _dot((qi * qd[c]).astype(BF16), state.astype(BF16)))
        state = state * cdec[c] + block_diag * _dot((kti * kdt[c]).astype(BF16), vi)
        return state, o

    states = []
    for c, (p, dirn) in enumerate(chains):
        ps = slice(p * 128, (p + 1) * 128)
        state = jnp.zeros((128, 128), F32)
        for n in (range(n_ctx_ch) if dirn == 0 else range(n_ctx_ch - 1, -1, -1)):
            rs = slice(n * ch, (n + 1) * ch)
            state, o = step(c, state, qc_ref[rs, ps].astype(F32), kc_ref[rs, ps].astype(F32), kct[n, ps, :],
                            vc_ref[rs, ps])
            if need_ctx:
                (oc_f if dirn == 0 else oc_b)[rs, ps] = o
        states.append(state)

    def lat_body(i, states):
        new = []
        for c, (p, dirn) in enumerate(chains):
            ps = slice(p * 128, (p + 1) * 128)
            n = i if dirn == 0 else n_lat_ch - 1 - i
            rs = pl.ds(pl.multiple_of(n * ch, ch), ch)
            state, o = step(c, states[c], qr[rs, ps], kr[rs, ps], krt[n, ps, :], v_ref[rs, ps])
            (o_f if dirn == 0 else o_b)[rs, ps] = o
            new.append(state)
        return tuple(new)

    lax.fori_loop(0, n_lat_ch, lat_body, tuple(states), unroll=4)

    def head_mean(x):
        hi = x.astype(BF16)
        lo = (x - hi.astype(F32)).astype(BF16)
        bd = bd_ref[...]
        return (_dot(hi, bd) + _dot(lo, bd)) * (1.0 / RET_DK)

    def finish(o, g):
        cen = o - head_mean(o)
        on = cen * lax.rsqrt(head_mean(cen * cen) + EPS)
        g = g.astype(F32)
        return (g * _sigmoid(g) * on).astype(BF16)

    def fin_body(i, carry):
        rs = pl.multiple_of(i * ch, ch)
        y_ref[pl.ds(rs, ch), :] = finish(o_f[pl.ds(rs, ch), :] + o_b[pl.ds(rs, ch), :], g_ref[pl.ds(rs, ch), :])
        return carry

    lax.fori_loop(0, n_lat_ch, fin_body, 0, unroll=2)
    if need_ctx:
        for n in range(n_ctx_ch):
            rs = slice(n * ch, (n + 1) * ch)
            yc_ref[rs, :] = finish(oc_f[rs, :] + oc_b[rs, :], gc_ref[rs, :])


def _rope_tables(seq, head_dim, width, dist):
    half = head_dim // 2
    nf = half // 2
    assert nf == dist
    inv = 1.0 / (ROPE_THETA ** (np.arange(nf, dtype=np.float32) / nf))
    t = np.arange(seq)
    rows = (t // GRID_W).astype(np.float32)[:, None] * inv[None, :]
    cols = (t % GRID_W).astype(np.float32)[:, None] * inv[None, :]
    cos = np.concatenate([np.cos(rows), np.cos(rows), np.cos(cols), np.cos(cols)], axis=-1)
    sin = np.concatenate([-np.sin(rows), np.sin(rows), -np.sin(cols), np.sin(cols)], axis=-1)
    reps = width // head_dim
    return (jnp.asarray(np.tile(cos, (1, reps)), F32), jnp.asarray(np.tile(sin, (1, reps)), F32))


def _ret_mixer(p_all, log_decay, batch, seq, n_ctx, need_ctx):
    n_lat = batch * seq
    cb = n_lat // n_ctx
    w = 256
    ch = RET_CHUNK
    log_gamma = jnp.log1p(-jnp.exp(log_decay.astype(F32)))
    cos, sin = _rope_tables(seq, RET_DK, w, 16)
    lat = lambda c: pl.BlockSpec((seq, w), lambda b, c=c: (b, c))
    ctx = lambda c: pl.BlockSpec((n_ctx, w), lambda b, c=c: (cb + b, c))
    whole = pl.BlockSpec((seq, w), lambda b: (0, 0))
    out_shape = [jax.ShapeDtypeStruct((n_lat, w), BF16)]
    out_specs = [pl.BlockSpec((seq, w), lambda b: (b, 0))]
    if need_ctx:
        out_shape.append(jax.ShapeDtypeStruct((batch * n_ctx, w), BF16))
        out_specs.append(pl.BlockSpec((n_ctx, w), lambda b: (b, 0)))
    return pl.pallas_call(
        functools.partial(_ret_kernel, need_ctx, seq, n_ctx),
        grid=(batch,),
        in_specs=[pl.BlockSpec(memory_space=pltpu.SMEM),
                  lat(RET_Q), lat(RET_K), lat(RET_V), lat(RET_G),
                  ctx(RET_Q), ctx(RET_K), ctx(RET_V), ctx(RET_G), whole, whole,
                  pl.BlockSpec((w, w), lambda b: (0, 0)), pl.BlockSpec((w, w), lambda b: (0, 0))],
        out_specs=out_specs,
        out_shape=out_shape,
        scratch_shapes=[pltpu.VMEM((seq, w), F32), pltpu.VMEM((seq, w), F32),
                        pltpu.VMEM((seq // ch, w, ch), F32), pltpu.VMEM((n_ctx // ch, w, ch), F32),
                        pltpu.VMEM((seq, w), F32), pltpu.VMEM((seq, w), F32),
                        pltpu.VMEM((n_ctx, w), F32), pltpu.VMEM((n_ctx, w), F32),
                        pltpu.VMEM((2 * RET_HEADS, ch, ch), F32),
                        pltpu.VMEM((RET_HEADS, ch, 128), F32), pltpu.VMEM((RET_HEADS, 128, ch), F32)],
        compiler_params=_params("parallel"),
        name="retention_mixer",
    )(log_gamma, p_all, p_all, p_all, p_all, p_all, p_all, p_all, p_all, cos, sin,
      _swap_matrix(w, 16), _block_ones(w, RET_DK))


def _swap_matrix(width, dist):
    i = np.arange(width)
    partner = np.where(i % (2 * dist) < dist, i + dist, i - dist)
    p = np.zeros((width, width), np.float32)
    p[partner, i] = 1.0
    return jnp.asarray(p, BF16)


def _block_ones(width, block):
    i = np.arange(width)
    return jnp.asarray((i[:, None] // block == i[None, :] // block).astype(np.float32), BF16)


def _sumsq_blocks(xf, bd_ref):
    sq = xf * xf
    hi = sq.astype(BF16)
    lo = (sq - hi.astype(F32)).astype(BF16)
    bd = bd_ref[...]
    return _dot(hi, bd) + _dot(lo, bd)


def _rope_mxu(x, cos, sin, perm_ref):
    return x.astype(F32) * cos + _dot(x, perm_ref[...]) * sin


def _softmax_pv(qm, k, v_ones):
    s = _dot_nt(qm, k)
    p = jnp.exp2((s - jnp.max(s, axis=-1, keepdims=True)).astype(BF16))
    oe = _dot(p, v_ones)
    return oe[:, :128] / oe[:, 128:]


def _lane_masks(n_parts):
    lane = lax.broadcasted_iota(jnp.int32, (1, 128), 1)
    return [jnp.where(lane // (128 // n_parts) == i, 1.0, 0.0).astype(BF16) for i in range(n_parts)]


def _gqa_kernel(n_lat_k, n_ctx, *refs):
    if n_lat_k:
        (q_ref, k_ref, v_ref, kc_ref, vc_ref, qa_ref, qb_ref, ka_ref, kb_ref, kg_ref, pq_ref, pk_ref,
         bdq_ref, bdk_ref, y_ref, kp, vx) = refs
    else:
        q_ref, kc_ref, vc_ref, qa_ref, kg_ref, bdq_ref, bdk_ref, y_ref, kp, vx = refs
    dim = GQA_DIM
    nk = n_lat_k + n_ctx

    def inv_rms(xf, bd_ref):
        return lax.rsqrt(_sumsq_blocks(xf, bd_ref) * (1.0 / dim) + EPS)

    @pl.when(pl.program_id(1) == 0)
    def _():
        def put(rs, kn, v):
            ones = jnp.ones(v.shape, BF16)
            kp[0, rs, :] = kn.astype(BF16)
            kp[1, rs, :] = pltpu.roll(kn, dim, 1).astype(BF16)
            vx[0, rs, :] = jnp.concatenate([v, ones], axis=-1)
            vx[1, rs, :] = jnp.concatenate([pltpu.roll(v.astype(F32), dim, 1).astype(BF16), ones], axis=-1)

        kc = kc_ref[...]
        kcf = kc.astype(F32)
        put(slice(n_lat_k, nk), kcf * kg_ref[...] * inv_rms(kcf, bdk_ref), vc_ref[...])
        ck = 512
        for c in range(n_lat_k // ck):
            rs = slice(c * ck, (c + 1) * ck)
            k = k_ref[rs, :]
            put(rs, _rope_mxu(k, ka_ref[rs, :], kb_ref[rs, :], pk_ref) * inv_rms(k.astype(F32), bdk_ref), v_ref[rs, :])

    q = q_ref[...]
    qf = q.astype(F32)
    qn = _rope_mxu(q, qa_ref[...], qb_ref[...], pq_ref) if n_lat_k else qf * qa_ref[...]
    qn = (qn * inv_rms(qf, bdq_ref)).astype(BF16)
    low, high = _lane_masks(2)
    outs = []
    for g in range(GQA_KV_HEADS):
        qv = qn[:, g * 128:(g + 1) * 128]
        o_low = _softmax_pv(qv * low, kp[g], vx[g])
        o_high = _softmax_pv(qv * high, kp[1 - g], vx[1 - g])
        outs.append(jnp.where(low > 0, o_low, o_high))
    y_ref[...] = jnp.concatenate(outs, axis=-1).astype(BF16)


def _gqa_mixer(p_all, q_gain, k_gain, batch, seq, n_ctx, need_ctx):
    n_lat = batch * seq
    cb = n_lat // n_ctx
    dim = GQA_DIM
    qw = GQA_HEADS * dim
    kvw = GQA_KV_HEADS * dim
    assert qw == 256 and kvw == 128

    def swapped(g):
        return g.reshape(-1, 2, 16)[:, ::-1].reshape(1, -1)

    qg = jnp.tile(q_gain.astype(F32) * (dim ** -0.5 * LOG2E), GQA_HEADS).reshape(1, qw)
    kg = jnp.tile(k_gain.astype(F32), GQA_KV_HEADS).reshape(1, kvw)
    cq, sq = _rope_tables(seq, dim, qw, 16)
    ck, sk = _rope_tables(seq, dim, kvw, 16)
    qa, qb = cq * qg, sq * swapped(qg)
    ka, kb = ck * kg, sk * swapped(kg)
    pq, pk = _swap_matrix(qw, 16), _swap_matrix(kvw, 16)
    bdq, bdk = _block_ones(qw, dim), _block_ones(kvw, dim)
    tq = 512
    nqb = seq // tq
    const = lambda shape: pl.BlockSpec(shape, lambda b, i: (0,) * len(shape))
    scratch = lambda nk: [pltpu.VMEM((2, nk, kvw), BF16), pltpu.VMEM((2, nk, 2 * kvw), BF16)]
    y_lat = pl.pallas_call(
        functools.partial(_gqa_kernel, seq, n_ctx),
        grid=(batch, nqb),
        in_specs=[pl.BlockSpec((tq, qw), lambda b, i: (b * nqb + i, GQA_Q)),
                  pl.BlockSpec((seq, kvw), lambda b, i: (b, 16)),
                  pl.BlockSpec((seq, kvw), lambda b, i: (b, 17)),
                  pl.BlockSpec((n_ctx, kvw), lambda b, i: (cb + b, 16)),
                  pl.BlockSpec((n_ctx, kvw), lambda b, i: (cb + b, 17)),
                  pl.BlockSpec((tq, qw), lambda b, i: (i, 0)),
                  pl.BlockSpec((tq, qw), lambda b, i: (i, 0)),
                  const((seq, kvw)), const((seq, kvw)), const((1, kvw)),
                  const((qw, qw)), const((kvw, kvw)), const((qw, qw)), const((kvw, kvw))],
        out_specs=pl.BlockSpec((tq, qw), lambda b, i: (b * nqb + i, 0)),
        out_shape=jax.ShapeDtypeStruct((n_lat, qw), BF16),
        scratch_shapes=scratch(seq + n_ctx),
        compiler_params=_params("parallel", "arbitrary"),
        name="gqa_mixer",
    )(p_all, p_all, p_all, p_all, p_all, qa, qb, ka, kb, kg, pq, pk, bdq, bdk)
    if not need_ctx:
        return y_lat, None
    y_ctx = pl.pallas_call(
        functools.partial(_gqa_kernel, 0, n_ctx),
        grid=(batch, 1),
        in_specs=[pl.BlockSpec((n_ctx, qw), lambda b, i: (cb + b, GQA_Q)),
                  pl.BlockSpec((n_ctx, kvw), lambda b, i: (cb + b, 16)),
                  pl.BlockSpec((n_ctx, kvw), lambda b, i: (cb + b, 17)),
                  const((1, qw)), const((1, kvw)), const((qw, qw)), const((kvw, kvw))],
        out_specs=pl.BlockSpec((n_ctx, qw), lambda b, i: (b, 0)),
        out_shape=jax.ShapeDtypeStruct((batch * n_ctx, qw), BF16),
        scratch_shapes=scratch(n_ctx),
        compiler_params=_params("parallel", "arbitrary"),
        name="gqa_mixer_ctx",
    )(p_all, p_all, p_all, qg, kg, bdq, bdk)
    return y_lat, y_ctx


def _diff_kernel(n_lat_k, n_ctx, lam_init, *refs):
    if n_lat_k:
        (q_ref, k_ref, v_ref, kc_ref, vc_ref, lp_ref, sg_ref, cq_ref, sq_ref, ck_ref, sk_ref, perm_ref, bd_ref,
         y_ref, kp, vx) = refs
    else:
        q_ref, kc_ref, vc_ref, lp_ref, sg_ref, bd_ref, y_ref, kp, vx = refs
    nk = n_lat_k + n_ctx
    n_pairs = DIFF_HEADS // 2

    @pl.when(pl.program_id(1) == 0)
    def _():
        def put(rs, k, v):
            kp[rs, :] = k
            ones = jnp.ones((v.shape[0], 128), BF16)
            for pr in range(n_pairs):
                vx[pr, rs, :] = jnp.concatenate([v[:, pr * 128:(pr + 1) * 128], ones], axis=-1)

        put(slice(n_lat_k, nk), kc_ref[...], vc_ref[...])
        ck = 512
        for c in range(n_lat_k // ck):
            rs = slice(c * ck, (c + 1) * ck)
            put(rs, _rope_mxu(k_ref[rs, :], ck_ref[rs, :], sk_ref[rs, :], perm_ref).astype(BF16), v_ref[rs, :])

    lp = lp_ref[...]
    lam = (jnp.exp(jnp.sum(lp[0:1, :] * lp[1:2, :], axis=-1, keepdims=True))
           - jnp.exp(jnp.sum(lp[2:3, :] * lp[3:4, :], axis=-1, keepdims=True)) + lam_init)

    q = q_ref[...]
    if n_lat_k:
        q = _rope_mxu(q, cq_ref[...], sq_ref[...], perm_ref).astype(BF16)
    quarter = _lane_masks(4)
    low = _lane_masks(2)[0]
    outs = []
    for pr in range(n_pairs):
        ps = slice(pr * 128, (pr + 1) * 128)
        qv = q[:, ps]
        kv = kp[:, ps]
        o_head = [_softmax_pv(qv * quarter[2 * hh], kv, vx[pr]) - lam * _softmax_pv(qv * quarter[2 * hh + 1], kv, vx[pr])
                  for hh in range(2)]
        outs.append(jnp.where(low > 0, o_head[0], o_head[1]))
    o = jnp.concatenate(outs, axis=-1)
    inv = lax.rsqrt(_sumsq_blocks(o, bd_ref) * (1.0 / DIFF_V_DIM) + EPS)
    y_ref[...] = (o * inv * sg_ref[...] * (1.0 - lam_init)).astype(BF16)


def _diff_mixer(p_all, lam_params, subln, layer_idx, batch, seq, n_ctx, need_ctx):
    n_lat = batch * seq
    cb = n_lat // n_ctx
    lam_init = 0.8 - 0.6 * math.exp(-0.3 * layer_idx)
    lp = jnp.zeros((8, 128), F32).at[:4, :DIFF_QK_DIM].set(lam_params.astype(F32))
    w = DIFF_HEADS * DIFF_V_DIM
    assert w == 256 and DIFF_HEADS * 2 * DIFF_QK_DIM == w
    sg = jnp.tile(subln.astype(F32), DIFF_HEADS).reshape(1, w)
    cq, sq = _rope_tables(seq, DIFF_QK_DIM, w, 8)
    perm = _swap_matrix(w, 8)
    bd = _block_ones(w, DIFF_V_DIM)
    tq = 512
    nqb = seq // tq
    const = lambda shape: pl.BlockSpec(shape, lambda b, i: (0,) * len(shape))
    scratch = lambda nk: [pltpu.VMEM((nk, w), BF16), pltpu.VMEM((DIFF_HEADS // 2, nk, 256), BF16)]
    y_lat = pl.pallas_call(
        functools.partial(_diff_kernel, seq, n_ctx, lam_init),
        grid=(batch, nqb),
        in_specs=[pl.BlockSpec((tq, w), lambda b, i: (b * nqb + i, DIFF_Q)),
                  pl.BlockSpec((seq, w), lambda b, i: (b, DIFF_K)),
                  pl.BlockSpec((seq, w), lambda b, i: (b, DIFF_V)),
                  pl.BlockSpec((n_ctx, w), lambda b, i: (cb + b, DIFF_K)),
                  pl.BlockSpec((n_ctx, w), lambda b, i: (cb + b, DIFF_V)),
                  const((8, 128)), const((1, w)),
                  pl.BlockSpec((tq, w), lambda b, i: (i, 0)),
                  pl.BlockSpec((tq, w), lambda b, i: (i, 0)),
                  const((seq, w)), const((seq, w)), const((w, w)), const((w, w))],
        out_specs=pl.BlockSpec((tq, w), lambda b, i: (b * nqb + i, 0)),
        out_shape=jax.ShapeDtypeStruct((n_lat, w), BF16),
        scratch_shapes=scratch(seq + n_ctx),
        compiler_params=_params("parallel", "arbitrary"),
        name="diff_mixer",
    )(p_all, p_all, p_all, p_all, p_all, lp, sg, cq, sq, cq, sq, perm, bd)
    if not need_ctx:
        return y_lat, None
    y_ctx = pl.pallas_call(
        functools.partial(_diff_kernel, 0, n_ctx, lam_init),
        grid=(batch, 1),
        in_specs=[pl.BlockSpec((n_ctx, w), lambda b, i: (cb + b, DIFF_Q)),
                  pl.BlockSpec((n_ctx, w), lambda b, i: (cb + b, DIFF_K)),
                  pl.BlockSpec((n_ctx, w), lambda b, i: (cb + b, DIFF_V)),
                  const((8, 128)), const((1, w)), const((w, w))],
        out_specs=pl.BlockSpec((n_ctx, w), lambda b, i: (b, 0)),
        out_shape=jax.ShapeDtypeStruct((batch * n_ctx, w), BF16),
        scratch_shapes=scratch(n_ctx),
        compiler_params=_params("parallel", "arbitrary"),
        name="diff_mixer_ctx",
    )(p_all, p_all, p_all, lp, sg, bd)
    return y_lat, y_ctx


def _merge_kernel(nlat_blk, has_ctx, n_parts, *refs):
    x_refs, refs = refs[:n_parts], refs[n_parts:]
    g_refs = refs[:N_BRANCH]
    y_refs = refs[N_BRANCH:2 * N_BRANCH]
    refs = refs[2 * N_BRANCH:]
    if has_ctx:
        yc_refs, refs = refs[:N_BRANCH], refs[N_BRANCH:]
        is_ctx = pl.program_id(0) >= nlat_blk
    wb_ref, wo_ref, mod_ref, gn_ref, wr_ref, xo_ref, m_ref, lg_ref = refs
    acc = None
    for n in range(N_BRANCH):
        y = y_refs[n][...]
        if has_ctx:
            y = jnp.where(is_ctx, yc_refs[n][...], y)
        term = g_refs[n][...].astype(F32) * _dot(y, wb_ref[n])
        acc = term if acc is None else acc + term
    y = _dot(acc.astype(BF16), wo_ref[...])
    x = _pick_rows(x_refs, nlat_blk) + mod_ref[2:3, :] * y
    xo_ref[...] = x
    xn = x * lax.rsqrt(jnp.mean(x * x, axis=-1, keepdims=True) + EPS) * gn_ref[...]
    m = xn * (1.0 + mod_ref[4:5, :]) + mod_ref[3:4, :]
    _store_tile_rows(m_ref, m)
    m_hi = m.astype(BF16)
    m_lo = (m - m_hi.astype(F32)).astype(BF16)
    w = wr_ref[...]
    w_hi = w.astype(BF16)
    w_lo = (w - w_hi.astype(F32)).astype(BF16)
    both = _dot_nt(jnp.concatenate([w_hi, w_lo], axis=0), m_hi)
    lg_ref[...] = both[:N_EXPERTS] + (_dot_nt(w_hi, m_lo) + both[N_EXPERTS:])


def _merge(x_parts, p_all, ys_lat, ys_ctx, wb_bf, wo_bf, modtab, gn, wr_t, tm, n_rows, n_lat, seq, batch):
    d = D_MODEL
    midx = _mod_index(tm, n_lat, seq, batch)
    nlat_blk = n_lat // tm
    has_ctx = ys_ctx is not None
    gate = lambda n: pl.BlockSpec((tm, d), lambda i, n=n: (i, MIX_COLS // d + n))
    y_specs = [pl.BlockSpec((tm, BRANCH_W), lambda i: (jnp.minimum(i, nlat_blk - 1), 0))] * N_BRANCH
    ys = list(ys_lat)
    if has_ctx:
        y_specs += [pl.BlockSpec((tm, BRANCH_W), lambda i: (jnp.maximum(i - nlat_blk, 0), 0))] * N_BRANCH
        ys += list(ys_ctx)
    return pl.pallas_call(
        functools.partial(_merge_kernel, nlat_blk, has_ctx, len(x_parts)),
        grid=(n_rows // tm,),
        in_specs=[*_row_blocks(x_parts, tm, n_lat),
                  gate(0), gate(1), gate(2), gate(3), *y_specs,
                  pl.BlockSpec((N_BRANCH, BRANCH_W, d), lambda i: (0, 0, 0)),
                  pl.BlockSpec((d, d), lambda i: (0, 0)),
                  pl.BlockSpec((None, 8, d), lambda i: (midx(i), 0, 0)),
                  pl.BlockSpec((1, d), lambda i: (0, 0)),
                  pl.BlockSpec((N_EXPERTS, d), lambda i: (0, 0))],
        out_specs=[pl.BlockSpec((tm, d), lambda i: (i, 0)),
                   pl.BlockSpec((tm * ROW_TILE, 128), lambda i: (i, 0)),
                   pl.BlockSpec((N_EXPERTS, tm), lambda i: (0, i))],
        out_shape=[jax.ShapeDtypeStruct((n_rows, d), F32),
                   jax.ShapeDtypeStruct((n_rows * ROW_TILE, 128), F32),
                   jax.ShapeDtypeStruct((N_EXPERTS, n_rows), F32)],
        compiler_params=_params("parallel"),
        name="merge_norm_route",
    )(*x_parts, p_all, p_all, p_all, p_all, *ys, wb_bf, wo_bf, modtab, gn.reshape(1, d), wr_t)


def _route_kernel(lg_ref, b_ref, tri_ref, idx_ref, w_ref, rank_ref, cnt_ref):
    s = _sigmoid(lg_ref[...])
    sel = s + b_ref[...]
    row = lambda a, e: a[e:e + 1, :]
    gsz = EXPERTS_PER_GROUP
    g_idx = None
    best = None
    for g in range(N_GROUPS):
        v = [row(sel, g * gsz + i) for i in range(gsz)]
        score = None
        for i in range(gsz):
            for j in range(i + 1, gsz):
                pair = v[i] + v[j]
                score = pair if score is None else jnp.maximum(score, pair)
        if g == 0:
            best, g_idx = score, jnp.zeros(score.shape, jnp.int32)
        else:
            better = score > best
            best = jnp.where(better, score, best)
            g_idx = jnp.where(better, g, g_idx)

    def in_group(a, i):
        out = row(a, i)
        for g in range(1, N_GROUPS):
            out = jnp.where(g_idx == g, row(a, g * gsz + i), out)
        return out

    v = [in_group(sel, i) for i in range(gsz)]
    sv = [in_group(s, i) for i in range(gsz)]

    def arg_first_max(vals):
        bv, bi = vals[0], jnp.zeros(vals[0].shape, jnp.int32)
        for i in range(1, gsz):
            better = vals[i] > bv
            bv = jnp.where(better, vals[i], bv)
            bi = jnp.where(better, i, bi)
        return bi

    i1 = arg_first_max(v)
    i2 = arg_first_max([jnp.where(i1 == i, -jnp.inf, v[i]) for i in range(gsz)])

    def pick(vals, idx):
        out = vals[0]
        for i in range(1, gsz):
            out = jnp.where(idx == i, vals[i], out)
        return out

    w1 = pick(sv, i1)
    w2 = pick(sv, i2)
    tot = w1 + w2
    e1 = g_idx * gsz + i1
    e2 = g_idx * gsz + i2
    idx_ref[0:1, :] = e1
    idx_ref[1:2, :] = e2
    w_ref[0:1, :] = w1 / tot
    w_ref[1:2, :] = w2 / tot

    @pl.when(pl.program_id(0) == 0)
    def _():
        cnt_ref[...] = jnp.zeros(cnt_ref.shape, cnt_ref.dtype)

    n_e, tn = s.shape
    erow = lax.broadcasted_iota(jnp.int32, (n_e, tn), 0)
    oh1 = jnp.where(erow == e1, 1.0, 0.0)
    oh2 = jnp.where(erow == e2, 1.0, 0.0)
    oh = (oh1 + oh2).astype(BF16)
    base = cnt_ref[...]
    r1, r2 = [], []
    for c in range(tn // 128):
        cs = slice(c * 128, (c + 1) * 128)
        before = base + _dot(oh[:, cs], tri_ref[...])
        r1.append(jnp.sum(oh1[:, cs] * before, axis=0, keepdims=True))
        r2.append(jnp.sum(oh2[:, cs] * before, axis=0, keepdims=True))
        base = base + jnp.sum(oh[:, cs].astype(F32), axis=1, keepdims=True)
    cnt_ref[...] = base
    rank_ref[0:1, :] = jnp.concatenate(r1, axis=-1).astype(jnp.int32)
    rank_ref[1:2, :] = jnp.concatenate(r2, axis=-1).astype(jnp.int32)


def _route(logits_t, router_bias):
    e, n = logits_t.shape
    tn = math.gcd(n, 2048)
    i = np.arange(128)
    tri = jnp.asarray((i[:, None] < i[None, :]).astype(np.float32), BF16)
    idx2, w2, rank, cnt = pl.pallas_call(
        _route_kernel,
        grid=(n // tn,),
        in_specs=[pl.BlockSpec((e, tn), lambda i: (0, i)),
                  pl.BlockSpec((e, 1), lambda i: (0, 0)),
                  pl.BlockSpec((128, 128), lambda i: (0, 0))],
        out_specs=[pl.BlockSpec((TOP_K, tn), lambda i: (0, i)),
                   pl.BlockSpec((TOP_K, tn), lambda i: (0, i)),
                   pl.BlockSpec((TOP_K, tn), lambda i: (0, i)),
                   pl.BlockSpec((e, 128), lambda i: (0, 0))],
        out_shape=[jax.ShapeDtypeStruct((TOP_K, n), jnp.int32),
                   jax.ShapeDtypeStruct((TOP_K, n), F32),
                   jax.ShapeDtypeStruct((TOP_K, n), jnp.int32),
                   jax.ShapeDtypeStruct((e, 128), F32)],
        compiler_params=_params("arbitrary"),
        name="route_top2",
    )(logits_t, router_bias.astype(F32).reshape(e, 1), tri)
    return idx2, w2, rank, cnt[:, 0].astype(jnp.int32)


def _dispatch_kernel(n_tok_steps, tm, pos_ref, m_ref, xs_out, zrow, sem):
    i = pl.program_id(0)
    rt = ROW_TILE
    n = pos_ref.shape[1]

    def wait():
        pltpu.make_async_copy(xs_out.at[pl.ds(0, n * rt)], xs_out.at[pl.ds(0, n * rt)], sem).wait()

    @pl.when(i < n_tok_steps)
    def _():
        for j in range(n):
            part, within = divmod(j, TOP_K * tm)
            row = part * tm + within % tm
            dst = pl.multiple_of(pos_ref[i, j], rt)
            pltpu.make_async_copy(m_ref.at[pl.ds(row * rt, rt)], xs_out.at[pl.ds(dst, rt)],
                                  sem).start(priority=j % 2)
        wait()

    @pl.when(i >= n_tok_steps)
    def _():
        zrow[...] = jnp.zeros(zrow.shape, zrow.dtype)
        for j in range(n):
            dst = pl.multiple_of(pos_ref[i, j], rt)
            pltpu.make_async_copy(zrow, xs_out.at[pl.ds(dst, rt)], sem).start(priority=j % 2)
        wait()


def _dispatch(m_tiles, pos, pad_slots, n_slots):
    rt = ROW_TILE
    fold = DISPATCH_FOLD
    tm = pos.shape[1] // TOP_K
    assert pos.shape[0] % fold == 0 and pad_slots.shape[0] % fold == 0
    n_tok_steps = pos.shape[0] // fold
    steps = jnp.concatenate([pos, pad_slots], axis=0).reshape(-1, fold * pos.shape[1])
    grid_spec = pltpu.PrefetchScalarGridSpec(
        num_scalar_prefetch=1,
        grid=(steps.shape[0],),
        in_specs=[pl.BlockSpec((fold * tm * rt, 128), lambda i, pos: (jnp.minimum(i, n_tok_steps - 1), 0))],
        out_specs=pl.BlockSpec(memory_space=pl.ANY),
        scratch_shapes=[pltpu.VMEM((rt, 128), F32), pltpu.SemaphoreType.DMA(())])
    return pl.pallas_call(
        functools.partial(_dispatch_kernel, n_tok_steps, tm),
        grid_spec=grid_spec,
        out_shape=jax.ShapeDtypeStruct((n_slots * rt, 128), F32),
        compiler_params=_params("arbitrary"),
        name="moe_dispatch",
    )(steps, m_tiles)


def _expert_kernel(be_ref, x_ref, wg_ref, wu_ref, wd_ref, y_ref):
    blk = x_ref.shape[0] // ROW_TILE
    x = _load_tile_rows(x_ref, 0, blk).astype(BF16)
    a = _dot(x, wg_ref[...].astype(BF16))
    h = a * _sigmoid(a) * _dot(x, wu_ref[...].astype(BF16))
    _store_tile_rows(y_ref, _dot(h.astype(BF16), wd_ref[...].astype(BF16)))


def _experts(x_slots, blk_e, blk, layer, w_gate, w_up, w_down):
    rt = ROW_TILE
    d = D_MODEL
    n_slots = x_slots.shape[0] // rt
    grid_spec = pltpu.PrefetchScalarGridSpec(
        num_scalar_prefetch=1,
        grid=(n_slots // blk,),
        in_specs=[pl.BlockSpec((blk * rt, 128), lambda i, be: (i, 0)),
                  pl.BlockSpec((None, None, d, D_EXPERT), lambda i, be: (layer, be[i], 0, 0)),
                  pl.BlockSpec((None, None, d, D_EXPERT), lambda i, be: (layer, be[i], 0, 0)),
                  pl.BlockSpec((None, None, D_EXPERT, d), lambda i, be: (layer, be[i], 0, 0))],
        out_specs=pl.BlockSpec((blk * rt, 128), lambda i, be: (i, 0)))
    return pl.pallas_call(
        _expert_kernel,
        grid_spec=grid_spec,
        out_shape=jax.ShapeDtypeStruct((n_slots * rt, 128), F32),
        compiler_params=_params("arbitrary"),
        name="expert_ffn",
    )(blk_e, x_slots, w_gate, w_up, w_down)


def _combine_kernel(pos_ref, y_hbm, x_ref, w_ref, mod_ref, gf_ref, o_ref, ybuf, sem):
    i = pl.program_id(0)
    last = pl.num_programs(0) - 1
    slot = i % 2
    rt = ROW_TILE
    tm = x_ref.shape[0]
    n = TOP_K * tm

    def issue(step, dst_slot):
        for j in range(n):
            src = pl.multiple_of(pos_ref[step, j], rt)
            pltpu.make_async_copy(y_hbm.at[pl.ds(src, rt)], ybuf.at[dst_slot, pl.ds(j * rt, rt)],
                                  sem.at[dst_slot]).start(priority=j % 2)

    def wait(dst_slot):
        pltpu.make_async_copy(y_hbm.at[pl.ds(0, n * rt)], ybuf.at[dst_slot], sem.at[dst_slot]).wait()

    @pl.when(i == 0)
    def _():
        issue(0, 0)

    wait(slot)
    issue(jnp.minimum(i + 1, last), 1 - slot)
    w = w_ref[...]
    yb = ybuf.at[slot]
    f = w[:, 0:1] * _load_tile_rows(yb, 0, tm) + w[:, 1:2] * _load_tile_rows(yb, tm * rt, tm)
    x = x_ref[...] + mod_ref[5:6, :] * f
    o_ref[...] = x * lax.rsqrt(jnp.mean(x * x, axis=-1, keepdims=True) + EPS) * gf_ref[...]

    @pl.when(i == last)
    def _():
        wait(1 - slot)


def _combine(x_rows, y_slots, pos, w_tok, modtab, g_final, n_lat, seq, batch):
    n_rows, d = x_rows.shape
    n_steps, n = pos.shape
    tm = n // TOP_K
    midx = _mod_index(tm, n_lat, seq, batch)
    grid_spec = pltpu.PrefetchScalarGridSpec(
        num_scalar_prefetch=1,
        grid=(n_steps,),
        in_specs=[pl.BlockSpec(memory_space=pl.ANY),
                  pl.BlockSpec((tm, d), lambda i, pos: (i, 0)),
                  pl.BlockSpec((tm, TOP_K), lambda i, pos: (i, 0)),
                  pl.BlockSpec((None, 8, d), lambda i, pos: (midx(i), 0, 0)),
                  pl.BlockSpec((1, d), lambda i, pos: (0, 0))],
        out_specs=pl.BlockSpec((tm, d), lambda i, pos: (i, 0)),
        scratch_shapes=[pltpu.VMEM((2, n * ROW_TILE, 128), F32), pltpu.SemaphoreType.DMA((2,))])
    return pl.pallas_call(
        _combine_kernel,
        grid_spec=grid_spec,
        out_shape=jax.ShapeDtypeStruct((n_rows, d), F32),
        compiler_params=_params("arbitrary"),
        name="moe_combine",
    )(pos, y_slots, x_rows, w_tok, modtab, g_final.astype(F32).reshape(1, d))


def _combine_project_kernel(n_col_steps, pos_ref, y_hbm, x_ref, wt_ref, modp_ref, g_ref, mod_ref, w_ref, cs_ref,
                            p_ref, xo_ref, h_ref, ybuf, sem):
    i = pl.program_id(0)
    j = pl.program_id(1)
    last = pl.num_programs(0) - 1
    rt = ROW_TILE
    per_step = ybuf.shape[1] // rt
    per_k = n_col_steps // TOP_K

    def issue(block, step):
        for jj in range(per_step):
            src = pl.multiple_of(pos_ref[block * n_col_steps + step, jj], rt)
            pltpu.make_async_copy(y_hbm.at[pl.ds(src, rt)], ybuf.at[step, pl.ds(jj * rt, rt)],
                                  sem).start(priority=jj % 2)

    def wait():
        for step in range(n_col_steps):
            pltpu.make_async_copy(y_hbm.at[pl.ds(0, per_step * rt)], ybuf.at[step], sem).wait()

    @pl.when((i == 0) & (j == 0))
    def _():
        for step in range(n_col_steps):
            issue(0, step)

    @pl.when(j == 0)
    def _():
        wait()
        w = wt_ref[...]
        yk = [jnp.concatenate([_load_tile_rows(ybuf.at[k * per_k + c], 0, per_step) for c in range(per_k)],
                              axis=0) for k in range(TOP_K)]
        f = w[:, 0:1] * yk[0] + w[:, 1:2] * yk[1]
        x = x_ref[...] + modp_ref[5:6, :] * f
        xo_ref[...] = x
        y = x * lax.rsqrt(jnp.mean(x * x, axis=-1, keepdims=True) + EPS) * g_ref[...]
        h_ref[...] = (y * (1.0 + mod_ref[1:2, :]) + mod_ref[0:1, :]).astype(BF16)

    issue(jnp.minimum(i + 1, last), j)
    acc = _dot(h_ref[...], w_ref[...])
    gate = 0.5 * jnp.tanh(0.5 * acc) + 0.5
    cs = cs_ref[...]
    p_ref[...] = jnp.where(cs > 0.0, acc * cs, gate).astype(BF16)

    @pl.when((i == last) & (j == n_col_steps - 1))
    def _():
        wait()


def _combine_project(x_rows, y_slots, pos, w_tok, modtab_prev, g, modtab, w_bf, n_lat, seq, batch):
    n_rows, d = x_rows.shape
    ncol = w_bf.shape[1]
    tn = PROJ_TN
    n_col_steps = ncol // tn
    per_k = n_col_steps // TOP_K
    per_step = pos.shape[1] // TOP_K
    tm = per_k * per_step
    n_steps = n_rows // tm
    assert n_col_steps == TOP_K * per_k and n_rows % tm == 0
    pos = pos.reshape(n_steps, per_k, TOP_K, per_step).transpose(0, 2, 1, 3).reshape(n_steps * n_col_steps, per_step)
    midx = _mod_index(tm, n_lat, seq, batch)
    grid_spec = pltpu.PrefetchScalarGridSpec(
        num_scalar_prefetch=1,
        grid=(n_steps, n_col_steps),
        in_specs=[pl.BlockSpec(memory_space=pl.ANY),
                  pl.BlockSpec((tm, d), lambda i, j, pos: (i, 0)),
                  pl.BlockSpec((tm, TOP_K), lambda i, j, pos: (i, 0)),
                  pl.BlockSpec((None, 8, d), lambda i, j, pos: (midx(i), 0, 0)),
                  pl.BlockSpec((1, d), lambda i, j, pos: (0, 0)),
                  pl.BlockSpec((None, 8, d), lambda i, j, pos: (midx(i), 0, 0)),
                  pl.BlockSpec((d, tn), lambda i, j, pos: (0, j)),
                  pl.BlockSpec((1, tn), lambda i, j, pos: (0, j))],
        out_specs=[pl.BlockSpec((tm, tn), lambda i, j, pos: (i, j)),
                   pl.BlockSpec((tm, d), lambda i, j, pos: (i, 0))],
        scratch_shapes=[pltpu.VMEM((tm, d), BF16), pltpu.VMEM((n_col_steps, per_step * ROW_TILE, 128), F32),
                        pltpu.SemaphoreType.DMA(())])
    p_all, x_new = pl.pallas_call(
        functools.partial(_combine_project_kernel, n_col_steps),
        grid_spec=grid_spec,
        out_shape=[jax.ShapeDtypeStruct((n_rows, ncol), BF16), jax.ShapeDtypeStruct((n_rows, d), F32)],
        compiler_params=_params("arbitrary", "arbitrary"),
        name="combine_norm_project",
    )(pos, y_slots, x_rows, w_tok, modtab_prev, g.reshape(1, d), modtab, w_bf, _mix_col_scale())
    return p_all, x_new


def _dispatch_plan(idx2, rank, counts, blk, tm):
    n = idx2.shape[1]
    a = TOP_K * n
    padded = (counts + blk - 1) // blk * blk
    pad_end = jnp.cumsum(padded)
    pad_start = pad_end - padded
    start_of = jnp.zeros(idx2.shape, jnp.int32)
    for e in range(N_EXPERTS):
        start_of = jnp.where(idx2 == e, pad_start[e], start_of)
    dest = (start_of + rank).astype(jnp.int32)
    n_blocks = a // blk + N_EXPERTS
    first_slot = jnp.arange(n_blocks, dtype=jnp.int32) * blk
    blk_e = jnp.minimum(jnp.sum(pad_end[None, :] <= first_slot[:, None], axis=1), N_EXPERTS - 1).astype(jnp.int32)
    pos = dest.reshape(TOP_K, n // tm, tm).transpose(1, 0, 2).reshape(n // tm, TOP_K * tm) * ROW_TILE
    n_slots = n_blocks * blk
    starts = jnp.concatenate([pad_start + counts, pad_end[-1:]])
    lens = jnp.concatenate([padded - counts, n_slots - pad_end[-1:]])
    ends = jnp.cumsum(lens)
    j = jnp.arange(N_EXPERTS * blk, dtype=jnp.int32)
    shift = starts - (ends - lens)
    shift_of = jnp.zeros(j.shape, jnp.int32)
    for s in range(N_EXPERTS + 1):
        shift_of = jnp.where((j >= ends[s] - lens[s]) & (j < ends[s]), shift[s], shift_of)
    pad_slots = ((shift_of + j) * ROW_TILE).astype(jnp.int32)
    return pos, pad_slots.reshape(-1, TOP_K * tm), blk_e, n_slots


def kernel(x, c, ctx, c_ctx, w_mod, b_mod, g_norm1, g_norm2, w_in, na_rpb, ret_log_decay, gqa_q_gain, gqa_k_gain,
           diff_lambda, diff_subln, w_branch, w_out, w_router, router_bias, w_gate_e, w_up_e, w_down_e, g_final):
    batch, seq, d = x.shape
    n_ctx = ctx.shape[1]
    n_lat = batch * seq
    n_ctx_tot = batch * n_ctx
    n_tot = n_lat + n_ctx_tot
    tm = 1024 if n_ctx_tot % 1024 == 0 else 512

    x_parts = (x.reshape(n_lat, d), ctx.reshape(n_ctx_tot, d))

    nrow = -(-(batch + 1) // 8) * 8
    cc = jnp.zeros((nrow, d), F32).at[:batch].set(c).at[batch].set(c_ctx)
    mod = _modulation(cc, w_mod, b_mod)
    modtab = jnp.pad(mod.reshape(DEPTH, nrow, 6, d), ((0, 0), (0, 0), (0, 2), (0, 0)))

    wr_t = w_router.astype(F32).T
    out = None
    pending = None
    for layer in range(DEPTH):
        need_ctx = layer < DEPTH - 1
        n_rows = n_tot if need_ctx else n_lat
        if pending is None:
            p_all = _project(x_parts, g_norm1[layer], modtab[layer], w_in[layer].astype(BF16), tm, n_lat, seq, batch)
        else:
            p_all, x_new = _combine_project(*pending, g_norm1[layer], modtab[layer], w_in[layer].astype(BF16),
                                            n_lat, seq, batch)
            x_parts = (x_new,)
        ya = _na_mixer(p_all, na_rpb[layer], batch, seq, n_ctx, need_ctx)
        yb = _ret_mixer(p_all, ret_log_decay[layer], batch, seq, n_ctx, need_ctx)
        yc = _gqa_mixer(p_all, gqa_q_gain[layer], gqa_k_gain[layer], batch, seq, n_ctx, need_ctx)
        yd = _diff_mixer(p_all, diff_lambda[layer], diff_subln[layer], layer, batch, seq, n_ctx, need_ctx)
        ys_lat = [ya[0], yb[0], yc[0], yd[0]]
        ys_ctx = [ya[1], yb[1], yc[1], yd[1]] if need_ctx else None
        x_mid, m_all, logits_t = _merge(x_parts, p_all, ys_lat, ys_ctx, w_branch[layer].astype(BF16),
                                        w_out[layer].astype(BF16), modtab[layer], g_norm2[layer], wr_t, MERGE_TM,
                                        n_rows, n_lat, seq, batch)
        idx2, w2, rank, counts = _route(logits_t, router_bias)
        pos, pad_slots, blk_e, n_slots = _dispatch_plan(idx2, rank, counts, MOE_BLK, CMB_TM)
        x_slots = _dispatch(m_all, pos, pad_slots, n_slots)
        y_slots = _experts(x_slots, blk_e, MOE_BLK, layer, w_gate_e, w_up_e, w_down_e)
        if need_ctx:
            pending = (x_mid, y_slots, pos, w2.T, modtab[layer])
        else:
            out = _combine(x_mid, y_slots, pos, w2.T, modtab[layer], g_final, n_lat, seq, batch)
    return out.reshape(batch, seq, d)
```

```python
import functools
import math

import numpy as np
import jax
import jax.numpy as jnp
from jax import lax
from jax.experimental import pallas as pl
from jax.experimental.pallas import tpu as pltpu

F32 = jnp.float32
BF16 = jnp.bfloat16

D_MODEL = 1024
DEPTH = 2
GRID_W = 64
NA_HEADS = 4
NA_DIM = 64
NA_WIN_H = 8
NA_WIN_W = 16
RET_HEADS = 4
RET_DK = 64
RET_CHUNK = 128
GQA_HEADS = 4
GQA_KV_HEADS = 2
GQA_DIM = 64
DIFF_HEADS = 4
DIFF_QK_DIM = 32
DIFF_V_DIM = 64
N_BRANCH = 4
BRANCH_W = 256
ROPE_THETA = 10000.0
EPS = 1e-6
NEG_INF = -1e30
N_EXPERTS = 16
N_GROUPS = 4
EXPERTS_PER_GROUP = 4
TOP_K = 2
D_EXPERT = 512

MIX_COLS = 3072
IN_COLS = MIX_COLS + N_BRANCH * D_MODEL
NA_Q, NA_K, NA_V = 0, 1, 2
RET_Q, RET_K, RET_V, RET_G = 3, 4, 5, 6
GQA_Q = 7
DIFF_Q, DIFF_K, DIFF_V = 9, 10, 11

LOG2E = math.log2(math.e)
NA_GROUP_ROWS = 4
NA_BAND_ROWS = 12

VMEM_LIMIT = 56 * 1024 * 1024
MOE_BLK = 512
CMB_TM = 512
DISPATCH_FOLD = 1
PROJ_TN = 1792
MERGE_TM = 512


ROW_TILE = D_MODEL // 128


def _store_tile_rows(ref, x):
    n = x.shape[0]
    for j in range(ROW_TILE):
        ref[pl.ds(j, n, stride=ROW_TILE), :] = x[:, j * 128:(j + 1) * 128]


def _load_tile_rows(ref, start, n):
    return jnp.concatenate([ref[pl.ds(start + j, n, stride=ROW_TILE), :] for j in range(ROW_TILE)], axis=-1)


def _dot(a, b):
    return jnp.dot(a, b, preferred_element_type=F32)


def _dot_nt(a, b):
    return lax.dot_general(a, b, (((1,), (1,)), ((), ())), preferred_element_type=F32)


def _sigmoid(x):
    return 1.0 / (1.0 + jnp.exp(-x))


def _params(*sem):
    return pltpu.CompilerParams(dimension_semantics=sem, vmem_limit_bytes=VMEM_LIMIT)


def _mod_kernel(c_ref, w_ref, b_ref, o_ref):
    c = c_ref[...]
    a = (c * _sigmoid(c)).astype(BF16)
    o_ref[...] = _dot(a, w_ref[...].astype(BF16)) + b_ref[...]


def _modulation(cc, w_mod, b_mod):
    nrow = cc.shape[0]
    depth, d, n6 = w_mod.shape
    tn = 1536
    return pl.pallas_call(
        _mod_kernel,
        grid=(depth, n6 // tn),
        in_specs=[pl.BlockSpec((nrow, d), lambda l, j: (0, 0)),
                  pl.BlockSpec((None, d, tn), lambda l, j: (l, 0, j)),
                  pl.BlockSpec((None, 1, tn), lambda l, j: (l, 0, j))],
        out_specs=pl.BlockSpec((None, nrow, tn), lambda l, j: (l, 0, j)),
        out_shape=jax.ShapeDtypeStruct((depth, nrow, n6), F32),
        compiler_params=_params("parallel", "arbitrary"),
        name="modulation",
    )(cc, w_mod, b_mod.reshape(depth, 1, n6))


def _mix_col_scale():
    cs = np.ones((1, IN_COLS), np.float32)
    cs[0, NA_Q * 256:(NA_Q + 1) * 256] = NA_DIM ** -0.5 * LOG2E
    cs[0, DIFF_Q * 256:(DIFF_Q + 1) * 256] = DIFF_QK_DIM ** -0.5 * LOG2E
    cs[0, RET_K * 256:(RET_K + 1) * 256] = RET_DK ** -0.5
    cs[0, MIX_COLS:] = 0.0
    return jnp.asarray(cs)


def _row_blocks(parts, tm, n_lat):
    d = parts[0].shape[1]
    nlat_blk = n_lat // tm
    if len(parts) == 1:
        return [pl.BlockSpec((tm, d), lambda i, *_: (i, 0))]
    return [pl.BlockSpec((tm, d), lambda i, *_: (jnp.minimum(i, nlat_blk - 1), 0)),
            pl.BlockSpec((tm, d), lambda i, *_: (jnp.maximum(i - nlat_blk, 0), 0))]


def _pick_rows(x_refs, nlat_blk):
    if len(x_refs) == 1:
        return x_refs[0][...]
    return jnp.where(pl.program_id(0) >= nlat_blk, x_refs[1][...], x_refs[0][...])


def _proj_kernel(nlat_blk, n_parts, *refs):
    x_refs, (g_ref, mod_ref, w_ref, cs_ref, o_ref, h_ref) = refs[:n_parts], refs[n_parts:]
    j = pl.program_id(1)

    @pl.when(j == 0)
    def _():
        x = _pick_rows(x_refs, nlat_blk)
        y = x * lax.rsqrt(jnp.mean(x * x, axis=-1, keepdims=True) + EPS) * g_ref[...]
        h_ref[...] = (y * (1.0 + mod_ref[1:2, :]) + mod_ref[0:1, :]).astype(BF16)

    acc = _dot(h_ref[...], w_ref[...])
    gate = 0.5 * jnp.tanh(0.5 * acc) + 0.5
    cs = cs_ref[...]
    o_ref[...] = jnp.where(cs > 0.0, acc * cs, gate).astype(BF16)


def _mod_index(tm, n_lat, seq, batch):
    nlat_blk = n_lat // tm
    bpb = seq // tm

    def index(i):
        return jnp.where(i < nlat_blk, i // bpb, batch)
    return index


def _project(x_parts, g, modtab, w_bf, tm, n_lat, seq, batch):
    n_tot = sum(p.shape[0] for p in x_parts)
    d = x_parts[0].shape[1]
    ncol = w_bf.shape[1]
    tn = PROJ_TN
    midx = _mod_index(tm, n_lat, seq, batch)
    return pl.pallas_call(
        functools.partial(_proj_kernel, n_lat // tm, len(x_parts)),
        grid=(n_tot // tm, ncol // tn),
        in_specs=[*_row_blocks(x_parts, tm, n_lat),
                  pl.BlockSpec((1, d), lambda i, j: (0, 0)),
                  pl.BlockSpec((None, 8, d), lambda i, j: (midx(i), 0, 0)),
                  pl.BlockSpec((d, tn), lambda i, j: (0, j)),
                  pl.BlockSpec((1, tn), lambda i, j: (0, j))],
        out_specs=pl.BlockSpec((tm, tn), lambda i, j: (i, j)),
        out_shape=jax.ShapeDtypeStruct((n_tot, ncol), BF16),
        scratch_shapes=[pltpu.VMEM((tm, d), BF16)],
        compiler_params=_params("parallel", "arbitrary"),
        name="norm_project",
    )(*x_parts, g.reshape(1, d), modtab, w_bf, _mix_col_scale())


def _na_kernel(need_ctx, rows, q_ref, k_ref, v_ref, qc_ref, kc_ref, vc_ref, tb_ref, y_ref, *rest):
    if need_ctx:
        yc_ref, vx, vcx = rest
    else:
        vx, vcx = rest
    gq = NA_GROUP_ROWS * GRID_W
    band = NA_BAND_ROWS * GRID_W
    n_groups = rows // NA_GROUP_ROWS
    n_pairs = NA_HEADS // 2
    masks = _lane_masks(2)
    low = masks[0] > 0

    for p in range(n_pairs):
        ps = slice(p * 128, (p + 1) * 128)
        vx[p] = jnp.concatenate([v_ref[:, ps], jnp.ones((v_ref.shape[0], 128), BF16)], axis=-1)
        vcx[p] = jnp.concatenate([vc_ref[:, ps], jnp.ones((vc_ref.shape[0], 128), BF16)], axis=-1)

    def group_body(g, carry):
        u = jnp.clip(g * NA_GROUP_ROWS - NA_WIN_H // 2, 0, rows - NA_BAND_ROWS)
        typ = jnp.where(g == 0, 0, jnp.where(g == n_groups - 1, 2, 1))
        qs = pl.multiple_of(g * gq, gq)
        ks = pl.multiple_of(u * GRID_W, GRID_W)
        for p in range(n_pairs):
            ps = slice(p * 128, (p + 1) * 128)
            q = q_ref[pl.ds(qs, gq), ps]
            kb = k_ref[pl.ds(ks, band), ps]
            kc = kc_ref[:, ps]
            outs = []
            for hh in range(2):
                qm = q * masks[hh]
                s_w = _dot_nt(qm, kb) + tb_ref[2 * p + hh, typ]
                s_c = _dot_nt(qm, kc)
                m = jnp.maximum(jnp.max(s_w, axis=-1, keepdims=True), jnp.max(s_c, axis=-1, keepdims=True))
                oe = (_dot(jnp.exp2((s_w - m).astype(BF16)), vx[p, pl.ds(ks, band), :])
                      + _dot(jnp.exp2((s_c - m).astype(BF16)), vcx[p]))
                outs.append(oe[:, :128] / oe[:, 128:])
            y_ref[pl.ds(qs, gq), ps] = jnp.where(low, outs[0], outs[1]).astype(BF16)
        return carry

    lax.fori_loop(0, n_groups, group_body, 0, unroll=4)

    if need_ctx:
        for p in range(n_pairs):
            ps = slice(p * 128, (p + 1) * 128)
            outs = [_softmax_pv(qc_ref[:, ps] * masks[hh], kc_ref[:, ps], vcx[p]) for hh in range(2)]
            yc_ref[:, ps] = jnp.where(low, outs[0], outs[1]).astype(BF16)


def _na_bias_table(rpb, rows):
    assert rows % NA_GROUP_ROWS == 0 and rows >= NA_BAND_ROWS + NA_GROUP_ROWS
    n_groups = rows // NA_GROUP_ROWS
    qc = np.arange(GRID_W)[:, None]
    kc = np.arange(GRID_W)[None, :]
    win_start = np.clip(qc - NA_WIN_W // 2, 0, GRID_W - NA_WIN_W)
    col_ok = (kc >= win_start) & (kc < win_start + NA_WIN_W)
    col_idx = np.clip(kc - qc, -(NA_WIN_W - 1), NA_WIN_W - 1) + NA_WIN_W - 1
    n_col = 2 * NA_WIN_W - 1
    onehot = jnp.asarray((np.arange(n_col)[:, None] == col_idx.reshape(1, -1)).astype(np.float32))
    picked = jnp.dot(rpb.astype(F32).reshape(-1, n_col), onehot, precision=lax.Precision.HIGHEST)
    picked = picked.reshape(NA_HEADS, 2 * NA_WIN_H - 1, GRID_W, GRID_W)
    tiles = jnp.where(col_ok[None, None], picked * LOG2E, NEG_INF)
    masked = jnp.full((NA_HEADS, GRID_W, GRID_W), NEG_INF, F32)
    tables = []
    for g in (0, 1, n_groups - 1):
        u = int(np.clip(g * NA_GROUP_ROWS - NA_WIN_H // 2, 0, rows - NA_BAND_ROWS))
        per_row = []
        for a in range(NA_GROUP_ROWS):
            r = g * NA_GROUP_ROWS + a
            r0 = int(np.clip(r - NA_WIN_H // 2, 0, rows - NA_WIN_H))
            pieces = [tiles[:, kr - r + NA_WIN_H - 1] if r0 <= kr < r0 + NA_WIN_H else masked
                      for kr in range(u, u + NA_BAND_ROWS)]
            per_row.append(jnp.concatenate(pieces, axis=-1))
        tables.append(jnp.concatenate(per_row, axis=1))
    return jnp.stack(tables, axis=1)


def _na_mixer(p_all, rpb, batch, seq, n_ctx, need_ctx):
    n_lat = batch * seq
    rows = seq // GRID_W
    tb = _na_bias_table(rpb, rows)
    cb = n_lat // n_ctx
    w = 256
    lat = lambda c: pl.BlockSpec((seq, w), lambda b, c=c: (b, c))
    ctx = lambda c: pl.BlockSpec((n_ctx, w), lambda b, c=c: (cb + b, c))
    out_shape = [jax.ShapeDtypeStruct((n_lat, w), BF16)]
    out_specs = [pl.BlockSpec((seq, w), lambda b: (b, 0))]
    if need_ctx:
        out_shape.append(jax.ShapeDtypeStruct((batch * n_ctx, w), BF16))
        out_specs.append(pl.BlockSpec((n_ctx, w), lambda b: (b, 0)))
    return pl.pallas_call(
        functools.partial(_na_kernel, need_ctx, rows),
        grid=(batch,),
        in_specs=[lat(NA_Q), lat(NA_K), lat(NA_V), ctx(NA_Q), ctx(NA_K), ctx(NA_V),
                  pl.BlockSpec(tb.shape, lambda b: (0, 0, 0, 0))],
        out_specs=out_specs,
        out_shape=out_shape,
        scratch_shapes=[pltpu.VMEM((NA_HEADS // 2, seq, 256), BF16), pltpu.VMEM((NA_HEADS // 2, n_ctx, 256), BF16)],
        compiler_params=_params("parallel"),
        name="na_mixer",
    )(p_all, p_all, p_all, p_all, p_all, p_all, tb)


def _ret_kernel(need_ctx, seq, n_ctx, lg_ref, q_ref, k_ref, v_ref, g_ref, qc_ref, kc_ref, vc_ref, gc_ref,
                cos_ref, sin_ref, perm_ref, bd_ref, y_ref, *rest):
    if need_ctx:
        yc_ref, qr, kr, krt, kct, o_f, o_b, oc_f, oc_b, dm, qd, kdt = rest
    else:
        qr, kr, krt, kct, o_f, o_b, oc_f, oc_b, dm, qd, kdt = rest
    ch = RET_CHUNK
    n_lat_ch = seq // ch
    n_ctx_ch = n_ctx // ch
    n_pairs = RET_HEADS // 2
    assert ch == 128 and 2 * RET_DK == 128

    def prep(i, carry):
        rs = pl.multiple_of(i * ch, ch)
        c = cos_ref[pl.ds(rs, ch), :]
        s = sin_ref[pl.ds(rs, ch), :]
        qr[pl.ds(rs, ch), :] = _rope_mxu(q_ref[pl.ds(rs, ch), :], c, s, perm_ref)
        k = _rope_mxu(k_ref[pl.ds(rs, ch), :], c, s, perm_ref)
        kr[pl.ds(rs, ch), :] = k
        for p in range(n_pairs):
            krt[i, p * 128:(p + 1) * 128, :] = k[:, p * 128:(p + 1) * 128].T
        return carry

    lax.fori_loop(0, n_lat_ch, prep, 0, unroll=2)
    for n in range(n_ctx_ch):
        kc = kc_ref[n * ch:(n + 1) * ch, :].astype(F32)
        for p in range(n_pairs):
            kct[n, p * 128:(p + 1) * 128, :] = kc[:, p * 128:(p + 1) * 128].T

    ri = lax.broadcasted_iota(jnp.int32, (ch, ch), 0)
    ci = lax.broadcasted_iota(jnp.int32, (ch, ch), 1)
    lag = (ri - ci).astype(F32)
    rowf = ri.astype(F32)
    colf = ci.astype(F32)
    low_lane = ci < RET_DK
    low_row = ri < RET_DK
    block_diag = jnp.where(low_lane == low_row, 1.0, 0.0)
    low, high = _lane_masks(2)

    chains = [(p, dirn) for p in range(n_pairs) for dirn in range(2)]
    cdec = []
    for c, (p, dirn) in enumerate(chains):
        lg_a = lg_ref[dirn, 2 * p]
        lg_b = lg_ref[dirn, 2 * p + 1]
        lg_lane = jnp.where(low_lane, lg_a, lg_b)
        lg_row = jnp.where(low_row, lg_a, lg_b)
        for hh, lg in enumerate((lg_a, lg_b)):
            if dirn == 0:
                keep = ri >= ci
                dm[2 * c + hh] = jnp.where(keep, jnp.exp(jnp.where(keep, lag, 0.0) * lg), 0.0)
            else:
                keep = ci >= ri
                dm[2 * c + hh] = jnp.where(keep, jnp.exp(jnp.where(keep, -lag, 0.0) * lg), 0.0)
        if dirn == 0:
            qd[c] = jnp.exp((rowf + 1.0) * lg_lane)
            kdt[c] = jnp.exp((ch - 1.0 - colf) * lg_row)
        else:
            qd[c] = jnp.exp((ch - rowf) * lg_lane)
            kdt[c] = jnp.exp(colf * lg_row)
        cdec.append(jnp.exp(ch * lg_lane[0:1, :]))

    def step(c, state, qi, ki, kti, vi):
        kb = ki.astype(BF16)
        inner_a = (_dot_nt((qi * low).astype(BF16), kb) * dm[2 * c]).astype(BF16)
        inner_b = (_dot_nt((qi * high).astype(BF16), kb) * dm[2 * c + 1]).astype(BF16)
        o = (jnp.where(low_lane, _dot(inner_a, vi), _dot(inner_b, vi))
             + _dot((qi * qd[c]).astype(BF16), state.astype(BF16)))
        state = state * cdec[c] + block_diag * _dot((kti * kdt[c]).astype(BF16), vi)
        return state, o

    states = []
    for c, (p, dirn) in enumerate(chains):
        ps = slice(p * 128, (p + 1) * 128)
        state = jnp.zeros((128, 128), F32)
        for n in (range(n_ctx_ch) if dirn == 0 else range(n_ctx_ch - 1, -1, -1)):
            rs = slice(n * ch, (n + 1) * ch)
            state, o = step(c, state, qc_ref[rs, ps].astype(F32), kc_ref[rs, ps].astype(F32), kct[n, ps, :],
                            vc_ref[rs, ps])
            if need_ctx:
                (oc_f if dirn == 0 else oc_b)[rs, ps] = o
        states.append(state)

    def lat_body(i, states):
        new = []
        for c, (p, dirn) in enumerate(chains):
            ps = slice(p * 128, (p + 1) * 128)
            n = i if dirn == 0 else n_lat_ch - 1 - i
            rs = pl.ds(pl.multiple_of(n * ch, ch), ch)
            state, o = step(c, states[c], qr[rs, ps], kr[rs, ps], krt[n, ps, :], v_ref[rs, ps])
            (o_f if dirn == 0 else o_b)[rs, ps] = o
            new.append(state)
        return tuple(new)

    lax.fori_loop(0, n_lat_ch, lat_body, tuple(states), unroll=4)

    def head_mean(x):
        hi = x.astype(BF16)
        lo = (x - hi.astype(F32)).astype(BF16)
        bd = bd_ref[...]
        return (_dot(hi, bd) + _dot(lo, bd)) * (1.0 / RET_DK)

    def finish(o, g):
        cen = o - head_mean(o)
        on = cen * lax.rsqrt(head_mean(cen * cen) + EPS)
        g = g.astype(F32)
        return (g * _sigmoid(g) * on).astype(BF16)

    def fin_body(i, carry):
        rs = pl.multiple_of(i * ch, ch)
        y_ref[pl.ds(rs, ch), :] = finish(o_f[pl.ds(rs, ch), :] + o_b[pl.ds(rs, ch), :], g_ref[pl.ds(rs, ch), :])
        return carry

    lax.fori_loop(0, n_lat_ch, fin_body, 0, unroll=2)
    if need_ctx:
        for n in range(n_ctx_ch):
            rs = slice(n * ch, (n + 1) * ch)
            yc_ref[rs, :] = finish(oc_f[rs, :] + oc_b[rs, :], gc_ref[rs, :])


def _rope_tables(seq, head_dim, width, dist):
    half = head_dim // 2
    nf = half // 2
    assert nf == dist
    inv = 1.0 / (ROPE_THETA ** (np.arange(nf, dtype=np.float32) / nf))
    t = np.arange(seq)
    rows = (t // GRID_W).astype(np.float32)[:, None] * inv[None, :]
    cols = (t % GRID_W).astype(np.float32)[:, None] * inv[None, :]
    cos = np.concatenate([np.cos(rows), np.cos(rows), np.cos(cols), np.cos(cols)], axis=-1)
    sin = np.concatenate([-np.sin(rows), np.sin(rows), -np.sin(cols), np.sin(cols)], axis=-1)
    reps = width // head_dim
    return (jnp.asarray(np.tile(cos, (1, reps)), F32), jnp.asarray(np.tile(sin, (1, reps)), F32))


def _ret_mixer(p_all, log_decay, batch, seq, n_ctx, need_ctx):
    n_lat = batch * seq
    cb = n_lat // n_ctx
    w = 256
    ch = RET_CHUNK
    log_gamma = jnp.log1p(-jnp.exp(log_decay.astype(F32)))
    cos, sin = _rope_tables(seq, RET_DK, w, 16)
    lat = lambda c: pl.BlockSpec((seq, w), lambda b, c=c: (b, c))
    ctx = lambda c: pl.BlockSpec((n_ctx, w), lambda b, c=c: (cb + b, c))
    whole = pl.BlockSpec((seq, w), lambda b: (0, 0))
    out_shape = [jax.ShapeDtypeStruct((n_lat, w), BF16)]
    out_specs = [pl.BlockSpec((seq, w), lambda b: (b, 0))]
    if need_ctx:
        out_shape.append(jax.ShapeDtypeStruct((batch * n_ctx, w), BF16))
        out_specs.append(pl.BlockSpec((n_ctx, w), lambda b: (b, 0)))
    return pl.pallas_call(
        functools.partial(_ret_kernel, need_ctx, seq, n_ctx),
        grid=(batch,),
        in_specs=[pl.BlockSpec(memory_space=pltpu.SMEM),
                  lat(RET_Q), lat(RET_K), lat(RET_V), lat(RET_G),
                  ctx(RET_Q), ctx(RET_K), ctx(RET_V), ctx(RET_G), whole, whole,
                  pl.BlockSpec((w, w), lambda b: (0, 0)), pl.BlockSpec((w, w), lambda b: (0, 0))],
        out_specs=out_specs,
        out_shape=out_shape,
        scratch_shapes=[pltpu.VMEM((seq, w), F32), pltpu.VMEM((seq, w), F32),
                        pltpu.VMEM((seq // ch, w, ch), F32), pltpu.VMEM((n_ctx // ch, w, ch), F32),
                        pltpu.VMEM((seq, w), F32), pltpu.VMEM((seq, w), F32),
                        pltpu.VMEM((n_ctx, w), F32), pltpu.VMEM((n_ctx, w), F32),
                        pltpu.VMEM((2 * RET_HEADS, ch, ch), F32),
                        pltpu.VMEM((RET_HEADS, ch, 128), F32), pltpu.VMEM((RET_HEADS, 128, ch), F32)],
        compiler_params=_params("parallel"),
        name="retention_mixer",
    )(log_gamma, p_all, p_all, p_all, p_all, p_all, p_all, p_all, p_all, cos, sin,
      _swap_matrix(w, 16), _block_ones(w, RET_DK))


def _swap_matrix(width, dist):
    i = np.arange(width)
    partner = np.where(i % (2 * dist) < dist, i + dist, i - dist)
    p = np.zeros((width, width), np.float32)
    p[partner, i] = 1.0
    return jnp.asarray(p, BF16)


def _block_ones(width, block):
    i = np.arange(width)
    return jnp.asarray((i[:, None] // block == i[None, :] // block).astype(np.float32), BF16)


def _sumsq_blocks(xf, bd_ref):
    sq = xf * xf
    hi = sq.astype(BF16)
    lo = (sq - hi.astype(F32)).astype(BF16)
    bd = bd_ref[...]
    return _dot(hi, bd) + _dot(lo, bd)


def _rope_mxu(x, cos, sin, perm_ref):
    return x.astype(F32) * cos + _dot(x, perm_ref[...]) * sin


def _softmax_pv(qm, k, v_ones):
    s = _dot_nt(qm, k)
    p = jnp.exp2((s - jnp.max(s, axis=-1, keepdims=True)).astype(BF16))
    oe = _dot(p, v_ones)
    return oe[:, :128] / oe[:, 128:]


def _lane_masks(n_parts):
    lane = lax.broadcasted_iota(jnp.int32, (1, 128), 1)
    return [jnp.where(lane // (128 // n_parts) == i, 1.0, 0.0).astype(BF16) for i in range(n_parts)]


def _gqa_kernel(n_lat_k, n_ctx, *refs):
    if n_lat_k:
        (q_ref, k_ref, v_ref, kc_ref, vc_ref, qa_ref, qb_ref, ka_ref, kb_ref, kg_ref, pq_ref, pk_ref,
         bdq_ref, bdk_ref, y_ref, kp, vx) = refs
    else:
        q_ref, kc_ref, vc_ref, qa_ref, kg_ref, bdq_ref, bdk_ref, y_ref, kp, vx = refs
    dim = GQA_DIM
    nk = n_lat_k + n_ctx

    def inv_rms(xf, bd_ref):
        return lax.rsqrt(_sumsq_blocks(xf, bd_ref) * (1.0 / dim) + EPS)

    @pl.when(pl.program_id(1) == 0)
    def _():
        def put(rs, kn, v):
            ones = jnp.ones(v.shape, BF16)
            kp[0, rs, :] = kn.astype(BF16)
            kp[1, rs, :] = pltpu.roll(kn, dim, 1).astype(BF16)
            vx[0, rs, :] = jnp.concatenate([v, ones], axis=-1)
            vx[1, rs, :] = jnp.concatenate([pltpu.roll(v.astype(F32), dim, 1).astype(BF16), ones], axis=-1)

        kc = kc_ref[...]
        kcf = kc.astype(F32)
        put(slice(n_lat_k, nk), kcf * kg_ref[...] * inv_rms(kcf, bdk_ref), vc_ref[...])
        ck = 512
        for c in range(n_lat_k // ck):
            rs = slice(c * ck, (c + 1) * ck)
            k = k_ref[rs, :]
            put(rs, _rope_mxu(k, ka_ref[rs, :], kb_ref[rs, :], pk_ref) * inv_rms(k.astype(F32), bdk_ref), v_ref[rs, :])

    q = q_ref[...]
    qf = q.astype(F32)
    qn = _rope_mxu(q, qa_ref[...], qb_ref[...], pq_ref) if n_lat_k else qf * qa_ref[...]
    qn = (qn * inv_rms(qf, bdq_ref)).astype(BF16)
    low, high = _lane_masks(2)
    outs = []
    for g in range(GQA_KV_HEADS):
        qv = qn[:, g * 128:(g + 1) * 128]
        o_low = _softmax_pv(qv * low, kp[g], vx[g])
        o_high = _softmax_pv(qv * high, kp[1 - g], vx[1 - g])
        outs.append(jnp.where(low > 0, o_low, o_high))
    y_ref[...] = jnp.concatenate(outs, axis=-1).astype(BF16)


def _gqa_mixer(p_all, q_gain, k_gain, batch, seq, n_ctx, need_ctx):
    n_lat = batch * seq
    cb = n_lat // n_ctx
    dim = GQA_DIM
    qw = GQA_HEADS * dim
    kvw = GQA_KV_HEADS * dim
    assert qw == 256 and kvw == 128

    def swapped(g):
        return g.reshape(-1, 2, 16)[:, ::-1].reshape(1, -1)

    qg = jnp.tile(q_gain.astype(F32) * (dim ** -0.5 * LOG2E), GQA_HEADS).reshape(1, qw)
    kg = jnp.tile(k_gain.astype(F32), GQA_KV_HEADS).reshape(1, kvw)
    cq, sq = _rope_tables(seq, dim, qw, 16)
    ck, sk = _rope_tables(seq, dim, kvw, 16)
    qa, qb = cq * qg, sq * swapped(qg)
    ka, kb = ck * kg, sk * swapped(kg)
    pq, pk = _swap_matrix(qw, 16), _swap_matrix(kvw, 16)
    bdq, bdk = _block_ones(qw, dim), _block_ones(kvw, dim)
    tq = 512
    nqb = seq // tq
    const = lambda shape: pl.BlockSpec(shape, lambda b, i: (0,) * len(shape))
    scratch = lambda nk: [pltpu.VMEM((2, nk, kvw), BF16), pltpu.VMEM((2, nk, 2 * kvw), BF16)]
    y_lat = pl.pallas_call(
        functools.partial(_gqa_kernel, seq, n_ctx),
        grid=(batch, nqb),
        in_specs=[pl.BlockSpec((tq, qw), lambda b, i: (b * nqb + i, GQA_Q)),
                  pl.BlockSpec((seq, kvw), lambda b, i: (b, 16)),
                  pl.BlockSpec((seq, kvw), lambda b, i: (b, 17)),
                  pl.BlockSpec((n_ctx, kvw), lambda b, i: (cb + b, 16)),
                  pl.BlockSpec((n_ctx, kvw), lambda b, i: (cb + b, 17)),
                  pl.BlockSpec((tq, qw), lambda b, i: (i, 0)),
                  pl.BlockSpec((tq, qw), lambda b, i: (i, 0)),
                  const((seq, kvw)), const((seq, kvw)), const((1, kvw)),
                  const((qw, qw)), const((kvw, kvw)), const((qw, qw)), const((kvw, kvw))],
        out_specs=pl.BlockSpec((tq, qw), lambda b, i: (b * nqb + i, 0)),
        out_shape=jax.ShapeDtypeStruct((n_lat, qw), BF16),
        scratch_shapes=scratch(seq + n_ctx),
        compiler_params=_params("parallel", "arbitrary"),
        name="gqa_mixer",
    )(p_all, p_all, p_all, p_all, p_all, qa, qb, ka, kb, kg, pq, pk, bdq, bdk)
    if not need_ctx:
        return y_lat, None
    y_ctx = pl.pallas_call(
        functools.partial(_gqa_kernel, 0, n_ctx),
        grid=(batch, 1),
        in_specs=[pl.BlockSpec((n_ctx, qw), lambda b, i: (cb + b, GQA_Q)),
                  pl.BlockSpec((n_ctx, kvw), lambda b, i: (cb + b, 16)),
                  pl.BlockSpec((n_ctx, kvw), lambda b, i: (cb + b, 17)),
                  const((1, qw)), const((1, kvw)), const((qw, qw)), const((kvw, kvw))],
        out_specs=pl.BlockSpec((n_ctx, qw), lambda b, i: (b, 0)),
        out_shape=jax.ShapeDtypeStruct((batch * n_ctx, qw), BF16),
        scratch_shapes=scratch(n_ctx),
        compiler_params=_params("parallel", "arbitrary"),
        name="gqa_mixer_ctx",
    )(p_all, p_all, p_all, qg, kg, bdq, bdk)
    return y_lat, y_ctx


def _diff_kernel(n_lat_k, n_ctx, lam_init, *refs):
    if n_lat_k:
        (q_ref, k_ref, v_ref, kc_ref, vc_ref, lp_ref, sg_ref, cq_ref, sq_ref, ck_ref, sk_ref, perm_ref, bd_ref,
         y_ref, kp, vx) = refs
    else:
        q_ref, kc_ref, vc_ref, lp_ref, sg_ref, bd_ref, y_ref, kp, vx = refs
    nk = n_lat_k + n_ctx
    n_pairs = DIFF_HEADS // 2

    @pl.when(pl.program_id(1) == 0)
    def _():
        def put(rs, k, v):
            kp[rs, :] = k
            ones = jnp.ones((v.shape[0], 128), BF16)
            for pr in range(n_pairs):
                vx[pr, rs, :] = jnp.concatenate([v[:, pr * 128:(pr + 1) * 128], ones], axis=-1)

        put(slice(n_lat_k, nk), kc_ref[...], vc_ref[...])
        ck = 512
        for c in range(n_lat_k // ck):
            rs = slice(c * ck, (c + 1) * ck)
            put(rs, _rope_mxu(k_ref[rs, :], ck_ref[rs, :], sk_ref[rs, :], perm_ref).astype(BF16), v_ref[rs, :])

    lp = lp_ref[...]
    lam = (jnp.exp(jnp.sum(lp[0:1, :] * lp[1:2, :], axis=-1, keepdims=True))
           - jnp.exp(jnp.sum(lp[2:3, :] * lp[3:4, :], axis=-1, keepdims=True)) + lam_init)

    q = q_ref[...]
    if n_lat_k:
        q = _rope_mxu(q, cq_ref[...], sq_ref[...], perm_ref).astype(BF16)
    quarter = _lane_masks(4)
    low = _lane_masks(2)[0]
    outs = []
    for pr in range(n_pairs):
        ps = slice(pr * 128, (pr + 1) * 128)
        qv = q[:, ps]
        kv = kp[:, ps]
        o_head = [_softmax_pv(qv * quarter[2 * hh], kv, vx[pr]) - lam * _softmax_pv(qv * quarter[2 * hh + 1], kv, vx[pr])
                  for hh in range(2)]
        outs.append(jnp.where(low > 0, o_head[0], o_head[1]))
    o = jnp.concatenate(outs, axis=-1)
    inv = lax.rsqrt(_sumsq_blocks(o, bd_ref) * (1.0 / DIFF_V_DIM) + EPS)
    y_ref[...] = (o * inv * sg_ref[...] * (1.0 - lam_init)).astype(BF16)


def _diff_mixer(p_all, lam_params, subln, layer_idx, batch, seq, n_ctx, need_ctx):
    n_lat = batch * seq
    cb = n_lat // n_ctx
    lam_init = 0.8 - 0.6 * math.exp(-0.3 * layer_idx)
    lp = jnp.zeros((8, 128), F32).at[:4, :DIFF_QK_DIM].set(lam_params.astype(F32))
    w = DIFF_HEADS * DIFF_V_DIM
    assert w == 256 and DIFF_HEADS * 2 * DIFF_QK_DIM == w
    sg = jnp.tile(subln.astype(F32), DIFF_HEADS).reshape(1, w)
    cq, sq = _rope_tables(seq, DIFF_QK_DIM, w, 8)
    perm = _swap_matrix(w, 8)
    bd = _block_ones(w, DIFF_V_DIM)
    tq = 512
    nqb = seq // tq
    const = lambda shape: pl.BlockSpec(shape, lambda b, i: (0,) * len(shape))
    scratch = lambda nk: [pltpu.VMEM((nk, w), BF16), pltpu.VMEM((DIFF_HEADS // 2, nk, 256), BF16)]
    y_lat = pl.pallas_call(
        functools.partial(_diff_kernel, seq, n_ctx, lam_init),
        grid=(batch, nqb),
        in_specs=[pl.BlockSpec((tq, w), lambda b, i: (b * nqb + i, DIFF_Q)),
                  pl.BlockSpec((seq, w), lambda b, i: (b, DIFF_K)),
                  pl.BlockSpec((seq, w), lambda b, i: (b, DIFF_V)),
                  pl.BlockSpec((n_ctx, w), lambda b, i: (cb + b, DIFF_K)),
                  pl.BlockSpec((n_ctx, w), lambda b, i: (cb + b, DIFF_V)),
                  const((8, 128)), const((1, w)),
                  pl.BlockSpec((tq, w), lambda b, i: (i, 0)),
                  pl.BlockSpec((tq, w), lambda b, i: (i, 0)),
                  const((seq, w)), const((seq, w)), const((w, w)), const((w, w))],
        out_specs=pl.BlockSpec((tq, w), lambda b, i: (b * nqb + i, 0)),
        out_shape=jax.ShapeDtypeStruct((n_lat, w), BF16),
        scratch_shapes=scratch(seq + n_ctx),
        compiler_params=_params("parallel", "arbitrary"),
        name="diff_mixer",
    )(p_all, p_all, p_all, p_all, p_all, lp, sg, cq, sq, cq, sq, perm, bd)
    if not need_ctx:
        return y_lat, None
    y_ctx = pl.pallas_call(
        functools.partial(_diff_kernel, 0, n_ctx, lam_init),
        grid=(batch, 1),
        in_specs=[pl.BlockSpec((n_ctx, w), lambda b, i: (cb + b, DIFF_Q)),
                  pl.BlockSpec((n_ctx, w), lambda b, i: (cb + b, DIFF_K)),
                  pl.BlockSpec((n_ctx, w), lambda b, i: (cb + b, DIFF_V)),
                  const((8, 128)), const((1, w)), const((w, w))],
        out_specs=pl.BlockSpec((n_ctx, w), lambda b, i: (b, 0)),
        out_shape=jax.ShapeDtypeStruct((batch * n_ctx, w), BF16),
        scratch_shapes=scratch(n_ctx),
        compiler_params=_params("parallel", "arbitrary"),
        name="diff_mixer_ctx",
    )(p_all, p_all, p_all, lp, sg, bd)
    return y_lat, y_ctx


def _merge_kernel(nlat_blk, has_ctx, n_parts, *refs):
    x_refs, refs = refs[:n_parts], refs[n_parts:]
    g_refs = refs[:N_BRANCH]
    y_refs = refs[N_BRANCH:2 * N_BRANCH]
    refs = refs[2 * N_BRANCH:]
    if has_ctx:
        yc_refs, refs = refs[:N_BRANCH], refs[N_BRANCH:]
        is_ctx = pl.program_id(0) >= nlat_blk
    wb_ref, wo_ref, mod_ref, gn_ref, wr_ref, xo_ref, m_ref, lg_ref = refs
    acc = None
    for n in range(N_BRANCH):
        y = y_refs[n][...]
        if has_ctx:
            y = jnp.where(is_ctx, yc_refs[n][...], y)
        term = g_refs[n][...].astype(F32) * _dot(y, wb_ref[n])
        acc = term if acc is None else acc + term
    y = _dot(acc.astype(BF16), wo_ref[...])
    x = _pick_rows(x_refs, nlat_blk) + mod_ref[2:3, :] * y
    xo_ref[...] = x
    xn = x * lax.rsqrt(jnp.mean(x * x, axis=-1, keepdims=True) + EPS) * gn_ref[...]
    m = xn * (1.0 + mod_ref[4:5, :]) + mod_ref[3:4, :]
    _store_tile_rows(m_ref, m)
    m_hi = m.astype(BF16)
    m_lo = (m - m_hi.astype(F32)).astype(BF16)
    w = wr_ref[...]
    w_hi = w.astype(BF16)
    w_lo = (w - w_hi.astype(F32)).astype(BF16)
    both = _dot_nt(jnp.concatenate([w_hi, w_lo], axis=0), m_hi)
    lg_ref[...] = both[:N_EXPERTS] + (_dot_nt(w_hi, m_lo) + both[N_EXPERTS:])


def _merge(x_parts, p_all, ys_lat, ys_ctx, wb_bf, wo_bf, modtab, gn, wr_t, tm, n_rows, n_lat, seq, batch):
    d = D_MODEL
    midx = _mod_index(tm, n_lat, seq, batch)
    nlat_blk = n_lat // tm
    has_ctx = ys_ctx is not None
    gate = lambda n: pl.BlockSpec((tm, d), lambda i, n=n: (i, MIX_COLS // d + n))
    y_specs = [pl.BlockSpec((tm, BRANCH_W), lambda i: (jnp.minimum(i, nlat_blk - 1), 0))] * N_BRANCH
    ys = list(ys_lat)
    if has_ctx:
        y_specs += [pl.BlockSpec((tm, BRANCH_W), lambda i: (jnp.maximum(i - nlat_blk, 0), 0))] * N_BRANCH
        ys += list(ys_ctx)
    return pl.pallas_call(
        functools.partial(_merge_kernel, nlat_blk, has_ctx, len(x_parts)),
        grid=(n_rows // tm,),
        in_specs=[*_row_blocks(x_parts, tm, n_lat),
                  gate(0), gate(1), gate(2), gate(3), *y_specs,
                  pl.BlockSpec((N_BRANCH, BRANCH_W, d), lambda i: (0, 0, 0)),
                  pl.BlockSpec((d, d), lambda i: (0, 0)),
                  pl.BlockSpec((None, 8, d), lambda i: (midx(i), 0, 0)),
                  pl.BlockSpec((1, d), lambda i: (0, 0)),
                  pl.BlockSpec((N_EXPERTS, d), lambda i: (0, 0))],
        out_specs=[pl.BlockSpec((tm, d), lambda i: (i, 0)),
                   pl.BlockSpec((tm * ROW_TILE, 128), lambda i: (i, 0)),
                   pl.BlockSpec((N_EXPERTS, tm), lambda i: (0, i))],
        out_shape=[jax.ShapeDtypeStruct((n_rows, d), F32),
                   jax.ShapeDtypeStruct((n_rows * ROW_TILE, 128), F32),
                   jax.ShapeDtypeStruct((N_EXPERTS, n_rows), F32)],
        compiler_params=_params("parallel"),
        name="merge_norm_route",
    )(*x_parts, p_all, p_all, p_all, p_all, *ys, wb_bf, wo_bf, modtab, gn.reshape(1, d), wr_t)


def _route_kernel(lg_ref, b_ref, tri_ref, idx_ref, w_ref, rank_ref, cnt_ref):
    s = _sigmoid(lg_ref[...])
    sel = s + b_ref[...]
    row = lambda a, e: a[e:e + 1, :]
    gsz = EXPERTS_PER_GROUP
    g_idx = None
    best = None
    for g in range(N_GROUPS):
        v = [row(sel, g * gsz + i) for i in range(gsz)]
        score = None
        for i in range(gsz):
            for j in range(i + 1, gsz):
                pair = v[i] + v[j]
                score = pair if score is None else jnp.maximum(score, pair)
        if g == 0:
            best, g_idx = score, jnp.zeros(score.shape, jnp.int32)
        else:
            better = score > best
            best = jnp.where(better, score, best)
            g_idx = jnp.where(better, g, g_idx)

    def in_group(a, i):
        out = row(a, i)
        for g in range(1, N_GROUPS):
            out = jnp.where(g_idx == g, row(a, g * gsz + i), out)
        return out

    v = [in_group(sel, i) for i in range(gsz)]
    sv = [in_group(s, i) for i in range(gsz)]

    def arg_first_max(vals):
        bv, bi = vals[0], jnp.zeros(vals[0].shape, jnp.int32)
        for i in range(1, gsz):
            better = vals[i] > bv
            bv = jnp.where(better, vals[i], bv)
            bi = jnp.where(better, i, bi)
        return bi

    i1 = arg_first_max(v)
    i2 = arg_first_max([jnp.where(i1 == i, -jnp.inf, v[i]) for i in range(gsz)])

    def pick(vals, idx):
        out = vals[0]
        for i in range(1, gsz):
            out = jnp.where(idx == i, vals[i], out)
        return out

    w1 = pick(sv, i1)
    w2 = pick(sv, i2)
    tot = w1 + w2
    e1 = g_idx * gsz + i1
    e2 = g_idx * gsz + i2
    idx_ref[0:1, :] = e1
    idx_ref[1:2, :] = e2
    w_ref[0:1, :] = w1 / tot
    w_ref[1:2, :] = w2 / tot

    @pl.when(pl.program_id(0) == 0)
    def _():
        cnt_ref[...] = jnp.zeros(cnt_ref.shape, cnt_ref.dtype)

    n_e, tn = s.shape
    erow = lax.broadcasted_iota(jnp.int32, (n_e, tn), 0)
    oh1 = jnp.where(erow == e1, 1.0, 0.0)
    oh2 = jnp.where(erow == e2, 1.0, 0.0)
    oh = (oh1 + oh2).astype(BF16)
    base = cnt_ref[...]
    r1, r2 = [], []
    for c in range(tn // 128):
        cs = slice(c * 128, (c + 1) * 128)
        before = base + _dot(oh[:, cs], tri_ref[...])
        r1.append(jnp.sum(oh1[:, cs] * before, axis=0, keepdims=True))
        r2.append(jnp.sum(oh2[:, cs] * before, axis=0, keepdims=True))
        base = base + jnp.sum(oh[:, cs].astype(F32), axis=1, keepdims=True)
    cnt_ref[...] = base
    rank_ref[0:1, :] = jnp.concatenate(r1, axis=-1).astype(jnp.int32)
    rank_ref[1:2, :] = jnp.concatenate(r2, axis=-1).astype(jnp.int32)


def _route(logits_t, router_bias):
    e, n = logits_t.shape
    tn = math.gcd(n, 2048)
    i = np.arange(128)
    tri = jnp.asarray((i[:, None] < i[None, :]).astype(np.float32), BF16)
    idx2, w2, rank, cnt = pl.pallas_call(
        _route_kernel,
        grid=(n // tn,),
        in_specs=[pl.BlockSpec((e, tn), lambda i: (0, i)),
                  pl.BlockSpec((e, 1), lambda i: (0, 0)),
                  pl.BlockSpec((128, 128), lambda i: (0, 0))],
        out_specs=[pl.BlockSpec((TOP_K, tn), lambda i: (0, i)),
                   pl.BlockSpec((TOP_K, tn), lambda i: (0, i)),
                   pl.BlockSpec((TOP_K, tn), lambda i: (0, i)),
                   pl.BlockSpec((e, 128), lambda i: (0, 0))],
        out_shape=[jax.ShapeDtypeStruct((TOP_K, n), jnp.int32),
                   jax.ShapeDtypeStruct((TOP_K, n), F32),
                   jax.ShapeDtypeStruct((TOP_K, n), jnp.int32),
                   jax.ShapeDtypeStruct((e, 128), F32)],
        compiler_params=_params("arbitrary"),
        name="route_top2",
    )(logits_t, router_bias.astype(F32).reshape(e, 1), tri)
    return idx2, w2, rank, cnt[:, 0].astype(jnp.int32)


def _dispatch_kernel(n_tok_steps, tm, pos_ref, m_ref, xs_out, zrow, sem):
    i = pl.program_id(0)
    rt = ROW_TILE
    n = pos_ref.shape[1]

    def wait():
        pltpu.make_async_copy(xs_out.at[pl.ds(0, n * rt)], xs_out.at[pl.ds(0, n * rt)], sem).wait()

    @pl.when(i < n_tok_steps)
    def _():
        for j in range(n):
            part, within = divmod(j, TOP_K * tm)
            row = part * tm + within % tm
            dst = pl.multiple_of(pos_ref[i, j], rt)
            pltpu.make_async_copy(m_ref.at[pl.ds(row * rt, rt)], xs_out.at[pl.ds(dst, rt)],
                                  sem).start(priority=j % 2)
        wait()

    @pl.when(i >= n_tok_steps)
    def _():
        zrow[...] = jnp.zeros(zrow.shape, zrow.dtype)
        for j in range(n):
            dst = pl.multiple_of(pos_ref[i, j], rt)
            pltpu.make_async_copy(zrow, xs_out.at[pl.ds(dst, rt)], sem).start(priority=j % 2)
        wait()


def _dispatch(m_tiles, pos, pad_slots, n_slots):
    rt = ROW_TILE
    fold = DISPATCH_FOLD
    tm = pos.shape[1] // TOP_K
    assert pos.shape[0] % fold == 0 and pad_slots.shape[0] % fold == 0
    n_tok_steps = pos.shape[0] // fold
    steps = jnp.concatenate([pos, pad_slots], axis=0).reshape(-1, fold * pos.shape[1])
    grid_spec = pltpu.PrefetchScalarGridSpec(
        num_scalar_prefetch=1,
        grid=(steps.shape[0],),
        in_specs=[pl.BlockSpec((fold * tm * rt, 128), lambda i, pos: (jnp.minimum(i, n_tok_steps - 1), 0))],
        out_specs=pl.BlockSpec(memory_space=pl.ANY),
        scratch_shapes=[pltpu.VMEM((rt, 128), F32), pltpu.SemaphoreType.DMA(())])
    return pl.pallas_call(
        functools.partial(_dispatch_kernel, n_tok_steps, tm),
        grid_spec=grid_spec,
        out_shape=jax.ShapeDtypeStruct((n_slots * rt, 128), F32),
        compiler_params=_params("arbitrary"),
        name="moe_dispatch",
    )(steps, m_tiles)


def _expert_kernel(be_ref, x_ref, wg_ref, wu_ref, wd_ref, y_ref):
    blk = x_ref.shape[0] // ROW_TILE
    x = _load_tile_rows(x_ref, 0, blk).astype(BF16)
    a = _dot(x, wg_ref[...].astype(BF16))
    h = a * _sigmoid(a) * _dot(x, wu_ref[...].astype(BF16))
    _store_tile_rows(y_ref, _dot(h.astype(BF16), wd_ref[...].astype(BF16)))


def _experts(x_slots, blk_e, blk, layer, w_gate, w_up, w_down):
    rt = ROW_TILE
    d = D_MODEL
    n_slots = x_slots.shape[0] // rt
    grid_spec = pltpu.PrefetchScalarGridSpec(
        num_scalar_prefetch=1,
        grid=(n_slots // blk,),
        in_specs=[pl.BlockSpec((blk * rt, 128), lambda i, be: (i, 0)),
                  pl.BlockSpec((None, None, d, D_EXPERT), lambda i, be: (layer, be[i], 0, 0)),
                  pl.BlockSpec((None, None, d, D_EXPERT), lambda i, be: (layer, be[i], 0, 0)),
                  pl.BlockSpec((None, None, D_EXPERT, d), lambda i, be: (layer, be[i], 0, 0))],
        out_specs=pl.BlockSpec((blk * rt, 128), lambda i, be: (i, 0)))
    return pl.pallas_call(
        _expert_kernel,
        grid_spec=grid_spec,
        out_shape=jax.ShapeDtypeStruct((n_slots * rt, 128), F32),
        compiler_params=_params("arbitrary"),
        name="expert_ffn",
    )(blk_e, x_slots, w_gate, w_up, w_down)


def _combine_kernel(pos_ref, y_hbm, x_ref, w_ref, mod_ref, gf_ref, o_ref, ybuf, sem):
    i = pl.program_id(0)
    last = pl.num_programs(0) - 1
    slot = i % 2
    rt = ROW_TILE
    tm = x_ref.shape[0]
    n = TOP_K * tm

    def issue(step, dst_slot):
        for j in range(n):
            src = pl.multiple_of(pos_ref[step, j], rt)
            pltpu.make_async_copy(y_hbm.at[pl.ds(src, rt)], ybuf.at[dst_slot, pl.ds(j * rt, rt)],
                                  sem.at[dst_slot]).start(priority=j % 2)

    def wait(dst_slot):
        pltpu.make_async_copy(y_hbm.at[pl.ds(0, n * rt)], ybuf.at[dst_slot], sem.at[dst_slot]).wait()

    @pl.when(i == 0)
    def _():
        issue(0, 0)

    wait(slot)
    issue(jnp.minimum(i + 1, last), 1 - slot)
    w = w_ref[...]
    yb = ybuf.at[slot]
    f = w[:, 0:1] * _load_tile_rows(yb, 0, tm) + w[:, 1:2] * _load_tile_rows(yb, tm * rt, tm)
    x = x_ref[...] + mod_ref[5:6, :] * f
    o_ref[...] = x * lax.rsqrt(jnp.mean(x * x, axis=-1, keepdims=True) + EPS) * gf_ref[...]

    @pl.when(i == last)
    def _():
        wait(1 - slot)


def _combine(x_rows, y_slots, pos, w_tok, modtab, g_final, n_lat, seq, batch):
    n_rows, d = x_rows.shape
    n_steps, n = pos.shape
    tm = n // TOP_K
    midx = _mod_index(tm, n_lat, seq, batch)
    grid_spec = pltpu.PrefetchScalarGridSpec(
        num_scalar_prefetch=1,
        grid=(n_steps,),
        in_specs=[pl.BlockSpec(memory_space=pl.ANY),
                  pl.BlockSpec((tm, d), lambda i, pos: (i, 0)),
                  pl.BlockSpec((tm, TOP_K), lambda i, pos: (i, 0)),
                  pl.BlockSpec((None, 8, d), lambda i, pos: (midx(i), 0, 0)),
                  pl.BlockSpec((1, d), lambda i, pos: (0, 0))],
        out_specs=pl.BlockSpec((tm, d), lambda i, pos: (i, 0)),
        scratch_shapes=[pltpu.VMEM((2, n * ROW_TILE, 128), F32), pltpu.SemaphoreType.DMA((2,))])
    return pl.pallas_call(
        _combine_kernel,
        grid_spec=grid_spec,
        out_shape=jax.ShapeDtypeStruct((n_rows, d), F32),
        compiler_params=_params("arbitrary"),
        name="moe_combine",
    )(pos, y_slots, x_rows, w_tok, modtab, g_final.astype(F32).reshape(1, d))


def _combine_project_kernel(n_col_steps, pos_ref, y_hbm, x_ref, wt_ref, modp_ref, g_ref, mod_ref, w_ref, cs_ref,
                            p_ref, xo_ref, h_ref, ybuf, sem):
    i = pl.program_id(0)
    j = pl.program_id(1)
    last = pl.num_programs(0) - 1
    rt = ROW_TILE
    per_step = ybuf.shape[1] // rt
    per_k = n_col_steps // TOP_K

    def issue(block, step):
        for jj in range(per_step):
            src = pl.multiple_of(pos_ref[block * n_col_steps + step, jj], rt)
            pltpu.make_async_copy(y_hbm.at[pl.ds(src, rt)], ybuf.at[step, pl.ds(jj * rt, rt)],
                                  sem).start(priority=jj % 2)

    def wait():
        for step in range(n_col_steps):
            pltpu.make_async_copy(y_hbm.at[pl.ds(0, per_step * rt)], ybuf.at[step], sem).wait()

    @pl.when((i == 0) & (j == 0))
    def _():
        for step in range(n_col_steps):
            issue(0, step)

    @pl.when(j == 0)
    def _():
        wait()
        w = wt_ref[...]
        yk = [jnp.concatenate([_load_tile_rows(ybuf.at[k * per_k + c], 0, per_step) for c in range(per_k)],
                              axis=0) for k in range(TOP_K)]
        f = w[:, 0:1] * yk[0] + w[:, 1:2] * yk[1]
        x = x_ref[...] + modp_ref[5:6, :] * f
        xo_ref[...] = x
        y = x * lax.rsqrt(jnp.mean(x * x, axis=-1, keepdims=True) + EPS) * g_ref[...]
        h_ref[...] = (y * (1.0 + mod_ref[1:2, :]) + mod_ref[0:1, :]).astype(BF16)

    issue(jnp.minimum(i + 1, last), j)
    acc = _dot(h_ref[...], w_ref[...])
    gate = 0.5 * jnp.tanh(0.5 * acc) + 0.5
    cs = cs_ref[...]
    p_ref[...] = jnp.where(cs > 0.0, acc * cs, gate).astype(BF16)

    @pl.when((i == last) & (j == n_col_steps - 1))
    def _():
        wait()


def _combine_project(x_rows, y_slots, pos, w_tok, modtab_prev, g, modtab, w_bf, n_lat, seq, batch):
    n_rows, d = x_rows.shape
    ncol = w_bf.shape[1]
    tn = PROJ_TN
    n_col_steps = ncol // tn
    per_k = n_col_steps // TOP_K
    per_step = pos.shape[1] // TOP_K
    tm = per_k * per_step
    n_steps = n_rows // tm
    assert n_col_steps == TOP_K * per_k and n_rows % tm == 0
    pos = pos.reshape(n_steps, per_k, TOP_K, per_step).transpose(0, 2, 1, 3).reshape(n_steps * n_col_steps, per_step)
    midx = _mod_index(tm, n_lat, seq, batch)
    grid_spec = pltpu.PrefetchScalarGridSpec(
        num_scalar_prefetch=1,
        grid=(n_steps, n_col_steps),
        in_specs=[pl.BlockSpec(memory_space=pl.ANY),
                  pl.BlockSpec((tm, d), lambda i, j, pos: (i, 0)),
                  pl.BlockSpec((tm, TOP_K), lambda i, j, pos: (i, 0)),
                  pl.BlockSpec((None, 8, d), lambda i, j, pos: (midx(i), 0, 0)),
                  pl.BlockSpec((1, d), lambda i, j, pos: (0, 0)),
                  pl.BlockSpec((None, 8, d), lambda i, j, pos: (midx(i), 0, 0)),
                  pl.BlockSpec((d, tn), lambda i, j, pos: (0, j)),
                  pl.BlockSpec((1, tn), lambda i, j, pos: (0, j))],
        out_specs=[pl.BlockSpec((tm, tn), lambda i, j, pos: (i, j)),
                   pl.BlockSpec((tm, d), lambda i, j, pos: (i, 0))],
        scratch_shapes=[pltpu.VMEM((tm, d), BF16), pltpu.VMEM((n_col_steps, per_step * ROW_TILE, 128), F32),
                        pltpu.SemaphoreType.DMA(())])
    p_all, x_new = pl.pallas_call(
        functools.partial(_combine_project_kernel, n_col_steps),
        grid_spec=grid_spec,
        out_shape=[jax.ShapeDtypeStruct((n_rows, ncol), BF16), jax.ShapeDtypeStruct((n_rows, d), F32)],
        compiler_params=_params("arbitrary", "arbitrary"),
        name="combine_norm_project",
    )(pos, y_slots, x_rows, w_tok, modtab_prev, g.reshape(1, d), modtab, w_bf, _mix_col_scale())
    return p_all, x_new


def _dispatch_plan(idx2, rank, counts, blk, tm):
    n = idx2.shape[1]
    a = TOP_K * n
    padded = (counts + blk - 1) // blk * blk
    pad_end = jnp.cumsum(padded)
    pad_start = pad_end - padded
    start_of = jnp.zeros(idx2.shape, jnp.int32)
    for e in range(N_EXPERTS):
        start_of = jnp.where(idx2 == e, pad_start[e], start_of)
    dest = (start_of + rank).astype(jnp.int32)
    n_blocks = a // blk + N_EXPERTS
    first_slot = jnp.arange(n_blocks, dtype=jnp.int32) * blk
    blk_e = jnp.minimum(jnp.sum(pad_end[None, :] <= first_slot[:, None], axis=1), N_EXPERTS - 1).astype(jnp.int32)
    pos = dest.reshape(TOP_K, n // tm, tm).transpose(1, 0, 2).reshape(n // tm, TOP_K * tm) * ROW_TILE
    n_slots = n_blocks * blk
    starts = jnp.concatenate([pad_start + counts, pad_end[-1:]])
    lens = jnp.concatenate([padded - counts, n_slots - pad_end[-1:]])
    ends = jnp.cumsum(lens)
    j = jnp.arange(N_EXPERTS * blk, dtype=jnp.int32)
    shift = starts - (ends - lens)
    shift_of = jnp.zeros(j.shape, jnp.int32)
    for s in range(N_EXPERTS + 1):
        shift_of = jnp.where((j >= ends[s] - lens[s]) & (j < ends[s]), shift[s], shift_of)
    pad_slots = ((shift_of + j) * ROW_TILE).astype(jnp.int32)
    return pos, pad_slots.reshape(-1, TOP_K * tm), blk_e, n_slots


def kernel(x, c, ctx, c_ctx, w_mod, b_mod, g_norm1, g_norm2, w_in, na_rpb, ret_log_decay, gqa_q_gain, gqa_k_gain,
           diff_lambda, diff_subln, w_branch, w_out, w_router, router_bias, w_gate_e, w_up_e, w_down_e, g_final):
    batch, seq, d = x.shape
    n_ctx = ctx.shape[1]
    n_lat = batch * seq
    n_ctx_tot = batch * n_ctx
    n_tot = n_lat + n_ctx_tot
    tm = 1024 if n_ctx_tot % 1024 == 0 else 512

    x_parts = (x.reshape(n_lat, d), ctx.reshape(n_ctx_tot, d))

    nrow = -(-(batch + 1) // 8) * 8
    cc = jnp.zeros((nrow, d), F32).at[:batch].set(c).at[batch].set(c_ctx)
    mod = _modulation(cc, w_mod, b_mod)
    modtab = jnp.pad(mod.reshape(DEPTH, nrow, 6, d), ((0, 0), (0, 0), (0, 2), (0, 0)))

    wr_t = w_router.astype(F32).T
    out = None
    pending = None
    for layer in range(DEPTH):
        need_ctx = layer < DEPTH - 1
        n_rows = n_tot if need_ctx else n_lat
        if pending is None:
            p_all = _project(x_parts, g_norm1[layer], modtab[layer], w_in[layer].astype(BF16), tm, n_lat, seq, batch)
        else:
            p_all, x_new = _combine_project(*pending, g_norm1[layer], modtab[layer], w_in[layer].astype(BF16),
                                            n_lat, seq, batch)
            x_parts = (x_new,)
        ya = _na_mixer(p_all, na_rpb[layer], batch, seq, n_ctx, need_ctx)
        yb = _ret_mixer(p_all, ret_log_decay[layer], batch, seq, n_ctx, need_ctx)
        yc = _gqa_mixer(p_all, gqa_q_gain[layer], gqa_k_gain[layer], batch, seq, n_ctx, need_ctx)
        yd = _diff_mixer(p_all, diff_lambda[layer], diff_subln[layer], layer, batch, seq, n_ctx, need_ctx)
        ys_lat = [ya[0], yb[0], yc[0], yd[0]]
        ys_ctx = [ya[1], yb[1], yc[1], yd[1]] if need_ctx else None
        x_mid, m_all, logits_t = _merge(x_parts, p_all, ys_lat, ys_ctx, w_branch[layer].astype(BF16),
                                        w_out[layer].astype(BF16), modtab[layer], g_norm2[layer], wr_t, MERGE_TM,
                                        n_rows, n_lat, seq, batch)
        idx2, w2, rank, counts = _route(logits_t, router_bias)
        pos, pad_slots, blk_e, n_slots = _dispatch_plan(idx2, rank, counts, MOE_BLK, CMB_TM)
        x_slots = _dispatch(m_all, pos, pad_slots, n_slots)
        y_slots = _experts(x_slots, blk_e, MOE_BLK, layer, w_gate_e, w_up_e, w_down_e)
        if need_ctx:
            pending = (x_mid, y_slots, pos, w2.T, modtab[layer])
        else:
            out = _combine(x_mid, y_slots, pos, w2.T, modtab[layer], g_final, n_lat, seq, batch)
    return out.reshape(batch, seq, d)
```
